```python
import math
import jax, jax.numpy as jnp
from jax import lax
import numpy as np

D_MODEL = 1024
BATCH = 8
SEQ = 2048
DEPTH = 2

BRANCH_WIDTH = D_MODEL
N_BRANCH = 3
EPS = 1e-6
NEG_INF = -1e30
FORCED = 1e4

LRU_WIDTH = BRANCH_WIDTH
LRU_BLOCKS = 8
LRU_BLOCK = LRU_WIDTH // LRU_BLOCKS
CONV_WIDTH = 4
LRU_C = 8.0

NSA_HEADS = 16
NSA_KV_HEADS = 4
NSA_GROUP = NSA_HEADS // NSA_KV_HEADS
NSA_HEAD_DIM = BRANCH_WIDTH // NSA_HEADS
NSA_WIDTH = NSA_HEADS * NSA_HEAD_DIM
CMP_BLOCK = 32
CMP_STRIDE = 16
CMP_HIDDEN = 256
SLC_BLOCK = 64
SLC_TOP_N = 8
SLC_Q_BLOCK = 64
WINDOW = 256
Q_BLOCK = 128

GLA_HEADS = 4
GLA_KEY_WIDTH = D_MODEL // 2
GLA_VALUE_WIDTH = BRANCH_WIDTH
GLA_DK = GLA_KEY_WIDTH // GLA_HEADS
GLA_DV = GLA_VALUE_WIDTH // GLA_HEADS
GLA_GATE_RANK = 16
GLA_TAU = 16.0
GLA_CHUNK = 32

REL_BUCKETS = 32
REL_MAX_EXACT = 16
REL_MAX_DIST = 128

D_FF = -(-8 * D_MODEL // (3 * 256)) * 256

IN_SIZES = (
    LRU_WIDTH,
    LRU_WIDTH,
    NSA_WIDTH,
    6 * NSA_KV_HEADS * NSA_HEAD_DIM,
    3 * NSA_HEADS,
    GLA_KEY_WIDTH,
    GLA_KEY_WIDTH,
    GLA_VALUE_WIDTH,
    GLA_VALUE_WIDTH,
    GLA_GATE_RANK,
    N_BRANCH * D_MODEL,
)
IN_WIDTH = sum(IN_SIZES)

kernel_name = "hybrid_rglru_nsa_gla_block"


def rms_norm(x, g):
    xf = x.astype(jnp.float32)
    y = xf * lax.rsqrt(jnp.mean(xf * xf, axis=-1, keepdims=True) + EPS)
    return (y * g.astype(jnp.float32)).astype(x.dtype)


def rel_bucket(dist):
    n = jnp.maximum(dist, 0)
    nf = jnp.maximum(n, REL_MAX_EXACT).astype(jnp.float32)
    large = REL_MAX_EXACT + (jnp.log(nf / REL_MAX_EXACT) / math.log(REL_MAX_DIST / REL_MAX_EXACT)
                             * (REL_BUCKETS - REL_MAX_EXACT)).astype(jnp.int32)
    large = jnp.minimum(large, REL_BUCKETS - 1)
    return jnp.where(n < REL_MAX_EXACT, n, large)


def rglru_mixer(xa, ga, conv_w, conv_b, w_gates, b_gates, lam):
    B_, S_, _ = xa.shape
    f32 = jnp.float32
    xc = lax.conv_general_dilated(
        xa, conv_w[:, None, :].astype(xa.dtype), window_strides=(1,),
        padding=[(CONV_WIDTH - 1, 0)], dimension_numbers=("NWC", "WIO", "NWC"),
        feature_group_count=LRU_WIDTH) + conv_b
    xb = xc.reshape(B_, S_, LRU_BLOCKS, LRU_BLOCK)
    gates = jnp.einsum("bsnc,knce->kbsne", xb, w_gates).reshape(2, B_, S_, LRU_WIDTH) + b_gates[:, None, None, :]
    r = jax.nn.sigmoid(gates[0].astype(f32))
    i = jax.nn.sigmoid(gates[1].astype(f32))
    log_a = -LRU_C * r * jax.nn.softplus(-lam.astype(f32))
    a = jnp.exp(log_a)
    u = jnp.sqrt(-jnp.expm1(2.0 * log_a)) * (i * xc.astype(f32))

    def combine(left, right):
        a1, b1 = left
        a2, b2 = right
        return a1 * a2, a2 * b1 + b2

    _, h = lax.associative_scan(combine, (a, u), axis=1)
    return h.astype(xa.dtype) * jax.nn.gelu(ga)


def compress_blocks(t, pos_emb, w1, w2):
    B_, G_, S_, hd = t.shape
    r = CMP_BLOCK // CMP_STRIDE
    n_cmp = S_ // CMP_STRIDE - r + 1
    ch = t.reshape(B_, G_, S_ // CMP_STRIDE, CMP_STRIDE, hd)
    blocks = jnp.concatenate([ch[:, :, j:j + n_cmp] for j in range(r)], axis=3) + pos_emb
    return jax.nn.gelu(blocks.reshape(B_, G_, n_cmp, CMP_BLOCK * hd) @ w1) @ w2


def nsa_mixer(q, kv, gate, rel_table, cmp_pos, cmp_w1, cmp_w2):
    B_, S_, _ = q.shape
    G, R, hd = NSA_KV_HEADS, NSA_GROUP, NSA_HEAD_DIM
    f32 = jnp.float32
    q = q.reshape(B_, S_, G, R, hd).transpose(0, 2, 3, 1, 4) * (hd ** -0.5)
    kv = kv.reshape(B_, S_, 6, G, hd).transpose(2, 0, 3, 1, 4)
    k_cmp, v_cmp, k_slc, v_slc, k_win, v_win = kv[0], kv[1], kv[2], kv[3], kv[4], kv[5]
    tbl = rel_table.reshape(REL_BUCKETS, G, R)
    pos = jnp.arange(S_)

    kc = compress_blocks(k_cmp, cmp_pos[0], cmp_w1[0], cmp_w2[0])
    vc = compress_blocks(v_cmp, cmp_pos[1], cmp_w1[1], cmp_w2[1])
    n_cmp = kc.shape[2]
    cmp_end = jnp.arange(n_cmp) * CMP_STRIDE + CMP_BLOCK - 1
    valid_c = cmp_end[None, :] <= pos[:, None]
    bias_c = tbl[rel_bucket(pos[:, None] - cmp_end[None, :])].transpose(2, 3, 0, 1)
    logit_c = jnp.einsum("bgrsd,bgcd->bgrsc", q, kc).astype(f32) + bias_c
    p_c = jax.nn.softmax(jnp.where(valid_c, logit_c, NEG_INF), axis=-1) * valid_c
    o_cmp = jnp.einsum("bgrsc,bgcd->bgrsd", p_c.astype(q.dtype), vc)

    n_slc = S_ // SLC_BLOCK
    top_n = min(SLC_TOP_N, n_slc)
    slc_start = jnp.arange(n_slc) * SLC_BLOCK
    cmp_start = cmp_end - (CMP_BLOCK - 1)
    overlap = jnp.clip(jnp.minimum(cmp_start[:, None] + CMP_BLOCK, slc_start[None, :] + SLC_BLOCK)
                       - jnp.maximum(cmp_start[:, None], slc_start[None, :]), 0).astype(f32) / CMP_BLOCK
    imp = jnp.einsum("bgrsc,cj->bgsj", p_c, overlap)
    blk = jnp.arange(n_slc)[None, :]
    cur = (pos // SLC_BLOCK)[:, None]
    forced = (blk == 0) | (blk == cur) | (blk == cur - 1)
    score = jnp.where(forced, FORCED, jnp.where(blk <= cur, imp, -FORCED))
    _, sel = lax.top_k(score, top_n)

    kb = k_slc.reshape(B_, G, n_slc, SLC_BLOCK, hd)
    vb = v_slc.reshape(B_, G, n_slc, SLC_BLOCK, hd)
    nqs = S_ // SLC_Q_BLOCK
    q_blk = q.reshape(B_, G, R, nqs, SLC_Q_BLOCK, hd).transpose(3, 0, 1, 2, 4, 5)
    sel_blk = sel.reshape(B_, G, nqs, SLC_Q_BLOCK, top_n).transpose(2, 0, 1, 3, 4)
    bi = jnp.arange(B_)[:, None, None, None]
    gi = jnp.arange(G)[None, :, None, None]
    gi5 = jnp.arange(G)[None, :, None, None, None]
    tbl_g = tbl.transpose(1, 0, 2)

    def slc_block(args):
        qb, sb, start = args
        ks = kb[bi, gi, sb]
        vs = vb[bi, gi, sb]
        tq = start + jnp.arange(SLC_Q_BLOCK)
        tk = sb[..., None] * SLC_BLOCK + jnp.arange(SLC_BLOCK)
        dist = tq[:, None, None] - tk
        bias = tbl_g[gi5, rel_bucket(dist)].transpose(0, 1, 5, 2, 3, 4)
        logit = jnp.einsum("bgrqd,bgqnld->bgrqnl", qb, ks).astype(f32) + bias
        logit = jnp.where((dist >= 0)[:, :, None], logit, NEG_INF)
        p = jax.nn.softmax(logit.reshape(B_, G, R, SLC_Q_BLOCK, top_n * SLC_BLOCK), axis=-1)
        p = p.reshape(B_, G, R, SLC_Q_BLOCK, top_n, SLC_BLOCK)
        return jnp.einsum("bgrqnl,bgqnld->bgrqd", p.astype(qb.dtype), vs)

    o_slc = lax.map(slc_block, (q_blk, sel_blk, jnp.arange(nqs) * SLC_Q_BLOCK))
    o_slc = o_slc.transpose(1, 2, 3, 0, 4, 5).reshape(B_, G, R, S_, hd)

    nqb = S_ // Q_BLOCK
    nband = WINDOW // Q_BLOCK + 1

    def band(t):
        tp = jnp.pad(t, ((0, 0), (0, 0), ((nband - 1) * Q_BLOCK, 0), (0, 0)))
        tb = tp.reshape(B_, G, nqb + nband - 1, Q_BLOCK, hd)
        return jnp.concatenate([tb[:, :, j:j + nqb] for j in range(nband)], axis=3)

    kw, vw = band(k_win), band(v_win)
    qi = jnp.arange(Q_BLOCK)
    kj = jnp.arange(nband * Q_BLOCK)
    dist_w = (nband - 1) * Q_BLOCK + qi[:, None] - kj[None, :]
    key_pos = jnp.arange(nqb)[:, None] * Q_BLOCK - (nband - 1) * Q_BLOCK + kj[None, :]
    mask_w = ((dist_w >= 0) & (dist_w < WINDOW))[None] & (key_pos >= 0)[:, None, :]
    bias_w = tbl[rel_bucket(dist_w)].transpose(2, 3, 0, 1)[:, :, None]
    qw = q.reshape(B_, G, R, nqb, Q_BLOCK, hd)
    logit_w = jnp.einsum("bgrnqd,bgnkd->bgrnqk", qw, kw).astype(f32) + bias_w
    p_w = jax.nn.softmax(jnp.where(mask_w, logit_w, NEG_INF), axis=-1)
    o_win = jnp.einsum("bgrnqk,bgnkd->bgrnqd", p_w.astype(q.dtype), vw).reshape(B_, G, R, S_, hd)

    g = jax.nn.sigmoid(gate).reshape(B_, S_, 3, G, R).transpose(2, 0, 3, 4, 1)[..., None]
    o = g[0] * o_cmp + g[1] * o_slc + g[2] * o_win
    return o.transpose(0, 3, 1, 2, 4).reshape(B_, S_, NSA_WIDTH)


def gla_mixer(q, k, v, og, lr, wa2, ba, norm_g):
    B_, S_, _ = q.shape
    f32 = jnp.float32
    nc = S_ // GLA_CHUNK
    log_alpha = jax.nn.log_sigmoid((lr @ wa2 + ba).astype(f32)) / GLA_TAU

    def heads(t, d):
        return t.reshape(B_, nc, GLA_CHUNK, GLA_HEADS, d).transpose(1, 0, 3, 2, 4).astype(f32)

    qh = heads(q, GLA_DK) * (GLA_DK ** -0.5)
    kh = heads(k, GLA_DK)
    vh = heads(v, GLA_DV)
    b = jnp.cumsum(heads(log_alpha, GLA_DK), axis=3)
    q_t = qh * jnp.exp(b)
    k_t = kh * jnp.exp(-b)
    b_last = b[:, :, :, -1:]
    k_end = kh * jnp.exp(b_last - b)
    decay = jnp.exp(b_last[:, :, :, 0])
    causal = jnp.tril(jnp.ones((GLA_CHUNK, GLA_CHUNK), bool))
    att = jnp.where(causal, jnp.einsum("nbhcd,nbhsd->nbhcs", q_t, k_t), 0.0)
    o_intra = jnp.einsum("nbhcs,nbhse->nbhce", att, vh)

    def step(state, xs):
        q_c, ke_c, v_c, dec_c = xs
        o = jnp.einsum("bhcd,bhde->bhce", q_c, state)
        state = state * dec_c[..., None] + jnp.einsum("bhcd,bhce->bhde", ke_c, v_c)
        return state, o

    s0 = jnp.zeros((B_, GLA_HEADS, GLA_DK, GLA_DV), f32)
    _, o_inter = lax.scan(step, s0, (q_t, k_end, vh, decay))
    o = rms_norm(o_intra + o_inter, norm_g)
    o = o.transpose(1, 0, 3, 2, 4).reshape(B_, S_, GLA_VALUE_WIDTH)
    return o.astype(og.dtype) * jax.nn.silu(og)


def hybrid_layer(x, rel_table, norm_g, w_in, conv_w, conv_b, lru_w_gates, lru_b_gates, lru_lambda,
                 cmp_pos, cmp_w1, cmp_w2, gla_wa2, gla_ba, gla_norm, w_branch, w_out, w_ffn_in, w_ffn_out):
    B_, S_, D_ = x.shape
    h = rms_norm(x, norm_g[0])
    proj = h @ w_in
    cuts, acc = [], 0
    for size in IN_SIZES[:-1]:
        acc += size
        cuts.append(acc)
    (lru_x, lru_g, nsa_q, nsa_kv, nsa_gate, gla_q, gla_k, gla_v, gla_og, gla_lr,
     merge_logit) = jnp.split(proj, cuts, axis=-1)
    y_a = rglru_mixer(lru_x, lru_g, conv_w, conv_b, lru_w_gates, lru_b_gates, lru_lambda)
    y_b = nsa_mixer(nsa_q, nsa_kv, nsa_gate, rel_table, cmp_pos, cmp_w1, cmp_w2)
    y_c = gla_mixer(gla_q, gla_k, gla_v, gla_og, gla_lr, gla_wa2, gla_ba, gla_norm)
    gates = jax.nn.sigmoid(merge_logit.reshape(B_, S_, N_BRANCH, D_))
    merged = (gates[:, :, 0] * (y_a @ w_branch[0])
              + gates[:, :, 1] * (y_b @ w_branch[1])
              + gates[:, :, 2] * (y_c @ w_branch[2]))
    x = x + rms_norm(merged @ w_out, norm_g[1])
    h = rms_norm(x, norm_g[2])
    gt, up = jnp.split(h @ w_ffn_in, 2, axis=-1)
    x = x + rms_norm((jax.nn.silu(gt) * up) @ w_ffn_out, norm_g[3])
    return x


def setup_inputs(seed: int = 0) -> dict:
    key = jax.random.key(seed)
    ks = jax.random.split(key, 20)
    f32 = jnp.float32

    def nrm(k, shape, scale):
        return jax.random.normal(k, shape, f32) * scale

    u = jax.random.uniform(ks[8], (DEPTH, LRU_WIDTH), f32, 0.9, 0.999)
    s = u ** (1.0 / LRU_C)
    return {
        "x": nrm(ks[0], (BATCH, SEQ, D_MODEL), 1.0),
        "rel_table": nrm(ks[1], (REL_BUCKETS, NSA_HEADS), 0.5),
        "norm_g": 1.0 + nrm(ks[2], (DEPTH, 4, D_MODEL), 0.05),
        "w_in": nrm(ks[3], (DEPTH, D_MODEL, IN_WIDTH), D_MODEL ** -0.5),
        "conv_w": nrm(ks[4], (DEPTH, CONV_WIDTH, LRU_WIDTH), CONV_WIDTH ** -0.5),
        "conv_b": nrm(ks[5], (DEPTH, LRU_WIDTH), 0.01),
        "lru_w_gates": nrm(ks[6], (DEPTH, 2, LRU_BLOCKS, LRU_BLOCK, LRU_BLOCK), LRU_BLOCK ** -0.5),
        "lru_b_gates": nrm(ks[7], (DEPTH, 2, LRU_WIDTH), 0.01),
        "lru_lambda": jnp.log(s) - jnp.log1p(-s),
        "cmp_pos": nrm(ks[9], (DEPTH, 2, CMP_BLOCK, NSA_HEAD_DIM), 0.1),
        "cmp_w1": nrm(ks[10], (DEPTH, 2, CMP_BLOCK * NSA_HEAD_DIM, CMP_HIDDEN), (CMP_BLOCK * NSA_HEAD_DIM) ** -0.5),
        "cmp_w2": nrm(ks[11], (DEPTH, 2, CMP_HIDDEN, NSA_HEAD_DIM), CMP_HIDDEN ** -0.5),
        "gla_wa2": nrm(ks[12], (DEPTH, GLA_GATE_RANK, GLA_KEY_WIDTH), GLA_GATE_RANK ** -0.5),
        "gla_ba": 1.0 + nrm(ks[13], (DEPTH, GLA_KEY_WIDTH), 0.5),
        "gla_norm": 1.0 + nrm(ks[14], (DEPTH, GLA_DV), 0.05),
        "w_branch": nrm(ks[15], (DEPTH, N_BRANCH, BRANCH_WIDTH, D_MODEL), BRANCH_WIDTH ** -0.5),
        "w_out": nrm(ks[16], (DEPTH, D_MODEL, D_MODEL), D_MODEL ** -0.5),
        "w_ffn_in": nrm(ks[17], (DEPTH, D_MODEL, 2 * D_FF), D_MODEL ** -0.5),
        "w_ffn_out": nrm(ks[18], (DEPTH, D_FF, D_MODEL), D_FF ** -0.5),
    }


def reference(x, rel_table, norm_g, w_in, conv_w, conv_b, lru_w_gates, lru_b_gates, lru_lambda,
              cmp_pos, cmp_w1, cmp_w2, gla_wa2, gla_ba, gla_norm, w_branch, w_out, w_ffn_in, w_ffn_out):
    for l in range(DEPTH):
        x = hybrid_layer(x, rel_table, norm_g[l], w_in[l], conv_w[l], conv_b[l], lru_w_gates[l],
                         lru_b_gates[l], lru_lambda[l], cmp_pos[l], cmp_w1[l], cmp_w2[l],
                         gla_wa2[l], gla_ba[l], gla_norm[l], w_branch[l], w_out[l],
                         w_ffn_in[l], w_ffn_out[l])
    return x
```

```python
import functools
import math

import jax
import jax.numpy as jnp
from jax import lax
from jax.experimental import pallas as pl
from jax.experimental.pallas import tpu as pltpu

F32 = jnp.float32
BF16 = jnp.bfloat16

D_MODEL = 1024
N_BRANCH = 3
EPS = 1e-6
NEG_INF = -1e30
FORCED = 1e4

LRU_BLOCKS = 8
LRU_BLOCK = D_MODEL // LRU_BLOCKS
CONV_WIDTH = 4
LRU_C = 8.0

NSA_HEADS = 16
NSA_G = 4
NSA_R = NSA_HEADS // NSA_G
NSA_HD = D_MODEL // NSA_HEADS
CMP_BLOCK = 32
CMP_STRIDE = 16
CMP_HIDDEN = 256
SLC_BLOCK = 64
SLC_TOP_N = 8
WINDOW = 256

GLA_HEADS = 4
GLA_DK = (D_MODEL // 2) // GLA_HEADS
GLA_DV = D_MODEL // GLA_HEADS
GLA_RANK = 16
GLA_TAU = 16.0
GLA_CHUNK = 32

REL_BUCKETS = 32
REL_MAX_EXACT = 16
REL_MAX_DIST = 128

D_FF = -(-8 * D_MODEL // (3 * 256)) * 256

IN_SIZES = (D_MODEL, D_MODEL, D_MODEL, 6 * NSA_G * NSA_HD, 3 * NSA_HEADS, D_MODEL // 2, D_MODEL // 2,
            D_MODEL, D_MODEL, GLA_RANK, N_BRANCH * D_MODEL)

LANE = 128
COL_LRU_X = 0
COL_LRU_G = 1024
COL_NSA_Q = 2048
COL_GLA_V = 3072
COL_GLA_OG = 4096
COL_MERGE = 5120
COL_NSA_KV = 8192
COL_GLA_Q = 9728
COL_GLA_K = 10240
COL_SMALL = 10752
PROJ_WIDTH = 10880
SMALL_GATE = 0
SMALL_LR = 3 * NSA_HEADS

Q_TILE = 128
VMEM_LIMIT = 56 * 1024 * 1024


def _cparams(sem):
    return pltpu.CompilerParams(dimension_semantics=sem, vmem_limit_bytes=VMEM_LIMIT)


def _const_spec(shape):
    nd = len(shape)
    return pl.BlockSpec(shape, lambda *_: (0,) * nd, pipeline_mode=pl.Buffered(1))


def _rms(x, g):
    return x * lax.rsqrt(jnp.mean(x * x, axis=-1, keepdims=True) + EPS) * g


def _gelu_tanh(x):
    return 0.5 * x * (1.0 + jnp.tanh(math.sqrt(2.0 / math.pi) * (x + 0.044715 * (x * x * x))))


def _softplus(z):
    return jnp.maximum(z, 0.0) + jnp.log1p(jnp.exp(-jnp.abs(z)))


def _in_proj_kernel(x_ref, g_ref, w_ref, o_ref, h_ref):
    @pl.when(pl.program_id(1) == 0)
    def _():
        h_ref[...] = _rms(x_ref[...], g_ref[...]).astype(BF16)

    o_ref[...] = jnp.dot(h_ref[...], w_ref[...], preferred_element_type=F32).astype(o_ref.dtype)


def _in_proj(x2, g, w, tm=1024, tn=640):
    T, D = x2.shape
    N = w.shape[1]
    return pl.pallas_call(
        _in_proj_kernel,
        grid=(T // tm, N // tn),
        in_specs=[pl.BlockSpec((tm, D), lambda i, j: (i, 0)),
                  pl.BlockSpec((1, D), lambda i, j: (0, 0)),
                  pl.BlockSpec((D, tn), lambda i, j: (0, j))],
        out_specs=pl.BlockSpec((tm, tn), lambda i, j: (i, j)),
        out_shape=jax.ShapeDtypeStruct((T, N), BF16),
        scratch_shapes=[pltpu.VMEM((tm, D), BF16)],
        compiler_params=_cparams(("parallel", "arbitrary")),
        name="in_proj",
    )(x2, g, w)


def _lru_kernel(xa_ref, ga_ref, cw_ref, cb_ref, wg_ref, bg_ref, lam_ref, y_ref, xe_ref, h_ref, *, R):
    @pl.when(pl.program_id(1) == 0)
    def _():
        xe_ref[0:8, :] = jnp.zeros((8, D_MODEL), F32)
        h_ref[...] = jnp.zeros_like(h_ref)

    xe_ref[8:8 + R, :] = xa_ref[0].astype(F32)
    xc = cb_ref[...]
    for j in range(CONV_WIDTH):
        off = 8 - (CONV_WIDTH - 1) + j
        xc = xc + cw_ref[j:j + 1, :] * xe_ref[off:off + R, :]
    xe_ref[0:8, :] = xe_ref[R:R + 8, :]

    sp = _softplus(-lam_ref[...])
    row8 = lax.broadcasted_iota(jnp.int32, (R, LRU_BLOCK), 0) & 7
    for n in range(LRU_BLOCKS):
        sl = slice(n * LRU_BLOCK, (n + 1) * LRU_BLOCK)
        xcn = xc[:, sl]
        gz = jnp.dot(xcn.astype(BF16), wg_ref[n], preferred_element_type=F32)
        r = jax.nn.sigmoid(gz[:, :LRU_BLOCK] + bg_ref[0:1, sl])
        i = jax.nn.sigmoid(gz[:, LRU_BLOCK:] + bg_ref[1:2, sl])
        log_a = (-LRU_C) * r * sp[:, sl]
        a = jnp.exp(log_a)
        u = jnp.sqrt(1.0 - jnp.exp(2.0 * log_a)) * (i * xcn)
        for d in (1, 2, 4):
            keep = row8 >= d
            a_s = pltpu.roll(a, d, axis=0)
            u_s = pltpu.roll(u, d, axis=0)
            u = jnp.where(keep, a * u_s + u, u)
            a = jnp.where(keep, a * a_s, a)
        hprev = h_ref[:, sl]
        hs = []
        for t in range(R // 8):
            ht = a[t * 8:(t + 1) * 8] * hprev + u[t * 8:(t + 1) * 8]
            hprev = ht[7:8]
            hs.append(ht)
        h_ref[:, sl] = hprev
        h = jnp.concatenate(hs, axis=0)
        y_ref[0, :, sl] = (h * _gelu_tanh(ga_ref[0, :, sl].astype(F32))).astype(y_ref.dtype)


def _lru(proj3, conv_w, conv_b, wg, bg, lam, R=256):
    B, S, _ = proj3.shape
    D = D_MODEL
    return pl.pallas_call(
        functools.partial(_lru_kernel, R=R),
        grid=(B, S // R),
        in_specs=[pl.BlockSpec((1, R, D), lambda b, s: (b, s, COL_LRU_X // D)),
                  pl.BlockSpec((1, R, D), lambda b, s: (b, s, COL_LRU_G // D)),
                  _const_spec((CONV_WIDTH, D)),
                  _const_spec((1, D)),
                  _const_spec((LRU_BLOCKS, LRU_BLOCK, 2 * LRU_BLOCK)),
                  _const_spec((2, D)),
                  _const_spec((1, D))],
        out_specs=pl.BlockSpec((1, R, D), lambda b, s: (b, s, 0)),
        out_shape=jax.ShapeDtypeStruct((B, S, D), BF16),
        scratch_shapes=[pltpu.VMEM((R + 8, D), F32), pltpu.VMEM((1, D), F32)],
        compiler_params=_cparams(("parallel", "arbitrary")),
        name="rglru",
    )(proj3, proj3, conv_w, conv_b, wg, bg, lam)


def _gla_kernel(q_ref, k_ref, v_ref, og_ref, sm_ref, wa_ref, ba_ref, gn_ref, y_ref, st_ref, *, Tb):
    nc = Tb // GLA_CHUNK

    @pl.when(pl.program_id(1) == 0)
    def _():
        st_ref[...] = jnp.zeros_like(st_ref)

    la_pre = jnp.dot(sm_ref[0], wa_ref[...], preferred_element_type=F32)
    rowc = lax.broadcasted_iota(jnp.int32, (Tb, GLA_DK), 0) & (GLA_CHUNK - 1)
    ri = lax.broadcasted_iota(jnp.int32, (Tb, Tb), 0)
    ci = lax.broadcasted_iota(jnp.int32, (Tb, Tb), 1)
    same_chunk_causal = ((ri // GLA_CHUNK) == (ci // GLA_CHUNK)) & (ci <= ri)
    lane_chunk = lax.broadcasted_iota(jnp.int32, (GLA_DK, Tb), 1) // GLA_CHUNK

    for h in range(GLA_HEADS):
        ks = slice(h * GLA_DK, (h + 1) * GLA_DK)
        vs = slice(h * GLA_DV, (h + 1) * GLA_DV)
        z = la_pre[:, ks] + ba_ref[:, ks]
        b = (jnp.minimum(z, 0.0) - jnp.log1p(jnp.exp(-jnp.abs(z)))) * (1.0 / GLA_TAU)
        d = 1
        while d < GLA_CHUNK:
            b = b + jnp.where(rowc >= d, pltpu.roll(b, d, axis=0), 0.0)
            d *= 2
        b_last = jnp.concatenate(
            [jnp.broadcast_to(b[c * GLA_CHUNK + GLA_CHUNK - 1:(c + 1) * GLA_CHUNK, :], (GLA_CHUNK, GLA_DK))
             for c in range(nc)], axis=0)
        q = q_ref[0, :, ks].astype(F32) * (GLA_DK ** -0.5)
        k = k_ref[0, :, ks].astype(F32)
        qt = (q * jnp.exp(b)).astype(BF16)
        kt = (k * jnp.exp(-b)).astype(BF16)
        ke_t = (k * jnp.exp(b_last - b)).T
        b_t = b.T
        v = v_ref[0, :, vs]

        att = lax.dot_general(qt, kt, (((1,), (1,)), ((), ())), preferred_element_type=F32)
        att = jnp.where(same_chunk_causal, att, 0.0).astype(BF16)
        o = jnp.dot(att, v, preferred_element_type=F32)

        state = st_ref[h]
        inter = []
        for c in range(nc):
            rs = slice(c * GLA_CHUNK, (c + 1) * GLA_CHUNK)
            inter.append(jnp.dot(qt[rs], state.astype(BF16), preferred_element_type=F32))
            ke_c = jnp.where(lane_chunk == c, ke_t, 0.0).astype(BF16)
            kv = jnp.dot(ke_c, v, preferred_element_type=F32)
            last = c * GLA_CHUNK + GLA_CHUNK - 1
            state = state * jnp.exp(b_t[:, last:last + 1]) + kv
        st_ref[h] = state
        o = o + jnp.concatenate(inter, axis=0)

        on = o * lax.rsqrt(jnp.mean(o * o, axis=-1, keepdims=True) + EPS) * gn_ref[...]
        og = og_ref[0, :, vs].astype(F32)
        y_ref[0, :, vs] = (on * (og * jax.nn.sigmoid(og))).astype(y_ref.dtype)


def _gla(proj3, wa_pad, ba, gn, Tb=256):
    B, S, _ = proj3.shape
    HK = GLA_HEADS * GLA_DK
    D = D_MODEL
    return pl.pallas_call(
        functools.partial(_gla_kernel, Tb=Tb),
        grid=(B, S // Tb),
        in_specs=[pl.BlockSpec((1, Tb, HK), lambda b, s: (b, s, COL_GLA_Q // HK)),
                  pl.BlockSpec((1, Tb, HK), lambda b, s: (b, s, COL_GLA_K // HK)),
                  pl.BlockSpec((1, Tb, D), lambda b, s: (b, s, COL_GLA_V // D)),
                  pl.BlockSpec((1, Tb, D), lambda b, s: (b, s, COL_GLA_OG // D)),
                  pl.BlockSpec((1, Tb, LANE), lambda b, s: (b, s, COL_SMALL // LANE)),
                  _const_spec((LANE, HK)),
                  _const_spec((1, HK)),
                  _const_spec((1, GLA_DV))],
        out_specs=pl.BlockSpec((1, Tb, D), lambda b, s: (b, s, 0)),
        out_shape=jax.ShapeDtypeStruct((B, S, D), BF16),
        scratch_shapes=[pltpu.VMEM((GLA_HEADS, GLA_DK, GLA_DV), F32)],
        compiler_params=_cparams(("parallel", "arbitrary")),
        name="gla",
    )(proj3, proj3, proj3, proj3, proj3, wa_pad, ba, gn)


def _cmp_kernel(x_ref, pos_ref, w1_ref, w2_ref, o_ref):
    out = None
    for kv in range(2):
        x = x_ref[0, 0, kv].astype(F32)
        lo = jnp.dot((x + pos_ref[kv, 0]).astype(BF16), w1_ref[kv, 0], preferred_element_type=F32)
        hi = jnp.dot((x + pos_ref[kv, 1]).astype(BF16), w1_ref[kv, 1], preferred_element_type=F32)
        nrow = hi.shape[0]
        pre = lo + pltpu.roll(hi, nrow - 1, axis=0)
        hid = _gelu_tanh(pre).astype(BF16)
        part = jnp.dot(hid, w2_ref[kv], preferred_element_type=F32)
        out = part if out is None else out + part
    o_ref[0, 0] = out.astype(o_ref.dtype)


def _compress(xcmp, pos, w1, w2):
    B, G, _, NC, W = xcmp.shape
    return pl.pallas_call(
        _cmp_kernel,
        grid=(B, G),
        in_specs=[pl.BlockSpec((1, 1, 2, NC, W), lambda b, g: (b, g, 0, 0, 0)),
                  _const_spec(pos.shape), _const_spec(w1.shape), _const_spec(w2.shape)],
        out_specs=pl.BlockSpec((1, 1, NC, LANE), lambda b, g: (b, g, 0, 0)),
        out_shape=jax.ShapeDtypeStruct((B, G, NC, LANE), BF16),
        compiler_params=_cparams(("parallel", "parallel")),
        name="nsa_compress",
    )(xcmp, pos, w1, w2)


_NT = (((1,), (1,)), ((), ()))


def _nsa_kernel(q_ref, kvc_ref, kvs_ref, kvw_ref, gate_ref, bc_ref, bs_ref, bw_ref, exp_ref, ovl_ref,
                gsel_ref, y_ref):
    TQ = Q_TILE
    qi = pl.program_id(2)
    lane = lax.broadcasted_iota(jnp.int32, (TQ, LANE), 1)
    left = lane < NSA_HD

    qf = q_ref[0].astype(F32) * (NSA_HD ** -0.5)
    parts = []
    for r in range(NSA_R):
        blk = qf[:, (r // 2) * LANE:(r // 2 + 1) * LANE]
        if r % 2 == 1:
            blk = pltpu.roll(blk, NSA_HD, axis=1)
        parts.append(jnp.where(left, blk, 0.0))
    qs = jnp.concatenate(parts, axis=0).astype(BF16)

    kvc = kvc_ref[0, 0]
    bias_c = bc_ref[0].reshape(NSA_R * TQ, LANE)
    lg = lax.dot_general(qs, kvc, _NT, preferred_element_type=F32) + bias_c
    e = jnp.exp(lg - jnp.max(lg, axis=-1, keepdims=True))
    p = e / jnp.sum(e, axis=-1, keepdims=True)
    p = jnp.where(bias_c > 0.5 * NEG_INF, p, 0.0)
    o_cmp = jnp.dot(p.astype(BF16), kvc, preferred_element_type=F32)

    psum = p[0:TQ] + p[TQ:2 * TQ] + p[2 * TQ:3 * TQ] + p[3 * TQ:4 * TQ]
    p_hi = psum.astype(BF16)
    p_lo = (psum - p_hi.astype(F32)).astype(BF16)
    ovl = ovl_ref[...]
    imp = (lax.dot_general(ovl, p_hi, _NT, preferred_element_type=F32)
           + lax.dot_general(ovl, p_lo, _NT, preferred_element_type=F32))
    ns = imp.shape[0]
    blk = lax.broadcasted_iota(jnp.int32, (ns, TQ), 0)
    cur = (qi * TQ + lax.broadcasted_iota(jnp.int32, (ns, TQ), 1)) // SLC_BLOCK
    forced = (blk == 0) | (blk == cur) | (blk == cur - 1)
    score = jnp.where(forced, FORCED, jnp.where(blk <= cur, imp, -FORCED))
    rank = jnp.zeros((ns, TQ), F32)
    for j in range(ns):
        sj = score[j:j + 1, :]
        tie = jnp.where(blk > j, 1.0, 0.0)
        rank = rank + jnp.where(sj > score, 1.0, jnp.where(sj == score, tie, 0.0))
    sel_t = jnp.where(rank < float(SLC_TOP_N), 1.0, 0.0)
    sel = jnp.concatenate([sel_t, jnp.zeros((LANE - ns, TQ), F32)], axis=0).T
    sel4 = jnp.concatenate([sel] * NSA_R, axis=0).astype(BF16)

    def attend(carry, kv, bias, mask=None):
        m, l, acc = carry
        lg = lax.dot_general(qs, kv, _NT, preferred_element_type=F32) + bias
        if mask is not None:
            lg = jnp.where(mask, lg, NEG_INF)
        m_new = jnp.maximum(m, jnp.max(lg, axis=-1, keepdims=True))
        alpha = jnp.exp(m - m_new)
        pp = jnp.exp(lg - m_new)
        l = alpha * l + jnp.sum(pp, axis=-1, keepdims=True)
        acc = alpha * acc + jnp.dot(pp.astype(BF16), kv, preferred_element_type=F32)
        return m_new, l, acc

    init = (jnp.full((NSA_R * TQ, 1), NEG_INF, F32), jnp.zeros((NSA_R * TQ, 1), F32),
            jnp.zeros((NSA_R * TQ, LANE), F32))

    def slc_body(j, carry):
        off = pl.multiple_of(j * TQ, TQ)
        kv = kvs_ref[0, pl.ds(off, TQ), :]
        chosen = jnp.dot(sel4, exp_ref[:, pl.ds(off, TQ)], preferred_element_type=F32)
        return attend(carry, kv, bs_ref[0, jnp.minimum(qi - j, 2)], chosen > 0.5)

    _, l_s, acc_s = lax.fori_loop(0, qi + 1, slc_body, init)
    o_slc = acc_s / l_s

    carry = init
    nband = WINDOW // TQ + 1
    for dlt in range(nband - 1, -1, -1):
        j = qi - dlt
        off = pl.multiple_of(jnp.maximum(j, 0) * TQ, TQ)
        kv = kvw_ref[0, pl.ds(off, TQ), :]
        carry = attend(carry, kv, bw_ref[0, jnp.where(j >= 0, dlt, nband)])
    _, l_w, acc_w = carry
    o_win = acc_w / l_w

    gl = jnp.dot(gate_ref[0], gsel_ref[0], preferred_element_type=F32)
    gs = jax.nn.sigmoid(gl)
    outs = []
    for r in range(NSA_R):
        rs = slice(r * TQ, (r + 1) * TQ)
        outs.append(gs[:, r:r + 1] * o_cmp[rs]
                    + gs[:, NSA_R + r:NSA_R + r + 1] * o_slc[rs]
                    + gs[:, 2 * NSA_R + r:2 * NSA_R + r + 1] * o_win[rs])
    for pr in range(NSA_R // 2):
        pair = jnp.where(left, pltpu.roll(outs[2 * pr], NSA_HD, axis=1), outs[2 * pr + 1])
        y_ref[0, :, pr * LANE:(pr + 1) * LANE] = pair.astype(y_ref.dtype)


def _nsa(proj3, kvc, bias_c, bias_s, bias_w, expand, ovl_t, gsel):
    B, S, _ = proj3.shape
    TQ = Q_TILE
    G = NSA_G
    RW = NSA_R * NSA_HD
    kvb = COL_NSA_KV // LANE
    return pl.pallas_call(
        _nsa_kernel,
        grid=(B, G, S // TQ),
        in_specs=[pl.BlockSpec((1, TQ, RW), lambda b, g, i: (b, i, COL_NSA_Q // RW + g)),
                  pl.BlockSpec((1, 1, kvc.shape[2], LANE), lambda b, g, i: (b, g, 0, 0)),
                  pl.BlockSpec((1, S, LANE), lambda b, g, i: (b, 0, kvb + G + g)),
                  pl.BlockSpec((1, S, LANE), lambda b, g, i: (b, 0, kvb + 2 * G + g)),
                  pl.BlockSpec((1, TQ, LANE), lambda b, g, i: (b, i, COL_SMALL // LANE)),
                  pl.BlockSpec((1, NSA_R, TQ, LANE), lambda b, g, i: (g, 0, i, 0)),
                  pl.BlockSpec((1,) + bias_s.shape[1:], lambda b, g, i: (g, 0, 0, 0)),
                  pl.BlockSpec((1,) + bias_w.shape[1:], lambda b, g, i: (g, 0, 0, 0)),
                  pl.BlockSpec(expand.shape, lambda b, g, i: (0, 0)),
                  pl.BlockSpec(ovl_t.shape, lambda b, g, i: (0, 0)),
                  pl.BlockSpec((1, LANE, LANE), lambda b, g, i: (g, 0, 0))],
        out_specs=pl.BlockSpec((1, TQ, RW), lambda b, g, i: (b, i, g)),
        out_shape=jax.ShapeDtypeStruct((B, S, D_MODEL), BF16),
        compiler_params=_cparams(("parallel", "parallel", "arbitrary")),
        name="nsa_attention",
    )(proj3, kvc, proj3, proj3, proj3, bias_c, bias_s, bias_w, expand, ovl_t, gsel)


def _merge_kernel(ya_ref, yb_ref, yc_ref, g0_ref, g1_ref, g2_ref, x_ref, wb_ref, wo_ref, gn_ref, o_ref):
    m = None
    for br, (y_ref, g_ref) in enumerate(((ya_ref, g0_ref), (yb_ref, g1_ref), (yc_ref, g2_ref))):
        t = jax.nn.sigmoid(g_ref[...].astype(F32)) * jnp.dot(y_ref[...], wb_ref[br], preferred_element_type=F32)
        m = t if m is None else m + t
    z = jnp.dot(m.astype(BF16), wo_ref[...], preferred_element_type=F32)
    o_ref[...] = x_ref[...] + _rms(z, gn_ref[...])


def _merge(ya, yb, yc, proj, x2, wb, wo, gn, tm=512):
    T, D = x2.shape
    row = lambda i: (i, 0)
    mcol = COL_MERGE // D
    return pl.pallas_call(
        _merge_kernel,
        grid=(T // tm,),
        in_specs=[pl.BlockSpec((tm, D), row), pl.BlockSpec((tm, D), row), pl.BlockSpec((tm, D), row),
                  pl.BlockSpec((tm, D), lambda i: (i, mcol)),
                  pl.BlockSpec((tm, D), lambda i: (i, mcol + 1)),
                  pl.BlockSpec((tm, D), lambda i: (i, mcol + 2)),
                  pl.BlockSpec((tm, D), row),
                  _const_spec(wb.shape), _const_spec(wo.shape), _const_spec((1, D))],
        out_specs=pl.BlockSpec((tm, D), row),
        out_shape=jax.ShapeDtypeStruct((T, D), F32),
        compiler_params=_cparams(("parallel",)),
        name="merge_out",
    )(ya, yb, yc, proj, proj, proj, x2, wb, wo, gn)


def _ffn_kernel(x_ref, gpre_ref, win_ref, wout_ref, gpost_ref, o_ref, *, chunk):
    x = x_ref[...]
    h = _rms(x, gpre_ref[...]).astype(BF16)
    acc = None
    for c in range(D_FF // chunk):
        gt = jnp.dot(h, win_ref[:, c * chunk:(c + 1) * chunk], preferred_element_type=F32)
        up = jnp.dot(h, win_ref[:, D_FF + c * chunk:D_FF + (c + 1) * chunk], preferred_element_type=F32)
        a = (gt * jax.nn.sigmoid(gt) * up).astype(BF16)
        t = jnp.dot(a, wout_ref[c * chunk:(c + 1) * chunk, :], preferred_element_type=F32)
        acc = t if acc is None else acc + t
    o_ref[...] = x + _rms(acc, gpost_ref[...])


def _ffn(x2, gpre, win, wout, gpost, tm=512, chunk=256):
    T, D = x2.shape
    row = lambda i: (i, 0)
    return pl.pallas_call(
        functools.partial(_ffn_kernel, chunk=chunk),
        grid=(T // tm,),
        in_specs=[pl.BlockSpec((tm, D), row), _const_spec((1, D)), _const_spec(win.shape),
                  _const_spec(wout.shape), _const_spec((1, D))],
        out_specs=pl.BlockSpec((tm, D), row),
        out_shape=jax.ShapeDtypeStruct((T, D), F32),
        compiler_params=_cparams(("parallel",)),
        name="ffn",
    )(x2, gpre, win, wout, gpost)


def _permute_in_columns(w):
    cuts = []
    acc = 0
    for size in IN_SIZES[:-1]:
        acc += size
        cuts.append(acc)
    (lru_x, lru_g, nsa_q, nsa_kv, nsa_gate, gla_q, gla_k, gla_v, gla_og, gla_lr, merge) = jnp.split(w, cuts, axis=-1)
    lead = w.shape[:-1]
    kv = nsa_kv.reshape(lead + (3, 2, NSA_G, NSA_HD))
    kv = jnp.moveaxis(kv, -3, -2).reshape(lead + (6 * NSA_G * NSA_HD,))
    pad = jnp.zeros(lead + (LANE - nsa_gate.shape[-1] - gla_lr.shape[-1],), w.dtype)
    return jnp.concatenate([lru_x, lru_g, nsa_q, gla_v, gla_og, merge, kv, gla_q, gla_k, nsa_gate, gla_lr, pad],
                           axis=-1)


def _rel_bucket(dist):
    n = jnp.maximum(dist, 0)
    nf = jnp.maximum(n, REL_MAX_EXACT).astype(F32)
    large = REL_MAX_EXACT + (jnp.log(nf / REL_MAX_EXACT) / math.log(REL_MAX_DIST / REL_MAX_EXACT)
                             * (REL_BUCKETS - REL_MAX_EXACT)).astype(jnp.int32)
    large = jnp.minimum(large, REL_BUCKETS - 1)
    return jnp.where(n < REL_MAX_EXACT, n, large)


def _bias_tables(rel_table, S):
    TQ = Q_TILE
    tbl = rel_table.astype(F32).reshape(REL_BUCKETS, NSA_G, NSA_R)

    def lookup(dist, ok):
        b = jnp.moveaxis(tbl[_rel_bucket(dist)], (-2, -1), (0, 1))
        b = jnp.where(ok, b, NEG_INF)
        b = jnp.moveaxis(b, 1, -3)
        return b.reshape(b.shape[:-3] + (NSA_R * dist.shape[-2], dist.shape[-1]))

    i = jnp.arange(TQ)[:, None]
    j = jnp.arange(TQ)[None, :]
    nband = WINDOW // TQ + 1
    dist = jnp.arange(nband)[:, None, None] * TQ + i - j
    bias_s = lookup(dist, dist >= 0)
    bias_w = lookup(dist, (dist >= 0) & (dist < WINDOW))
    bias_w = jnp.concatenate([bias_w, jnp.full_like(bias_w[:, :1], NEG_INF)], axis=1)
    n_cmp = S // CMP_STRIDE - CMP_BLOCK // CMP_STRIDE + 1
    cmp_end = jnp.arange(LANE) * CMP_STRIDE + CMP_BLOCK - 1
    dist_c = jnp.arange(S)[:, None] - cmp_end[None, :]
    ok_c = (dist_c >= 0) & (jnp.arange(LANE)[None, :] < n_cmp)
    tb = jnp.moveaxis(tbl[_rel_bucket(dist_c)], (-2, -1), (0, 1))
    bias_c = jnp.where(ok_c, tb, NEG_INF)
    return bias_c, bias_s, bias_w


def _selection_constants(S):
    n_slc = S // SLC_BLOCK
    n_cmp = S // CMP_STRIDE - CMP_BLOCK // CMP_STRIDE + 1
    expand = (jnp.arange(LANE)[:, None] == (jnp.arange(S)[None, :] // SLC_BLOCK)).astype(BF16)
    cmp_start = jnp.arange(LANE) * CMP_STRIDE
    slc_start = jnp.arange(n_slc) * SLC_BLOCK
    overlap = jnp.clip(jnp.minimum(cmp_start[:, None] + CMP_BLOCK, slc_start[None, :] + SLC_BLOCK)
                       - jnp.maximum(cmp_start[:, None], slc_start[None, :]), 0).astype(F32) / CMP_BLOCK
    overlap = jnp.where(jnp.arange(LANE)[:, None] < n_cmp, overlap, 0.0)
    g = jnp.arange(NSA_G)[:, None, None]
    row = jnp.arange(LANE)[None, :, None]
    col = jnp.arange(LANE)[None, None, :]
    src = SMALL_GATE + (col // NSA_R) * NSA_HEADS + g * NSA_R + col % NSA_R
    gsel = ((row == src) & (col < 3 * NSA_R)).astype(BF16)
    return expand, overlap.T.astype(BF16), gsel


def _layer(x2, B, S, tables, norm_g, w_in, conv_w, conv_b, lru_w_gates, lru_b_gates, lru_lambda,
           cmp_pos, cmp_w1, cmp_w2, gla_wa2, gla_ba, gla_norm, w_branch, w_out, w_ffn_in, w_ffn_out):
    bias_c, bias_s, bias_w, expand, ovl_t, gsel = tables
    D = D_MODEL
    w_perm = _permute_in_columns(w_in).astype(BF16)
    proj = _in_proj(x2, norm_g[0][None, :], w_perm)
    proj3 = proj.reshape(B, S, PROJ_WIDTH)

    wg = jnp.concatenate([lru_w_gates[0], lru_w_gates[1]], axis=-1).astype(BF16)
    y_a = _lru(proj3, conv_w, conv_b[None, :], wg, lru_b_gates, lru_lambda[None, :])

    nrow = S // CMP_STRIDE
    xc = proj3[:, :, COL_NSA_KV:COL_NSA_KV + 2 * NSA_G * NSA_HD]
    xc = xc.reshape(B, nrow, CMP_STRIDE, NSA_G, 2, NSA_HD).transpose(0, 3, 4, 1, 2, 5)
    xc = xc.reshape(B, NSA_G, 2, nrow, CMP_STRIDE * NSA_HD)
    half = CMP_STRIDE * NSA_HD
    pos = cmp_pos.reshape(2, 2, 1, half)
    w1 = cmp_w1.reshape(2, 2, half, CMP_HIDDEN).astype(BF16)
    zeros = jnp.zeros_like(cmp_w2[0])
    w2 = jnp.stack([jnp.concatenate([cmp_w2[0], zeros], axis=-1),
                    jnp.concatenate([zeros, cmp_w2[1]], axis=-1)]).astype(BF16)
    kvc = _compress(xc, pos, w1, w2)
    y_b = _nsa(proj3, kvc, bias_c, bias_s, bias_w, expand, ovl_t, gsel)

    wa_pad = jnp.zeros((LANE, GLA_HEADS * GLA_DK), F32).at[SMALL_LR:SMALL_LR + GLA_RANK].set(gla_wa2).astype(BF16)
    y_c = _gla(proj3, wa_pad, gla_ba[None, :], gla_norm[None, :])

    x2 = _merge(y_a.reshape(B * S, D), y_b.reshape(B * S, D), y_c.reshape(B * S, D), proj, x2,
                w_branch.astype(BF16), w_out.astype(BF16), norm_g[1][None, :])
    x2 = _ffn(x2, norm_g[2][None, :], w_ffn_in.astype(BF16), w_ffn_out.astype(BF16), norm_g[3][None, :])
    return x2


def kernel(x, rel_table, norm_g, w_in, conv_w, conv_b, lru_w_gates, lru_b_gates, lru_lambda, cmp_pos, cmp_w1,
           cmp_w2, gla_wa2, gla_ba, gla_norm, w_branch, w_out, w_ffn_in, w_ffn_out):
    B, S, D = x.shape
    tables = _bias_tables(rel_table, S) + _selection_constants(S)
    x2 = x.reshape(B * S, D)
    for l in range(norm_g.shape[0]):
        x2 = _layer(x2, B, S, tables, norm_g[l], w_in[l], conv_w[l], conv_b[l], lru_w_gates[l], lru_b_gates[l],
                    lru_lambda[l], cmp_pos[l], cmp_w1[l], cmp_w2[l], gla_wa2[l], gla_ba[l], gla_norm[l],
                    w_branch[l], w_out[l], w_ffn_in[l], w_ffn_out[l])
    return x2.reshape(B, S, D)
```

```python
import functools
import math

import jax
import jax.numpy as jnp
from jax import lax
from jax.experimental import pallas as pl
from jax.experimental.pallas import tpu as pltpu

F32 = jnp.float32
BF16 = jnp.bfloat16

D_MODEL = 1024
N_BRANCH = 3
EPS = 1e-6
NEG_INF = -1e30
FORCED = 1e4

LRU_BLOCKS = 8
LRU_BLOCK = D_MODEL // LRU_BLOCKS
CONV_WIDTH = 4
LRU_C = 8.0

NSA_HEADS = 16
NSA_G = 4
NSA_R = NSA_HEADS // NSA_G
NSA_HD = D_MODEL // NSA_HEADS
CMP_BLOCK = 32
CMP_STRIDE = 16
CMP_HIDDEN = 256
SLC_BLOCK = 64
SLC_TOP_N = 8
WINDOW = 256

GLA_HEADS = 4
GLA_DK = (D_MODEL // 2) // GLA_HEADS
GLA_DV = D_MODEL // GLA_HEADS
GLA_RANK = 16
GLA_TAU = 16.0
GLA_CHUNK = 32

REL_BUCKETS = 32
REL_MAX_EXACT = 16
REL_MAX_DIST = 128

D_FF = -(-8 * D_MODEL // (3 * 256)) * 256

IN_SIZES = (D_MODEL, D_MODEL, D_MODEL, 6 * NSA_G * NSA_HD, 3 * NSA_HEADS, D_MODEL // 2, D_MODEL // 2,
            D_MODEL, D_MODEL, GLA_RANK, N_BRANCH * D_MODEL)

LANE = 128
COL_LRU_X = 0
COL_LRU_G = 1024
COL_NSA_Q = 2048
COL_GLA_V = 3072
COL_GLA_OG = 4096
COL_MERGE = 5120
COL_NSA_KV = 8192
COL_GLA_Q = 9728
COL_GLA_K = 10240
COL_SMALL = 10752
PROJ_WIDTH = 10880
SMALL_GATE = 0
SMALL_LR = 3 * NSA_HEADS

Q_TILE = 128
VMEM_LIMIT = 56 * 1024 * 1024


def _cparams(sem):
    return pltpu.CompilerParams(dimension_semantics=sem, vmem_limit_bytes=VMEM_LIMIT)


def _const_spec(shape):
    nd = len(shape)
    return pl.BlockSpec(shape, lambda *_: (0,) * nd, pipeline_mode=pl.Buffered(1))


def _rms(x, g):
    return x * lax.rsqrt(jnp.mean(x * x, axis=-1, keepdims=True) + EPS) * g


def _gelu_tanh(x):
    return 0.5 * x * (1.0 + jnp.tanh(math.sqrt(2.0 / math.pi) * (x + 0.044715 * (x * x * x))))


def _softplus(z):
    return jnp.maximum(z, 0.0) + jnp.log1p(jnp.exp(-jnp.abs(z)))


def _in_proj_kernel(x_ref, g_ref, w_ref, o_ref, h_ref):
    @pl.when(pl.program_id(1) == 0)
    def _():
        h_ref[...] = _rms(x_ref[...], g_ref[...]).astype(BF16)

    o_ref[...] = jnp.dot(h_ref[...], w_ref[...], preferred_element_type=F32).astype(o_ref.dtype)


def _in_proj(x2, g, w, tm=1024, tn=640):
    T, D = x2.shape
    N = w.shape[1]
    return pl.pallas_call(
        _in_proj_kernel,
        grid=(T // tm, N // tn),
        in_specs=[pl.BlockSpec((tm, D), lambda i, j: (i, 0)),
                  pl.BlockSpec((1, D), lambda i, j: (0, 0)),
                  pl.BlockSpec((D, tn), lambda i, j: (0, j))],
        out_specs=pl.BlockSpec((tm, tn), lambda i, j: (i, j)),
        out_shape=jax.ShapeDtypeStruct((T, N), BF16),
        scratch_shapes=[pltpu.VMEM((tm, D), BF16)],
        compiler_params=_cparams(("parallel", "arbitrary")),
        name="in_proj",
    )(x2, g, w)


def _lru_kernel(xa_ref, ga_ref, cw_ref, cb_ref, wg_ref, bg_ref, lam_ref, y_ref, xe_ref, h_ref, *, R):
    @pl.when(pl.program_id(1) == 0)
    def _():
        xe_ref[0:8, :] = jnp.zeros((8, D_MODEL), F32)
        h_ref[...] = jnp.zeros_like(h_ref)

    xe_ref[8:8 + R, :] = xa_ref[0].astype(F32)
    xc = cb_ref[...]
    for j in range(CONV_WIDTH):
        off = 8 - (CONV_WIDTH - 1) + j
        xc = xc + cw_ref[j:j + 1, :] * xe_ref[off:off + R, :]
    xe_ref[0:8, :] = xe_ref[R:R + 8, :]

    sp = _softplus(-lam_ref[...])
    row8 = lax.broadcasted_iota(jnp.int32, (R, LRU_BLOCK), 0) & 7
    for n in range(LRU_BLOCKS):
        sl = slice(n * LRU_BLOCK, (n + 1) * LRU_BLOCK)
        xcn = xc[:, sl]
        gz = jnp.dot(xcn.astype(BF16), wg_ref[n], preferred_element_type=F32)
        r = jax.nn.sigmoid(gz[:, :LRU_BLOCK] + bg_ref[0:1, sl])
        i = jax.nn.sigmoid(gz[:, LRU_BLOCK:] + bg_ref[1:2, sl])
        log_a = (-LRU_C) * r * sp[:, sl]
        a = jnp.exp(log_a)
        u = jnp.sqrt(1.0 - jnp.exp(2.0 * log_a)) * (i * xcn)
        for d in (1, 2, 4):
            keep = row8 >= d
            a_s = pltpu.roll(a, d, axis=0)
            u_s = pltpu.roll(u, d, axis=0)
            u = jnp.where(keep, a * u_s + u, u)
            a = jnp.where(keep, a * a_s, a)
        hprev = h_ref[:, sl]
        hs = []
        for t in range(R // 8):
            ht = a[t * 8:(t + 1) * 8] * hprev + u[t * 8:(t + 1) * 8]
            hprev = ht[7:8]
            hs.append(ht)
        h_ref[:, sl] = hprev
        h = jnp.concatenate(hs, axis=0)
        y_ref[0, :, sl] = (h * _gelu_tanh(ga_ref[0, :, sl].astype(F32))).astype(y_ref.dtype)


def _lru(proj3, conv_w, conv_b, wg, bg, lam, R=256):
    B, S, _ = proj3.shape
    D = D_MODEL
    return pl.pallas_call(
        functools.partial(_lru_kernel, R=R),
        grid=(B, S // R),
        in_specs=[pl.BlockSpec((1, R, D), lambda b, s: (b, s, COL_LRU_X // D)),
                  pl.BlockSpec((1, R, D), lambda b, s: (b, s, COL_LRU_G // D)),
                  _const_spec((CONV_WIDTH, D)),
                  _const_spec((1, D)),
                  _const_spec((LRU_BLOCKS, LRU_BLOCK, 2 * LRU_BLOCK)),
                  _const_spec((2, D)),
                  _const_spec((1, D))],
        out_specs=pl.BlockSpec((1, R, D), lambda b, s: (b, s, 0)),
        out_shape=jax.ShapeDtypeStruct((B, S, D), BF16),
        scratch_shapes=[pltpu.VMEM((R + 8, D), F32), pltpu.VMEM((1, D), F32)],
        compiler_params=_cparams(("parallel", "arbitrary")),
        name="rglru",
    )(proj3, proj3, conv_w, conv_b, wg, bg, lam)


def _gla_kernel(q_ref, k_ref, v_ref, og_ref, sm_ref, wa_ref, ba_ref, gn_ref, y_ref, st_ref, *, Tb):
    nc = Tb // GLA_CHUNK

    @pl.when(pl.program_id(1) == 0)
    def _():
        st_ref[...] = jnp.zeros_like(st_ref)

    la_pre = jnp.dot(sm_ref[0], wa_ref[...], preferred_element_type=F32)
    rowc = lax.broadcasted_iota(jnp.int32, (Tb, GLA_DK), 0) & (GLA_CHUNK - 1)
    ri = lax.broadcasted_iota(jnp.int32, (Tb, Tb), 0)
    ci = lax.broadcasted_iota(jnp.int32, (Tb, Tb), 1)
    same_chunk_causal = ((ri // GLA_CHUNK) == (ci // GLA_CHUNK)) & (ci <= ri)
    lane_chunk = lax.broadcasted_iota(jnp.int32, (GLA_DK, Tb), 1) // GLA_CHUNK

    for h in range(GLA_HEADS):
        ks = slice(h * GLA_DK, (h + 1) * GLA_DK)
        vs = slice(h * GLA_DV, (h + 1) * GLA_DV)
        z = la_pre[:, ks] + ba_ref[:, ks]
        b = (jnp.minimum(z, 0.0) - jnp.log1p(jnp.exp(-jnp.abs(z)))) * (1.0 / GLA_TAU)
        d = 1
        while d < GLA_CHUNK:
            b = b + jnp.where(rowc >= d, pltpu.roll(b, d, axis=0), 0.0)
            d *= 2
        b_last = jnp.concatenate(
            [jnp.broadcast_to(b[c * GLA_CHUNK + GLA_CHUNK - 1:(c + 1) * GLA_CHUNK, :], (GLA_CHUNK, GLA_DK))
             for c in range(nc)], axis=0)
        q = q_ref[0, :, ks].astype(F32) * (GLA_DK ** -0.5)
        k = k_ref[0, :, ks].astype(F32)
        qt = (q * jnp.exp(b)).astype(BF16)
        kt = (k * jnp.exp(-b)).astype(BF16)
        ke_t = (k * jnp.exp(b_last - b)).T
        b_t = b.T
        v = v_ref[0, :, vs]

        att = lax.dot_general(qt, kt, (((1,), (1,)), ((), ())), preferred_element_type=F32)
        att = jnp.where(same_chunk_causal, att, 0.0).astype(BF16)
        o = jnp.dot(att, v, preferred_element_type=F32)

        state = st_ref[h]
        inter = []
        for c in range(nc):
            rs = slice(c * GLA_CHUNK, (c + 1) * GLA_CHUNK)
            inter.append(jnp.dot(qt[rs], state.astype(BF16), preferred_element_type=F32))
            ke_c = jnp.where(lane_chunk == c, ke_t, 0.0).astype(BF16)
            kv = jnp.dot(ke_c, v, preferred_element_type=F32)
            last = c * GLA_CHUNK + GLA_CHUNK - 1
            state = state * jnp.exp(b_t[:, last:last + 1]) + kv
        st_ref[h] = state
        o = o + jnp.concatenate(inter, axis=0)

        on = o * lax.rsqrt(jnp.mean(o * o, axis=-1, keepdims=True) + EPS) * gn_ref[...]
        og = og_ref[0, :, vs].astype(F32)
        y_ref[0, :, vs] = (on * (og * jax.nn.sigmoid(og))).astype(y_ref.dtype)


def _gla(proj3, wa_pad, ba, gn, Tb=256):
    B, S, _ = proj3.shape
    HK = GLA_HEADS * GLA_DK
    D = D_MODEL
    return pl.pallas_call(
        functools.partial(_gla_kernel, Tb=Tb),
        grid=(B, S // Tb),
        in_specs=[pl.BlockSpec((1, Tb, HK), lambda b, s: (b, s, COL_GLA_Q // HK)),
                  pl.BlockSpec((1, Tb, HK), lambda b, s: (b, s, COL_GLA_K // HK)),
                  pl.BlockSpec((1, Tb, D), lambda b, s: (b, s, COL_GLA_V // D)),
                  pl.BlockSpec((1, Tb, D), lambda b, s: (b, s, COL_GLA_OG // D)),
                  pl.BlockSpec((1, Tb, LANE), lambda b, s: (b, s, COL_SMALL // LANE)),
                  _const_spec((LANE, HK)),
                  _const_spec((1, HK)),
                  _const_spec((1, GLA_DV))],
        out_specs=pl.BlockSpec((1, Tb, D), lambda b, s: (b, s, 0)),
        out_shape=jax.ShapeDtypeStruct((B, S, D), BF16),
        scratch_shapes=[pltpu.VMEM((GLA_HEADS, GLA_DK, GLA_DV), F32)],
        compiler_params=_cparams(("parallel", "arbitrary")),
        name="gla",
    )(proj3, proj3, proj3, proj3, proj3, wa_pad, ba, gn)


def _cmp_kernel(x_ref, pos_ref, w1_ref, w2_ref, o_ref):
    out = None
    for kv in range(2):
        x = x_ref[0, 0, kv].astype(F32)
        lo = jnp.dot((x + pos_ref[kv, 0]).astype(BF16), w1_ref[kv, 0], preferred_element_type=F32)
        hi = jnp.dot((x + pos_ref[kv, 1]).astype(BF16), w1_ref[kv, 1], preferred_element_type=F32)
        nrow = hi.shape[0]
        pre = lo + pltpu.roll(hi, nrow - 1, axis=0)
        hid = _gelu_tanh(pre).astype(BF16)
        part = jnp.dot(hid, w2_ref[kv], preferred_element_type=F32)
        out = part if out is None else out + part
    o_ref[0, 0] = out.astype(o_ref.dtype)


def _compress(xcmp, pos, w1, w2):
    B, G, _, NC, W = xcmp.shape
    return pl.pallas_call(
        _cmp_kernel,
        grid=(B, G),
        in_specs=[pl.BlockSpec((1, 1, 2, NC, W), lambda b, g: (b, g, 0, 0, 0)),
                  _const_spec(pos.shape), _const_spec(w1.shape), _const_spec(w2.shape)],
        out_specs=pl.BlockSpec((1, 1, NC, LANE), lambda b, g: (b, g, 0, 0)),
        out_shape=jax.ShapeDtypeStruct((B, G, NC, LANE), BF16),
        compiler_params=_cparams(("parallel", "parallel")),
        name="nsa_compress",
    )(xcmp, pos, w1, w2)


_NT = (((1,), (1,)), ((), ()))


def _nsa_kernel(q_ref, kvc_ref, kvs_ref, kvw_ref, gate_ref, bc_ref, bs_ref, bw_ref, exp_ref, ovl_ref,
                gsel_ref, y_ref):
    TQ = Q_TILE
    qi = pl.program_id(2)
    lane = lax.broadcasted_iota(jnp.int32, (TQ, LANE), 1)
    left = lane < NSA_HD

    qf = q_ref[0].astype(F32) * (NSA_HD ** -0.5)
    parts = []
    for r in range(NSA_R):
        blk = qf[:, (r // 2) * LANE:(r // 2 + 1) * LANE]
        if r % 2 == 1:
            blk = pltpu.roll(blk, NSA_HD, axis=1)
        parts.append(jnp.where(left, blk, 0.0))
    qs = jnp.concatenate(parts, axis=0).astype(BF16)

    kvc = kvc_ref[0, 0]
    bias_c = bc_ref[0].reshape(NSA_R * TQ, LANE)
    lg = lax.dot_general(qs, kvc, _NT, preferred_element_type=F32) + bias_c
    e = jnp.exp(lg - jnp.max(lg, axis=-1, keepdims=True))
    p = e / jnp.sum(e, axis=-1, keepdims=True)
    p = jnp.where(bias_c > 0.5 * NEG_INF, p, 0.0)
    o_cmp = jnp.dot(p.astype(BF16), kvc, preferred_element_type=F32)

    psum = p[0:TQ] + p[TQ:2 * TQ] + p[2 * TQ:3 * TQ] + p[3 * TQ:4 * TQ]
    p_hi = psum.astype(BF16)
    p_lo = (psum - p_hi.astype(F32)).astype(BF16)
    ovl = ovl_ref[...]
    imp = (lax.dot_general(ovl, p_hi, _NT, preferred_element_type=F32)
           + lax.dot_general(ovl, p_lo, _NT, preferred_element_type=F32))
    ns = imp.shape[0]
    blk = lax.broadcasted_iota(jnp.int32, (ns, TQ), 0)
    cur = (qi * TQ + lax.broadcasted_iota(jnp.int32, (ns, TQ), 1)) // SLC_BLOCK
    forced = (blk == 0) | (blk == cur) | (blk == cur - 1)
    score = jnp.where(forced, FORCED, jnp.where(blk <= cur, imp, -FORCED))
    rank = jnp.zeros((ns, TQ), F32)
    for j in range(ns):
        sj = score[j:j + 1, :]
        tie = jnp.where(blk > j, 1.0, 0.0)
        rank = rank + jnp.where(sj > score, 1.0, jnp.where(sj == score, tie, 0.0))
    sel_t = jnp.where(rank < float(SLC_TOP_N), 1.0, 0.0)
    sel = jnp.concatenate([sel_t, jnp.zeros((LANE - ns, TQ), F32)], axis=0).T
    sel4 = jnp.concatenate([sel] * NSA_R, axis=0).astype(BF16)

    def attend(carry, kv, bias, mask=None):
        m, l, acc = carry
        lg = lax.dot_general(qs, kv, _NT, preferred_element_type=F32) + bias
        if mask is not None:
            lg = jnp.where(mask, lg, NEG_INF)
        m_new = jnp.maximum(m, jnp.max(lg, axis=-1, keepdims=True))
        alpha = jnp.exp(m - m_new)
        pp = jnp.exp(lg - m_new)
        l = alpha * l + jnp.sum(pp, axis=-1, keepdims=True)
        acc = alpha * acc + jnp.dot(pp.astype(BF16), kv, preferred_element_type=F32)
        return m_new, l, acc

    init = (jnp.full((NSA_R * TQ, 1), NEG_INF, F32), jnp.zeros((NSA_R * TQ, 1), F32),
            jnp.zeros((NSA_R * TQ, LANE), F32))

    def slc_body(j, carry):
        off = pl.multiple_of(j * TQ, TQ)
        kv = kvs_ref[0, pl.ds(off, TQ), :]
        chosen = jnp.dot(sel4, exp_ref[:, pl.ds(off, TQ)], preferred_element_type=F32)
        return attend(carry, kv, bs_ref[0, jnp.minimum(qi - j, 2)], chosen > 0.5)

    _, l_s, acc_s = lax.fori_loop(0, qi + 1, slc_body, init)
    o_slc = acc_s / l_s

    carry = init
    nband = WINDOW // TQ + 1
    for dlt in range(nband - 1, -1, -1):
        j = qi - dlt
        off = pl.multiple_of(jnp.maximum(j, 0) * TQ, TQ)
        kv = kvw_ref[0, pl.ds(off, TQ), :]
        carry = attend(carry, kv, bw_ref[0, jnp.where(j >= 0, dlt, nband)])
    _, l_w, acc_w = carry
    o_win = acc_w / l_w

    gl = jnp.dot(gate_ref[0], gsel_ref[0], preferred_element_type=F32)
    gs = jax.nn.sigmoid(gl)
    outs = []
    for r in range(NSA_R):
        rs = slice(r * TQ, (r + 1) * TQ)
        outs.append(gs[:, r:r + 1] * o_cmp[rs]
                    + gs[:, NSA_R + r:NSA_R + r + 1] * o_slc[rs]
                    + gs[:, 2 * NSA_R + r:2 * NSA_R + r + 1] * o_win[rs])
    for pr in range(NSA_R // 2):
        pair = jnp.where(left, pltpu.roll(outs[2 * pr], NSA_HD, axis=1), outs[2 * pr + 1])
        y_ref[0, :, pr * LANE:(pr + 1) * LANE] = pair.astype(y_ref.dtype)


def _nsa(proj3, kvc, bias_c, bias_s, bias_w, expand, ovl_t, gsel):
    B, S, _ = proj3.shape
    TQ = Q_TILE
    G = NSA_G
    RW = NSA_R * NSA_HD
    kvb = COL_NSA_KV // LANE
    return pl.pallas_call(
        _nsa_kernel,
        grid=(B, G, S // TQ),
        in_specs=[pl.BlockSpec((1, TQ, RW), lambda b, g, i: (b, i, COL_NSA_Q // RW + g)),
                  pl.BlockSpec((1, 1, kvc.shape[2], LANE), lambda b, g, i: (b, g, 0, 0)),
                  pl.BlockSpec((1, S, LANE), lambda b, g, i: (b, 0, kvb + G + g)),
                  pl.BlockSpec((1, S, LANE), lambda b, g, i: (b, 0, kvb + 2 * G + g)),
                  pl.BlockSpec((1, TQ, LANE), lambda b, g, i: (b, i, COL_SMALL // LANE)),
                  pl.BlockSpec((1, NSA_R, TQ, LANE), lambda b, g, i: (g, 0, i, 0)),
                  pl.BlockSpec((1,) + bias_s.shape[1:], lambda b, g, i: (g, 0, 0, 0)),
                  pl.BlockSpec((1,) + bias_w.shape[1:], lambda b, g, i: (g, 0, 0, 0)),
                  pl.BlockSpec(expand.shape, lambda b, g, i: (0, 0)),
                  pl.BlockSpec(ovl_t.shape, lambda b, g, i: (0, 0)),
                  pl.BlockSpec((1, LANE, LANE), lambda b, g, i: (g, 0, 0))],
        out_specs=pl.BlockSpec((1, TQ, RW), lambda b, g, i: (b, i, g)),
        out_shape=jax.ShapeDtypeStruct((B, S, D_MODEL), BF16),
        compiler_params=_cparams(("parallel", "parallel", "arbitrary")),
        name="nsa_attention",
    )(proj3, kvc, proj3, proj3, proj3, bias_c, bias_s, bias_w, expand, ovl_t, gsel)


V_ROWS = NSA_HD + 16
MASK_BIG = 1e30


def _nsa_t_kernel(q_ref, kvc_ref, kvs_ref, kvw_ref, gate_ref, bc_ref, bs_ref, bw_ref, xtra_ref, ovl_ref,
                  gsel_ref, y_ref, a_ref, b_ref, vs_ref, vw_ref, vc_ref, lg_ref):
    TQ = Q_TILE
    HW = NSA_R * TQ
    qi = pl.program_id(2)
    S = kvs_ref.shape[1]

    def value_rows(tile):
        t = tile.astype(F32).T
        return jnp.concatenate([t[NSA_HD:], jnp.ones((V_ROWS - NSA_HD, TQ), F32)], axis=0).astype(BF16)

    @pl.when(qi == 0)
    def _():
        for j in range(S // TQ):
            rows = slice(j * TQ, (j + 1) * TQ)
            ks = kvs_ref[0, rows, :]
            a_ref[rows, 0:LANE] = ks
            a_ref[rows, LANE:2 * LANE] = xtra_ref[rows, :]
            vs_ref[:, rows] = value_rows(ks)
            vw_ref[:, rows] = value_rows(kvw_ref[0, rows, :])
        vc_ref[...] = value_rows(kvc_ref[0, 0])

    qf = q_ref[0].astype(F32) * (NSA_HD ** -0.5)
    zero_rows = jnp.zeros((LANE - NSA_HD, TQ), F32)
    heads = []
    for pr in range(NSA_R // 2):
        t = qf[:, pr * LANE:(pr + 1) * LANE].T
        heads.append(jnp.concatenate([t[:NSA_HD], zero_rows], axis=0))
        heads.append(jnp.concatenate([t[NSA_HD:], zero_rows], axis=0))
    qt32 = jnp.concatenate(heads, axis=1)
    qt = qt32.astype(BF16)

    def attend(carry, lg, vt):
        m, acc = carry
        m_new = jnp.maximum(m, jnp.max(lg, axis=0, keepdims=True))
        pp = jnp.exp(lg - m_new).astype(BF16)
        acc = acc * jnp.exp(m - m_new) + jnp.dot(vt, pp, preferred_element_type=F32)
        return m_new, acc

    init = (jnp.full((1, HW), NEG_INF, F32), jnp.zeros((V_ROWS, HW), F32))

    carry = init
    nband = WINDOW // TQ + 1
    for dlt in range(nband - 1, -1, -1):
        j = qi - dlt
        off = pl.multiple_of(jnp.maximum(j, 0) * TQ, TQ)
        lg = jnp.dot(kvw_ref[0, pl.ds(off, TQ), :], qt, preferred_element_type=F32)
        carry = attend(carry, lg + bw_ref[0, jnp.where(j >= 0, dlt, nband)], vw_ref[:, pl.ds(off, TQ)])
    acc_w = carry[1]
    o_win = acc_w[:NSA_HD] * (1.0 / acc_w[NSA_HD:NSA_HD + 1])

    kvc = kvc_ref[0, 0]
    bias_c = bc_ref[0, pl.ds(pl.multiple_of(LANE - (TQ // CMP_STRIDE) * qi, 8), LANE), :]
    lg = jnp.dot(kvc, qt, preferred_element_type=F32) + bias_c
    e = jnp.exp(lg - jnp.max(lg, axis=0, keepdims=True))
    p = e * (1.0 / jnp.sum(e, axis=0, keepdims=True))
    p = jnp.where(bias_c > 0.5 * NEG_INF, p, 0.0)
    o_cmp = jnp.dot(vc_ref[...], p.astype(BF16), preferred_element_type=F32)[:NSA_HD]

    psum = p[:, 0:TQ]
    for r in range(1, NSA_R):
        psum = psum + p[:, r * TQ:(r + 1) * TQ]
    p_hi = psum.astype(BF16)
    p_lo = (psum - p_hi.astype(F32)).astype(BF16)
    ovl = ovl_ref[...]
    imp = jnp.dot(ovl, p_hi, preferred_element_type=F32) + jnp.dot(ovl, p_lo, preferred_element_type=F32)
    ns = imp.shape[0]
    blk = lax.broadcasted_iota(jnp.int32, (ns, TQ), 0)
    cur = (qi * TQ + lax.broadcasted_iota(jnp.int32, (ns, TQ), 1)) // SLC_BLOCK
    forced = (blk == 0) | (blk == cur) | (blk == cur - 1)
    score = jnp.where(forced, FORCED, jnp.where(blk <= cur, imp, -FORCED))
    rank = jnp.zeros((ns, TQ), F32)
    for j in range(ns):
        sj = score[j:j + 1, :]
        tie = jnp.where(blk > j, 1.0, 0.0)
        rank = rank + jnp.where(sj > score, 1.0, jnp.where(sj == score, tie, 0.0))
    neg_sel = jnp.where(rank < float(SLC_TOP_N), 0.0, -MASK_BIG)
    b_ref[...] = jnp.concatenate([qt32, jnp.concatenate([neg_sel] * NSA_R, axis=1),
                                  jnp.zeros((LANE - ns, HW), F32)], axis=0).astype(BF16)

    def logits(j):
        return jnp.dot(a_ref[pl.ds(pl.multiple_of(j * TQ, TQ), TQ), :], b_ref[...], preferred_element_type=F32)

    def far_tile(j, carry):
        lg_next = logits(j + 1)
        carry = attend(carry, lg_ref[...], vs_ref[:, pl.ds(pl.multiple_of(j * TQ, TQ), TQ)])
        lg_ref[...] = lg_next
        return carry

    n_far = jnp.maximum(qi - 1, 0)
    lg_ref[...] = logits(0)
    carry = lax.fori_loop(0, n_far, far_tile, init)
    off = pl.multiple_of(n_far * TQ, TQ)
    carry = attend(carry, lg_ref[...] + bs_ref[0, jnp.where(qi >= 1, 1, 2)], vs_ref[:, pl.ds(off, TQ)])
    off = pl.multiple_of(qi * TQ, TQ)
    carry = attend(carry, logits(qi) + bs_ref[0, 0], vs_ref[:, pl.ds(off, TQ)])
    acc_s = carry[1]
    o_slc = acc_s[:NSA_HD] * (1.0 / acc_s[NSA_HD:NSA_HD + 1])

    gs = jax.nn.sigmoid(lax.dot_general(gsel_ref[0], gate_ref[0], _NT, preferred_element_type=F32))

    def gate_row(br):
        return jnp.concatenate([gs[br * NSA_R + r:br * NSA_R + r + 1] for r in range(NSA_R)], axis=1)

    o = gate_row(0) * o_cmp + gate_row(1) * o_slc + gate_row(2) * o_win
    o = jnp.concatenate([o[:, r * TQ:(r + 1) * TQ] for r in range(NSA_R)], axis=0)
    y_ref[0] = o.T.astype(y_ref.dtype)


def _nsa_t(proj3, kvc, bias_c, bias_s, bias_w, xtra, ovl_t, gsel):
    B, S, _ = proj3.shape
    TQ = Q_TILE
    G = NSA_G
    RW = NSA_R * NSA_HD
    HW = NSA_R * TQ
    kvb = COL_NSA_KV // LANE
    assert kvc.shape[2] == LANE and S % TQ == 0
    return pl.pallas_call(
        _nsa_t_kernel,
        grid=(B, G, S // TQ),
        in_specs=[pl.BlockSpec((1, TQ, RW), lambda b, g, i: (b, i, COL_NSA_Q // RW + g)),
                  pl.BlockSpec((1, 1, LANE, LANE), lambda b, g, i: (b, g, 0, 0)),
                  pl.BlockSpec((1, S, LANE), lambda b, g, i: (b, 0, kvb + G + g)),
                  pl.BlockSpec((1, S, LANE), lambda b, g, i: (b, 0, kvb + 2 * G + g)),
                  pl.BlockSpec((1, TQ, LANE), lambda b, g, i: (b, i, COL_SMALL // LANE)),
                  pl.BlockSpec((1,) + bias_c.shape[1:], lambda b, g, i: (g, 0, 0)),
                  pl.BlockSpec((1,) + bias_s.shape[1:], lambda b, g, i: (g, 0, 0, 0)),
                  pl.BlockSpec((1,) + bias_w.shape[1:], lambda b, g, i: (g, 0, 0, 0)),
                  pl.BlockSpec(xtra.shape, lambda b, g, i: (0, 0)),
                  pl.BlockSpec(ovl_t.shape, lambda b, g, i: (0, 0)),
                  pl.BlockSpec((1,) + gsel.shape[1:], lambda b, g, i: (g, 0, 0))],
        out_specs=pl.BlockSpec((1, TQ, RW), lambda b, g, i: (b, i, g)),
        out_shape=jax.ShapeDtypeStruct((B, S, D_MODEL), BF16),
        scratch_shapes=[pltpu.VMEM((S, 2 * LANE), BF16), pltpu.VMEM((2 * LANE, HW), BF16),
                        pltpu.VMEM((V_ROWS, S), BF16), pltpu.VMEM((V_ROWS, S), BF16),
                        pltpu.VMEM((V_ROWS, LANE), BF16), pltpu.VMEM((TQ, HW), F32)],
        compiler_params=_cparams(("parallel", "parallel", "arbitrary")),
        name="nsa_attention",
    )(proj3, kvc, proj3, proj3, proj3, bias_c, bias_s, bias_w, xtra, ovl_t, gsel)


def _nsa_tables(rel_table, S):
    TQ = Q_TILE
    tbl = rel_table.astype(F32).reshape(REL_BUCKETS, NSA_G, NSA_R)
    tbl = tbl - tbl[REL_BUCKETS - 1]

    def table(dist, ok):
        onehot = jax.nn.one_hot(_rel_bucket(dist), REL_BUCKETS, dtype=F32)
        b = jnp.einsum("...kqn,ngr->g...krq", onehot, tbl, precision=lax.Precision.HIGHEST)
        b = jnp.where(ok[..., :, None, :], b, NEG_INF)
        return b.reshape(b.shape[:-2] + (NSA_R * dist.shape[-1],))

    kk = jnp.arange(TQ)[:, None]
    qq = jnp.arange(TQ)[None, :]
    nband = WINDOW // TQ + 1
    dist = jnp.arange(nband)[:, None, None] * TQ + qq - kk
    bias_s = table(dist[:2], dist[:2] >= 0)
    bias_w = table(dist, (dist >= 0) & (dist < WINDOW))
    bias_s = jnp.concatenate([bias_s, jnp.full_like(bias_s[:, :1], NEG_INF)], axis=1)
    bias_w = jnp.concatenate([bias_w, jnp.full_like(bias_w[:, :1], NEG_INF)], axis=1)
    c_rel = jnp.arange(2 * LANE)[:, None] - LANE
    dist_c = qq - CMP_STRIDE * c_rel - (CMP_BLOCK - 1)
    bias_c = table(dist_c, dist_c >= 0)

    n_slc = S // SLC_BLOCK
    n_cmp = S // CMP_STRIDE - CMP_BLOCK // CMP_STRIDE + 1
    xtra = (jnp.arange(LANE)[None, :] == (jnp.arange(S)[:, None] // SLC_BLOCK)).astype(BF16)
    cmp_start = jnp.arange(LANE) * CMP_STRIDE
    slc_start = jnp.arange(n_slc) * SLC_BLOCK
    overlap = jnp.clip(jnp.minimum(cmp_start[:, None] + CMP_BLOCK, slc_start[None, :] + SLC_BLOCK)
                       - jnp.maximum(cmp_start[:, None], slc_start[None, :]), 0).astype(F32) / CMP_BLOCK
    overlap = jnp.where(jnp.arange(LANE)[:, None] < n_cmp, overlap, 0.0)
    g = jnp.arange(NSA_G)[:, None, None]
    row = jnp.arange(16)[None, :, None]
    col = jnp.arange(LANE)[None, None, :]
    src = SMALL_GATE + (row // NSA_R) * NSA_HEADS + g * NSA_R + row % NSA_R
    gsel = ((col == src) & (row < 3 * NSA_R)).astype(BF16)
    return bias_c, bias_s, bias_w, xtra, overlap.T.astype(BF16), gsel


def _merge_kernel(ya_ref, yb_ref, yc_ref, g0_ref, g1_ref, g2_ref, x_ref, wb_ref, wo_ref, gn_ref, o_ref):
    m = None
    for br, (y_ref, g_ref) in enumerate(((ya_ref, g0_ref), (yb_ref, g1_ref), (yc_ref, g2_ref))):
        t = jax.nn.sigmoid(g_ref[...].astype(F32)) * jnp.dot(y_ref[...], wb_ref[br], preferred_element_type=F32)
        m = t if m is None else m + t
    z = jnp.dot(m.astype(BF16), wo_ref[...], preferred_element_type=F32)
    o_ref[...] = x_ref[...] + _rms(z, gn_ref[...])


def _merge(ya, yb, yc, proj, x2, wb, wo, gn, tm=512):
    T, D = x2.shape
    row = lambda i: (i, 0)
    mcol = COL_MERGE // D
    return pl.pallas_call(
        _merge_kernel,
        grid=(T // tm,),
        in_specs=[pl.BlockSpec((tm, D), row), pl.BlockSpec((tm, D), row), pl.BlockSpec((tm, D), row),
                  pl.BlockSpec((tm, D), lambda i: (i, mcol)),
                  pl.BlockSpec((tm, D), lambda i: (i, mcol + 1)),
                  pl.BlockSpec((tm, D), lambda i: (i, mcol + 2)),
                  pl.BlockSpec((tm, D), row),
                  _const_spec(wb.shape), _const_spec(wo.shape), _const_spec((1, D))],
        out_specs=pl.BlockSpec((tm, D), row),
        out_shape=jax.ShapeDtypeStruct((T, D), F32),
        compiler_params=_cparams(("parallel",)),
        name="merge_out",
    )(ya, yb, yc, proj, proj, proj, x2, wb, wo, gn)


def _ffn_kernel(x_ref, gpre_ref, win_ref, wout_ref, gpost_ref, o_ref, *, chunk):
    x = x_ref[...]
    h = _rms(x, gpre_ref[...]).astype(BF16)
    acc = None
    for c in range(D_FF // chunk):
        gt = jnp.dot(h, win_ref[:, c * chunk:(c + 1) * chunk], preferred_element_type=F32)
        up = jnp.dot(h, win_ref[:, D_FF + c * chunk:D_FF + (c + 1) * chunk], preferred_element_type=F32)
        a = (gt * jax.nn.sigmoid(gt) * up).astype(BF16)
        t = jnp.dot(a, wout_ref[c * chunk:(c + 1) * chunk, :], preferred_element_type=F32)
        acc = t if acc is None else acc + t
    o_ref[...] = x + _rms(acc, gpost_ref[...])


def _ffn(x2, gpre, win, wout, gpost, tm=512, chunk=256):
    T, D = x2.shape
    row = lambda i: (i, 0)
    return pl.pallas_call(
        functools.partial(_ffn_kernel, chunk=chunk),
        grid=(T // tm,),
        in_specs=[pl.BlockSpec((tm, D), row), _const_spec((1, D)), _const_spec(win.shape),
                  _const_spec(wout.shape), _const_spec((1, D))],
        out_specs=pl.BlockSpec((tm, D), row),
        out_shape=jax.ShapeDtypeStruct((T, D), F32),
        compiler_params=_cparams(("parallel",)),
        name="ffn",
    )(x2, gpre, win, wout, gpost)


def _permute_in_columns(w):
    cuts = []
    acc = 0
    for size in IN_SIZES[:-1]:
        acc += size
        cuts.append(acc)
    (lru_x, lru_g, nsa_q, nsa_kv, nsa_gate, gla_q, gla_k, gla_v, gla_og, gla_lr, merge) = jnp.split(w, cuts, axis=-1)
    lead = w.shape[:-1]
    kv = nsa_kv.reshape(lead + (3, 2, NSA_G, NSA_HD))
    kv = jnp.moveaxis(kv, -3, -2).reshape(lead + (6 * NSA_G * NSA_HD,))
    pad = jnp.zeros(lead + (LANE - nsa_gate.shape[-1] - gla_lr.shape[-1],), w.dtype)
    return jnp.concatenate([lru_x, lru_g, nsa_q, gla_v, gla_og, merge, kv, gla_q, gla_k, nsa_gate, gla_lr, pad],
                           axis=-1)


def _rel_bucket(dist):
    n = jnp.maximum(dist, 0)
    nf = jnp.maximum(n, REL_MAX_EXACT).astype(F32)
    large = REL_MAX_EXACT + (jnp.log(nf / REL_MAX_EXACT) / math.log(REL_MAX_DIST / REL_MAX_EXACT)
                             * (REL_BUCKETS - REL_MAX_EXACT)).astype(jnp.int32)
    large = jnp.minimum(large, REL_BUCKETS - 1)
    return jnp.where(n < REL_MAX_EXACT, n, large)


def _bias_tables(rel_table, S):
    TQ = Q_TILE
    tbl = rel_table.astype(F32).reshape(REL_BUCKETS, NSA_G, NSA_R)

    def lookup(dist, ok):
        b = jnp.moveaxis(tbl[_rel_bucket(dist)], (-2, -1), (0, 1))
        b = jnp.where(ok, b, NEG_INF)
        b = jnp.moveaxis(b, 1, -3)
        return b.reshape(b.shape[:-3] + (NSA_R * dist.shape[-2], dist.shape[-1]))

    i = jnp.arange(TQ)[:, None]
    j = jnp.arange(TQ)[None, :]
    nband = WINDOW // TQ + 1
    dist = jnp.arange(nband)[:, None, None] * TQ + i - j
    bias_s = lookup(dist, dist >= 0)
    bias_w = lookup(dist, (dist >= 0) & (dist < WINDOW))
    bias_w = jnp.concatenate([bias_w, jnp.full_like(bias_w[:, :1], NEG_INF)], axis=1)
    n_cmp = S // CMP_STRIDE - CMP_BLOCK // CMP_STRIDE + 1
    cmp_end = jnp.arange(LANE) * CMP_STRIDE + CMP_BLOCK - 1
    dist_c = jnp.arange(S)[:, None] - cmp_end[None, :]
    ok_c = (dist_c >= 0) & (jnp.arange(LANE)[None, :] < n_cmp)
    tb = jnp.moveaxis(tbl[_rel_bucket(dist_c)], (-2, -1), (0, 1))
    bias_c = jnp.where(ok_c, tb, NEG_INF)
    return bias_c, bias_s, bias_w


def _selection_constants(S):
    n_slc = S // SLC_BLOCK
    n_cmp = S // CMP_STRIDE - CMP_BLOCK // CMP_STRIDE + 1
    expand = (jnp.arange(LANE)[:, None] == (jnp.arange(S)[None, :] // SLC_BLOCK)).astype(BF16)
    cmp_start = jnp.arange(LANE) * CMP_STRIDE
    slc_start = jnp.arange(n_slc) * SLC_BLOCK
    overlap = jnp.clip(jnp.minimum(cmp_start[:, None] + CMP_BLOCK, slc_start[None, :] + SLC_BLOCK)
                       - jnp.maximum(cmp_start[:, None], slc_start[None, :]), 0).astype(F32) / CMP_BLOCK
    overlap = jnp.where(jnp.arange(LANE)[:, None] < n_cmp, overlap, 0.0)
    g = jnp.arange(NSA_G)[:, None, None]
    row = jnp.arange(LANE)[None, :, None]
    col = jnp.arange(LANE)[None, None, :]
    src = SMALL_GATE + (col // NSA_R) * NSA_HEADS + g * NSA_R + col % NSA_R
    gsel = ((row == src) & (col < 3 * NSA_R)).astype(BF16)
    return expand, overlap.T.astype(BF16), gsel


def _layer(x2, B, S, tables, norm_g, w_in, conv_w, conv_b, lru_w_gates, lru_b_gates, lru_lambda,
           cmp_pos, cmp_w1, cmp_w2, gla_wa2, gla_ba, gla_norm, w_branch, w_out, w_ffn_in, w_ffn_out):
    bias_c, bias_s, bias_w, expand, ovl_t, gsel = tables
    D = D_MODEL
    w_perm = _permute_in_columns(w_in).astype(BF16)
    proj = _in_proj(x2, norm_g[0][None, :], w_perm)
    proj3 = proj.reshape(B, S, PROJ_WIDTH)

    wg = jnp.concatenate([lru_w_gates[0], lru_w_gates[1]], axis=-1).astype(BF16)
    y_a = _lru(proj3, conv_w, conv_b[None, :], wg, lru_b_gates, lru_lambda[None, :])

    nrow = S // CMP_STRIDE
    xc = proj3[:, :, COL_NSA_KV:COL_NSA_KV + 2 * NSA_G * NSA_HD]
    xc = xc.reshape(B, nrow, CMP_STRIDE, NSA_G, 2, NSA_HD).transpose(0, 3, 4, 1, 2, 5)
    xc = xc.reshape(B, NSA_G, 2, nrow, CMP_STRIDE * NSA_HD)
    half = CMP_STRIDE * NSA_HD
    pos = cmp_pos.reshape(2, 2, 1, half)
    w1 = cmp_w1.reshape(2, 2, half, CMP_HIDDEN).astype(BF16)
    zeros = jnp.zeros_like(cmp_w2[0])
    w2 = jnp.stack([jnp.concatenate([cmp_w2[0], zeros], axis=-1),
                    jnp.concatenate([zeros, cmp_w2[1]], axis=-1)]).astype(BF16)
    kvc = _compress(xc, pos, w1, w2)
    y_b = _nsa_t(proj3, kvc, bias_c, bias_s, bias_w, expand, ovl_t, gsel)

    wa_pad = jnp.zeros((LANE, GLA_HEADS * GLA_DK), F32).at[SMALL_LR:SMALL_LR + GLA_RANK].set(gla_wa2).astype(BF16)
    y_c = _gla(proj3, wa_pad, gla_ba[None, :], gla_norm[None, :])

    x2 = _merge(y_a.reshape(B * S, D), y_b.reshape(B * S, D), y_c.reshape(B * S, D), proj, x2,
                w_branch.astype(BF16), w_out.astype(BF16), norm_g[1][None, :])
    x2 = _ffn(x2, norm_g[2][None, :], w_ffn_in.astype(BF16), w_ffn_out.astype(BF16), norm_g[3][None, :])
    return x2


def kernel(x, rel_table, norm_g, w_in, conv_w, conv_b, lru_w_gates, lru_b_gates, lru_lambda, cmp_pos, cmp_w1,
           cmp_w2, gla_wa2, gla_ba, gla_norm, w_branch, w_out, w_ffn_in, w_ffn_out):
    B, S, D = x.shape
    tables = _nsa_tables(rel_table, S)
    x2 = x.reshape(B * S, D)
    for l in range(norm_g.shape[0]):
        x2 = _layer(x2, B, S, tables, norm_g[l], w_in[l], conv_w[l], conv_b[l], lru_w_gates[l], lru_b_gates[l],
                    lru_lambda[l], cmp_pos[l], cmp_w1[l], cmp_w2[l], gla_wa2[l], gla_ba[l], gla_norm[l],
                    w_branch[l], w_out[l], w_ffn_in[l], w_ffn_out[l])
    return x2.reshape(B, S, D)
```

```python
import functools
import math

import jax
import jax.numpy as jnp
from jax import lax
from jax.experimental import pallas as pl
from jax.experimental.pallas import tpu as pltpu

F32 = jnp.float32
BF16 = jnp.bfloat16

D_MODEL = 1024
N_BRANCH = 3
EPS = 1e-6
NEG_INF = -1e30
FORCED = 1e4

LRU_BLOCKS = 8
LRU_BLOCK = D_MODEL // LRU_BLOCKS
CONV_WIDTH = 4
LRU_C = 8.0

NSA_HEADS = 16
NSA_G = 4
NSA_R = NSA_HEADS // NSA_G
NSA_HD = D_MODEL // NSA_HEADS
CMP_BLOCK = 32
CMP_STRIDE = 16
CMP_HIDDEN = 256
SLC_BLOCK = 64
SLC_TOP_N = 8
WINDOW = 256

GLA_HEADS = 4
GLA_DK = (D_MODEL // 2) // GLA_HEADS
GLA_DV = D_MODEL // GLA_HEADS
GLA_RANK = 16
GLA_TAU = 16.0
GLA_CHUNK = 32

REL_BUCKETS = 32
REL_MAX_EXACT = 16
REL_MAX_DIST = 128

D_FF = -(-8 * D_MODEL // (3 * 256)) * 256

IN_SIZES = (D_MODEL, D_MODEL, D_MODEL, 6 * NSA_G * NSA_HD, 3 * NSA_HEADS, D_MODEL // 2, D_MODEL // 2,
            D_MODEL, D_MODEL, GLA_RANK, N_BRANCH * D_MODEL)

LANE = 128
COL_LRU_X = 0
COL_LRU_G = 1024
COL_NSA_Q = 2048
COL_GLA_V = 3072
COL_GLA_OG = 4096
COL_MERGE = 5120
COL_NSA_KV = 8192
COL_GLA_Q = 9728
COL_GLA_K = 10240
COL_SMALL = 10752
PROJ_WIDTH = 10880
SMALL_GATE = 0
SMALL_LR = 3 * NSA_HEADS

Q_TILE = 256
VMEM_LIMIT = 56 * 1024 * 1024


def _cparams(sem):
    return pltpu.CompilerParams(dimension_semantics=sem, vmem_limit_bytes=VMEM_LIMIT)


def _const_spec(shape):
    nd = len(shape)
    return pl.BlockSpec(shape, lambda *_: (0,) * nd, pipeline_mode=pl.Buffered(1))


def _rms(x, g):
    return x * lax.rsqrt(jnp.mean(x * x, axis=-1, keepdims=True) + EPS) * g


def _gelu_tanh(x):
    return 0.5 * x * (1.0 + jnp.tanh(math.sqrt(2.0 / math.pi) * (x + 0.044715 * (x * x * x))))


def _softplus(z):
    return jnp.maximum(z, 0.0) + jnp.log1p(jnp.exp(-jnp.abs(z)))


def _in_proj_kernel(x_ref, g_ref, w_ref, o_ref, h_ref):
    @pl.when(pl.program_id(1) == 0)
    def _():
        h_ref[...] = _rms(x_ref[...], g_ref[...]).astype(BF16)

    o_ref[...] = jnp.dot(h_ref[...], w_ref[...], preferred_element_type=F32).astype(o_ref.dtype)


def _in_proj(x2, g, w, tm=1024, tn=640):
    T, D = x2.shape
    N = w.shape[1]
    return pl.pallas_call(
        _in_proj_kernel,
        grid=(T // tm, N // tn),
        in_specs=[pl.BlockSpec((tm, D), lambda i, j: (i, 0)),
                  pl.BlockSpec((1, D), lambda i, j: (0, 0)),
                  pl.BlockSpec((D, tn), lambda i, j: (0, j))],
        out_specs=pl.BlockSpec((tm, tn), lambda i, j: (i, j)),
        out_shape=jax.ShapeDtypeStruct((T, N), BF16),
        scratch_shapes=[pltpu.VMEM((tm, D), BF16)],
        compiler_params=_cparams(("parallel", "arbitrary")),
        name="in_proj",
    )(x2, g, w)


def _lru_kernel(xa_ref, ga_ref, cw_ref, cb_ref, wg_ref, bg_ref, lam_ref, y_ref, xe_ref, h_ref, *, R):
    @pl.when(pl.program_id(1) == 0)
    def _():
        xe_ref[0:8, :] = jnp.zeros((8, D_MODEL), F32)
        h_ref[...] = jnp.zeros_like(h_ref)

    xe_ref[8:8 + R, :] = xa_ref[0].astype(F32)
    xc = cb_ref[...]
    for j in range(CONV_WIDTH):
        off = 8 - (CONV_WIDTH - 1) + j
        xc = xc + cw_ref[j:j + 1, :] * xe_ref[off:off + R, :]
    xe_ref[0:8, :] = xe_ref[R:R + 8, :]

    sp = _softplus(-lam_ref[...])
    row8 = lax.broadcasted_iota(jnp.int32, (R, LRU_BLOCK), 0) & 7
    for n in range(LRU_BLOCKS):
        sl = slice(n * LRU_BLOCK, (n + 1) * LRU_BLOCK)
        xcn = xc[:, sl]
        gz = jnp.dot(xcn.astype(BF16), wg_ref[n], preferred_element_type=F32)
        r = jax.nn.sigmoid(gz[:, :LRU_BLOCK] + bg_ref[0:1, sl])
        i = jax.nn.sigmoid(gz[:, LRU_BLOCK:] + bg_ref[1:2, sl])
        log_a = (-LRU_C) * r * sp[:, sl]
        a = jnp.exp(log_a)
        u = jnp.sqrt(1.0 - jnp.exp(2.0 * log_a)) * (i * xcn)
        for d in (1, 2, 4):
            keep = row8 >= d
            a_s = pltpu.roll(a, d, axis=0)
            u_s = pltpu.roll(u, d, axis=0)
            u = jnp.where(keep, a * u_s + u, u)
            a = jnp.where(keep, a * a_s, a)
        hprev = h_ref[:, sl]
        hs = []
        for t in range(R // 8):
            ht = a[t * 8:(t + 1) * 8] * hprev + u[t * 8:(t + 1) * 8]
            hprev = ht[7:8]
            hs.append(ht)
        h_ref[:, sl] = hprev
        h = jnp.concatenate(hs, axis=0)
        y_ref[0, :, sl] = (h * _gelu_tanh(ga_ref[0, :, sl].astype(F32))).astype(y_ref.dtype)


def _lru(proj3, conv_w, conv_b, wg, bg, lam, R=256):
    B, S, _ = proj3.shape
    D = D_MODEL
    return pl.pallas_call(
        functools.partial(_lru_kernel, R=R),
        grid=(B, S // R),
        in_specs=[pl.BlockSpec((1, R, D), lambda b, s: (b, s, COL_LRU_X // D)),
                  pl.BlockSpec((1, R, D), lambda b, s: (b, s, COL_LRU_G // D)),
                  _const_spec((CONV_WIDTH, D)),
                  _const_spec((1, D)),
                  _const_spec((LRU_BLOCKS, LRU_BLOCK, 2 * LRU_BLOCK)),
                  _const_spec((2, D)),
                  _const_spec((1, D))],
        out_specs=pl.BlockSpec((1, R, D), lambda b, s: (b, s, 0)),
        out_shape=jax.ShapeDtypeStruct((B, S, D), BF16),
        scratch_shapes=[pltpu.VMEM((R + 8, D), F32), pltpu.VMEM((1, D), F32)],
        compiler_params=_cparams(("parallel", "arbitrary")),
        name="rglru",
    )(proj3, proj3, conv_w, conv_b, wg, bg, lam)


def _gla_kernel(q_ref, k_ref, v_ref, og_ref, sm_ref, wa_ref, ba_ref, gn_ref, y_ref, st_ref, *, Tb):
    nc = Tb // GLA_CHUNK

    @pl.when(pl.program_id(1) == 0)
    def _():
        st_ref[...] = jnp.zeros_like(st_ref)

    la_pre = jnp.dot(sm_ref[0], wa_ref[...], preferred_element_type=F32)
    rowc = lax.broadcasted_iota(jnp.int32, (Tb, GLA_DK), 0) & (GLA_CHUNK - 1)
    ri = lax.broadcasted_iota(jnp.int32, (Tb, Tb), 0)
    ci = lax.broadcasted_iota(jnp.int32, (Tb, Tb), 1)
    same_chunk_causal = ((ri // GLA_CHUNK) == (ci // GLA_CHUNK)) & (ci <= ri)
    lane_chunk = lax.broadcasted_iota(jnp.int32, (GLA_DK, Tb), 1) // GLA_CHUNK

    for h in range(GLA_HEADS):
        ks = slice(h * GLA_DK, (h + 1) * GLA_DK)
        vs = slice(h * GLA_DV, (h + 1) * GLA_DV)
        z = la_pre[:, ks] + ba_ref[:, ks]
        b = (jnp.minimum(z, 0.0) - jnp.log1p(jnp.exp(-jnp.abs(z)))) * (1.0 / GLA_TAU)
        d = 1
        while d < GLA_CHUNK:
            b = b + jnp.where(rowc >= d, pltpu.roll(b, d, axis=0), 0.0)
            d *= 2
        b_last = jnp.concatenate(
            [jnp.broadcast_to(b[c * GLA_CHUNK + GLA_CHUNK - 1:(c + 1) * GLA_CHUNK, :], (GLA_CHUNK, GLA_DK))
             for c in range(nc)], axis=0)
        q = q_ref[0, :, ks].astype(F32) * (GLA_DK ** -0.5)
        k = k_ref[0, :, ks].astype(F32)
        qt = (q * jnp.exp(b)).astype(BF16)
        kt = (k * jnp.exp(-b)).astype(BF16)
        ke_t = (k * jnp.exp(b_last - b)).T
        b_t = b.T
        v = v_ref[0, :, vs]

        att = lax.dot_general(qt, kt, (((1,), (1,)), ((), ())), preferred_element_type=F32)
        att = jnp.where(same_chunk_causal, att, 0.0).astype(BF16)
        o = jnp.dot(att, v, preferred_element_type=F32)

        state = st_ref[h]
        inter = []
        for c in range(nc):
            rs = slice(c * GLA_CHUNK, (c + 1) * GLA_CHUNK)
            inter.append(jnp.dot(qt[rs], state.astype(BF16), preferred_element_type=F32))
            ke_c = jnp.where(lane_chunk == c, ke_t, 0.0).astype(BF16)
            kv = jnp.dot(ke_c, v, preferred_element_type=F32)
            last = c * GLA_CHUNK + GLA_CHUNK - 1
            state = state * jnp.exp(b_t[:, last:last + 1]) + kv
        st_ref[h] = state
        o = o + jnp.concatenate(inter, axis=0)

        on = o * lax.rsqrt(jnp.mean(o * o, axis=-1, keepdims=True) + EPS) * gn_ref[...]
        og = og_ref[0, :, vs].astype(F32)
        y_ref[0, :, vs] = (on * (og * jax.nn.sigmoid(og))).astype(y_ref.dtype)


def _gla(proj3, wa_pad, ba, gn, Tb=256):
    B, S, _ = proj3.shape
    HK = GLA_HEADS * GLA_DK
    D = D_MODEL
    return pl.pallas_call(
        functools.partial(_gla_kernel, Tb=Tb),
        grid=(B, S // Tb),
        in_specs=[pl.BlockSpec((1, Tb, HK), lambda b, s: (b, s, COL_GLA_Q // HK)),
                  pl.BlockSpec((1, Tb, HK), lambda b, s: (b, s, COL_GLA_K // HK)),
                  pl.BlockSpec((1, Tb, D), lambda b, s: (b, s, COL_GLA_V // D)),
                  pl.BlockSpec((1, Tb, D), lambda b, s: (b, s, COL_GLA_OG // D)),
                  pl.BlockSpec((1, Tb, LANE), lambda b, s: (b, s, COL_SMALL // LANE)),
                  _const_spec((LANE, HK)),
                  _const_spec((1, HK)),
                  _const_spec((1, GLA_DV))],
        out_specs=pl.BlockSpec((1, Tb, D), lambda b, s: (b, s, 0)),
        out_shape=jax.ShapeDtypeStruct((B, S, D), BF16),
        scratch_shapes=[pltpu.VMEM((GLA_HEADS, GLA_DK, GLA_DV), F32)],
        compiler_params=_cparams(("parallel", "arbitrary")),
        name="gla",
    )(proj3, proj3, proj3, proj3, proj3, wa_pad, ba, gn)


def _cmp_kernel(kv_ref, pos_ref, w1_ref, w2_ref, o_ref, x_ref):
    x_ref[...] = kv_ref[0].astype(F32)
    nrow = x_ref.shape[0] // CMP_STRIDE
    first = None
    second = None
    for i in range(CMP_STRIDE):
        xi = x_ref[pl.ds(i, nrow, stride=CMP_STRIDE), :]
        f = jnp.dot((xi + pos_ref[0, i]).astype(BF16), w1_ref[0, i], preferred_element_type=F32)
        s = jnp.dot((xi + pos_ref[1, i]).astype(BF16), w1_ref[1, i], preferred_element_type=F32)
        first = f if first is None else first + f
        second = s if second is None else second + s
    pre = first + pltpu.roll(second, nrow - 1, axis=0)
    hid = _gelu_tanh(pre).astype(BF16)
    o_ref[0, 0] = jnp.dot(hid, w2_ref[...], preferred_element_type=F32).astype(o_ref.dtype)


def _compress(proj3, pos, w1, w2):
    B, S, _ = proj3.shape
    nrow = S // CMP_STRIDE
    kvb = COL_NSA_KV // LANE
    return pl.pallas_call(
        _cmp_kernel,
        grid=(B, NSA_G),
        in_specs=[pl.BlockSpec((1, S, LANE), lambda b, g: (b, 0, kvb + g)),
                  _const_spec(pos.shape), _const_spec(w1.shape), _const_spec(w2.shape)],
        out_specs=pl.BlockSpec((1, 1, nrow, LANE), lambda b, g: (b, g, 0, 0)),
        out_shape=jax.ShapeDtypeStruct((B, NSA_G, nrow, LANE), BF16),
        scratch_shapes=[pltpu.VMEM((S, LANE), F32)],
        compiler_params=_cparams(("parallel", "parallel")),
        name="nsa_compress",
    )(proj3, pos, w1, w2)


def _compress_params(cmp_pos, cmp_w1, cmp_w2):
    hd = NSA_HD
    w1 = cmp_w1.reshape(2, 2, CMP_STRIDE, hd, CMP_HIDDEN)
    z = jnp.zeros_like(w1[0])
    w1 = jnp.concatenate([jnp.concatenate([w1[0], z], axis=-1),
                          jnp.concatenate([z, w1[1]], axis=-1)], axis=-2)
    pos = cmp_pos.reshape(2, 2, CMP_STRIDE, 1, hd)
    pos = jnp.concatenate([pos[0], pos[1]], axis=-1)
    z2 = jnp.zeros_like(cmp_w2[0])
    w2 = jnp.concatenate([jnp.concatenate([cmp_w2[0], z2], axis=-1),
                          jnp.concatenate([z2, cmp_w2[1]], axis=-1)], axis=0)
    return pos, w1.astype(BF16), w2.astype(BF16)


_NT = (((1,), (1,)), ((), ()))
V_ROWS = NSA_HD + 16
MASK_BIG = 1e30


def _nsa_kernel(q_ref, kvc_ref, kvs_ref, kvw_ref, gate_ref, bc_ref, bs_ref, bw_ref, xtra_ref, ovl_ref,
                gsel_ref, y_ref, a_ref, b_ref, vs_ref, vw_ref, vc_ref, lg_ref, p_ref):
    TQ = Q_TILE
    HW = NSA_R * TQ
    qi = pl.program_id(2)
    S = kvs_ref.shape[1]
    n_cmp_rows = kvc_ref.shape[2]

    def value_rows(tile):
        t = tile.astype(F32).T
        return jnp.concatenate([t[NSA_HD:], jnp.ones((V_ROWS - NSA_HD, t.shape[1]), F32)], axis=0).astype(BF16)

    @pl.when(qi == 0)
    def _():
        for j in range(S // LANE):
            rows = slice(j * LANE, (j + 1) * LANE)
            ks = kvs_ref[0, rows, :]
            a_ref[rows, 0:LANE] = ks
            a_ref[rows, LANE:2 * LANE] = xtra_ref[rows, :]
            vs_ref[:, rows] = value_rows(ks)
            vw_ref[:, rows] = value_rows(kvw_ref[0, rows, :])
        vc_ref[...] = value_rows(kvc_ref[0, 0])

    def tile_rows(j, n=1):
        return pl.ds(pl.multiple_of(j * TQ, TQ), n * TQ)

    qf = q_ref[0].astype(F32) * (NSA_HD ** -0.5)
    zero_rows = jnp.zeros((LANE - NSA_HD, TQ), F32)
    heads = []
    for pr in range(NSA_R // 2):
        t = qf[:, pr * LANE:(pr + 1) * LANE].T
        heads.append(jnp.concatenate([t[:NSA_HD], zero_rows], axis=0))
        heads.append(jnp.concatenate([t[NSA_HD:], zero_rows], axis=0))
    qt32 = jnp.concatenate(heads, axis=1)
    qt = qt32.astype(BF16)

    def normalised(acc):
        return acc[:NSA_HD] * (1.0 / acc[NSA_HD:NSA_HD + 1])

    first = qi == 0
    rows_w = tile_rows(jnp.maximum(qi - 1, 0), 2)
    lg = jnp.dot(kvw_ref[0, rows_w, :], qt, preferred_element_type=F32)
    lg = lg + jnp.concatenate([bw_ref[0, jnp.where(first, 0, 1)], bw_ref[0, jnp.where(first, 2, 0)]], axis=0)
    pw = jnp.exp(lg - jnp.max(lg, axis=0, keepdims=True)).astype(BF16)
    o_win = normalised(jnp.dot(vw_ref[:, rows_w], pw, preferred_element_type=F32))

    kvc = kvc_ref[0, 0]
    cq = TQ // CMP_STRIDE
    start_c = pl.multiple_of(bc_ref.shape[1] - n_cmp_rows - cq * qi, 8)
    bias_c = bc_ref[0, pl.ds(start_c, n_cmp_rows), :]
    lg = jnp.dot(kvc, qt, preferred_element_type=F32) + bias_c
    e = jnp.exp(lg - jnp.max(lg, axis=0, keepdims=True))
    p = e * (1.0 / jnp.sum(e, axis=0, keepdims=True))
    p = jnp.where(bias_c > 0.5 * NEG_INF, p, 0.0)
    o_cmp = jnp.dot(vc_ref[...], p.astype(BF16), preferred_element_type=F32)[:NSA_HD]

    psum = p[:, 0:TQ]
    for r in range(1, NSA_R):
        psum = psum + p[:, r * TQ:(r + 1) * TQ]
    p_hi = psum.astype(BF16)
    p_lo = (psum - p_hi.astype(F32)).astype(BF16)
    ovl = ovl_ref[...]
    imp = jnp.dot(ovl, p_hi, preferred_element_type=F32) + jnp.dot(ovl, p_lo, preferred_element_type=F32)
    ns = imp.shape[0]
    blk = lax.broadcasted_iota(jnp.int32, (ns, TQ), 0)
    cur = (qi * TQ + lax.broadcasted_iota(jnp.int32, (ns, TQ), 1)) // SLC_BLOCK
    forced = (blk == 0) | (blk == cur) | (blk == cur - 1)
    score = jnp.where(forced, FORCED, jnp.where(blk <= cur, imp, -FORCED))
    rank = jnp.zeros((ns, TQ), F32)
    for j in range(ns):
        sj = score[j:j + 1, :]
        tie = jnp.where(blk > j, 1.0, 0.0)
        rank = rank + jnp.where(sj > score, 1.0, jnp.where(sj == score, tie, 0.0))
    neg_sel = jnp.where(rank < float(SLC_TOP_N), 0.0, -MASK_BIG)
    b_ref[...] = jnp.concatenate([qt32, jnp.concatenate([neg_sel] * NSA_R, axis=1),
                                  jnp.zeros((LANE - ns, HW), F32)], axis=0).astype(BF16)

    def logits(j):
        return jnp.dot(a_ref[tile_rows(j), :], b_ref[...], preferred_element_type=F32)

    lg_ref[...] = logits(0)
    p_ref[...] = jnp.zeros(p_ref.shape, BF16)

    def trip(j, carry):
        m, acc, alpha = carry
        lg_next = logits(jnp.minimum(j + 1, qi))
        acc = acc * alpha + jnp.dot(vs_ref[:, tile_rows(jnp.maximum(j - 1, 0))], p_ref[...],
                                    preferred_element_type=F32)
        lg = lg_ref[...] + bs_ref[0, jnp.minimum(qi - j, 2)]
        m_new = jnp.maximum(m, jnp.max(lg, axis=0, keepdims=True))
        p_ref[...] = jnp.exp(lg - m_new).astype(BF16)
        lg_ref[...] = lg_next
        return m_new, acc, jnp.exp(m - m_new)

    init = (jnp.full((1, HW), NEG_INF, F32), jnp.zeros((V_ROWS, HW), F32), jnp.ones((1, HW), F32))
    _, acc, alpha = lax.fori_loop(0, qi + 1, trip, init)
    acc = acc * alpha + jnp.dot(vs_ref[:, tile_rows(qi)], p_ref[...], preferred_element_type=F32)
    o_slc = normalised(acc)

    gs = jax.nn.sigmoid(lax.dot_general(gsel_ref[0], gate_ref[0], _NT, preferred_element_type=F32))

    def gate_row(br):
        return jnp.concatenate([gs[br * NSA_R + r:br * NSA_R + r + 1] for r in range(NSA_R)], axis=1)

    o = gate_row(0) * o_cmp + gate_row(1) * o_slc + gate_row(2) * o_win
    o = jnp.concatenate([o[:, r * TQ:(r + 1) * TQ] for r in range(NSA_R)], axis=0)
    y_ref[0] = o.T.astype(y_ref.dtype)


def _nsa(proj3, kvc, bias_c, bias_s, bias_w, xtra, ovl_t, gsel):
    B, S, _ = proj3.shape
    TQ = Q_TILE
    G = NSA_G
    RW = NSA_R * NSA_HD
    HW = NSA_R * TQ
    kvb = COL_NSA_KV // LANE
    ncr = kvc.shape[2]
    assert S % TQ == 0 and S >= 2 * TQ and WINDOW == TQ and ncr % LANE == 0
    return pl.pallas_call(
        _nsa_kernel,
        grid=(B, G, S // TQ),
        in_specs=[pl.BlockSpec((1, TQ, RW), lambda b, g, i: (b, i, COL_NSA_Q // RW + g)),
                  pl.BlockSpec((1, 1, ncr, LANE), lambda b, g, i: (b, g, 0, 0)),
                  pl.BlockSpec((1, S, LANE), lambda b, g, i: (b, 0, kvb + G + g)),
                  pl.BlockSpec((1, S, LANE), lambda b, g, i: (b, 0, kvb + 2 * G + g)),
                  pl.BlockSpec((1, TQ, LANE), lambda b, g, i: (b, i, COL_SMALL // LANE)),
                  pl.BlockSpec((1,) + bias_c.shape[1:], lambda b, g, i: (g, 0, 0)),
                  pl.BlockSpec((1,) + bias_s.shape[1:], lambda b, g, i: (g, 0, 0, 0)),
                  pl.BlockSpec((1,) + bias_w.shape[1:], lambda b, g, i: (g, 0, 0, 0)),
                  pl.BlockSpec(xtra.shape, lambda b, g, i: (0, 0)),
                  pl.BlockSpec(ovl_t.shape, lambda b, g, i: (0, 0)),
                  pl.BlockSpec((1,) + gsel.shape[1:], lambda b, g, i: (g, 0, 0))],
        out_specs=pl.BlockSpec((1, TQ, RW), lambda b, g, i: (b, i, g)),
        out_shape=jax.ShapeDtypeStruct((B, S, D_MODEL), BF16),
        scratch_shapes=[pltpu.VMEM((S, 2 * LANE), BF16), pltpu.VMEM((2 * LANE, HW), BF16),
                        pltpu.VMEM((V_ROWS, S), BF16), pltpu.VMEM((V_ROWS, S), BF16),
                        pltpu.VMEM((V_ROWS, ncr), BF16), pltpu.VMEM((TQ, HW), F32), pltpu.VMEM((TQ, HW), BF16)],
        compiler_params=_cparams(("parallel", "parallel", "arbitrary")),
        name="nsa_attention",
    )(proj3, kvc, proj3, proj3, proj3, bias_c, bias_s, bias_w, xtra, ovl_t, gsel)


def _rel_bucket(dist):
    n = jnp.maximum(dist, 0)
    nf = jnp.maximum(n, REL_MAX_EXACT).astype(F32)
    large = REL_MAX_EXACT + (jnp.log(nf / REL_MAX_EXACT) / math.log(REL_MAX_DIST / REL_MAX_EXACT)
                             * (REL_BUCKETS - REL_MAX_EXACT)).astype(jnp.int32)
    large = jnp.minimum(large, REL_BUCKETS - 1)
    return jnp.where(n < REL_MAX_EXACT, n, large)


def _nsa_tables(rel_table, S):
    TQ = Q_TILE
    tbl = rel_table.astype(F32).reshape(REL_BUCKETS, NSA_G, NSA_R)
    tbl = tbl - tbl[REL_BUCKETS - 1]

    def table(dist, ok):
        onehot = jax.nn.one_hot(_rel_bucket(dist), REL_BUCKETS, dtype=F32)
        b = jnp.einsum("...kqn,ngr->g...krq", onehot, tbl, precision=lax.Precision.HIGHEST)
        b = jnp.where(ok[..., :, None, :], b, NEG_INF)
        return b.reshape(b.shape[:-2] + (NSA_R * dist.shape[-1],))

    kk = jnp.arange(TQ)[:, None]
    qq = jnp.arange(TQ)[None, :]
    dist = jnp.arange(2)[:, None, None] * TQ + qq - kk
    bias_s = table(dist, dist >= 0)
    bias_s = jnp.concatenate([bias_s, jnp.zeros_like(bias_s[:, :1])], axis=1)
    bias_w = table(dist, (dist >= 0) & (dist < WINDOW))
    bias_w = jnp.concatenate([bias_w, jnp.full_like(bias_w[:, :1], NEG_INF)], axis=1)
    n_rows = S // CMP_STRIDE
    off = n_rows - TQ // CMP_STRIDE
    c_rel = jnp.arange(off + n_rows)[:, None] - off
    dist_c = qq - CMP_STRIDE * c_rel - (CMP_BLOCK - 1)
    bias_c = table(dist_c, dist_c >= 0)

    n_slc = S // SLC_BLOCK
    n_cmp = n_rows - CMP_BLOCK // CMP_STRIDE + 1
    xtra = (jnp.arange(LANE)[None, :] == (jnp.arange(S)[:, None] // SLC_BLOCK)).astype(BF16)
    cmp_start = jnp.arange(n_rows) * CMP_STRIDE
    slc_start = jnp.arange(n_slc) * SLC_BLOCK
    overlap = jnp.clip(jnp.minimum(cmp_start[:, None] + CMP_BLOCK, slc_start[None, :] + SLC_BLOCK)
                       - jnp.maximum(cmp_start[:, None], slc_start[None, :]), 0).astype(F32) / CMP_BLOCK
    overlap = jnp.where(jnp.arange(n_rows)[:, None] < n_cmp, overlap, 0.0)
    g = jnp.arange(NSA_G)[:, None, None]
    row = jnp.arange(16)[None, :, None]
    col = jnp.arange(LANE)[None, None, :]
    src = SMALL_GATE + (row // NSA_R) * NSA_HEADS + g * NSA_R + row % NSA_R
    gsel = ((col == src) & (row < 3 * NSA_R)).astype(BF16)
    return bias_c, bias_s, bias_w, xtra, overlap.T.astype(BF16), gsel


def _merge_kernel(ya_ref, yb_ref, yc_ref, g0_ref, g1_ref, g2_ref, x_ref, wb_ref, wo_ref, gn_ref, o_ref):
    m = None
    for br, (y_ref, g_ref) in enumerate(((ya_ref, g0_ref), (yb_ref, g1_ref), (yc_ref, g2_ref))):
        t = jax.nn.sigmoid(g_ref[...].astype(F32)) * jnp.dot(y_ref[...], wb_ref[br], preferred_element_type=F32)
        m = t if m is None else m + t
    z = jnp.dot(m.astype(BF16), wo_ref[...], preferred_element_type=F32)
    o_ref[...] = x_ref[...] + _rms(z, gn_ref[...])


def _merge(ya, yb, yc, proj, x2, wb, wo, gn, tm=512):
    T, D = x2.shape
    row = lambda i: (i, 0)
    mcol = COL_MERGE // D
    return pl.pallas_call(
        _merge_kernel,
        grid=(T // tm,),
        in_specs=[pl.BlockSpec((tm, D), row), pl.BlockSpec((tm, D), row), pl.BlockSpec((tm, D), row),
                  pl.BlockSpec((tm, D), lambda i: (i, mcol)),
                  pl.BlockSpec((tm, D), lambda i: (i, mcol + 1)),
                  pl.BlockSpec((tm, D), lambda i: (i, mcol + 2)),
                  pl.BlockSpec((tm, D), row),
                  _const_spec(wb.shape), _const_spec(wo.shape), _const_spec((1, D))],
        out_specs=pl.BlockSpec((tm, D), row),
        out_shape=jax.ShapeDtypeStruct((T, D), F32),
        compiler_params=_cparams(("parallel",)),
        name="merge_out",
    )(ya, yb, yc, proj, proj, proj, x2, wb, wo, gn)


def _ffn_kernel(x_ref, gpre_ref, win_ref, wout_ref, gpost_ref, o_ref, *, chunk):
    x = x_ref[...]
    h = _rms(x, gpre_ref[...]).astype(BF16)
    acc = None
    for c in range(D_FF // chunk):
        gt = jnp.dot(h, win_ref[:, c * chunk:(c + 1) * chunk], preferred_element_type=F32)
        up = jnp.dot(h, win_ref[:, D_FF + c * chunk:D_FF + (c + 1) * chunk], preferred_element_type=F32)
        a = (gt * jax.nn.sigmoid(gt) * up).astype(BF16)
        t = jnp.dot(a, wout_ref[c * chunk:(c + 1) * chunk, :], preferred_element_type=F32)
        acc = t if acc is None else acc + t
    o_ref[...] = x + _rms(acc, gpost_ref[...])


def _ffn(x2, gpre, win, wout, gpost, tm=512, chunk=256):
    T, D = x2.shape
    row = lambda i: (i, 0)
    return pl.pallas_call(
        functools.partial(_ffn_kernel, chunk=chunk),
        grid=(T // tm,),
        in_specs=[pl.BlockSpec((tm, D), row), _const_spec((1, D)), _const_spec(win.shape),
                  _const_spec(wout.shape), _const_spec((1, D))],
        out_specs=pl.BlockSpec((tm, D), row),
        out_shape=jax.ShapeDtypeStruct((T, D), F32),
        compiler_params=_cparams(("parallel",)),
        name="ffn",
    )(x2, gpre, win, wout, gpost)


def _permute_in_columns(w):
    cuts = []
    acc = 0
    for size in IN_SIZES[:-1]:
        acc += size
        cuts.append(acc)
    (lru_x, lru_g, nsa_q, nsa_kv, nsa_gate, gla_q, gla_k, gla_v, gla_og, gla_lr, merge) = jnp.split(w, cuts, axis=-1)
    lead = w.shape[:-1]
    kv = nsa_kv.reshape(lead + (3, 2, NSA_G, NSA_HD))
    kv = jnp.moveaxis(kv, -3, -2).reshape(lead + (6 * NSA_G * NSA_HD,))
    pad = jnp.zeros(lead + (LANE - nsa_gate.shape[-1] - gla_lr.shape[-1],), w.dtype)
    return jnp.concatenate([lru_x, lru_g, nsa_q, gla_v, gla_og, merge, kv, gla_q, gla_k, nsa_gate, gla_lr, pad],
                           axis=-1)


def _layer(x2, B, S, tables, norm_g, w_in, conv_w, conv_b, lru_w_gates, lru_b_gates, lru_lambda,
           cmp_pos, cmp_w1, cmp_w2, gla_wa2, gla_ba, gla_norm, w_branch, w_out, w_ffn_in, w_ffn_out):
    D = D_MODEL
    w_perm = _permute_in_columns(w_in).astype(BF16)
    proj = _in_proj(x2, norm_g[0][None, :], w_perm)
    proj3 = proj.reshape(B, S, PROJ_WIDTH)

    wg = jnp.concatenate([lru_w_gates[0], lru_w_gates[1]], axis=-1).astype(BF16)
    y_a = _lru(proj3, conv_w, conv_b[None, :], wg, lru_b_gates, lru_lambda[None, :])

    kvc = _compress(proj3, *_compress_params(cmp_pos, cmp_w1, cmp_w2))
    y_b = _nsa(proj3, kvc, *tables)

    wa_pad = jnp.zeros((LANE, GLA_HEADS * GLA_DK), F32).at[SMALL_LR:SMALL_LR + GLA_RANK].set(gla_wa2).astype(BF16)
    y_c = _gla(proj3, wa_pad, gla_ba[None, :], gla_norm[None, :])

    x2 = _merge(y_a.reshape(B * S, D), y_b.reshape(B * S, D), y_c.reshape(B * S, D), proj, x2,
                w_branch.astype(BF16), w_out.astype(BF16), norm_g[1][None, :])
    x2 = _ffn(x2, norm_g[2][None, :], w_ffn_in.astype(BF16), w_ffn_out.astype(BF16), norm_g[3][None, :])
    return x2


def kernel(x, rel_table, norm_g, w_in, conv_w, conv_b, lru_w_gates, lru_b_gates, lru_lambda, cmp_pos, cmp_w1,
           cmp_w2, gla_wa2, gla_ba, gla_norm, w_branch, w_out, w_ffn_in, w_ffn_out):
    B, S, D = x.shape
    tables = _nsa_tables(rel_table, S)
    x2 = x.reshape(B * S, D)
    for l in range(norm_g.shape[0]):
        x2 = _layer(x2, B, S, tables, norm_g[l], w_in[l], conv_w[l], conv_b[l], lru_w_gates[l], lru_b_gates[l],
                    lru_lambda[l], cmp_pos[l], cmp_w1[l], cmp_w2[l], gla_wa2[l], gla_ba[l], gla_norm[l],
                    w_branch[l], w_out[l], w_ffn_in[l], w_ffn_out[l])
    return x2.reshape(B, S, D)
```

```python
import functools
import math

import jax
import jax.numpy as jnp
from jax import lax
from jax.experimental import pallas as pl
from jax.experimental.pallas import tpu as pltpu

F32 = jnp.float32
BF16 = jnp.bfloat16

D_MODEL = 1024
N_BRANCH = 3
EPS = 1e-6
NEG_INF = -1e30
FORCED = 1e4

LRU_BLOCKS = 8
LRU_BLOCK = D_MODEL // LRU_BLOCKS
CONV_WIDTH = 4
LRU_C = 8.0

NSA_HEADS = 16
NSA_G = 4
NSA_R = NSA_HEADS // NSA_G
NSA_HD = D_MODEL // NSA_HEADS
CMP_BLOCK = 32
CMP_STRIDE = 16
CMP_HIDDEN = 256
SLC_BLOCK = 64
SLC_TOP_N = 8
WINDOW = 256

GLA_HEADS = 4
GLA_DK = (D_MODEL // 2) // GLA_HEADS
GLA_DV = D_MODEL // GLA_HEADS
GLA_RANK = 16
GLA_TAU = 16.0
GLA_CHUNK = 32

REL_BUCKETS = 32
REL_MAX_EXACT = 16
REL_MAX_DIST = 128

D_FF = -(-8 * D_MODEL // (3 * 256)) * 256

IN_SIZES = (D_MODEL, D_MODEL, D_MODEL, 6 * NSA_G * NSA_HD, 3 * NSA_HEADS, D_MODEL // 2, D_MODEL // 2,
            D_MODEL, D_MODEL, GLA_RANK, N_BRANCH * D_MODEL)

LANE = 128
COL_LRU_X = 0
COL_LRU_G = 1024
COL_NSA_Q = 2048
COL_GLA_V = 3072
COL_GLA_OG = 4096
COL_MERGE = 5120
COL_NSA_KV = 8192
COL_GLA_Q = 9728
COL_GLA_K = 10240
COL_SMALL = 10752
PROJ_WIDTH = 10880
SMALL_GATE = 0
SMALL_LR = 3 * NSA_HEADS

Q_TILE = 256
VMEM_LIMIT = 56 * 1024 * 1024
_NT = (((1,), (1,)), ((), ()))


def _cparams(sem):
    return pltpu.CompilerParams(dimension_semantics=sem, vmem_limit_bytes=VMEM_LIMIT)


def _const_spec(shape):
    nd = len(shape)
    return pl.BlockSpec(shape, lambda *_: (0,) * nd, pipeline_mode=pl.Buffered(1))


def _rms(x, g):
    return x * lax.rsqrt(jnp.mean(x * x, axis=-1, keepdims=True) + EPS) * g


def _gelu_tanh(x):
    return 0.5 * x * (1.0 + jnp.tanh(math.sqrt(2.0 / math.pi) * (x + 0.044715 * (x * x * x))))


def _softplus(z):
    return jnp.maximum(z, 0.0) + jnp.log1p(jnp.exp(-jnp.abs(z)))


def _in_proj_kernel(x_ref, g_ref, w_ref, o_ref, h_ref):
    @pl.when(pl.program_id(1) == 0)
    def _():
        h_ref[...] = _rms(x_ref[...], g_ref[...]).astype(BF16)

    o_ref[...] = jnp.dot(h_ref[...], w_ref[...], preferred_element_type=F32).astype(o_ref.dtype)


def _in_proj(x2, g, w, tm=1024, tn=2176):
    T, D = x2.shape
    N = w.shape[1]
    return pl.pallas_call(
        _in_proj_kernel,
        grid=(T // tm, N // tn),
        in_specs=[pl.BlockSpec((tm, D), lambda i, j: (i, 0)),
                  pl.BlockSpec((1, D), lambda i, j: (0, 0)),
                  pl.BlockSpec((D, tn), lambda i, j: (0, j))],
        out_specs=pl.BlockSpec((tm, tn), lambda i, j: (i, j)),
        out_shape=jax.ShapeDtypeStruct((T, N), BF16),
        scratch_shapes=[pltpu.VMEM((tm, D), BF16)],
        compiler_params=_cparams(("parallel", "arbitrary")),
        name="in_proj",
    )(x2, g, w)


def _lru_kernel(xa_ref, ga_ref, cw_ref, cb_ref, wg_ref, bg_ref, lam_ref, y_ref, xe_ref, h_ref, *, R):
    @pl.when(pl.program_id(1) == 0)
    def _():
        xe_ref[0:8, :] = jnp.zeros((8, D_MODEL), F32)
        h_ref[...] = jnp.zeros_like(h_ref)

    xe_ref[8:8 + R, :] = xa_ref[0].astype(F32)
    xc = cb_ref[...]
    for j in range(CONV_WIDTH):
        off = 8 - (CONV_WIDTH - 1) + j
        xc = xc + cw_ref[j:j + 1, :] * xe_ref[off:off + R, :]
    xe_ref[0:8, :] = xe_ref[R:R + 8, :]

    sp = _softplus(-lam_ref[...])
    row8 = lax.broadcasted_iota(jnp.int32, (R, LRU_BLOCK), 0) & 7
    for n in range(LRU_BLOCKS):
        sl = slice(n * LRU_BLOCK, (n + 1) * LRU_BLOCK)
        xcn = xc[:, sl]
        gz = jnp.dot(xcn.astype(BF16), wg_ref[n], preferred_element_type=F32)
        r = jax.nn.sigmoid(gz[:, :LRU_BLOCK] + bg_ref[0:1, sl])
        i = jax.nn.sigmoid(gz[:, LRU_BLOCK:] + bg_ref[1:2, sl])
        log_a = (-LRU_C) * r * sp[:, sl]
        a = jnp.exp(log_a)
        u = jnp.sqrt(1.0 - jnp.exp(2.0 * log_a)) * (i * xcn)
        for d in (1, 2, 4):
            keep = row8 >= d
            a_s = pltpu.roll(a, d, axis=0)
            u_s = pltpu.roll(u, d, axis=0)
            u = jnp.where(keep, a * u_s + u, u)
            a = jnp.where(keep, a * a_s, a)
        hprev = h_ref[:, sl]
        hs = []
        for t in range(R // 8):
            ht = a[t * 8:(t + 1) * 8] * hprev + u[t * 8:(t + 1) * 8]
            hprev = ht[7:8]
            hs.append(ht)
        h_ref[:, sl] = hprev
        h = jnp.concatenate(hs, axis=0)
        y_ref[0, :, sl] = (h * _gelu_tanh(ga_ref[0, :, sl].astype(F32))).astype(y_ref.dtype)


def _lru(proj3, conv_w, conv_b, wg, bg, lam, R=256):
    B, S, _ = proj3.shape
    D = D_MODEL
    return pl.pallas_call(
        functools.partial(_lru_kernel, R=R),
        grid=(B, S // R),
        in_specs=[pl.BlockSpec((1, R, D), lambda b, s: (b, s, COL_LRU_X // D)),
                  pl.BlockSpec((1, R, D), lambda b, s: (b, s, COL_LRU_G // D)),
                  _const_spec((CONV_WIDTH, D)),
                  _const_spec((1, D)),
                  _const_spec((LRU_BLOCKS, LRU_BLOCK, 2 * LRU_BLOCK)),
                  _const_spec((2, D)),
                  _const_spec((1, D))],
        out_specs=pl.BlockSpec((1, R, D), lambda b, s: (b, s, 0)),
        out_shape=jax.ShapeDtypeStruct((B, S, D), BF16),
        scratch_shapes=[pltpu.VMEM((R + 8, D), F32), pltpu.VMEM((1, D), F32)],
        compiler_params=_cparams(("parallel", "arbitrary")),
        name="rglru",
    )(proj3, proj3, conv_w, conv_b, wg, bg, lam)


def _gla_kernel(q_ref, k_ref, v_ref, og_ref, sm_ref, wa_ref, ba_ref, gn_ref, y_ref, st_ref, *, Tb):
    nc = Tb // GLA_CHUNK

    @pl.when(pl.program_id(1) == 0)
    def _():
        st_ref[...] = jnp.zeros_like(st_ref)

    la_pre = jnp.dot(sm_ref[0], wa_ref[...], preferred_element_type=F32)
    rowc = lax.broadcasted_iota(jnp.int32, (Tb, GLA_DK), 0) & (GLA_CHUNK - 1)
    ri = lax.broadcasted_iota(jnp.int32, (Tb, Tb), 0)
    ci = lax.broadcasted_iota(jnp.int32, (Tb, Tb), 1)
    same_chunk_causal = ((ri // GLA_CHUNK) == (ci // GLA_CHUNK)) & (ci <= ri)
    lane_chunk = lax.broadcasted_iota(jnp.int32, (GLA_DK, Tb), 1) // GLA_CHUNK

    for h in range(GLA_HEADS):
        ks = slice(h * GLA_DK, (h + 1) * GLA_DK)
        vs = slice(h * GLA_DV, (h + 1) * GLA_DV)
        z = la_pre[:, ks] + ba_ref[:, ks]
        b = (jnp.minimum(z, 0.0) - jnp.log1p(jnp.exp(-jnp.abs(z)))) * (1.0 / GLA_TAU)
        d = 1
        while d < GLA_CHUNK:
            b = b + jnp.where(rowc >= d, pltpu.roll(b, d, axis=0), 0.0)
            d *= 2
        b_last = jnp.concatenate(
            [jnp.broadcast_to(b[c * GLA_CHUNK + GLA_CHUNK - 1:(c + 1) * GLA_CHUNK, :], (GLA_CHUNK, GLA_DK))
             for c in range(nc)], axis=0)
        q = q_ref[0, :, ks].astype(F32) * (GLA_DK ** -0.5)
        k = k_ref[0, :, ks].astype(F32)
        qt = (q * jnp.exp(b)).astype(BF16)
        kt = (k * jnp.exp(-b)).astype(BF16)
        ke_t = (k * jnp.exp(b_last - b)).T
        b_t = b.T
        v = v_ref[0, :, vs]

        att = lax.dot_general(qt, kt, _NT, preferred_element_type=F32)
        att = jnp.where(same_chunk_causal, att, 0.0).astype(BF16)
        o = jnp.dot(att, v, preferred_element_type=F32)

        state = st_ref[h]
        inter = []
        for c in range(nc):
            rs = slice(c * GLA_CHUNK, (c + 1) * GLA_CHUNK)
            inter.append(jnp.dot(qt[rs], state.astype(BF16), preferred_element_type=F32))
            ke_c = jnp.where(lane_chunk == c, ke_t, 0.0).astype(BF16)
            kv = jnp.dot(ke_c, v, preferred_element_type=F32)
            last = c * GLA_CHUNK + GLA_CHUNK - 1
            state = state * jnp.exp(b_t[:, last:last + 1]) + kv
        st_ref[h] = state
        o = o + jnp.concatenate(inter, axis=0)

        on = o * lax.rsqrt(jnp.mean(o * o, axis=-1, keepdims=True) + EPS) * gn_ref[...]
        og = og_ref[0, :, vs].astype(F32)
        y_ref[0, :, vs] = (on * (og * jax.nn.sigmoid(og))).astype(y_ref.dtype)


def _gla(proj3, wa_pad, ba, gn, Tb=256):
    B, S, _ = proj3.shape
    HK = GLA_HEADS * GLA_DK
    D = D_MODEL
    return pl.pallas_call(
        functools.partial(_gla_kernel, Tb=Tb),
        grid=(B, S // Tb),
        in_specs=[pl.BlockSpec((1, Tb, HK), lambda b, s: (b, s, COL_GLA_Q // HK)),
                  pl.BlockSpec((1, Tb, HK), lambda b, s: (b, s, COL_GLA_K // HK)),
                  pl.BlockSpec((1, Tb, D), lambda b, s: (b, s, COL_GLA_V // D)),
                  pl.BlockSpec((1, Tb, D), lambda b, s: (b, s, COL_GLA_OG // D)),
                  pl.BlockSpec((1, Tb, LANE), lambda b, s: (b, s, COL_SMALL // LANE)),
                  _const_spec((LANE, HK)),
                  _const_spec((1, HK)),
                  _const_spec((1, GLA_DV))],
        out_specs=pl.BlockSpec((1, Tb, D), lambda b, s: (b, s, 0)),
        out_shape=jax.ShapeDtypeStruct((B, S, D), BF16),
        scratch_shapes=[pltpu.VMEM((GLA_HEADS, GLA_DK, GLA_DV), F32)],
        compiler_params=_cparams(("parallel", "arbitrary")),
        name="gla",
    )(proj3, proj3, proj3, proj3, proj3, wa_pad, ba, gn)


def _cmp_kernel(kv_ref, pos_ref, w1_ref, w2_ref, o_ref, x_ref):
    x_ref[...] = kv_ref[0].astype(F32)
    nrow = x_ref.shape[0] // CMP_STRIDE
    first = None
    second = None
    for i in range(CMP_STRIDE):
        xi = x_ref[pl.ds(i, nrow, stride=CMP_STRIDE), :]
        f = jnp.dot((xi + pos_ref[0, i]).astype(BF16), w1_ref[0, i], preferred_element_type=F32)
        s = jnp.dot((xi + pos_ref[1, i]).astype(BF16), w1_ref[1, i], preferred_element_type=F32)
        first = f if first is None else first + f
        second = s if second is None else second + s
    pre = first + pltpu.roll(second, nrow - 1, axis=0)
    hid = _gelu_tanh(pre).astype(BF16)
    o_ref[0, 0] = jnp.dot(hid, w2_ref[...], preferred_element_type=F32).astype(o_ref.dtype)


def _compress(proj3, pos, w1, w2):
    B, S, _ = proj3.shape
    nrow = S // CMP_STRIDE
    kvb = COL_NSA_KV // LANE
    return pl.pallas_call(
        _cmp_kernel,
        grid=(B, NSA_G),
        in_specs=[pl.BlockSpec((1, S, LANE), lambda b, g: (b, 0, kvb + g)),
                  _const_spec(pos.shape), _const_spec(w1.shape), _const_spec(w2.shape)],
        out_specs=pl.BlockSpec((1, 1, nrow, LANE), lambda b, g: (b, g, 0, 0)),
        out_shape=jax.ShapeDtypeStruct((B, NSA_G, nrow, LANE), BF16),
        scratch_shapes=[pltpu.VMEM((S, LANE), F32)],
        compiler_params=_cparams(("parallel", "parallel")),
        name="nsa_compress",
    )(proj3, pos, w1, w2)


def _compress_params(cmp_pos, cmp_w1, cmp_w2):
    hd = NSA_HD
    w1 = cmp_w1.reshape(2, 2, CMP_STRIDE, hd, CMP_HIDDEN)
    z = jnp.zeros_like(w1[0])
    w1 = jnp.concatenate([jnp.concatenate([w1[0], z], axis=-1),
                          jnp.concatenate([z, w1[1]], axis=-1)], axis=-2)
    pos = cmp_pos.reshape(2, 2, CMP_STRIDE, 1, hd)
    pos = jnp.concatenate([pos[0], pos[1]], axis=-1)
    z2 = jnp.zeros_like(cmp_w2[0])
    w2 = jnp.concatenate([jnp.concatenate([cmp_w2[0], z2], axis=-1),
                          jnp.concatenate([z2, cmp_w2[1]], axis=-1)], axis=0)
    return pos, w1.astype(BF16), w2.astype(BF16)


V_ROWS = NSA_HD + 16
MASK_BIG = 1e30
LOG2_E = 1.4426950408889634


def _nsa_kernel(q_ref, kvc_ref, kvs_ref, kvw_ref, gate_ref, bc_ref, bs_ref, bw_ref, xtra_ref, ovl_ref,
                gsel_ref, y_ref, a_ref, b_ref, vs_ref, vw_ref, vc_ref, lg_ref, p_ref):
    TQ = Q_TILE
    HW = NSA_R * TQ
    qi = pl.program_id(2)
    S = kvs_ref.shape[1]
    n_cmp_rows = kvc_ref.shape[2]

    def value_rows(tile):
        t = tile.astype(F32).T
        return jnp.concatenate([t[NSA_HD:], jnp.ones((V_ROWS - NSA_HD, t.shape[1]), F32)], axis=0).astype(BF16)

    @pl.when(qi == 0)
    def _():
        for j in range(S // LANE):
            rows = slice(j * LANE, (j + 1) * LANE)
            ks = kvs_ref[0, rows, :]
            a_ref[rows, 0:LANE] = ks
            a_ref[rows, LANE:2 * LANE] = xtra_ref[rows, :]
            vs_ref[:, rows] = value_rows(ks)
            vw_ref[:, rows] = value_rows(kvw_ref[0, rows, :])
        vc_ref[...] = value_rows(kvc_ref[0, 0])

    def tile_rows(j, n=1):
        return pl.ds(pl.multiple_of(j * TQ, TQ), n * TQ)

    qf = q_ref[0].astype(F32) * (NSA_HD ** -0.5)
    zero_rows = jnp.zeros((LANE - NSA_HD, TQ), F32)
    heads = []
    for pr in range(NSA_R // 2):
        t = qf[:, pr * LANE:(pr + 1) * LANE].T
        heads.append(jnp.concatenate([t[:NSA_HD], zero_rows], axis=0))
        heads.append(jnp.concatenate([t[NSA_HD:], zero_rows], axis=0))
    qt32 = jnp.concatenate(heads, axis=1)
    qt = qt32.astype(BF16)
    qt32_l2 = qt32 * LOG2_E
    qt_l2 = qt32_l2.astype(BF16)

    def normalised(acc):
        return acc[:NSA_HD] * (1.0 / acc[NSA_HD:NSA_HD + 1])

    first = qi == 0
    rows_w = tile_rows(jnp.maximum(qi - 1, 0), 2)
    lg = jnp.dot(kvw_ref[0, rows_w, :], qt_l2, preferred_element_type=F32)
    lg = lg + jnp.concatenate([bw_ref[0, jnp.where(first, 0, 1)], bw_ref[0, jnp.where(first, 2, 0)]], axis=0)
    pw = jnp.exp2(lg - jnp.max(lg, axis=0, keepdims=True)).astype(BF16)
    o_win = normalised(jnp.dot(vw_ref[:, rows_w], pw, preferred_element_type=F32))

    kvc = kvc_ref[0, 0]
    cq = TQ // CMP_STRIDE
    start_c = pl.multiple_of(bc_ref.shape[1] - n_cmp_rows - cq * qi, 8)
    bias_c = bc_ref[0, pl.ds(start_c, n_cmp_rows), :]
    lg = jnp.dot(kvc, qt, preferred_element_type=F32) + bias_c
    e = jnp.exp(lg - jnp.max(lg, axis=0, keepdims=True))
    p = e * (1.0 / jnp.sum(e, axis=0, keepdims=True))
    p = jnp.where(bias_c > 0.5 * NEG_INF, p, 0.0)
    o_cmp = jnp.dot(vc_ref[...], p.astype(BF16), preferred_element_type=F32)[:NSA_HD]

    psum = p[:, 0:TQ]
    for r in range(1, NSA_R):
        psum = psum + p[:, r * TQ:(r + 1) * TQ]
    p_hi = psum.astype(BF16)
    p_lo = (psum - p_hi.astype(F32)).astype(BF16)
    ovl = ovl_ref[...]
    imp = jnp.dot(ovl, p_hi, preferred_element_type=F32) + jnp.dot(ovl, p_lo, preferred_element_type=F32)
    ns = imp.shape[0]
    blk = lax.broadcasted_iota(jnp.int32, (ns, TQ), 0)
    cur = (qi * TQ + lax.broadcasted_iota(jnp.int32, (ns, TQ), 1)) // SLC_BLOCK
    forced = (blk == 0) | (blk == cur) | (blk == cur - 1)
    score = jnp.where(forced, FORCED, jnp.where(blk <= cur, imp, -FORCED))
    rank = jnp.zeros((ns, TQ), F32)
    for j in range(ns):
        sj = score[j:j + 1, :]
        tie = jnp.where(blk > j, 1.0, 0.0)
        rank = rank + jnp.where(sj > score, 1.0, jnp.where(sj == score, tie, 0.0))
    neg_sel = jnp.where(rank < float(SLC_TOP_N), 0.0, -MASK_BIG)
    b_ref[...] = jnp.concatenate([qt32_l2, jnp.concatenate([neg_sel] * NSA_R, axis=1),
                                  jnp.zeros((LANE - ns, HW), F32)], axis=0).astype(BF16)

    def logits(j):
        return jnp.dot(a_ref[tile_rows(j), :], b_ref[...], preferred_element_type=F32)

    lg_ref[...] = logits(0)
    p_ref[...] = jnp.zeros(p_ref.shape, BF16)

    def trip(j, carry):
        m, acc, alpha = carry
        lg_next = logits(jnp.minimum(j + 1, qi))
        acc = acc * alpha + jnp.dot(vs_ref[:, tile_rows(jnp.maximum(j - 1, 0))], p_ref[...],
                                    preferred_element_type=F32)
        lg = lg_ref[...] + bs_ref[0, jnp.minimum(qi - j, 2)]
        m_new = jnp.maximum(m, jnp.max(lg, axis=0, keepdims=True))
        p_ref[...] = jnp.exp2(lg - m_new).astype(BF16)
        lg_ref[...] = lg_next
        return m_new, acc, jnp.exp2(m - m_new)

    init = (jnp.full((1, HW), NEG_INF, F32), jnp.zeros((V_ROWS, HW), F32), jnp.ones((1, HW), F32))
    _, acc, alpha = lax.fori_loop(0, qi + 1, trip, init)
    acc = acc * alpha + jnp.dot(vs_ref[:, tile_rows(qi)], p_ref[...], preferred_element_type=F32)
    o_slc = normalised(acc)

    gs = jax.nn.sigmoid(lax.dot_general(gsel_ref[0], gate_ref[0], _NT, preferred_element_type=F32))

    def gate_row(br):
        return jnp.concatenate([gs[br * NSA_R + r:br * NSA_R + r + 1] for r in range(NSA_R)], axis=1)

    o = gate_row(0) * o_cmp + gate_row(1) * o_slc + gate_row(2) * o_win
    o = jnp.concatenate([o[:, r * TQ:(r + 1) * TQ] for r in range(NSA_R)], axis=0)
    y_ref[0] = o.T.astype(y_ref.dtype)


def _nsa(proj3, kvc, bias_c, bias_s, bias_w, xtra, ovl_t, gsel):
    B, S, _ = proj3.shape
    TQ = Q_TILE
    G = NSA_G
    RW = NSA_R * NSA_HD
    HW = NSA_R * TQ
    kvb = COL_NSA_KV // LANE
    ncr = kvc.shape[2]
    assert S % TQ == 0 and S >= 2 * TQ and WINDOW == TQ and ncr % LANE == 0
    return pl.pallas_call(
        _nsa_kernel,
        grid=(G, B, S // TQ),
        in_specs=[pl.BlockSpec((1, TQ, RW), lambda g, b, i: (b, i, COL_NSA_Q // RW + g)),
                  pl.BlockSpec((1, 1, ncr, LANE), lambda g, b, i: (b, g, 0, 0)),
                  pl.BlockSpec((1, S, LANE), lambda g, b, i: (b, 0, kvb + G + g)),
                  pl.BlockSpec((1, S, LANE), lambda g, b, i: (b, 0, kvb + 2 * G + g)),
                  pl.BlockSpec((1, TQ, LANE), lambda g, b, i: (b, i, COL_SMALL // LANE)),
                  pl.BlockSpec((1,) + bias_c.shape[1:], lambda g, b, i: (g, 0, 0)),
                  pl.BlockSpec((1,) + bias_s.shape[1:], lambda g, b, i: (g, 0, 0, 0)),
                  pl.BlockSpec((1,) + bias_w.shape[1:], lambda g, b, i: (g, 0, 0, 0)),
                  pl.BlockSpec(xtra.shape, lambda g, b, i: (0, 0)),
                  pl.BlockSpec(ovl_t.shape, lambda g, b, i: (0, 0)),
                  pl.BlockSpec((1,) + gsel.shape[1:], lambda g, b, i: (g, 0, 0))],
        out_specs=pl.BlockSpec((1, TQ, RW), lambda g, b, i: (b, i, g)),
        out_shape=jax.ShapeDtypeStruct((B, S, D_MODEL), BF16),
        scratch_shapes=[pltpu.VMEM((S, 2 * LANE), BF16), pltpu.VMEM((2 * LANE, HW), BF16),
                        pltpu.VMEM((V_ROWS, S), BF16), pltpu.VMEM((V_ROWS, S), BF16),
                        pltpu.VMEM((V_ROWS, ncr), BF16), pltpu.VMEM((TQ, HW), F32), pltpu.VMEM((TQ, HW), BF16)],
        compiler_params=_cparams(("parallel", "parallel", "arbitrary")),
        name="nsa_attention",
    )(proj3, kvc, proj3, proj3, proj3, bias_c, bias_s, bias_w, xtra, ovl_t, gsel)


def _rel_bucket(dist):
    n = jnp.maximum(dist, 0)
    nf = jnp.maximum(n, REL_MAX_EXACT).astype(F32)
    large = REL_MAX_EXACT + (jnp.log(nf / REL_MAX_EXACT) / math.log(REL_MAX_DIST / REL_MAX_EXACT)
                             * (REL_BUCKETS - REL_MAX_EXACT)).astype(jnp.int32)
    large = jnp.minimum(large, REL_BUCKETS - 1)
    return jnp.where(n < REL_MAX_EXACT, n, large)


def _nsa_tables(rel_table, S):
    TQ = Q_TILE
    tbl = rel_table.astype(F32).reshape(REL_BUCKETS, NSA_G, NSA_R)
    tbl = tbl - tbl[REL_BUCKETS - 1]

    def table(dist, ok):
        onehot = jax.nn.one_hot(_rel_bucket(dist), REL_BUCKETS, dtype=F32)
        b = jnp.einsum("...kqn,ngr->g...krq", onehot, tbl, precision=lax.Precision.HIGHEST)
        b = jnp.where(ok[..., :, None, :], b, NEG_INF)
        return b.reshape(b.shape[:-2] + (NSA_R * dist.shape[-1],))

    kk = jnp.arange(TQ)[:, None]
    qq = jnp.arange(TQ)[None, :]
    dist = jnp.arange(2)[:, None, None] * TQ + qq - kk
    bias_s = table(dist, dist >= 0) * LOG2_E
    bias_s = jnp.concatenate([bias_s, jnp.zeros_like(bias_s[:, :1])], axis=1)
    bias_w = table(dist, (dist >= 0) & (dist < WINDOW)) * LOG2_E
    bias_w = jnp.concatenate([bias_w, jnp.full_like(bias_w[:, :1], NEG_INF)], axis=1)
    n_rows = S // CMP_STRIDE
    off = n_rows - TQ // CMP_STRIDE
    c_rel = jnp.arange(off + n_rows)[:, None] - off
    dist_c = qq - CMP_STRIDE * c_rel - (CMP_BLOCK - 1)
    bias_c = table(dist_c, dist_c >= 0)

    n_slc = S // SLC_BLOCK
    n_cmp = n_rows - CMP_BLOCK // CMP_STRIDE + 1
    xtra = (jnp.arange(LANE)[None, :] == (jnp.arange(S)[:, None] // SLC_BLOCK)).astype(BF16)
    cmp_start = jnp.arange(n_rows) * CMP_STRIDE
    slc_start = jnp.arange(n_slc) * SLC_BLOCK
    overlap = jnp.clip(jnp.minimum(cmp_start[:, None] + CMP_BLOCK, slc_start[None, :] + SLC_BLOCK)
                       - jnp.maximum(cmp_start[:, None], slc_start[None, :]), 0).astype(F32) / CMP_BLOCK
    overlap = jnp.where(jnp.arange(n_rows)[:, None] < n_cmp, overlap, 0.0)
    g = jnp.arange(NSA_G)[:, None, None]
    row = jnp.arange(16)[None, :, None]
    col = jnp.arange(LANE)[None, None, :]
    src = SMALL_GATE + (row // NSA_R) * NSA_HEADS + g * NSA_R + row % NSA_R
    gsel = ((col == src) & (row < 3 * NSA_R)).astype(BF16)
    return bias_c, bias_s, bias_w, xtra, overlap.T.astype(BF16), gsel


def _merge_kernel(ya_ref, yb_ref, yc_ref, g0_ref, g1_ref, g2_ref, x_ref, wb_ref, wo_ref, gn_ref, o_ref):
    m = None
    for br, (y_ref, g_ref) in enumerate(((ya_ref, g0_ref), (yb_ref, g1_ref), (yc_ref, g2_ref))):
        t = jax.nn.sigmoid(g_ref[...].astype(F32)) * jnp.dot(y_ref[...], wb_ref[br], preferred_element_type=F32)
        m = t if m is None else m + t
    z = jnp.dot(m.astype(BF16), wo_ref[...], preferred_element_type=F32)
    o_ref[...] = x_ref[...] + _rms(z, gn_ref[...])


def _merge(ya, yb, yc, proj, x2, wb, wo, gn, tm=512):
    T, D = x2.shape
    row = lambda i: (i, 0)
    mcol = COL_MERGE // D
    return pl.pallas_call(
        _merge_kernel,
        grid=(T // tm,),
        in_specs=[pl.BlockSpec((tm, D), row), pl.BlockSpec((tm, D), row), pl.BlockSpec((tm, D), row),
                  pl.BlockSpec((tm, D), lambda i: (i, mcol)),
                  pl.BlockSpec((tm, D), lambda i: (i, mcol + 1)),
                  pl.BlockSpec((tm, D), lambda i: (i, mcol + 2)),
                  pl.BlockSpec((tm, D), row),
                  _const_spec(wb.shape), _const_spec(wo.shape), _const_spec((1, D))],
        out_specs=pl.BlockSpec((tm, D), row),
        out_shape=jax.ShapeDtypeStruct((T, D), F32),
        compiler_params=_cparams(("parallel",)),
        name="merge_out",
    )(ya, yb, yc, proj, proj, proj, x2, wb, wo, gn)


def _ffn_kernel(x_ref, gpre_ref, win_ref, wout_ref, gpost_ref, o_ref, *, chunk):
    x = x_ref[...]
    h = _rms(x, gpre_ref[...]).astype(BF16)
    acc = None
    for c in range(D_FF // chunk):
        gt = jnp.dot(h, win_ref[:, c * chunk:(c + 1) * chunk], preferred_element_type=F32)
        up = jnp.dot(h, win_ref[:, D_FF + c * chunk:D_FF + (c + 1) * chunk], preferred_element_type=F32)
        a = (gt * jax.nn.sigmoid(gt) * up).astype(BF16)
        t = jnp.dot(a, wout_ref[c * chunk:(c + 1) * chunk, :], preferred_element_type=F32)
        acc = t if acc is None else acc + t
    o_ref[...] = x + _rms(acc, gpost_ref[...])


def _ffn(x2, gpre, win, wout, gpost, tm=512, chunk=256):
    T, D = x2.shape
    row = lambda i: (i, 0)
    return pl.pallas_call(
        functools.partial(_ffn_kernel, chunk=chunk),
        grid=(T // tm,),
        in_specs=[pl.BlockSpec((tm, D), row), _const_spec((1, D)), _const_spec(win.shape),
                  _const_spec(wout.shape), _const_spec((1, D))],
        out_specs=pl.BlockSpec((tm, D), row),
        out_shape=jax.ShapeDtypeStruct((T, D), F32),
        compiler_params=_cparams(("parallel",)),
        name="ffn",
    )(x2, gpre, win, wout, gpost)


def _permute_in_columns(w):
    cuts = []
    acc = 0
    for size in IN_SIZES[:-1]:
        acc += size
        cuts.append(acc)
    (lru_x, lru_g, nsa_q, nsa_kv, nsa_gate, gla_q, gla_k, gla_v, gla_og, gla_lr, merge) = jnp.split(w, cuts, axis=-1)
    lead = w.shape[:-1]
    kv = nsa_kv.reshape(lead + (3, 2, NSA_G, NSA_HD))
    kv = jnp.moveaxis(kv, -3, -2).reshape(lead + (6 * NSA_G * NSA_HD,))
    pad = jnp.zeros(lead + (LANE - nsa_gate.shape[-1] - gla_lr.shape[-1],), w.dtype)
    return jnp.concatenate([lru_x, lru_g, nsa_q, gla_v, gla_og, merge, kv, gla_q, gla_k, nsa_gate, gla_lr, pad],
                           axis=-1)


def _layer(x2, B, S, tables, norm_g, w_in, conv_w, conv_b, lru_w_gates, lru_b_gates, lru_lambda,
           cmp_pos, cmp_w1, cmp_w2, gla_wa2, gla_ba, gla_norm, w_branch, w_out, w_ffn_in, w_ffn_out):
    D = D_MODEL
    w_perm = _permute_in_columns(w_in.astype(BF16))
    proj = _in_proj(x2, norm_g[0][None, :], w_perm)
    proj3 = proj.reshape(B, S, PROJ_WIDTH)

    wg = jnp.concatenate([lru_w_gates[0], lru_w_gates[1]], axis=-1).astype(BF16)
    y_a = _lru(proj3, conv_w, conv_b[None, :], wg, lru_b_gates, lru_lambda[None, :])

    kvc = _compress(proj3, *_compress_params(cmp_pos, cmp_w1, cmp_w2))
    y_b = _nsa(proj3, kvc, *tables)

    wa_pad = jnp.zeros((LANE, GLA_HEADS * GLA_DK), F32).at[SMALL_LR:SMALL_LR + GLA_RANK].set(gla_wa2).astype(BF16)
    y_c = _gla(proj3, wa_pad, gla_ba[None, :], gla_norm[None, :])

    x2 = _merge(y_a.reshape(B * S, D), y_b.reshape(B * S, D), y_c.reshape(B * S, D), proj, x2,
                w_branch.astype(BF16), w_out.astype(BF16), norm_g[1][None, :])
    x2 = _ffn(x2, norm_g[2][None, :], w_ffn_in.astype(BF16), w_ffn_out.astype(BF16), norm_g[3][None, :])
    return x2


def kernel(x, rel_table, norm_g, w_in, conv_w, conv_b, lru_w_gates, lru_b_gates, lru_lambda, cmp_pos, cmp_w1,
           cmp_w2, gla_wa2, gla_ba, gla_norm, w_branch, w_out, w_ffn_in, w_ffn_out):
    B, S, D = x.shape
    tables = _nsa_tables(rel_table, S)
    x2 = x.reshape(B * S, D)
    for l in range(norm_g.shape[0]):
        x2 = _layer(x2, B, S, tables, norm_g[l], w_in[l], conv_w[l], conv_b[l], lru_w_gates[l], lru_b_gates[l],
                    lru_lambda[l], cmp_pos[l], cmp_w1[l], cmp_w2[l], gla_wa2[l], gla_ba[l], gla_norm[l],
                    w_branch[l], w_out[l], w_ffn_in[l], w_ffn_out[l])
    return x2.reshape(B, S, D)
```

```python
import functools
import math

import jax
import jax.numpy as jnp
from jax import lax
from jax.experimental import pallas as pl
from jax.experimental.pallas import tpu as pltpu

F32 = jnp.float32
BF16 = jnp.bfloat16

D_MODEL = 1024
N_BRANCH = 3
EPS = 1e-6
NEG_INF = -1e30
FORCED = 1e4

LRU_BLOCKS = 8
LRU_BLOCK = D_MODEL // LRU_BLOCKS
CONV_WIDTH = 4
LRU_C = 8.0

NSA_HEADS = 16
NSA_G = 4
NSA_R = NSA_HEADS // NSA_G
NSA_HD = D_MODEL // NSA_HEADS
CMP_BLOCK = 32
CMP_STRIDE = 16
CMP_HIDDEN = 256
SLC_BLOCK = 64
SLC_TOP_N = 8
WINDOW = 256

GLA_HEADS = 4
GLA_DK = (D_MODEL // 2) // GLA_HEADS
GLA_DV = D_MODEL // GLA_HEADS
GLA_RANK = 16
GLA_TAU = 16.0
GLA_CHUNK = 32

REL_BUCKETS = 32
REL_MAX_EXACT = 16
REL_MAX_DIST = 128

D_FF = -(-8 * D_MODEL // (3 * 256)) * 256

IN_SIZES = (D_MODEL, D_MODEL, D_MODEL, 6 * NSA_G * NSA_HD, 3 * NSA_HEADS, D_MODEL // 2, D_MODEL // 2,
            D_MODEL, D_MODEL, GLA_RANK, N_BRANCH * D_MODEL)

LANE = 128
COL_LRU_X = 0
COL_LRU_G = 1024
COL_NSA_Q = 2048
COL_GLA_V = 3072
COL_GLA_OG = 4096
COL_MERGE = 5120
COL_NSA_KV = 8192
COL_GLA_Q = 9728
COL_GLA_K = 10240
COL_SMALL = 10752
PROJ_WIDTH = 10880
SMALL_GATE = 0
SMALL_LR = 3 * NSA_HEADS

Q_TILE = 256
K_TILE = 256
VMEM_LIMIT = 56 * 1024 * 1024
_NT = (((1,), (1,)), ((), ()))


def _cparams(sem):
    return pltpu.CompilerParams(dimension_semantics=sem, vmem_limit_bytes=VMEM_LIMIT)


def _const_spec(shape):
    nd = len(shape)
    return pl.BlockSpec(shape, lambda *_: (0,) * nd, pipeline_mode=pl.Buffered(1))


def _rms(x, g):
    return x * lax.rsqrt(jnp.mean(x * x, axis=-1, keepdims=True) + EPS) * g


def _gelu_tanh(x):
    return 0.5 * x * (1.0 + jnp.tanh(math.sqrt(2.0 / math.pi) * (x + 0.044715 * (x * x * x))))


def _sigmoid(x):
    return 0.5 * jnp.tanh(0.5 * x) + 0.5


def _sqrt_nonneg(x):
    return jnp.where(x > 0.0, x * lax.rsqrt(x), 0.0)


def _softplus(z):
    return jnp.maximum(z, 0.0) + jnp.log1p(jnp.exp(-jnp.abs(z)))


def _in_proj_kernel(x_ref, g_ref, w_ref, o_ref, h_ref):
    @pl.when(pl.program_id(1) == 0)
    def _():
        h_ref[...] = _rms(x_ref[...], g_ref[...]).astype(BF16)

    o_ref[...] = jnp.dot(h_ref[...], w_ref[...], preferred_element_type=F32).astype(o_ref.dtype)


def _in_proj(x2, g, w, tm=1024, tn=2176):
    T, D = x2.shape
    N = w.shape[1]
    return pl.pallas_call(
        _in_proj_kernel,
        grid=(T // tm, N // tn),
        in_specs=[pl.BlockSpec((tm, D), lambda i, j: (i, 0)),
                  pl.BlockSpec((1, D), lambda i, j: (0, 0)),
                  pl.BlockSpec((D, tn), lambda i, j: (0, j))],
        out_specs=pl.BlockSpec((tm, tn), lambda i, j: (i, j)),
        out_shape=jax.ShapeDtypeStruct((T, N), BF16),
        scratch_shapes=[pltpu.VMEM((tm, D), BF16)],
        compiler_params=_cparams(("parallel", "arbitrary")),
        name="in_proj",
    )(x2, g, w)


def _lru_kernel(xa_ref, ga_ref, cw_ref, cb_ref, wg_ref, bg_ref, lam_ref, y_ref, tail_ref, h_ref, *, R):
    @pl.when(pl.program_id(1) == 0)
    def _():
        tail_ref[...] = jnp.zeros_like(tail_ref)
        h_ref[...] = jnp.zeros_like(h_ref)

    sp = _softplus(-lam_ref[...])
    groups = R // 8
    row8 = lax.broadcasted_iota(jnp.int32, (groups, 8, LRU_BLOCK), 1)
    for n in range(LRU_BLOCKS):
        sl = slice(n * LRU_BLOCK, (n + 1) * LRU_BLOCK)
        x3 = xa_ref[0, :, sl].astype(F32).reshape(groups, 8, LRU_BLOCK)
        tail = tail_ref[:, sl].reshape(1, 8, LRU_BLOCK)
        xc3 = cb_ref[:, sl] + cw_ref[CONV_WIDTH - 1:CONV_WIDTH, sl] * x3
        for s in range(1, CONV_WIDTH):
            rolled = pltpu.roll(x3, s, axis=1)
            prev = jnp.concatenate([pltpu.roll(tail, s, axis=1), rolled[:groups - 1]], axis=0)
            w = cw_ref[CONV_WIDTH - 1 - s:CONV_WIDTH - s, sl]
            xc3 = xc3 + w * jnp.where(row8 >= s, rolled, prev)
        tail_ref[:, sl] = x3[groups - 1]
        xcn = xc3.reshape(R, LRU_BLOCK)
        gz =jnp.dot(xcn.astype(BF16), wg_ref[n], preferred_element_type=F32)
        r = _sigmoid(gz[:, :LRU_BLOCK] + bg_ref[0:1, sl])
        i = _sigmoid(gz[:, LRU_BLOCK:] + bg_ref[1:2, sl])
        a = jnp.exp((-LRU_C) * r * sp[:, sl])
        u = _sqrt_nonneg(1.0 - a * a) * (i * xcn)
        a = a.reshape(R // 8, 8, LRU_BLOCK)
        u = u.reshape(R // 8, 8, LRU_BLOCK)
        for d in (1, 2, 4):
            keep = row8 >= d
            a_s = pltpu.roll(a, d, axis=1)
            u_s = pltpu.roll(u, d, axis=1)
            u = jnp.where(keep, a * u_s + u, u)
            a = jnp.where(keep, a * a_s, a)
        hprev = h_ref[:, sl]
        hs = []
        for t in range(R // 8):
            ht = a[t] * hprev + u[t]
            hprev = ht[7:8]
            hs.append(ht)
        h_ref[:, sl] = hprev
        h = jnp.concatenate(hs, axis=0)
        y_ref[0, :, sl] = (h * _gelu_tanh(ga_ref[0, :, sl].astype(F32))).astype(y_ref.dtype)


def _lru(proj3, conv_w, conv_b, wg, bg, lam, R=256):
    B, S, _ = proj3.shape
    D = D_MODEL
    return pl.pallas_call(
        functools.partial(_lru_kernel, R=R),
        grid=(B, S // R),
        in_specs=[pl.BlockSpec((1, R, D), lambda b, s: (b, s, COL_LRU_X // D)),
                  pl.BlockSpec((1, R, D), lambda b, s: (b, s, COL_LRU_G // D)),
                  _const_spec((CONV_WIDTH, D)),
                  _const_spec((1, D)),
                  _const_spec((LRU_BLOCKS, LRU_BLOCK, 2 * LRU_BLOCK)),
                  _const_spec((2, D)),
                  _const_spec((1, D))],
        out_specs=pl.BlockSpec((1, R, D), lambda b, s: (b, s, 0)),
        out_shape=jax.ShapeDtypeStruct((B, S, D), BF16),
        scratch_shapes=[pltpu.VMEM((8, D), F32), pltpu.VMEM((1, D), F32)],
        compiler_params=_cparams(("parallel", "arbitrary")),
        name="rglru",
    )(proj3, proj3, conv_w, conv_b, wg, bg, lam)


def _gla_kernel(q_ref, k_ref, v_ref, og_ref, sm_ref, wa_ref, ba_ref, gn_ref, y_ref, st_ref, *, Tb):
    nc = Tb // GLA_CHUNK

    @pl.when(pl.program_id(1) == 0)
    def _():
        st_ref[...] = jnp.zeros_like(st_ref)

    la_pre = jnp.dot(sm_ref[0], wa_ref[...], preferred_element_type=F32)
    rowc = lax.broadcasted_iota(jnp.int32, (Tb, GLA_DK), 0) & (GLA_CHUNK - 1)
    ri = lax.broadcasted_iota(jnp.int32, (Tb, Tb), 0)
    ci = lax.broadcasted_iota(jnp.int32, (Tb, Tb), 1)
    same_chunk_causal = ((ri // GLA_CHUNK) == (ci // GLA_CHUNK)) & (ci <= ri)
    lane_chunk = lax.broadcasted_iota(jnp.int32, (GLA_DK, Tb), 1) // GLA_CHUNK

    for h in range(GLA_HEADS):
        ks = slice(h * GLA_DK, (h + 1) * GLA_DK)
        vs = slice(h * GLA_DV, (h + 1) * GLA_DV)
        z = la_pre[:, ks] + ba_ref[:, ks]
        b = (jnp.minimum(z, 0.0) - jnp.log1p(jnp.exp(-jnp.abs(z)))) * (1.0 / GLA_TAU)
        d = 1
        while d < GLA_CHUNK:
            b = b + jnp.where(rowc >= d, pltpu.roll(b, d, axis=0), 0.0)
            d *= 2
        b_last = jnp.concatenate(
            [jnp.broadcast_to(b[c * GLA_CHUNK + GLA_CHUNK - 1:(c + 1) * GLA_CHUNK, :], (GLA_CHUNK, GLA_DK))
             for c in range(nc)], axis=0)
        q = q_ref[0, :, ks].astype(F32) * (GLA_DK ** -0.5)
        k = k_ref[0, :, ks].astype(F32)
        qt = (q * jnp.exp(b)).astype(BF16)
        kt = (k * jnp.exp(-b)).astype(BF16)
        ke_t = (k * jnp.exp(b_last - b)).T
        b_t = b.T
        v = v_ref[0, :, vs]

        att = lax.dot_general(qt, kt, _NT, preferred_element_type=F32)
        att = jnp.where(same_chunk_causal, att, 0.0).astype(BF16)
        o = jnp.dot(att, v, preferred_element_type=F32)

        ke_blocks = jnp.concatenate([jnp.where(lane_chunk == c, ke_t, 0.0) for c in range(nc)], axis=0)
        kv_all = jnp.dot(ke_blocks.astype(BF16), v, preferred_element_type=F32)
        state = st_ref[h]
        inter = []
        for c in range(nc):
            rs = slice(c * GLA_CHUNK, (c + 1) * GLA_CHUNK)
            inter.append(jnp.dot(qt[rs], state.astype(BF16), preferred_element_type=F32))
            last = c * GLA_CHUNK + GLA_CHUNK - 1
            state = state * jnp.exp(b_t[:, last:last + 1]) + kv_all[c * GLA_DK:(c + 1) * GLA_DK]
        st_ref[h] = state
        o = o + jnp.concatenate(inter, axis=0)

        on = o * lax.rsqrt(jnp.mean(o * o, axis=-1, keepdims=True) + EPS) * gn_ref[...]
        og = og_ref[0, :, vs].astype(F32)
        y_ref[0, :, vs] = (on * (og * jax.nn.sigmoid(og))).astype(y_ref.dtype)


def _gla(proj3, wa_pad, ba, gn, Tb=256):
    B, S, _ = proj3.shape
    HK = GLA_HEADS * GLA_DK
    D = D_MODEL
    return pl.pallas_call(
        functools.partial(_gla_kernel, Tb=Tb),
        grid=(B, S // Tb),
        in_specs=[pl.BlockSpec((1, Tb, HK), lambda b, s: (b, s, COL_GLA_Q // HK)),
                  pl.BlockSpec((1, Tb, HK), lambda b, s: (b, s, COL_GLA_K // HK)),
                  pl.BlockSpec((1, Tb, D), lambda b, s: (b, s, COL_GLA_V // D)),
                  pl.BlockSpec((1, Tb, D), lambda b, s: (b, s, COL_GLA_OG // D)),
                  pl.BlockSpec((1, Tb, LANE), lambda b, s: (b, s, COL_SMALL // LANE)),
                  _const_spec((LANE, HK)),
                  _const_spec((1, HK)),
                  _const_spec((1, GLA_DV))],
        out_specs=pl.BlockSpec((1, Tb, D), lambda b, s: (b, s, 0)),
        out_shape=jax.ShapeDtypeStruct((B, S, D), BF16),
        scratch_shapes=[pltpu.VMEM((GLA_HEADS, GLA_DK, GLA_DV), F32)],
        compiler_params=_cparams(("parallel", "arbitrary")),
        name="gla",
    )(proj3, proj3, proj3, proj3, proj3, wa_pad, ba, gn)


def _cmp_kernel(kv_ref, pos_ref, w1_ref, w2_ref, o_ref, x_ref):
    x_ref[...] = kv_ref[0].astype(F32)
    nrow = x_ref.shape[0] // CMP_STRIDE
    first = None
    second = None
    for i in range(CMP_STRIDE):
        xi = x_ref[pl.ds(i, nrow, stride=CMP_STRIDE), :]
        f = jnp.dot((xi + pos_ref[0, i]).astype(BF16), w1_ref[0, i], preferred_element_type=F32)
        s = jnp.dot((xi + pos_ref[1, i]).astype(BF16), w1_ref[1, i], preferred_element_type=F32)
        first = f if first is None else first + f
        second = s if second is None else second + s
    pre = first + pltpu.roll(second, nrow - 1, axis=0)
    hid = _gelu_tanh(pre).astype(BF16)
    o_ref[0, 0] = jnp.dot(hid, w2_ref[...], preferred_element_type=F32).astype(o_ref.dtype)


def _compress(proj3, pos, w1, w2):
    B, S, _ = proj3.shape
    nrow = S // CMP_STRIDE
    kvb = COL_NSA_KV // LANE
    return pl.pallas_call(
        _cmp_kernel,
        grid=(B, NSA_G),
        in_specs=[pl.BlockSpec((1, S, LANE), lambda b, g: (b, 0, kvb + g)),
                  _const_spec(pos.shape), _const_spec(w1.shape), _const_spec(w2.shape)],
        out_specs=pl.BlockSpec((1, 1, nrow, LANE), lambda b, g: (b, g, 0, 0)),
        out_shape=jax.ShapeDtypeStruct((B, NSA_G, nrow, LANE), BF16),
        scratch_shapes=[pltpu.VMEM((S, LANE), F32)],
        compiler_params=_cparams(("parallel", "parallel")),
        name="nsa_compress",
    )(proj3, pos, w1, w2)


def _compress_params(cmp_pos, cmp_w1, cmp_w2):
    hd = NSA_HD
    w1 = cmp_w1.reshape(2, 2, CMP_STRIDE, hd, CMP_HIDDEN)
    z = jnp.zeros_like(w1[0])
    w1 = jnp.concatenate([jnp.concatenate([w1[0], z], axis=-1),
                          jnp.concatenate([z, w1[1]], axis=-1)], axis=-2)
    pos = cmp_pos.reshape(2, 2, CMP_STRIDE, 1, hd)
    pos = jnp.concatenate([pos[0], pos[1]], axis=-1)
    z2 = jnp.zeros_like(cmp_w2[0])
    w2 = jnp.concatenate([jnp.concatenate([cmp_w2[0], z2], axis=-1),
                          jnp.concatenate([z2, cmp_w2[1]], axis=-1)], axis=0)
    return pos, w1.astype(BF16), w2.astype(BF16)


V_ROWS = NSA_HD + 16
MASK_BIG = 1e30
LOG2_E = 1.4426950408889634


def _nsa_kernel(q_ref, kvc_ref, kvs_ref, kvw_ref, gate_ref, bc_ref, bs_ref, bw_ref, xtra_ref, ovl_ref,
                gsel_ref, y_ref, a_ref, b_ref, vs_ref, vw_ref, vc_ref, lg_ref, p_ref):
    TQ = Q_TILE
    HW = NSA_R * TQ
    qi = pl.program_id(2)
    S = kvs_ref.shape[1]
    n_cmp_rows = kvc_ref.shape[2]

    def value_rows(tile):
        t = tile.astype(F32).T
        return jnp.concatenate([t[NSA_HD:], jnp.ones((V_ROWS - NSA_HD, t.shape[1]), F32)], axis=0).astype(BF16)

    @pl.when(qi == 0)
    def _():
        for j in range(S // LANE):
            rows = slice(j * LANE, (j + 1) * LANE)
            ks = kvs_ref[0, rows, :]
            a_ref[rows, 0:LANE] = ks
            a_ref[rows, LANE:2 * LANE] = xtra_ref[rows, :]
            vs_ref[:, rows] = value_rows(ks)
            vw_ref[:, rows] = value_rows(kvw_ref[0, rows, :])
        vc_ref[...] = value_rows(kvc_ref[0, 0])

    KT = K_TILE
    RT = TQ // KT

    def tile_rows(j, n=1):
        return pl.ds(pl.multiple_of(j * KT, KT), n * KT)

    qf = q_ref[0].astype(F32) * (NSA_HD ** -0.5)
    zero_rows = jnp.zeros((LANE - NSA_HD, TQ), F32)
    heads = []
    for pr in range(NSA_R // 2):
        t = qf[:, pr * LANE:(pr + 1) * LANE].T
        heads.append(jnp.concatenate([t[:NSA_HD], zero_rows], axis=0))
        heads.append(jnp.concatenate([t[NSA_HD:], zero_rows], axis=0))
    qt32 = jnp.concatenate(heads, axis=1)
    qt = qt32.astype(BF16)
    qt32_l2 = qt32 * LOG2_E
    qt_l2 = qt32_l2.astype(BF16)

    def normalised(acc):
        return acc[:NSA_HD] * (1.0 / acc[NSA_HD:NSA_HD + 1])

    first = (qi == 0).astype(jnp.int32)
    rows_w = tile_rows(jnp.maximum(RT * qi - 1, 0), RT + 1)
    lg = jnp.dot(kvw_ref[0, rows_w, :], qt_l2, preferred_element_type=F32).astype(BF16)
    lg = lg + jnp.concatenate([bw_ref[0, t + first] for t in range(RT + 1)], axis=0)
    pw = jnp.exp2(lg - jnp.max(lg, axis=0, keepdims=True))
    o_win = normalised(jnp.dot(vw_ref[:, rows_w], pw, preferred_element_type=F32))

    kvc = kvc_ref[0, 0]
    cq = TQ // CMP_STRIDE
    start_c = pl.multiple_of(bc_ref.shape[1] - n_cmp_rows - cq * qi, 8)
    bias_c = bc_ref[0, pl.ds(start_c, n_cmp_rows), :]
    lg = jnp.dot(kvc, qt, preferred_element_type=F32) + bias_c
    e = jnp.exp(lg - jnp.max(lg, axis=0, keepdims=True))
    p = e * (1.0 / jnp.sum(e, axis=0, keepdims=True))
    p = jnp.where(bias_c > 0.5 * NEG_INF, p, 0.0)
    o_cmp = jnp.dot(vc_ref[...], p.astype(BF16), preferred_element_type=F32)[:NSA_HD]

    psum = p[:, 0:TQ]
    for r in range(1, NSA_R):
        psum = psum + p[:, r * TQ:(r + 1) * TQ]
    p_hi = psum.astype(BF16)
    p_lo = (psum - p_hi.astype(F32)).astype(BF16)
    ovl = ovl_ref[...]
    imp = jnp.dot(ovl, p_hi, preferred_element_type=F32) + jnp.dot(ovl, p_lo, preferred_element_type=F32)
    ns = imp.shape[0]
    blk = lax.broadcasted_iota(jnp.int32, (ns, TQ), 0)
    cur = (qi * TQ + lax.broadcasted_iota(jnp.int32, (ns, TQ), 1)) // SLC_BLOCK
    forced = (blk == 0) | (blk == cur) | (blk == cur - 1)
    score = jnp.where(forced, FORCED, jnp.where(blk <= cur, imp, -FORCED))
    rank = jnp.zeros((ns, TQ), F32)
    for j in range(ns):
        sj = score[j:j + 1, :]
        tie = jnp.where(blk > j, 1.0, 0.0)
        rank = rank + jnp.where(sj > score, 1.0, jnp.where(sj == score, tie, 0.0))
    neg_sel = jnp.where(rank < float(SLC_TOP_N), 0.0, -MASK_BIG)
    b_ref[...] = jnp.concatenate([qt32_l2, jnp.concatenate([neg_sel] * NSA_R, axis=1),
                                  jnp.zeros((LANE - ns, HW), F32)], axis=0).astype(BF16)

    def logits(j):
        return jnp.dot(a_ref[tile_rows(j), :], b_ref[...], preferred_element_type=F32).astype(BF16)

    lg_ref[...] = logits(0)
    p_ref[...] = jnp.zeros(p_ref.shape, BF16)

    last = RT * qi + RT - 1

    def trip(j, carry):
        m, acc, alpha = carry
        lg_next = logits(jnp.minimum(j + 1, last))
        acc = acc * alpha + jnp.dot(vs_ref[:, tile_rows(jnp.maximum(j - 1, 0))], p_ref[...],
                                    preferred_element_type=F32)
        lg = lg_ref[...] + bs_ref[0, jnp.where(j >= RT * qi - 1, j - RT * qi + 1, RT + 1)]
        m_new = jnp.maximum(m, jnp.max(lg, axis=0, keepdims=True).astype(F32))
        p_ref[...] = jnp.exp2(lg - m_new.astype(BF16))
        lg_ref[...] = lg_next
        return m_new, acc, jnp.exp2(m - m_new)

    init = (jnp.full((1, HW), NEG_INF, F32), jnp.zeros((V_ROWS, HW), F32), jnp.ones((1, HW), F32))
    _, acc, alpha = lax.fori_loop(0, last + 1, trip, init)
    acc = acc * alpha + jnp.dot(vs_ref[:, tile_rows(last)], p_ref[...], preferred_element_type=F32)
    o_slc = normalised(acc)

    gs = jax.nn.sigmoid(lax.dot_general(gsel_ref[0], gate_ref[0], _NT, preferred_element_type=F32))

    def gate_row(br):
        return jnp.concatenate([gs[br * NSA_R + r:br * NSA_R + r + 1] for r in range(NSA_R)], axis=1)

    o = gate_row(0) * o_cmp + gate_row(1) * o_slc + gate_row(2) * o_win
    o = jnp.concatenate([o[:, r * TQ:(r + 1) * TQ] for r in range(NSA_R)], axis=0)
    y_ref[0] = o.T.astype(y_ref.dtype)


def _nsa(proj3, kvc, bias_c, bias_s, bias_w, xtra, ovl_t, gsel):
    B, S, _ = proj3.shape
    TQ = Q_TILE
    G = NSA_G
    RW = NSA_R * NSA_HD
    HW = NSA_R * TQ
    kvb = COL_NSA_KV // LANE
    ncr = kvc.shape[2]
    KT = K_TILE
    assert S % TQ == 0 and TQ % KT == 0 and S >= TQ + KT and WINDOW == KT and ncr % LANE == 0
    return pl.pallas_call(
        _nsa_kernel,
        grid=(G, B, S // TQ),
        in_specs=[pl.BlockSpec((1, TQ, RW), lambda g, b, i: (b, i, COL_NSA_Q // RW + g)),
                  pl.BlockSpec((1, 1, ncr, LANE), lambda g, b, i: (b, g, 0, 0)),
                  pl.BlockSpec((1, S, LANE), lambda g, b, i: (b, 0, kvb + G + g)),
                  pl.BlockSpec((1, S, LANE), lambda g, b, i: (b, 0, kvb + 2 * G + g)),
                  pl.BlockSpec((1, TQ, LANE), lambda g, b, i: (b, i, COL_SMALL // LANE)),
                  pl.BlockSpec((1,) + bias_c.shape[1:], lambda g, b, i: (g, 0, 0)),
                  pl.BlockSpec((1,) + bias_s.shape[1:], lambda g, b, i: (g, 0, 0, 0)),
                  pl.BlockSpec((1,) + bias_w.shape[1:], lambda g, b, i: (g, 0, 0, 0)),
                  pl.BlockSpec(xtra.shape, lambda g, b, i: (0, 0)),
                  pl.BlockSpec(ovl_t.shape, lambda g, b, i: (0, 0)),
                  pl.BlockSpec((1,) + gsel.shape[1:], lambda g, b, i: (g, 0, 0))],
        out_specs=pl.BlockSpec((1, TQ, RW), lambda g, b, i: (b, i, g)),
        out_shape=jax.ShapeDtypeStruct((B, S, D_MODEL), BF16),
        scratch_shapes=[pltpu.VMEM((S, 2 * LANE), BF16), pltpu.VMEM((2 * LANE, HW), BF16),
                        pltpu.VMEM((V_ROWS, S), BF16), pltpu.VMEM((V_ROWS, S), BF16),
                        pltpu.VMEM((V_ROWS, ncr), BF16), pltpu.VMEM((KT, HW), BF16), pltpu.VMEM((KT, HW), BF16)],
        compiler_params=_cparams(("parallel", "parallel", "arbitrary")),
        name="nsa_attention",
    )(proj3, kvc, proj3, proj3, proj3, bias_c, bias_s, bias_w, xtra, ovl_t, gsel)


def _rel_bucket(dist):
    n = jnp.maximum(dist, 0)
    nf = jnp.maximum(n, REL_MAX_EXACT).astype(F32)
    large = REL_MAX_EXACT + (jnp.log(nf / REL_MAX_EXACT) / math.log(REL_MAX_DIST / REL_MAX_EXACT)
                             * (REL_BUCKETS - REL_MAX_EXACT)).astype(jnp.int32)
    large = jnp.minimum(large, REL_BUCKETS - 1)
    return jnp.where(n < REL_MAX_EXACT, n, large)


def _nsa_tables(rel_table, S):
    TQ = Q_TILE
    tbl = rel_table.astype(F32).reshape(REL_BUCKETS, NSA_G, NSA_R)
    tbl = tbl - tbl[REL_BUCKETS - 1]

    def table(dist, ok):
        onehot = jax.nn.one_hot(_rel_bucket(dist), REL_BUCKETS, dtype=F32)
        b = jnp.einsum("...kqn,ngr->g...krq", onehot, tbl, precision=lax.Precision.HIGHEST)
        b = jnp.where(ok[..., :, None, :], b, NEG_INF)
        return b.reshape(b.shape[:-2] + (NSA_R * dist.shape[-1],))

    KT = K_TILE
    kk = jnp.arange(KT)[:, None]
    qq = jnp.arange(TQ)[None, :]
    dist = qq - kk - KT * (jnp.arange(TQ // KT + 1)[:, None, None] - 1)
    bias_s = table(dist, dist >= 0) * LOG2_E
    bias_s = jnp.concatenate([bias_s, jnp.zeros_like(bias_s[:, :1])], axis=1).astype(BF16)
    bias_w = table(dist, (dist >= 0) & (dist < WINDOW)) * LOG2_E
    bias_w = jnp.concatenate([bias_w, jnp.full_like(bias_w[:, :1], NEG_INF)], axis=1).astype(BF16)
    n_rows = S // CMP_STRIDE
    off = n_rows - TQ // CMP_STRIDE
    c_rel = jnp.arange(off + n_rows)[:, None] - off
    dist_c = qq - CMP_STRIDE * c_rel - (CMP_BLOCK - 1)
    bias_c = table(dist_c, dist_c >= 0)

    n_slc = S // SLC_BLOCK
    n_cmp = n_rows - CMP_BLOCK // CMP_STRIDE + 1
    xtra = (jnp.arange(LANE)[None, :] == (jnp.arange(S)[:, None] // SLC_BLOCK)).astype(BF16)
    cmp_start = jnp.arange(n_rows) * CMP_STRIDE
    slc_start = jnp.arange(n_slc) * SLC_BLOCK
    overlap = jnp.clip(jnp.minimum(cmp_start[:, None] + CMP_BLOCK, slc_start[None, :] + SLC_BLOCK)
                       - jnp.maximum(cmp_start[:, None], slc_start[None, :]), 0).astype(F32) / CMP_BLOCK
    overlap = jnp.where(jnp.arange(n_rows)[:, None] < n_cmp, overlap, 0.0)
    g = jnp.arange(NSA_G)[:, None, None]
    row = jnp.arange(16)[None, :, None]
    col = jnp.arange(LANE)[None, None, :]
    src = SMALL_GATE + (row // NSA_R) * NSA_HEADS + g * NSA_R + row % NSA_R
    gsel = ((col == src) & (row < 3 * NSA_R)).astype(BF16)
    return bias_c, bias_s, bias_w, xtra, overlap.T.astype(BF16), gsel


def _merge_kernel(ya_ref, yb_ref, yc_ref, g0_ref, g1_ref, g2_ref, x_ref, wb_ref, wo_ref, gn_ref, o_ref):
    m = None
    for br, (y_ref, g_ref) in enumerate(((ya_ref, g0_ref), (yb_ref, g1_ref), (yc_ref, g2_ref))):
        t = jax.nn.sigmoid(g_ref[...].astype(F32)) * jnp.dot(y_ref[...], wb_ref[br], preferred_element_type=F32)
        m = t if m is None else m + t
    z = jnp.dot(m.astype(BF16), wo_ref[...], preferred_element_type=F32)
    o_ref[...] = x_ref[...] + _rms(z, gn_ref[...])


def _merge(ya, yb, yc, proj, x2, wb, wo, gn, tm=512):
    T, D = x2.shape
    row = lambda i: (i, 0)
    mcol = COL_MERGE // D
    return pl.pallas_call(
        _merge_kernel,
        grid=(T // tm,),
        in_specs=[pl.BlockSpec((tm, D), row), pl.BlockSpec((tm, D), row), pl.BlockSpec((tm, D), row),
                  pl.BlockSpec((tm, D), lambda i: (i, mcol)),
                  pl.BlockSpec((tm, D), lambda i: (i, mcol + 1)),
                  pl.BlockSpec((tm, D), lambda i: (i, mcol + 2)),
                  pl.BlockSpec((tm, D), row),
                  _const_spec(wb.shape), _const_spec(wo.shape), _const_spec((1, D))],
        out_specs=pl.BlockSpec((tm, D), row),
        out_shape=jax.ShapeDtypeStruct((T, D), F32),
        compiler_params=_cparams(("parallel",)),
        name="merge_out",
    )(ya, yb, yc, proj, proj, proj, x2, wb, wo, gn)


def _ffn_kernel(x_ref, gpre_ref, win_ref, wout_ref, gpost_ref, o_ref, *, chunk):
    x = x_ref[...]
    h = _rms(x, gpre_ref[...]).astype(BF16)
    acc = None
    for c in range(D_FF // chunk):
        gt = jnp.dot(h, win_ref[:, c * chunk:(c + 1) * chunk], preferred_element_type=F32)
        up = jnp.dot(h, win_ref[:, D_FF + c * chunk:D_FF + (c + 1) * chunk], preferred_element_type=F32)
        a = (gt * jax.nn.sigmoid(gt) * up).astype(BF16)
        t = jnp.dot(a, wout_ref[c * chunk:(c + 1) * chunk, :], preferred_element_type=F32)
        acc = t if acc is None else acc + t
    o_ref[...] = x + _rms(acc, gpost_ref[...])


def _ffn(x2, gpre, win, wout, gpost, tm=512, chunk=256):
    T, D = x2.shape
    row = lambda i: (i, 0)
    return pl.pallas_call(
        functools.partial(_ffn_kernel, chunk=chunk),
        grid=(T // tm,),
        in_specs=[pl.BlockSpec((tm, D), row), _const_spec((1, D)), _const_spec(win.shape),
                  _const_spec(wout.shape), _const_spec((1, D))],
        out_specs=pl.BlockSpec((tm, D), row),
        out_shape=jax.ShapeDtypeStruct((T, D), F32),
        compiler_params=_cparams(("parallel",)),
        name="ffn",
    )(x2, gpre, win, wout, gpost)


def _permute_in_columns(w):
    cuts = []
    acc = 0
    for size in IN_SIZES[:-1]:
        acc += size
        cuts.append(acc)
    (lru_x, lru_g, nsa_q, nsa_kv, nsa_gate, gla_q, gla_k, gla_v, gla_og, gla_lr, merge) = jnp.split(w, cuts, axis=-1)
    lead = w.shape[:-1]
    kv = nsa_kv.reshape(lead + (3, 2, NSA_G, NSA_HD))
    kv = jnp.moveaxis(kv, -3, -2).reshape(lead + (6 * NSA_G * NSA_HD,))
    pad = jnp.zeros(lead + (LANE - nsa_gate.shape[-1] - gla_lr.shape[-1],), w.dtype)
    return jnp.concatenate([lru_x, lru_g, nsa_q, gla_v, gla_og, merge, kv, gla_q, gla_k, nsa_gate, gla_lr, pad],
                           axis=-1)


def _layer(x2, B, S, tables, norm_g, w_in, conv_w, conv_b, lru_w_gates, lru_b_gates, lru_lambda,
           cmp_pos, cmp_w1, cmp_w2, gla_wa2, gla_ba, gla_norm, w_branch, w_out, w_ffn_in, w_ffn_out):
    D = D_MODEL
    w_perm = _permute_in_columns(w_in.astype(BF16))
    proj = _in_proj(x2, norm_g[0][None, :], w_perm)
    proj3 = proj.reshape(B, S, PROJ_WIDTH)

    wg = jnp.concatenate([lru_w_gates[0], lru_w_gates[1]], axis=-1).astype(BF16)
    y_a = _lru(proj3, conv_w, conv_b[None, :], wg, lru_b_gates, lru_lambda[None, :])

    kvc = _compress(proj3, *_compress_params(cmp_pos, cmp_w1, cmp_w2))
    y_b = _nsa(proj3, kvc, *tables)

    wa_pad = jnp.zeros((LANE, GLA_HEADS * GLA_DK), F32).at[SMALL_LR:SMALL_LR + GLA_RANK].set(gla_wa2).astype(BF16)
    y_c = _gla(proj3, wa_pad, gla_ba[None, :], gla_norm[None, :])

    x2 = _merge(y_a.reshape(B * S, D), y_b.reshape(B * S, D), y_c.reshape(B * S, D), proj, x2,
                w_branch.astype(BF16), w_out.astype(BF16), norm_g[1][None, :])
    x2 = _ffn(x2, norm_g[2][None, :], w_ffn_in.astype(BF16), w_ffn_out.astype(BF16), norm_g[3][None, :])
    return x2


def kernel(x, rel_table, norm_g, w_in, conv_w, conv_b, lru_w_gates, lru_b_gates, lru_lambda, cmp_pos, cmp_w1,
           cmp_w2, gla_wa2, gla_ba, gla_norm, w_branch, w_out, w_ffn_in, w_ffn_out):
    B, S, D = x.shape
    tables = _nsa_tables(rel_table, S)
    x2 = x.reshape(B * S, D)
    for l in range(norm_g.shape[0]):
        x2 = _layer(x2, B, S, tables, norm_g[l], w_in[l], conv_w[l], conv_b[l], lru_w_gates[l], lru_b_gates[l],
                    lru_lambda[l], cmp_pos[l], cmp_w1[l], cmp_w2[l], gla_wa2[l], gla_ba[l], gla_norm[l],
                    w_branch[l], w_out[l], w_ffn_in[l], w_ffn_out[l])
    return x2.reshape(B, S, D)
```

```python
import functools
import math

import jax
import jax.numpy as jnp
from jax import lax
from jax.experimental import pallas as pl
from jax.experimental.pallas import tpu as pltpu

F32 = jnp.float32
BF16 = jnp.bfloat16

D_MODEL = 1024
N_BRANCH = 3
EPS = 1e-6
NEG_INF = -1e30
FORCED = 1e4

LRU_BLOCKS = 8
LRU_BLOCK = D_MODEL // LRU_BLOCKS
CONV_WIDTH = 4
LRU_C = 8.0

NSA_HEADS = 16
NSA_G = 4
NSA_R = NSA_HEADS // NSA_G
NSA_HD = D_MODEL // NSA_HEADS
CMP_BLOCK = 32
CMP_STRIDE = 16
CMP_HIDDEN = 256
SLC_BLOCK = 64
SLC_TOP_N = 8
WINDOW = 256

GLA_HEADS = 4
GLA_DK = (D_MODEL // 2) // GLA_HEADS
GLA_DV = D_MODEL // GLA_HEADS
GLA_RANK = 16
GLA_TAU = 16.0
GLA_CHUNK = 32

REL_BUCKETS = 32
REL_MAX_EXACT = 16
REL_MAX_DIST = 128

D_FF = -(-8 * D_MODEL // (3 * 256)) * 256

IN_SIZES = (D_MODEL, D_MODEL, D_MODEL, 6 * NSA_G * NSA_HD, 3 * NSA_HEADS, D_MODEL // 2, D_MODEL // 2,
            D_MODEL, D_MODEL, GLA_RANK, N_BRANCH * D_MODEL)

LANE = 128
COL_LRU_X = 0
COL_LRU_G = 1024
COL_NSA_Q = 2048
COL_GLA_V = 3072
COL_GLA_OG = 4096
COL_MERGE = 5120
COL_NSA_KV = 8192
COL_GLA_Q = 9728
COL_GLA_K = 10240
COL_SMALL = 10752
PROJ_WIDTH = 10880
SMALL_GATE = 0
SMALL_LR = 3 * NSA_HEADS

Q_TILE = 256
K_TILE = 256
NSA_BATCH_ROWS = 2
VMEM_LIMIT = 56 * 1024 * 1024
_NT = (((1,), (1,)), ((), ()))


def _cparams(sem):
    return pltpu.CompilerParams(dimension_semantics=sem, vmem_limit_bytes=VMEM_LIMIT)


def _const_spec(shape):
    nd = len(shape)
    return pl.BlockSpec(shape, lambda *_: (0,) * nd, pipeline_mode=pl.Buffered(1))


def _rms(x, g):
    return x * lax.rsqrt(jnp.mean(x * x, axis=-1, keepdims=True) + EPS) * g


def _gelu_tanh(x):
    return 0.5 * x * (1.0 + jnp.tanh(math.sqrt(2.0 / math.pi) * (x + 0.044715 * (x * x * x))))


def _sigmoid(x):
    return 0.5 * jnp.tanh(0.5 * x) + 0.5


def _sqrt_nonneg(x):
    return jnp.where(x > 0.0, x * lax.rsqrt(x), 0.0)


def _softplus(z):
    return jnp.maximum(z, 0.0) + jnp.log1p(jnp.exp(-jnp.abs(z)))


def _in_proj_kernel(x_ref, g_ref, w_ref, o_ref, h_ref):
    @pl.when(pl.program_id(1) == 0)
    def _():
        h_ref[...] = _rms(x_ref[...], g_ref[...]).astype(BF16)

    o_ref[...] = jnp.dot(h_ref[...], w_ref[...], preferred_element_type=F32).astype(o_ref.dtype)


def _in_proj(x2, g, w, tm=1024, tn=2176):
    T, D = x2.shape
    N = w.shape[1]
    return pl.pallas_call(
        _in_proj_kernel,
        grid=(T // tm, N // tn),
        in_specs=[pl.BlockSpec((tm, D), lambda i, j: (i, 0)),
                  pl.BlockSpec((1, D), lambda i, j: (0, 0)),
                  pl.BlockSpec((D, tn), lambda i, j: (0, j))],
        out_specs=pl.BlockSpec((tm, tn), lambda i, j: (i, j)),
        out_shape=jax.ShapeDtypeStruct((T, N), BF16),
        scratch_shapes=[pltpu.VMEM((tm, D), BF16)],
        compiler_params=_cparams(("parallel", "arbitrary")),
        name="in_proj",
    )(x2, g, w)


def _lru_kernel(xa_ref, ga_ref, cw_ref, cb_ref, wg_ref, bg_ref, lam_ref, y_ref, tail_ref, h_ref, *, R):
    @pl.when(pl.program_id(1) == 0)
    def _():
        tail_ref[...] = jnp.zeros_like(tail_ref)
        h_ref[...] = jnp.zeros_like(h_ref)

    sp = _softplus(-lam_ref[...])
    groups = R // 8
    row8 = lax.broadcasted_iota(jnp.int32, (groups, 8, LRU_BLOCK), 1)
    for n in range(LRU_BLOCKS):
        sl = slice(n * LRU_BLOCK, (n + 1) * LRU_BLOCK)
        x3 = xa_ref[0, :, sl].astype(F32).reshape(groups, 8, LRU_BLOCK)
        tail = tail_ref[:, sl].reshape(1, 8, LRU_BLOCK)
        xc3 = cb_ref[:, sl] + cw_ref[CONV_WIDTH - 1:CONV_WIDTH, sl] * x3
        for s in range(1, CONV_WIDTH):
            rolled = pltpu.roll(x3, s, axis=1)
            prev = jnp.concatenate([pltpu.roll(tail, s, axis=1), rolled[:groups - 1]], axis=0)
            w = cw_ref[CONV_WIDTH - 1 - s:CONV_WIDTH - s, sl]
            xc3 = xc3 + w * jnp.where(row8 >= s, rolled, prev)
        tail_ref[:, sl] = x3[groups - 1]
        xcn = xc3.reshape(R, LRU_BLOCK)
        gz =jnp.dot(xcn.astype(BF16), wg_ref[n], preferred_element_type=F32)
        r = _sigmoid(gz[:, :LRU_BLOCK] + bg_ref[0:1, sl])
        i = _sigmoid(gz[:, LRU_BLOCK:] + bg_ref[1:2, sl])
        a = jnp.exp((-LRU_C) * r * sp[:, sl])
        u = _sqrt_nonneg(1.0 - a * a) * (i * xcn)
        a = a.reshape(R // 8, 8, LRU_BLOCK)
        u = u.reshape(R // 8, 8, LRU_BLOCK)
        for d in (1, 2, 4):
            keep = row8 >= d
            a_s = pltpu.roll(a, d, axis=1)
            u_s = pltpu.roll(u, d, axis=1)
            u = jnp.where(keep, a * u_s + u, u)
            a = jnp.where(keep, a * a_s, a)
        hprev = h_ref[:, sl]
        hs = []
        for t in range(R // 8):
            ht = a[t] * hprev + u[t]
            hprev = ht[7:8]
            hs.append(ht)
        h_ref[:, sl] = hprev
        h = jnp.concatenate(hs, axis=0)
        y_ref[0, :, sl] = (h * _gelu_tanh(ga_ref[0, :, sl].astype(F32))).astype(y_ref.dtype)


def _lru(proj3, conv_w, conv_b, wg, bg, lam, R=256):
    B, S, _ = proj3.shape
    D = D_MODEL
    return pl.pallas_call(
        functools.partial(_lru_kernel, R=R),
        grid=(B, S // R),
        in_specs=[pl.BlockSpec((1, R, D), lambda b, s: (b, s, COL_LRU_X // D)),
                  pl.BlockSpec((1, R, D), lambda b, s: (b, s, COL_LRU_G // D)),
                  _const_spec((CONV_WIDTH, D)),
                  _const_spec((1, D)),
                  _const_spec((LRU_BLOCKS, LRU_BLOCK, 2 * LRU_BLOCK)),
                  _const_spec((2, D)),
                  _const_spec((1, D))],
        out_specs=pl.BlockSpec((1, R, D), lambda b, s: (b, s, 0)),
        out_shape=jax.ShapeDtypeStruct((B, S, D), BF16),
        scratch_shapes=[pltpu.VMEM((8, D), F32), pltpu.VMEM((1, D), F32)],
        compiler_params=_cparams(("parallel", "arbitrary")),
        name="rglru",
    )(proj3, proj3, conv_w, conv_b, wg, bg, lam)


def _gla_kernel(q_ref, k_ref, v_ref, og_ref, sm_ref, wa_ref, ba_ref, gn_ref, y_ref, st_ref, *, Tb):
    nc = Tb // GLA_CHUNK

    @pl.when(pl.program_id(1) == 0)
    def _():
        st_ref[...] = jnp.zeros_like(st_ref)

    la_pre = jnp.dot(sm_ref[0], wa_ref[...], preferred_element_type=F32)
    rowc = lax.broadcasted_iota(jnp.int32, (Tb, GLA_DK), 0) & (GLA_CHUNK - 1)
    ri = lax.broadcasted_iota(jnp.int32, (Tb, Tb), 0)
    ci = lax.broadcasted_iota(jnp.int32, (Tb, Tb), 1)
    same_chunk_causal = ((ri // GLA_CHUNK) == (ci // GLA_CHUNK)) & (ci <= ri)
    lane_chunk = lax.broadcasted_iota(jnp.int32, (GLA_DK, Tb), 1) // GLA_CHUNK

    for h in range(GLA_HEADS):
        ks = slice(h * GLA_DK, (h + 1) * GLA_DK)
        vs = slice(h * GLA_DV, (h + 1) * GLA_DV)
        z = la_pre[:, ks] + ba_ref[:, ks]
        b = (jnp.minimum(z, 0.0) - jnp.log1p(jnp.exp(-jnp.abs(z)))) * (1.0 / GLA_TAU)
        d = 1
        while d < GLA_CHUNK:
            b = b + jnp.where(rowc >= d, pltpu.roll(b, d, axis=0), 0.0)
            d *= 2
        b_last = jnp.concatenate(
            [jnp.broadcast_to(b[c * GLA_CHUNK + GLA_CHUNK - 1:(c + 1) * GLA_CHUNK, :], (GLA_CHUNK, GLA_DK))
             for c in range(nc)], axis=0)
        q = q_ref[0, :, ks].astype(F32) * (GLA_DK ** -0.5)
        k = k_ref[0, :, ks].astype(F32)
        qt = (q * jnp.exp(b)).astype(BF16)
        kt = (k * jnp.exp(-b)).astype(BF16)
        ke_t = (k * jnp.exp(b_last - b)).T
        b_t = b.T
        v = v_ref[0, :, vs]

        att = lax.dot_general(qt, kt, _NT, preferred_element_type=F32)
        att = jnp.where(same_chunk_causal, att, 0.0).astype(BF16)
        o = jnp.dot(att, v, preferred_element_type=F32)

        ke_blocks = jnp.concatenate([jnp.where(lane_chunk == c, ke_t, 0.0) for c in range(nc)], axis=0)
        kv_all = jnp.dot(ke_blocks.astype(BF16), v, preferred_element_type=F32)
        state = st_ref[h]
        inter = []
        for c in range(nc):
            rs = slice(c * GLA_CHUNK, (c + 1) * GLA_CHUNK)
            inter.append(jnp.dot(qt[rs], state.astype(BF16), preferred_element_type=F32))
            last = c * GLA_CHUNK + GLA_CHUNK - 1
            state = state * jnp.exp(b_t[:, last:last + 1]) + kv_all[c * GLA_DK:(c + 1) * GLA_DK]
        st_ref[h] = state
        o = o + jnp.concatenate(inter, axis=0)

        on = o * lax.rsqrt(jnp.mean(o * o, axis=-1, keepdims=True) + EPS) * gn_ref[...]
        og = og_ref[0, :, vs].astype(F32)
        y_ref[0, :, vs] = (on * (og * jax.nn.sigmoid(og))).astype(y_ref.dtype)


def _gla(proj3, wa_pad, ba, gn, Tb=256):
    B, S, _ = proj3.shape
    HK = GLA_HEADS * GLA_DK
    D = D_MODEL
    return pl.pallas_call(
        functools.partial(_gla_kernel, Tb=Tb),
        grid=(B, S // Tb),
        in_specs=[pl.BlockSpec((1, Tb, HK), lambda b, s: (b, s, COL_GLA_Q // HK)),
                  pl.BlockSpec((1, Tb, HK), lambda b, s: (b, s, COL_GLA_K // HK)),
                  pl.BlockSpec((1, Tb, D), lambda b, s: (b, s, COL_GLA_V // D)),
                  pl.BlockSpec((1, Tb, D), lambda b, s: (b, s, COL_GLA_OG // D)),
                  pl.BlockSpec((1, Tb, LANE), lambda b, s: (b, s, COL_SMALL // LANE)),
                  _const_spec((LANE, HK)),
                  _const_spec((1, HK)),
                  _const_spec((1, GLA_DV))],
        out_specs=pl.BlockSpec((1, Tb, D), lambda b, s: (b, s, 0)),
        out_shape=jax.ShapeDtypeStruct((B, S, D), BF16),
        scratch_shapes=[pltpu.VMEM((GLA_HEADS, GLA_DK, GLA_DV), F32)],
        compiler_params=_cparams(("parallel", "arbitrary")),
        name="gla",
    )(proj3, proj3, proj3, proj3, proj3, wa_pad, ba, gn)


def _cmp_kernel(kv_ref, pos_ref, w1_ref, w2_ref, o_ref, x_ref):
    x_ref[...] = kv_ref[0].astype(F32)
    nrow = x_ref.shape[0] // CMP_STRIDE
    first = None
    second = None
    for i in range(CMP_STRIDE):
        xi = x_ref[pl.ds(i, nrow, stride=CMP_STRIDE), :]
        f = jnp.dot((xi + pos_ref[0, i]).astype(BF16), w1_ref[0, i], preferred_element_type=F32)
        s = jnp.dot((xi + pos_ref[1, i]).astype(BF16), w1_ref[1, i], preferred_element_type=F32)
        first = f if first is None else first + f
        second = s if second is None else second + s
    pre = first + pltpu.roll(second, nrow - 1, axis=0)
    hid = _gelu_tanh(pre).astype(BF16)
    o_ref[0, 0] = jnp.dot(hid, w2_ref[...], preferred_element_type=F32).astype(o_ref.dtype)


def _compress(proj3, pos, w1, w2):
    B, S, _ = proj3.shape
    nrow = S // CMP_STRIDE
    kvb = COL_NSA_KV // LANE
    return pl.pallas_call(
        _cmp_kernel,
        grid=(B, NSA_G),
        in_specs=[pl.BlockSpec((1, S, LANE), lambda b, g: (b, 0, kvb + g)),
                  _const_spec(pos.shape), _const_spec(w1.shape), _const_spec(w2.shape)],
        out_specs=pl.BlockSpec((1, 1, nrow, LANE), lambda b, g: (b, g, 0, 0)),
        out_shape=jax.ShapeDtypeStruct((B, NSA_G, nrow, LANE), BF16),
        scratch_shapes=[pltpu.VMEM((S, LANE), F32)],
        compiler_params=_cparams(("parallel", "parallel")),
        name="nsa_compress",
    )(proj3, pos, w1, w2)


def _compress_params(cmp_pos, cmp_w1, cmp_w2):
    hd = NSA_HD
    w1 = cmp_w1.reshape(2, 2, CMP_STRIDE, hd, CMP_HIDDEN)
    z = jnp.zeros_like(w1[0])
    w1 = jnp.concatenate([jnp.concatenate([w1[0], z], axis=-1),
                          jnp.concatenate([z, w1[1]], axis=-1)], axis=-2)
    pos = cmp_pos.reshape(2, 2, CMP_STRIDE, 1, hd)
    pos = jnp.concatenate([pos[0], pos[1]], axis=-1)
    z2 = jnp.zeros_like(cmp_w2[0])
    w2 = jnp.concatenate([jnp.concatenate([cmp_w2[0], z2], axis=-1),
                          jnp.concatenate([z2, cmp_w2[1]], axis=-1)], axis=0)
    return pos, w1.astype(BF16), w2.astype(BF16)


V_ROWS = NSA_HD + 16
MASK_BIG = 1e30
LOG2_E = 1.4426950408889634


def _nsa_kernel(q_ref, kvc_ref, kvs_ref, kvw_ref, gate_ref, bc_ref, bs_ref, bw_ref, xtra_ref, ovl_ref,
                gsel_ref, y_ref, a_ref, b_ref, vs_ref, vw_ref, vc_ref, lg_ref, p_ref):
    TQ = Q_TILE
    HW = NSA_R * TQ
    qi = pl.program_id(2)
    S = kvs_ref.shape[1]
    n_cmp_rows = kvc_ref.shape[2]

    def value_rows(tile):
        t = tile.astype(F32).T
        return jnp.concatenate([t[NSA_HD:], jnp.ones((V_ROWS - NSA_HD, t.shape[1]), F32)], axis=0).astype(BF16)

    @pl.when(qi == 0)
    def _():
        for j in range(S // LANE):
            rows = slice(j * LANE, (j + 1) * LANE)
            ks = kvs_ref[0, rows, :]
            a_ref[rows, 0:LANE] = ks
            a_ref[rows, LANE:2 * LANE] = xtra_ref[rows, :]
            vs_ref[:, rows] = value_rows(ks)
            vw_ref[:, rows] = value_rows(kvw_ref[0, rows, :])
        vc_ref[...] = value_rows(kvc_ref[0, 0])

    KT = K_TILE
    RT = TQ // KT

    def tile_rows(j, n=1):
        return pl.ds(pl.multiple_of(j * KT, KT), n * KT)

    qf = q_ref[0].astype(F32) * (NSA_HD ** -0.5)
    zero_rows = jnp.zeros((LANE - NSA_HD, TQ), F32)
    heads = []
    for pr in range(NSA_R // 2):
        t = qf[:, pr * LANE:(pr + 1) * LANE].T
        heads.append(jnp.concatenate([t[:NSA_HD], zero_rows], axis=0))
        heads.append(jnp.concatenate([t[NSA_HD:], zero_rows], axis=0))
    qt32 = jnp.concatenate(heads, axis=1)
    qt = qt32.astype(BF16)
    qt32_l2 = qt32 * LOG2_E
    qt_l2 = qt32_l2.astype(BF16)

    def normalised(acc):
        return acc[:NSA_HD] * (1.0 / acc[NSA_HD:NSA_HD + 1])

    first = (qi == 0).astype(jnp.int32)
    rows_w = tile_rows(jnp.maximum(RT * qi - 1, 0), RT + 1)
    lg = jnp.dot(kvw_ref[0, rows_w, :], qt_l2, preferred_element_type=F32).astype(BF16)
    lg = lg + jnp.concatenate([bw_ref[0, t + first] for t in range(RT + 1)], axis=0)
    pw = jnp.exp2(lg - jnp.max(lg, axis=0, keepdims=True))
    o_win = normalised(jnp.dot(vw_ref[:, rows_w], pw, preferred_element_type=F32))

    kvc = kvc_ref[0, 0]
    cq = TQ // CMP_STRIDE
    start_c = pl.multiple_of(bc_ref.shape[1] - n_cmp_rows - cq * qi, 8)
    bias_c = bc_ref[0, pl.ds(start_c, n_cmp_rows), :]
    lg = jnp.dot(kvc, qt, preferred_element_type=F32) + bias_c
    e = jnp.exp(lg - jnp.max(lg, axis=0, keepdims=True))
    p = e * (1.0 / jnp.sum(e, axis=0, keepdims=True))
    p = jnp.where(bias_c > 0.5 * NEG_INF, p, 0.0)
    o_cmp = jnp.dot(vc_ref[...], p.astype(BF16), preferred_element_type=F32)[:NSA_HD]

    psum = p[:, 0:TQ]
    for r in range(1, NSA_R):
        psum = psum + p[:, r * TQ:(r + 1) * TQ]
    p_hi = psum.astype(BF16)
    p_lo = (psum - p_hi.astype(F32)).astype(BF16)
    ovl = ovl_ref[...]
    imp = jnp.dot(ovl, p_hi, preferred_element_type=F32) + jnp.dot(ovl, p_lo, preferred_element_type=F32)
    ns = imp.shape[0]
    blk = lax.broadcasted_iota(jnp.int32, (ns, TQ), 0)
    cur = (qi * TQ + lax.broadcasted_iota(jnp.int32, (ns, TQ), 1)) // SLC_BLOCK
    forced = (blk == 0) | (blk == cur) | (blk == cur - 1)
    score = jnp.where(forced, FORCED, jnp.where(blk <= cur, imp, -FORCED))
    rank = jnp.zeros((ns, TQ), F32)
    for j in range(ns):
        sj = score[j:j + 1, :]
        tie = jnp.where(blk > j, 1.0, 0.0)
        rank = rank + jnp.where(sj > score, 1.0, jnp.where(sj == score, tie, 0.0))
    neg_sel = jnp.where(rank < float(SLC_TOP_N), 0.0, -MASK_BIG)
    b_ref[...] = jnp.concatenate([qt32_l2, jnp.concatenate([neg_sel] * NSA_R, axis=1),
                                  jnp.zeros((LANE - ns, HW), F32)], axis=0).astype(BF16)

    def logits(j):
        return jnp.dot(a_ref[tile_rows(j), :], b_ref[...], preferred_element_type=F32).astype(BF16)

    lg_ref[...] = logits(0)
    p_ref[...] = jnp.zeros(p_ref.shape, BF16)

    last = RT * qi + RT - 1

    def trip(j, carry):
        m, acc, alpha = carry
        lg_next = logits(jnp.minimum(j + 1, last))
        acc = acc * alpha + jnp.dot(vs_ref[:, tile_rows(jnp.maximum(j - 1, 0))], p_ref[...],
                                    preferred_element_type=F32)
        lg = lg_ref[...] + bs_ref[0, jnp.where(j >= RT * qi - 1, j - RT * qi + 1, RT + 1)]
        m_new = jnp.maximum(m, jnp.max(lg, axis=0, keepdims=True).astype(F32))
        p_ref[...] = jnp.exp2(lg - m_new.astype(BF16))
        lg_ref[...] = lg_next
        return m_new, acc, jnp.exp2(m - m_new)

    init = (jnp.full((1, HW), NEG_INF, F32), jnp.zeros((V_ROWS, HW), F32), jnp.ones((1, HW), F32))
    _, acc, alpha = lax.fori_loop(0, last + 1, trip, init)
    acc = acc * alpha + jnp.dot(vs_ref[:, tile_rows(last)], p_ref[...], preferred_element_type=F32)
    o_slc = normalised(acc)

    gs = jax.nn.sigmoid(lax.dot_general(gsel_ref[0], gate_ref[0], _NT, preferred_element_type=F32))

    def gate_row(br):
        return jnp.concatenate([gs[br * NSA_R + r:br * NSA_R + r + 1] for r in range(NSA_R)], axis=1)

    o = gate_row(0) * o_cmp + gate_row(1) * o_slc + gate_row(2) * o_win
    o = jnp.concatenate([o[:, r * TQ:(r + 1) * TQ] for r in range(NSA_R)], axis=0)
    y_ref[0] = o.T.astype(y_ref.dtype)


def _nsa_multi_kernel(q_ref, kvc_ref, kvs_ref, kvw_ref, gate_ref, bc_ref, bs_ref, bw_ref, xtra_ref, ovl_ref,
                      gsel_ref, y_ref, a_ref, b_ref, vs_ref, vw_ref, vc_ref, lg_ref, p_ref):
    NB = q_ref.shape[0]
    rows_nb = range(NB)
    TQ = Q_TILE
    KT = K_TILE
    RT = TQ // KT
    HW = NSA_R * TQ
    qi = pl.program_id(2)
    S = kvs_ref.shape[1]
    n_cmp_rows = kvc_ref.shape[2]

    def value_rows(tile):
        t = tile.astype(F32).T
        return jnp.concatenate([t[NSA_HD:], jnp.ones((V_ROWS - NSA_HD, t.shape[1]), F32)], axis=0).astype(BF16)

    @pl.when(qi == 0)
    def _():
        for nb in rows_nb:
            for j in range(S // LANE):
                rows = slice(j * LANE, (j + 1) * LANE)
                ks = kvs_ref[nb, rows, :]
                a_ref[nb, rows, 0:LANE] = ks
                a_ref[nb, rows, LANE:2 * LANE] = xtra_ref[rows, :]
                vs_ref[nb, :, rows] = value_rows(ks)
                vw_ref[nb, :, rows] = value_rows(kvw_ref[nb, rows, :])
            vc_ref[nb] = value_rows(kvc_ref[nb, 0])

    def tile_rows(j, n=1):
        return pl.ds(pl.multiple_of(j * KT, KT), n * KT)

    def normalised(acc):
        return acc[:NSA_HD] * (1.0 / acc[NSA_HD:NSA_HD + 1])

    def q_transposed(nb):
        qf = q_ref[nb].astype(F32) * (NSA_HD ** -0.5)
        zero_rows = jnp.zeros((LANE - NSA_HD, TQ), F32)
        heads = []
        for pr in range(NSA_R // 2):
            t = qf[:, pr * LANE:(pr + 1) * LANE].T
            heads.append(jnp.concatenate([t[:NSA_HD], zero_rows], axis=0))
            heads.append(jnp.concatenate([t[NSA_HD:], zero_rows], axis=0))
        return jnp.concatenate(heads, axis=1)

    qt32 = [q_transposed(nb) for nb in rows_nb]
    qt = [x.astype(BF16) for x in qt32]
    qt32_l2 = [x * LOG2_E for x in qt32]
    qt_l2 = [x.astype(BF16) for x in qt32_l2]

    first = (qi == 0).astype(jnp.int32)
    rows_w = tile_rows(jnp.maximum(RT * qi - 1, 0), RT + 1)
    bias_w = jnp.concatenate([bw_ref[0, t + first] for t in range(RT + 1)], axis=0)
    lg_w = [jnp.dot(kvw_ref[nb, rows_w, :], qt_l2[nb], preferred_element_type=F32).astype(BF16) + bias_w
            for nb in rows_nb]
    p_w = [jnp.exp2(lg - jnp.max(lg, axis=0, keepdims=True)) for lg in lg_w]
    o_win = [normalised(jnp.dot(vw_ref[nb, :, rows_w], p_w[nb], preferred_element_type=F32)) for nb in rows_nb]

    cq = TQ // CMP_STRIDE
    start_c = pl.multiple_of(bc_ref.shape[1] - n_cmp_rows - cq * qi, 8)
    bias_c = bc_ref[0, pl.ds(start_c, n_cmp_rows), :]
    valid_c = bias_c > 0.5 * NEG_INF
    lg_c = [jnp.dot(kvc_ref[nb, 0], qt[nb], preferred_element_type=F32) + bias_c for nb in rows_nb]
    e_c = [jnp.exp(lg - jnp.max(lg, axis=0, keepdims=True)) for lg in lg_c]
    p_c = [jnp.where(valid_c, e * (1.0 / jnp.sum(e, axis=0, keepdims=True)), 0.0) for e in e_c]
    o_cmp = [jnp.dot(vc_ref[nb], p_c[nb].astype(BF16), preferred_element_type=F32)[:NSA_HD] for nb in rows_nb]

    ovl = ovl_ref[...]
    ns = ovl.shape[0]
    blk = lax.broadcasted_iota(jnp.int32, (ns, TQ), 0)
    cur = (qi * TQ + lax.broadcasted_iota(jnp.int32, (ns, TQ), 1)) // SLC_BLOCK
    forced = (blk == 0) | (blk == cur) | (blk == cur - 1)
    allowed = blk <= cur

    def importance(p):
        psum = p[:, 0:TQ]
        for r in range(1, NSA_R):
            psum = psum + p[:, r * TQ:(r + 1) * TQ]
        p_hi = psum.astype(BF16)
        p_lo = (psum - p_hi.astype(F32)).astype(BF16)
        imp = jnp.dot(ovl, p_hi, preferred_element_type=F32) + jnp.dot(ovl, p_lo, preferred_element_type=F32)
        return jnp.where(forced, FORCED, jnp.where(allowed, imp, -FORCED))

    score = [importance(p) for p in p_c]
    rank = [jnp.zeros((ns, TQ), F32) for _ in rows_nb]
    for j in range(ns):
        tie = jnp.where(blk > j, 1.0, 0.0)
        for nb in rows_nb:
            sj = score[nb][j:j + 1, :]
            rank[nb] = rank[nb] + jnp.where(sj > score[nb], 1.0, jnp.where(sj == score[nb], tie, 0.0))
    for nb in rows_nb:
        neg_sel = jnp.where(rank[nb] < float(SLC_TOP_N), 0.0, -MASK_BIG)
        b_ref[nb] = jnp.concatenate([qt32_l2[nb], jnp.concatenate([neg_sel] * NSA_R, axis=1),
                                     jnp.zeros((LANE - ns, HW), F32)], axis=0).astype(BF16)

    def logits(nb, j):
        return jnp.dot(a_ref[nb, tile_rows(j), :], b_ref[nb], preferred_element_type=F32).astype(BF16)

    for nb in rows_nb:
        lg_ref[nb] = logits(nb, 0)
        p_ref[nb] = jnp.zeros(p_ref.shape[1:], BF16)

    last = RT * qi + RT - 1

    def trip(j, carry):
        lg_next = [logits(nb, jnp.minimum(j + 1, last)) for nb in rows_nb]
        prev_rows = tile_rows(jnp.maximum(j - 1, 0))
        bias = bs_ref[0, jnp.where(j >= RT * qi - 1, j - RT * qi + 1, RT + 1)]
        out = []
        for nb in rows_nb:
            m, acc, alpha = carry[nb]
            acc = acc * alpha + jnp.dot(vs_ref[nb, :, prev_rows], p_ref[nb], preferred_element_type=F32)
            lg = lg_ref[nb] + bias
            m_new = jnp.maximum(m, jnp.max(lg, axis=0, keepdims=True).astype(F32))
            p_ref[nb] = jnp.exp2(lg - m_new.astype(BF16))
            out.append((m_new, acc, jnp.exp2(m - m_new)))
        for nb in rows_nb:
            lg_ref[nb] = lg_next[nb]
        return tuple(out)

    init = (jnp.full((1, HW), NEG_INF, F32), jnp.zeros((V_ROWS, HW), F32), jnp.ones((1, HW), F32))
    carry = lax.fori_loop(0, last + 1, trip, tuple(init for _ in rows_nb))
    o_slc = [normalised(carry[nb][1] * carry[nb][2]
                        + jnp.dot(vs_ref[nb, :, tile_rows(last)], p_ref[nb], preferred_element_type=F32))
             for nb in rows_nb]

    for nb in rows_nb:
        gs = jax.nn.sigmoid(lax.dot_general(gsel_ref[0], gate_ref[nb], _NT, preferred_element_type=F32))

        def gate_row(br):
            return jnp.concatenate([gs[br * NSA_R + r:br * NSA_R + r + 1] for r in range(NSA_R)], axis=1)

        o = gate_row(0) * o_cmp[nb] + gate_row(1) * o_slc[nb] + gate_row(2) * o_win[nb]
        o = jnp.concatenate([o[:, r * TQ:(r + 1) * TQ] for r in range(NSA_R)], axis=0)
        y_ref[nb] = o.T.astype(y_ref.dtype)


def _nsa(proj3, kvc, bias_c, bias_s, bias_w, xtra, ovl_t, gsel):
    B, S, _ = proj3.shape
    TQ = Q_TILE
    G = NSA_G
    RW = NSA_R * NSA_HD
    HW = NSA_R * TQ
    kvb = COL_NSA_KV // LANE
    ncr = kvc.shape[2]
    KT = K_TILE
    NB = NSA_BATCH_ROWS
    assert S % TQ == 0 and TQ % KT == 0 and S >= TQ + KT and WINDOW == KT and ncr % LANE == 0 and B % NB == 0
    return pl.pallas_call(
        _nsa_multi_kernel,
        grid=(G, B // NB, S // TQ),
        in_specs=[pl.BlockSpec((NB, TQ, RW), lambda g, b, i: (b, i, COL_NSA_Q // RW + g)),
                  pl.BlockSpec((NB, 1, ncr, LANE), lambda g, b, i: (b, g, 0, 0)),
                  pl.BlockSpec((NB, S, LANE), lambda g, b, i: (b, 0, kvb + G + g)),
                  pl.BlockSpec((NB, S, LANE), lambda g, b, i: (b, 0, kvb + 2 * G + g)),
                  pl.BlockSpec((NB, TQ, LANE), lambda g, b, i: (b, i, COL_SMALL // LANE)),
                  pl.BlockSpec((1,) + bias_c.shape[1:], lambda g, b, i: (g, 0, 0)),
                  pl.BlockSpec((1,) + bias_s.shape[1:], lambda g, b, i: (g, 0, 0, 0)),
                  pl.BlockSpec((1,) + bias_w.shape[1:], lambda g, b, i: (g, 0, 0, 0)),
                  pl.BlockSpec(xtra.shape, lambda g, b, i: (0, 0)),
                  pl.BlockSpec(ovl_t.shape, lambda g, b, i: (0, 0)),
                  pl.BlockSpec((1,) + gsel.shape[1:], lambda g, b, i: (g, 0, 0))],
        out_specs=pl.BlockSpec((NB, TQ, RW), lambda g, b, i: (b, i, g)),
        out_shape=jax.ShapeDtypeStruct((B, S, D_MODEL), BF16),
        scratch_shapes=[pltpu.VMEM((NB, S, 2 * LANE), BF16), pltpu.VMEM((NB, 2 * LANE, HW), BF16),
                        pltpu.VMEM((NB, V_ROWS, S), BF16), pltpu.VMEM((NB, V_ROWS, S), BF16),
                        pltpu.VMEM((NB, V_ROWS, ncr), BF16), pltpu.VMEM((NB, KT, HW), BF16),
                        pltpu.VMEM((NB, KT, HW), BF16)],
        compiler_params=_cparams(("parallel", "parallel", "arbitrary")),
        name="nsa_attention",
    )(proj3, kvc, proj3, proj3, proj3, bias_c, bias_s, bias_w, xtra, ovl_t, gsel)


def _rel_bucket(dist):
    n = jnp.maximum(dist, 0)
    nf = jnp.maximum(n, REL_MAX_EXACT).astype(F32)
    large = REL_MAX_EXACT + (jnp.log(nf / REL_MAX_EXACT) / math.log(REL_MAX_DIST / REL_MAX_EXACT)
                             * (REL_BUCKETS - REL_MAX_EXACT)).astype(jnp.int32)
    large = jnp.minimum(large, REL_BUCKETS - 1)
    return jnp.where(n < REL_MAX_EXACT, n, large)


def _nsa_tables(rel_table, S):
    TQ = Q_TILE
    tbl = rel_table.astype(F32).reshape(REL_BUCKETS, NSA_G, NSA_R)
    tbl = tbl - tbl[REL_BUCKETS - 1]

    def table(dist, ok):
        onehot = jax.nn.one_hot(_rel_bucket(dist), REL_BUCKETS, dtype=F32)
        b = jnp.einsum("...kqn,ngr->g...krq", onehot, tbl, precision=lax.Precision.HIGHEST)
        b = jnp.where(ok[..., :, None, :], b, NEG_INF)
        return b.reshape(b.shape[:-2] + (NSA_R * dist.shape[-1],))

    KT = K_TILE
    kk = jnp.arange(KT)[:, None]
    qq = jnp.arange(TQ)[None, :]
    dist = qq - kk - KT * (jnp.arange(TQ // KT + 1)[:, None, None] - 1)
    bias_s = table(dist, dist >= 0) * LOG2_E
    bias_s = jnp.concatenate([bias_s, jnp.zeros_like(bias_s[:, :1])], axis=1).astype(BF16)
    bias_w = table(dist, (dist >= 0) & (dist < WINDOW)) * LOG2_E
    bias_w = jnp.concatenate([bias_w, jnp.full_like(bias_w[:, :1], NEG_INF)], axis=1).astype(BF16)
    n_rows = S // CMP_STRIDE
    off = n_rows - TQ // CMP_STRIDE
    c_rel = jnp.arange(off + n_rows)[:, None] - off
    dist_c = qq - CMP_STRIDE * c_rel - (CMP_BLOCK - 1)
    bias_c = table(dist_c, dist_c >= 0)

    n_slc = S // SLC_BLOCK
    n_cmp = n_rows - CMP_BLOCK // CMP_STRIDE + 1
    xtra = (jnp.arange(LANE)[None, :] == (jnp.arange(S)[:, None] // SLC_BLOCK)).astype(BF16)
    cmp_start = jnp.arange(n_rows) * CMP_STRIDE
    slc_start = jnp.arange(n_slc) * SLC_BLOCK
    overlap = jnp.clip(jnp.minimum(cmp_start[:, None] + CMP_BLOCK, slc_start[None, :] + SLC_BLOCK)
                       - jnp.maximum(cmp_start[:, None], slc_start[None, :]), 0).astype(F32) / CMP_BLOCK
    overlap = jnp.where(jnp.arange(n_rows)[:, None] < n_cmp, overlap, 0.0)
    g = jnp.arange(NSA_G)[:, None, None]
    row = jnp.arange(16)[None, :, None]
    col = jnp.arange(LANE)[None, None, :]
    src = SMALL_GATE + (row // NSA_R) * NSA_HEADS + g * NSA_R + row % NSA_R
    gsel = ((col == src) & (row < 3 * NSA_R)).astype(BF16)
    return bias_c, bias_s, bias_w, xtra, overlap.T.astype(BF16), gsel


def _merge_kernel(ya_ref, yb_ref, yc_ref, g0_ref, g1_ref, g2_ref, x_ref, wb_ref, wo_ref, gn_ref, o_ref):
    m = None
    for br, (y_ref, g_ref) in enumerate(((ya_ref, g0_ref), (yb_ref, g1_ref), (yc_ref, g2_ref))):
        t = jax.nn.sigmoid(g_ref[...].astype(F32)) * jnp.dot(y_ref[...], wb_ref[br], preferred_element_type=F32)
        m = t if m is None else m + t
    z = jnp.dot(m.astype(BF16), wo_ref[...], preferred_element_type=F32)
    o_ref[...] = x_ref[...] + _rms(z, gn_ref[...])


def _merge(ya, yb, yc, proj, x2, wb, wo, gn, tm=512):
    T, D = x2.shape
    row = lambda i: (i, 0)
    mcol = COL_MERGE // D
    return pl.pallas_call(
        _merge_kernel,
        grid=(T // tm,),
        in_specs=[pl.BlockSpec((tm, D), row), pl.BlockSpec((tm, D), row), pl.BlockSpec((tm, D), row),
                  pl.BlockSpec((tm, D), lambda i: (i, mcol)),
                  pl.BlockSpec((tm, D), lambda i: (i, mcol + 1)),
                  pl.BlockSpec((tm, D), lambda i: (i, mcol + 2)),
                  pl.BlockSpec((tm, D), row),
                  _const_spec(wb.shape), _const_spec(wo.shape), _const_spec((1, D))],
        out_specs=pl.BlockSpec((tm, D), row),
        out_shape=jax.ShapeDtypeStruct((T, D), F32),
        compiler_params=_cparams(("parallel",)),
        name="merge_out",
    )(ya, yb, yc, proj, proj, proj, x2, wb, wo, gn)


def _ffn_kernel(x_ref, gpre_ref, win_ref, wout_ref, gpost_ref, o_ref, *, chunk):
    x = x_ref[...]
    h = _rms(x, gpre_ref[...]).astype(BF16)
    acc = None
    for c in range(D_FF // chunk):
        gt = jnp.dot(h, win_ref[:, c * chunk:(c + 1) * chunk], preferred_element_type=F32)
        up = jnp.dot(h, win_ref[:, D_FF + c * chunk:D_FF + (c + 1) * chunk], preferred_element_type=F32)
        a = (gt * jax.nn.sigmoid(gt) * up).astype(BF16)
        t = jnp.dot(a, wout_ref[c * chunk:(c + 1) * chunk, :], preferred_element_type=F32)
        acc = t if acc is None else acc + t
    o_ref[...] = x + _rms(acc, gpost_ref[...])


def _ffn(x2, gpre, win, wout, gpost, tm=512, chunk=256):
    T, D = x2.shape
    row = lambda i: (i, 0)
    return pl.pallas_call(
        functools.partial(_ffn_kernel, chunk=chunk),
        grid=(T // tm,),
        in_specs=[pl.BlockSpec((tm, D), row), _const_spec((1, D)), _const_spec(win.shape),
                  _const_spec(wout.shape), _const_spec((1, D))],
        out_specs=pl.BlockSpec((tm, D), row),
        out_shape=jax.ShapeDtypeStruct((T, D), F32),
        compiler_params=_cparams(("parallel",)),
        name="ffn",
    )(x2, gpre, win, wout, gpost)


def _permute_in_columns(w):
    cuts = []
    acc = 0
    for size in IN_SIZES[:-1]:
        acc += size
        cuts.append(acc)
    (lru_x, lru_g, nsa_q, nsa_kv, nsa_gate, gla_q, gla_k, gla_v, gla_og, gla_lr, merge) = jnp.split(w, cuts, axis=-1)
    lead = w.shape[:-1]
    kv = nsa_kv.reshape(lead + (3, 2, NSA_G, NSA_HD))
    kv = jnp.moveaxis(kv, -3, -2).reshape(lead + (6 * NSA_G * NSA_HD,))
    pad = jnp.zeros(lead + (LANE - nsa_gate.shape[-1] - gla_lr.shape[-1],), w.dtype)
    return jnp.concatenate([lru_x, lru_g, nsa_q, gla_v, gla_og, merge, kv, gla_q, gla_k, nsa_gate, gla_lr, pad],
                           axis=-1)


def _layer(x2, B, S, tables, norm_g, w_in, conv_w, conv_b, lru_w_gates, lru_b_gates, lru_lambda,
           cmp_pos, cmp_w1, cmp_w2, gla_wa2, gla_ba, gla_norm, w_branch, w_out, w_ffn_in, w_ffn_out):
    D = D_MODEL
    w_perm = _permute_in_columns(w_in.astype(BF16))
    proj = _in_proj(x2, norm_g[0][None, :], w_perm)
    proj3 = proj.reshape(B, S, PROJ_WIDTH)

    wg = jnp.concatenate([lru_w_gates[0], lru_w_gates[1]], axis=-1).astype(BF16)
    y_a = _lru(proj3, conv_w, conv_b[None, :], wg, lru_b_gates, lru_lambda[None, :])

    kvc = _compress(proj3, *_compress_params(cmp_pos, cmp_w1, cmp_w2))
    y_b = _nsa(proj3, kvc, *tables)

    wa_pad = jnp.zeros((LANE, GLA_HEADS * GLA_DK), F32).at[SMALL_LR:SMALL_LR + GLA_RANK].set(gla_wa2).astype(BF16)
    y_c = _gla(proj3, wa_pad, gla_ba[None, :], gla_norm[None, :])

    x2 = _merge(y_a.reshape(B * S, D), y_b.reshape(B * S, D), y_c.reshape(B * S, D), proj, x2,
                w_branch.astype(BF16), w_out.astype(BF16), norm_g[1][None, :])
    x2 = _ffn(x2, norm_g[2][None, :], w_ffn_in.astype(BF16), w_ffn_out.astype(BF16), norm_g[3][None, :])
    return x2


def kernel(x, rel_table, norm_g, w_in, conv_w, conv_b, lru_w_gates, lru_b_gates, lru_lambda, cmp_pos, cmp_w1,
           cmp_w2, gla_wa2, gla_ba, gla_norm, w_branch, w_out, w_ffn_in, w_ffn_out):
    B, S, D = x.shape
    tables = _nsa_tables(rel_table, S)
    x2 = x.reshape(B * S, D)
    for l in range(norm_g.shape[0]):
        x2 = _layer(x2, B, S, tables, norm_g[l], w_in[l], conv_w[l], conv_b[l], lru_w_gates[l], lru_b_gates[l],
                    lru_lambda[l], cmp_pos[l], cmp_w1[l], cmp_w2[l], gla_wa2[l], gla_ba[l], gla_norm[l],
                    w_branch[l], w_out[l], w_ffn_in[l], w_ffn_out[l])
    return x2.reshape(B, S, D)
```

```python
import functools
import math

import jax
import jax.numpy as jnp
from jax import lax
from jax.experimental import pallas as pl
from jax.experimental.pallas import tpu as pltpu

F32 = jnp.float32
BF16 = jnp.bfloat16

D_MODEL = 1024
N_BRANCH = 3
EPS = 1e-6
NEG_INF = -1e30
FORCED = 1e4

LRU_BLOCKS = 8
LRU_BLOCK = D_MODEL // LRU_BLOCKS
CONV_WIDTH = 4
LRU_C = 8.0

NSA_HEADS = 16
NSA_G = 4
NSA_R = NSA_HEADS // NSA_G
NSA_HD = D_MODEL // NSA_HEADS
CMP_BLOCK = 32
CMP_STRIDE = 16
CMP_HIDDEN = 256
SLC_BLOCK = 64
SLC_TOP_N = 8
WINDOW = 256

GLA_HEADS = 4
GLA_DK = (D_MODEL // 2) // GLA_HEADS
GLA_DV = D_MODEL // GLA_HEADS
GLA_RANK = 16
GLA_TAU = 16.0
GLA_CHUNK = 32

REL_BUCKETS = 32
REL_MAX_EXACT = 16
REL_MAX_DIST = 128

D_FF = -(-8 * D_MODEL // (3 * 256)) * 256

IN_SIZES = (D_MODEL, D_MODEL, D_MODEL, 6 * NSA_G * NSA_HD, 3 * NSA_HEADS, D_MODEL // 2, D_MODEL // 2,
            D_MODEL, D_MODEL, GLA_RANK, N_BRANCH * D_MODEL)

LANE = 128
COL_LRU_X = 0
COL_LRU_G = 1024
COL_NSA_Q = 2048
COL_GLA_V = 3072
COL_GLA_OG = 4096
COL_MERGE = 5120
COL_NSA_KV = 8192
COL_GLA_Q = 9728
COL_GLA_K = 10240
COL_SMALL = 10752
PROJ_WIDTH = 10880
SMALL_GATE = 0
SMALL_LR = 3 * NSA_HEADS

Q_TILE = 256
K_TILE = 256
NSA_BATCH_ROWS = 4
VMEM_LIMIT = 56 * 1024 * 1024
_NT = (((1,), (1,)), ((), ()))


def _cparams(sem):
    return pltpu.CompilerParams(dimension_semantics=sem, vmem_limit_bytes=VMEM_LIMIT)


def _const_spec(shape, layer=None):
    nd = len(shape)
    if layer is None:
        return pl.BlockSpec(shape, lambda *_: (0,) * nd, pipeline_mode=pl.Buffered(1))
    return pl.BlockSpec((None,) + tuple(shape), lambda *_: (layer,) + (0,) * nd, pipeline_mode=pl.Buffered(1))


def _rms(x, g):
    return x * lax.rsqrt(jnp.mean(x * x, axis=-1, keepdims=True) + EPS) * g


def _gelu_tanh(x):
    return 0.5 * x * (1.0 + jnp.tanh(math.sqrt(2.0 / math.pi) * (x + 0.044715 * (x * x * x))))


def _sigmoid(x):
    return 0.5 * jnp.tanh(0.5 * x) + 0.5


def _sqrt_nonneg(x):
    return jnp.where(x > 0.0, x * lax.rsqrt(x), 0.0)


def _softplus(z):
    return jnp.maximum(z, 0.0) + jnp.log1p(jnp.exp(-jnp.abs(z)))


def _in_proj_kernel(x_ref, g_ref, w_ref, o_ref, h_ref):
    @pl.when(pl.program_id(1) == 0)
    def _():
        h_ref[...] = _rms(x_ref[...], g_ref[...]).astype(BF16)

    o_ref[...] = jnp.dot(h_ref[...], w_ref[...], preferred_element_type=F32).astype(o_ref.dtype)


def _in_proj(x2, g, w, layer, tm=1024, tn=2176):
    T, D = x2.shape
    N = w.shape[2]
    return pl.pallas_call(
        _in_proj_kernel,
        grid=(T // tm, N // tn),
        in_specs=[pl.BlockSpec((tm, D), lambda i, j: (i, 0)),
                  pl.BlockSpec((1, D), lambda i, j: (0, 0)),
                  pl.BlockSpec((None, D, tn), lambda i, j: (layer, 0, j))],
        out_specs=pl.BlockSpec((tm, tn), lambda i, j: (i, j)),
        out_shape=jax.ShapeDtypeStruct((T, N), BF16),
        scratch_shapes=[pltpu.VMEM((tm, D), BF16)],
        compiler_params=_cparams(("parallel", "arbitrary")),
        name="in_proj",
    )(x2, g, w)


def _lru_kernel(xa_ref, ga_ref, cw_ref, cb_ref, wg_ref, bg_ref, lam_ref, y_ref, tail_ref, h_ref, *, R):
    @pl.when(pl.program_id(1) == 0)
    def _():
        tail_ref[...] = jnp.zeros_like(tail_ref)
        h_ref[...] = jnp.zeros_like(h_ref)

    sp = _softplus(-lam_ref[...])
    groups = R // 8
    row8 = lax.broadcasted_iota(jnp.int32, (groups, 8, LRU_BLOCK), 1)
    for n in range(LRU_BLOCKS):
        sl = slice(n * LRU_BLOCK, (n + 1) * LRU_BLOCK)
        x3 = xa_ref[0, :, sl].astype(F32).reshape(groups, 8, LRU_BLOCK)
        tail = tail_ref[:, sl].reshape(1, 8, LRU_BLOCK)
        xc3 = cb_ref[:, sl] + cw_ref[CONV_WIDTH - 1:CONV_WIDTH, sl] * x3
        for s in range(1, CONV_WIDTH):
            rolled = pltpu.roll(x3, s, axis=1)
            prev = jnp.concatenate([pltpu.roll(tail, s, axis=1), rolled[:groups - 1]], axis=0)
            w = cw_ref[CONV_WIDTH - 1 - s:CONV_WIDTH - s, sl]
            xc3 = xc3 + w * jnp.where(row8 >= s, rolled, prev)
        tail_ref[:, sl] = x3[groups - 1]
        xcn = xc3.reshape(R, LRU_BLOCK)
        gz =jnp.dot(xcn.astype(BF16), wg_ref[n], preferred_element_type=F32)
        r = _sigmoid(gz[:, :LRU_BLOCK] + bg_ref[0:1, sl])
        i = _sigmoid(gz[:, LRU_BLOCK:] + bg_ref[1:2, sl])
        a = jnp.exp((-LRU_C) * r * sp[:, sl])
        u = _sqrt_nonneg(1.0 - a * a) * (i * xcn)
        a = a.reshape(R // 8, 8, LRU_BLOCK)
        u = u.reshape(R // 8, 8, LRU_BLOCK)
        for d in (1, 2, 4):
            keep = row8 >= d
            a_s = pltpu.roll(a, d, axis=1)
            u_s = pltpu.roll(u, d, axis=1)
            u = jnp.where(keep, a * u_s + u, u)
            a = jnp.where(keep, a * a_s, a)
        hprev = h_ref[:, sl]
        hs = []
        for t in range(R // 8):
            ht = a[t] * hprev + u[t]
            hprev = ht[7:8]
            hs.append(ht)
        h_ref[:, sl] = hprev
        h = jnp.concatenate(hs, axis=0)
        y_ref[0, :, sl] = (h * _gelu_tanh(ga_ref[0, :, sl].astype(F32))).astype(y_ref.dtype)


def _lru(proj3, conv_w, conv_b, wg, bg, lam, R=256):
    B, S, _ = proj3.shape
    D = D_MODEL
    return pl.pallas_call(
        functools.partial(_lru_kernel, R=R),
        grid=(B, S // R),
        in_specs=[pl.BlockSpec((1, R, D), lambda b, s: (b, s, COL_LRU_X // D)),
                  pl.BlockSpec((1, R, D), lambda b, s: (b, s, COL_LRU_G // D)),
                  _const_spec((CONV_WIDTH, D)),
                  _const_spec((1, D)),
                  _const_spec((LRU_BLOCKS, LRU_BLOCK, 2 * LRU_BLOCK)),
                  _const_spec((2, D)),
                  _const_spec((1, D))],
        out_specs=pl.BlockSpec((1, R, D), lambda b, s: (b, s, 0)),
        out_shape=jax.ShapeDtypeStruct((B, S, D), BF16),
        scratch_shapes=[pltpu.VMEM((8, D), F32), pltpu.VMEM((1, D), F32)],
        compiler_params=_cparams(("parallel", "arbitrary")),
        name="rglru",
    )(proj3, proj3, conv_w, conv_b, wg, bg, lam)


def _gla_kernel(q_ref, k_ref, v_ref, og_ref, sm_ref, wa_ref, ba_ref, gn_ref, y_ref, st_ref, *, Tb):
    nc = Tb // GLA_CHUNK

    @pl.when(pl.program_id(1) == 0)
    def _():
        st_ref[...] = jnp.zeros_like(st_ref)

    la_pre = jnp.dot(sm_ref[0], wa_ref[...], preferred_element_type=F32)
    rowc = lax.broadcasted_iota(jnp.int32, (Tb, GLA_DK), 0) & (GLA_CHUNK - 1)
    ri = lax.broadcasted_iota(jnp.int32, (Tb, Tb), 0)
    ci = lax.broadcasted_iota(jnp.int32, (Tb, Tb), 1)
    same_chunk_causal = ((ri // GLA_CHUNK) == (ci // GLA_CHUNK)) & (ci <= ri)
    lane_chunk = lax.broadcasted_iota(jnp.int32, (GLA_DK, Tb), 1) // GLA_CHUNK

    heads = range(GLA_HEADS)
    kcols = [slice(h * GLA_DK, (h + 1) * GLA_DK) for h in heads]
    vcols = [slice(h * GLA_DV, (h + 1) * GLA_DV) for h in heads]

    def chunk_log_decay(h):
        z = la_pre[:, kcols[h]] + ba_ref[:, kcols[h]]
        b = (jnp.minimum(z, 0.0) - jnp.log(1.0 + jnp.exp(-jnp.abs(z)))) * (1.0 / GLA_TAU)
        d = 1
        while d < GLA_CHUNK:
            b = b + jnp.where(rowc >= d, pltpu.roll(b, d, axis=0), 0.0)
            d *= 2
        return b

    b = [chunk_log_decay(h) for h in heads]
    b_last = [jnp.concatenate(
        [jnp.broadcast_to(b[h][c * GLA_CHUNK + GLA_CHUNK - 1:(c + 1) * GLA_CHUNK, :], (GLA_CHUNK, GLA_DK))
         for c in range(nc)], axis=0) for h in heads]
    q = [q_ref[0, :, kcols[h]].astype(F32) * (GLA_DK ** -0.5) for h in heads]
    k = [k_ref[0, :, kcols[h]].astype(F32) for h in heads]
    v = [v_ref[0, :, vcols[h]] for h in heads]
    qt = [(q[h] * jnp.exp(b[h])).astype(BF16) for h in heads]
    kt = [(k[h] * jnp.exp(-b[h])).astype(BF16) for h in heads]
    ke_t = [(k[h] * jnp.exp(b_last[h] - b[h])).T for h in heads]
    b_t = [b[h].T for h in heads]

    att = [jnp.where(same_chunk_causal, lax.dot_general(qt[h], kt[h], _NT, preferred_element_type=F32), 0.0)
           for h in heads]
    o = [jnp.dot(att[h].astype(BF16), v[h], preferred_element_type=F32) for h in heads]
    kv_all = [jnp.dot(jnp.concatenate([jnp.where(lane_chunk == c, ke_t[h], 0.0) for c in range(nc)],
                                      axis=0).astype(BF16), v[h], preferred_element_type=F32)
              for h in heads]

    state = [st_ref[h] for h in heads]
    inter = [[] for _ in heads]
    for c in range(nc):
        rs = slice(c * GLA_CHUNK, (c + 1) * GLA_CHUNK)
        last = c * GLA_CHUNK + GLA_CHUNK - 1
        for h in heads:
            inter[h].append(jnp.dot(qt[h][rs], state[h].astype(BF16), preferred_element_type=F32))
            state[h] = state[h] * jnp.exp(b_t[h][:, last:last + 1]) + kv_all[h][c * GLA_DK:(c + 1) * GLA_DK]
    for h in heads:
        st_ref[h] = state[h]
        oh = o[h] + jnp.concatenate(inter[h], axis=0)
        on = oh * lax.rsqrt(jnp.mean(oh * oh, axis=-1, keepdims=True) + EPS) * gn_ref[...]
        og = og_ref[0, :, vcols[h]].astype(F32)
        y_ref[0, :, vcols[h]] = (on * (og * jax.nn.sigmoid(og))).astype(y_ref.dtype)


def _gla(proj3, wa_pad, ba, gn, Tb=256):
    B, S, _ = proj3.shape
    HK = GLA_HEADS * GLA_DK
    D = D_MODEL
    return pl.pallas_call(
        functools.partial(_gla_kernel, Tb=Tb),
        grid=(B, S // Tb),
        in_specs=[pl.BlockSpec((1, Tb, HK), lambda b, s: (b, s, COL_GLA_Q // HK)),
                  pl.BlockSpec((1, Tb, HK), lambda b, s: (b, s, COL_GLA_K // HK)),
                  pl.BlockSpec((1, Tb, D), lambda b, s: (b, s, COL_GLA_V // D)),
                  pl.BlockSpec((1, Tb, D), lambda b, s: (b, s, COL_GLA_OG // D)),
                  pl.BlockSpec((1, Tb, LANE), lambda b, s: (b, s, COL_SMALL // LANE)),
                  _const_spec((LANE, HK)),
                  _const_spec((1, HK)),
                  _const_spec((1, GLA_DV))],
        out_specs=pl.BlockSpec((1, Tb, D), lambda b, s: (b, s, 0)),
        out_shape=jax.ShapeDtypeStruct((B, S, D), BF16),
        scratch_shapes=[pltpu.VMEM((GLA_HEADS, GLA_DK, GLA_DV), F32)],
        compiler_params=_cparams(("parallel", "arbitrary")),
        name="gla",
    )(proj3, proj3, proj3, proj3, proj3, wa_pad, ba, gn)


def _cmp_kernel(kv_ref, pos_ref, w1_ref, w2_ref, o_ref, x_ref):
    x_ref[...] = kv_ref[0].astype(F32)
    nrow = x_ref.shape[0] // CMP_STRIDE
    first = None
    second = None
    for i in range(CMP_STRIDE):
        xi = x_ref[pl.ds(i, nrow, stride=CMP_STRIDE), :]
        f = jnp.dot((xi + pos_ref[0, i]).astype(BF16), w1_ref[0, i], preferred_element_type=F32)
        s = jnp.dot((xi + pos_ref[1, i]).astype(BF16), w1_ref[1, i], preferred_element_type=F32)
        first = f if first is None else first + f
        second = s if second is None else second + s
    pre = first + pltpu.roll(second, nrow - 1, axis=0)
    hid = _gelu_tanh(pre).astype(BF16)
    o_ref[0, 0] = jnp.dot(hid, w2_ref[...], preferred_element_type=F32).astype(o_ref.dtype)


def _compress(proj3, pos, w1, w2):
    B, S, _ = proj3.shape
    nrow = S // CMP_STRIDE
    kvb = COL_NSA_KV // LANE
    return pl.pallas_call(
        _cmp_kernel,
        grid=(B, NSA_G),
        in_specs=[pl.BlockSpec((1, S, LANE), lambda b, g: (b, 0, kvb + g)),
                  _const_spec(pos.shape), _const_spec(w1.shape), _const_spec(w2.shape)],
        out_specs=pl.BlockSpec((1, 1, nrow, LANE), lambda b, g: (b, g, 0, 0)),
        out_shape=jax.ShapeDtypeStruct((B, NSA_G, nrow, LANE), BF16),
        scratch_shapes=[pltpu.VMEM((S, LANE), F32)],
        compiler_params=_cparams(("parallel", "parallel")),
        name="nsa_compress",
    )(proj3, pos, w1, w2)


def _compress_params(cmp_pos, cmp_w1, cmp_w2):
    hd = NSA_HD
    w1 = cmp_w1.reshape(2, 2, CMP_STRIDE, hd, CMP_HIDDEN)
    z = jnp.zeros_like(w1[0])
    w1 = jnp.concatenate([jnp.concatenate([w1[0], z], axis=-1),
                          jnp.concatenate([z, w1[1]], axis=-1)], axis=-2)
    pos = cmp_pos.reshape(2, 2, CMP_STRIDE, 1, hd)
    pos = jnp.concatenate([pos[0], pos[1]], axis=-1)
    z2 = jnp.zeros_like(cmp_w2[0])
    w2 = jnp.concatenate([jnp.concatenate([cmp_w2[0], z2], axis=-1),
                          jnp.concatenate([z2, cmp_w2[1]], axis=-1)], axis=0)
    return pos, w1.astype(BF16), w2.astype(BF16)


V_ROWS = NSA_HD + 16
MASK_BIG = 1e30
LOG2_E = 1.4426950408889634


def _nsa_kernel(q_ref, kvc_ref, kvs_ref, kvw_ref, gate_ref, bc_ref, bs_ref, bw_ref, xtra_ref, ovl_ref,
                gsel_ref, y_ref, a_ref, b_ref, vs_ref, vw_ref, vc_ref, lg_ref, p_ref):
    NB = q_ref.shape[0]
    rows_nb = range(NB)
    TQ = Q_TILE
    KT = K_TILE
    RT = TQ // KT
    HW = NSA_R * TQ
    qi = pl.program_id(2)
    S = kvs_ref.shape[1]
    n_cmp_rows = kvc_ref.shape[2]

    def value_rows(tile):
        t = tile.astype(F32).T
        return jnp.concatenate([t[NSA_HD:], jnp.ones((V_ROWS - NSA_HD, t.shape[1]), F32)], axis=0).astype(BF16)

    @pl.when(qi == 0)
    def _():
        for nb in rows_nb:
            for j in range(S // LANE):
                rows = slice(j * LANE, (j + 1) * LANE)
                ks = kvs_ref[nb, rows, :]
                a_ref[nb, rows, 0:LANE] = ks
                a_ref[nb, rows, LANE:2 * LANE] = xtra_ref[rows, :]
                vs_ref[nb, :, rows] = value_rows(ks)
                vw_ref[nb, :, rows] = value_rows(kvw_ref[nb, rows, :])
            vc_ref[nb] = value_rows(kvc_ref[nb, 0])

    def tile_rows(j, n=1):
        return pl.ds(pl.multiple_of(j * KT, KT), n * KT)

    def normalised(acc):
        return acc[:NSA_HD] * (1.0 / acc[NSA_HD:NSA_HD + 1])

    def q_transposed(nb):
        qf = q_ref[nb].astype(F32) * (NSA_HD ** -0.5)
        zero_rows = jnp.zeros((LANE - NSA_HD, TQ), F32)
        heads = []
        for pr in range(NSA_R // 2):
            t = qf[:, pr * LANE:(pr + 1) * LANE].T
            heads.append(jnp.concatenate([t[:NSA_HD], zero_rows], axis=0))
            heads.append(jnp.concatenate([t[NSA_HD:], zero_rows], axis=0))
        return jnp.concatenate(heads, axis=1)

    qt32 = [q_transposed(nb) for nb in rows_nb]
    qt = [x.astype(BF16) for x in qt32]
    qt32_l2 = [x * LOG2_E for x in qt32]
    qt_l2 = [x.astype(BF16) for x in qt32_l2]

    first = (qi == 0).astype(jnp.int32)
    rows_w = tile_rows(jnp.maximum(RT * qi - 1, 0), RT + 1)
    bias_w = jnp.concatenate([bw_ref[0, t + first] for t in range(RT + 1)], axis=0)
    lg_w = [jnp.dot(kvw_ref[nb, rows_w, :], qt_l2[nb], preferred_element_type=F32).astype(BF16) + bias_w
            for nb in rows_nb]
    p_w = [jnp.exp2(lg - jnp.max(lg, axis=0, keepdims=True)) for lg in lg_w]
    o_win = [normalised(jnp.dot(vw_ref[nb, :, rows_w], p_w[nb], preferred_element_type=F32)) for nb in rows_nb]

    cq = TQ // CMP_STRIDE
    start_c = pl.multiple_of(bc_ref.shape[1] - n_cmp_rows - cq * qi, 8)
    bias_c = bc_ref[0, pl.ds(start_c, n_cmp_rows), :]
    valid_c = bias_c > 0.5 * NEG_INF
    lg_c = [jnp.dot(kvc_ref[nb, 0], qt[nb], preferred_element_type=F32) + bias_c for nb in rows_nb]
    e_c = [jnp.exp(lg - jnp.max(lg, axis=0, keepdims=True)) for lg in lg_c]
    p_c = [jnp.where(valid_c, e * (1.0 / jnp.sum(e, axis=0, keepdims=True)), 0.0) for e in e_c]
    o_cmp = [jnp.dot(vc_ref[nb], p_c[nb].astype(BF16), preferred_element_type=F32)[:NSA_HD] for nb in rows_nb]

    ovl = ovl_ref[...]
    ns = ovl.shape[0]
    blk = lax.broadcasted_iota(jnp.int32, (ns, TQ), 0)
    cur = (qi * TQ + lax.broadcasted_iota(jnp.int32, (ns, TQ), 1)) // SLC_BLOCK
    forced = (blk == 0) | (blk == cur) | (blk == cur - 1)
    allowed = blk <= cur

    def importance(p):
        psum = p[:, 0:TQ]
        for r in range(1, NSA_R):
            psum = psum + p[:, r * TQ:(r + 1) * TQ]
        p_hi = psum.astype(BF16)
        p_lo = (psum - p_hi.astype(F32)).astype(BF16)
        imp = jnp.dot(ovl, p_hi, preferred_element_type=F32) + jnp.dot(ovl, p_lo, preferred_element_type=F32)
        return jnp.where(forced, FORCED, jnp.where(allowed, imp, -FORCED))

    score = [importance(p) for p in p_c]
    rank = [jnp.zeros((ns, TQ), F32) for _ in rows_nb]
    for j in range(ns):
        tie = jnp.where(blk > j, 1.0, 0.0)
        for nb in rows_nb:
            sj = score[nb][j:j + 1, :]
            rank[nb] = rank[nb] + jnp.where(sj > score[nb], 1.0, jnp.where(sj == score[nb], tie, 0.0))
    for nb in rows_nb:
        neg_sel = jnp.where(rank[nb] < float(SLC_TOP_N), 0.0, -MASK_BIG)
        b_ref[nb] = jnp.concatenate([qt32_l2[nb], jnp.concatenate([neg_sel] * NSA_R, axis=1),
                                     jnp.zeros((LANE - ns, HW), F32)], axis=0).astype(BF16)

    def logits(nb, j):
        return jnp.dot(a_ref[nb, tile_rows(j), :], b_ref[nb], preferred_element_type=F32).astype(BF16)

    for nb in rows_nb:
        lg_ref[nb] = logits(nb, 0)
        p_ref[nb] = jnp.zeros(p_ref.shape[1:], BF16)

    last = RT * qi + RT - 1

    def trip(j, carry):
        lg_next = [logits(nb, jnp.minimum(j + 1, last)) for nb in rows_nb]
        prev_rows = tile_rows(jnp.maximum(j - 1, 0))
        bias = bs_ref[0, jnp.where(j >= RT * qi - 1, j - RT * qi + 1, RT + 1)]
        out = []
        for nb in rows_nb:
            m, acc, alpha = carry[nb]
            acc = acc * alpha + jnp.dot(vs_ref[nb, :, prev_rows], p_ref[nb], preferred_element_type=F32)
            lg = lg_ref[nb] + bias
            m_new = jnp.maximum(m, jnp.max(lg, axis=0, keepdims=True).astype(F32))
            p_ref[nb] = jnp.exp2(lg - m_new.astype(BF16))
            out.append((m_new, acc, jnp.exp2(m - m_new)))
        for nb in rows_nb:
            lg_ref[nb] = lg_next[nb]
        return tuple(out)

    init = (jnp.full((1, HW), NEG_INF, F32), jnp.zeros((V_ROWS, HW), F32), jnp.ones((1, HW), F32))
    carry = lax.fori_loop(0, last + 1, trip, tuple(init for _ in rows_nb))
    o_slc = [normalised(carry[nb][1] * carry[nb][2]
                        + jnp.dot(vs_ref[nb, :, tile_rows(last)], p_ref[nb], preferred_element_type=F32))
             for nb in rows_nb]

    for nb in rows_nb:
        gs = jax.nn.sigmoid(lax.dot_general(gsel_ref[0], gate_ref[nb], _NT, preferred_element_type=F32))

        def gate_row(br):
            return jnp.concatenate([gs[br * NSA_R + r:br * NSA_R + r + 1] for r in range(NSA_R)], axis=1)

        o = gate_row(0) * o_cmp[nb] + gate_row(1) * o_slc[nb] + gate_row(2) * o_win[nb]
        o = jnp.concatenate([o[:, r * TQ:(r + 1) * TQ] for r in range(NSA_R)], axis=0)
        y_ref[nb] = o.T.astype(y_ref.dtype)


def _nsa(proj3, kvc, bias_c, bias_s, bias_w, xtra, ovl_t, gsel):
    B, S, _ = proj3.shape
    TQ = Q_TILE
    G = NSA_G
    RW = NSA_R * NSA_HD
    HW = NSA_R * TQ
    kvb = COL_NSA_KV // LANE
    ncr = kvc.shape[2]
    KT = K_TILE
    NB = NSA_BATCH_ROWS
    assert S % TQ == 0 and TQ % KT == 0 and S >= TQ + KT and WINDOW == KT and ncr % LANE == 0 and B % NB == 0
    return pl.pallas_call(
        _nsa_kernel,
        grid=(G, B // NB, S // TQ),
        in_specs=[pl.BlockSpec((NB, TQ, RW), lambda g, b, i: (b, i, COL_NSA_Q // RW + g)),
                  pl.BlockSpec((NB, 1, ncr, LANE), lambda g, b, i: (b, g, 0, 0)),
                  pl.BlockSpec((NB, S, LANE), lambda g, b, i: (b, 0, kvb + G + g)),
                  pl.BlockSpec((NB, S, LANE), lambda g, b, i: (b, 0, kvb + 2 * G + g)),
                  pl.BlockSpec((NB, TQ, LANE), lambda g, b, i: (b, i, COL_SMALL // LANE)),
                  pl.BlockSpec((1,) + bias_c.shape[1:], lambda g, b, i: (g, 0, 0)),
                  pl.BlockSpec((1,) + bias_s.shape[1:], lambda g, b, i: (g, 0, 0, 0)),
                  pl.BlockSpec((1,) + bias_w.shape[1:], lambda g, b, i: (g, 0, 0, 0)),
                  pl.BlockSpec(xtra.shape, lambda g, b, i: (0, 0)),
                  pl.BlockSpec(ovl_t.shape, lambda g, b, i: (0, 0)),
                  pl.BlockSpec((1,) + gsel.shape[1:], lambda g, b, i: (g, 0, 0))],
        out_specs=pl.BlockSpec((NB, TQ, RW), lambda g, b, i: (b, i, g)),
        out_shape=jax.ShapeDtypeStruct((B, S, D_MODEL), BF16),
        scratch_shapes=[pltpu.VMEM((NB, S, 2 * LANE), BF16), pltpu.VMEM((NB, 2 * LANE, HW), BF16),
                        pltpu.VMEM((NB, V_ROWS, S), BF16), pltpu.VMEM((NB, V_ROWS, S), BF16),
                        pltpu.VMEM((NB, V_ROWS, ncr), BF16), pltpu.VMEM((NB, KT, HW), BF16),
                        pltpu.VMEM((NB, KT, HW), BF16)],
        compiler_params=_cparams(("parallel", "parallel", "arbitrary")),
        name="nsa_attention",
    )(proj3, kvc, proj3, proj3, proj3, bias_c, bias_s, bias_w, xtra, ovl_t, gsel)


def _rel_bucket(dist):
    n = jnp.maximum(dist, 0)
    nf = jnp.maximum(n, REL_MAX_EXACT).astype(F32)
    large = REL_MAX_EXACT + (jnp.log(nf / REL_MAX_EXACT) / math.log(REL_MAX_DIST / REL_MAX_EXACT)
                             * (REL_BUCKETS - REL_MAX_EXACT)).astype(jnp.int32)
    large = jnp.minimum(large, REL_BUCKETS - 1)
    return jnp.where(n < REL_MAX_EXACT, n, large)


def _nsa_tables(rel_table, S):
    TQ = Q_TILE
    tbl = rel_table.astype(F32).reshape(REL_BUCKETS, NSA_G, NSA_R)
    tbl = tbl - tbl[REL_BUCKETS - 1]

    def table(dist, ok):
        onehot = jax.nn.one_hot(_rel_bucket(dist), REL_BUCKETS, dtype=F32)
        b = jnp.einsum("...kqn,ngr->g...krq", onehot, tbl, precision=lax.Precision.HIGHEST)
        b = jnp.where(ok[..., :, None, :], b, NEG_INF)
        return b.reshape(b.shape[:-2] + (NSA_R * dist.shape[-1],))

    KT = K_TILE
    kk = jnp.arange(KT)[:, None]
    qq = jnp.arange(TQ)[None, :]
    dist = qq - kk - KT * (jnp.arange(TQ // KT + 1)[:, None, None] - 1)
    bias_s = table(dist, dist >= 0) * LOG2_E
    bias_s = jnp.concatenate([bias_s, jnp.zeros_like(bias_s[:, :1])], axis=1).astype(BF16)
    bias_w = table(dist, (dist >= 0) & (dist < WINDOW)) * LOG2_E
    bias_w = jnp.concatenate([bias_w, jnp.full_like(bias_w[:, :1], NEG_INF)], axis=1).astype(BF16)
    n_rows = S // CMP_STRIDE
    off = n_rows - TQ // CMP_STRIDE
    c_rel = jnp.arange(off + n_rows)[:, None] - off
    dist_c = qq - CMP_STRIDE * c_rel - (CMP_BLOCK - 1)
    bias_c = table(dist_c, dist_c >= 0)

    n_slc = S // SLC_BLOCK
    n_cmp = n_rows - CMP_BLOCK // CMP_STRIDE + 1
    xtra = (jnp.arange(LANE)[None, :] == (jnp.arange(S)[:, None] // SLC_BLOCK)).astype(BF16)
    cmp_start = jnp.arange(n_rows) * CMP_STRIDE
    slc_start = jnp.arange(n_slc) * SLC_BLOCK
    overlap = jnp.clip(jnp.minimum(cmp_start[:, None] + CMP_BLOCK, slc_start[None, :] + SLC_BLOCK)
                       - jnp.maximum(cmp_start[:, None], slc_start[None, :]), 0).astype(F32) / CMP_BLOCK
    overlap = jnp.where(jnp.arange(n_rows)[:, None] < n_cmp, overlap, 0.0)
    g = jnp.arange(NSA_G)[:, None, None]
    row = jnp.arange(16)[None, :, None]
    col = jnp.arange(LANE)[None, None, :]
    src = SMALL_GATE + (row // NSA_R) * NSA_HEADS + g * NSA_R + row % NSA_R
    gsel = ((col == src) & (row < 3 * NSA_R)).astype(BF16)
    return bias_c, bias_s, bias_w, xtra, overlap.T.astype(BF16), gsel


def _merge_kernel(ya_ref, yb_ref, yc_ref, g0_ref, g1_ref, g2_ref, x_ref, wb_ref, wo_ref, gn_ref, o_ref):
    m = None
    for br, (y_ref, g_ref) in enumerate(((ya_ref, g0_ref), (yb_ref, g1_ref), (yc_ref, g2_ref))):
        t = jax.nn.sigmoid(g_ref[...].astype(F32)) * jnp.dot(y_ref[...], wb_ref[br], preferred_element_type=F32)
        m = t if m is None else m + t
    z = jnp.dot(m.astype(BF16), wo_ref[...], preferred_element_type=F32)
    o_ref[...] = x_ref[...] + _rms(z, gn_ref[...])


def _merge(ya, yb, yc, proj, x2, wb, wo, gn, layer, tm=512):
    T, D = x2.shape
    row = lambda i: (i, 0)
    mcol = COL_MERGE // D
    return pl.pallas_call(
        _merge_kernel,
        grid=(T // tm,),
        in_specs=[pl.BlockSpec((tm, D), row), pl.BlockSpec((tm, D), row), pl.BlockSpec((tm, D), row),
                  pl.BlockSpec((tm, D), lambda i: (i, mcol)),
                  pl.BlockSpec((tm, D), lambda i: (i, mcol + 1)),
                  pl.BlockSpec((tm, D), lambda i: (i, mcol + 2)),
                  pl.BlockSpec((tm, D), row),
                  _const_spec(wb.shape[1:], layer), _const_spec(wo.shape[1:], layer), _const_spec((1, D))],
        out_specs=pl.BlockSpec((tm, D), row),
        out_shape=jax.ShapeDtypeStruct((T, D), F32),
        compiler_params=_cparams(("parallel",)),
        name="merge_out",
    )(ya, yb, yc, proj, proj, proj, x2, wb, wo, gn)


def _ffn_kernel(x_ref, gpre_ref, win_ref, wout_ref, gpost_ref, o_ref, *, chunk):
    x = x_ref[...]
    h = _rms(x, gpre_ref[...]).astype(BF16)
    acc = None
    for c in range(D_FF // chunk):
        gt = jnp.dot(h, win_ref[:, c * chunk:(c + 1) * chunk], preferred_element_type=F32)
        up = jnp.dot(h, win_ref[:, D_FF + c * chunk:D_FF + (c + 1) * chunk], preferred_element_type=F32)
        a = (gt * jax.nn.sigmoid(gt) * up).astype(BF16)
        t = jnp.dot(a, wout_ref[c * chunk:(c + 1) * chunk, :], preferred_element_type=F32)
        acc = t if acc is None else acc + t
    o_ref[...] = x + _rms(acc, gpost_ref[...])


def _ffn(x2, gpre, win, wout, gpost, layer, tm=512, chunk=256):
    T, D = x2.shape
    row = lambda i: (i, 0)
    return pl.pallas_call(
        functools.partial(_ffn_kernel, chunk=chunk),
        grid=(T // tm,),
        in_specs=[pl.BlockSpec((tm, D), row), _const_spec((1, D)), _const_spec(win.shape[1:], layer),
                  _const_spec(wout.shape[1:], layer), _const_spec((1, D))],
        out_specs=pl.BlockSpec((tm, D), row),
        out_shape=jax.ShapeDtypeStruct((T, D), F32),
        compiler_params=_cparams(("parallel",)),
        name="ffn",
    )(x2, gpre, win, wout, gpost)


def _permute_in_columns(w):
    cuts = []
    acc = 0
    for size in IN_SIZES[:-1]:
        acc += size
        cuts.append(acc)
    (lru_x, lru_g, nsa_q, nsa_kv, nsa_gate, gla_q, gla_k, gla_v, gla_og, gla_lr, merge) = jnp.split(w, cuts, axis=-1)
    lead = w.shape[:-1]
    kv = nsa_kv.reshape(lead + (3, 2, NSA_G, NSA_HD))
    kv = jnp.moveaxis(kv, -3, -2).reshape(lead + (6 * NSA_G * NSA_HD,))
    pad = jnp.zeros(lead + (LANE - nsa_gate.shape[-1] - gla_lr.shape[-1],), w.dtype)
    return jnp.concatenate([lru_x, lru_g, nsa_q, gla_v, gla_og, merge, kv, gla_q, gla_k, nsa_gate, gla_lr, pad],
                           axis=-1)


def _layer(x2, B, S, tables, layer, norm_g, w_in, conv_w, conv_b, lru_w_gates, lru_b_gates, lru_lambda,
           cmp_pos, cmp_w1, cmp_w2, gla_wa2, gla_ba, gla_norm, w_branch, w_out, w_ffn_in, w_ffn_out):
    D = D_MODEL
    proj = _in_proj(x2, norm_g[0][None, :], w_in, layer)
    proj3 = proj.reshape(B, S, PROJ_WIDTH)

    wg = jnp.concatenate([lru_w_gates[0], lru_w_gates[1]], axis=-1).astype(BF16)
    y_a = _lru(proj3, conv_w, conv_b[None, :], wg, lru_b_gates, lru_lambda[None, :])

    kvc = _compress(proj3, *_compress_params(cmp_pos, cmp_w1, cmp_w2))
    y_b = _nsa(proj3, kvc, *tables)

    wa_pad = jnp.zeros((LANE, GLA_HEADS * GLA_DK), F32).at[SMALL_LR:SMALL_LR + GLA_RANK].set(gla_wa2).astype(BF16)
    y_c = _gla(proj3, wa_pad, gla_ba[None, :], gla_norm[None, :])

    x2 = _merge(y_a.reshape(B * S, D), y_b.reshape(B * S, D), y_c.reshape(B * S, D), proj, x2,
                w_branch, w_out, norm_g[1][None, :], layer)
    x2 = _ffn(x2, norm_g[2][None, :], w_ffn_in, w_ffn_out, norm_g[3][None, :], layer)
    return x2


def kernel(x, rel_table, norm_g, w_in, conv_w, conv_b, lru_w_gates, lru_b_gates, lru_lambda, cmp_pos, cmp_w1,
           cmp_w2, gla_wa2, gla_ba, gla_norm, w_branch, w_out, w_ffn_in, w_ffn_out):
    B, S, D = x.shape
    tables = _nsa_tables(rel_table, S)
    x2 = x.reshape(B * S, D)
    w_in = _permute_in_columns(w_in.astype(BF16))
    w_branch, w_out, w_ffn_in, w_ffn_out = (w.astype(BF16) for w in (w_branch, w_out, w_ffn_in, w_ffn_out))
    for l in range(norm_g.shape[0]):
        x2 = _layer(x2, B, S, tables, l, norm_g[l], w_in, conv_w[l], conv_b[l], lru_w_gates[l], lru_b_gates[l],
                    lru_lambda[l], cmp_pos[l], cmp_w1[l], cmp_w2[l], gla_wa2[l], gla_ba[l], gla_norm[l],
                    w_branch, w_out, w_ffn_in, w_ffn_out)
    return x2.reshape(B, S, D)
```

```python
import functools
import math

import jax
import jax.numpy as jnp
import numpy as np
from jax import lax
from jax.experimental import pallas as pl
from jax.experimental.pallas import tpu as pltpu

F32 = jnp.float32
BF16 = jnp.bfloat16

D_MODEL = 1024
N_BRANCH = 3
EPS = 1e-6
NEG_INF = -1e30
FORCED = 1e4

LRU_BLOCKS = 8
LRU_BLOCK = D_MODEL // LRU_BLOCKS
CONV_WIDTH = 4
LRU_C = 8.0

NSA_HEADS = 16
NSA_G = 4
NSA_R = NSA_HEADS // NSA_G
NSA_HD = D_MODEL // NSA_HEADS
CMP_BLOCK = 32
CMP_STRIDE = 16
CMP_HIDDEN = 256
SLC_BLOCK = 64
SLC_TOP_N = 8
WINDOW = 256

GLA_HEADS = 4
GLA_DK = (D_MODEL // 2) // GLA_HEADS
GLA_DV = D_MODEL // GLA_HEADS
GLA_RANK = 16
GLA_TAU = 16.0
GLA_CHUNK = 32

REL_BUCKETS = 32
REL_MAX_EXACT = 16
REL_MAX_DIST = 128

D_FF = -(-8 * D_MODEL // (3 * 256)) * 256

IN_SIZES = (D_MODEL, D_MODEL, D_MODEL, 6 * NSA_G * NSA_HD, 3 * NSA_HEADS, D_MODEL // 2, D_MODEL // 2,
            D_MODEL, D_MODEL, GLA_RANK, N_BRANCH * D_MODEL)

LANE = 128
COL_LRU_X = 0
COL_LRU_G = 1024
COL_NSA_Q = 2048
COL_GLA_V = 3072
COL_GLA_OG = 4096
COL_MERGE = 5120
COL_NSA_KV = 8192
COL_GLA_Q = 9728
COL_GLA_K = 10240
COL_SMALL = 10752
PROJ_WIDTH = 10880
SMALL_GATE = 0
SMALL_LR = 3 * NSA_HEADS

Q_TILE = 256
K_TILE = 256
NSA_BATCH_ROWS = 4
VMEM_LIMIT = 56 * 1024 * 1024
_NT = (((1,), (1,)), ((), ()))


def _cparams(sem):
    return pltpu.CompilerParams(dimension_semantics=sem, vmem_limit_bytes=VMEM_LIMIT)


def _const_spec(shape, layer=None):
    nd = len(shape)
    if layer is None:
        return pl.BlockSpec(shape, lambda *_: (0,) * nd, pipeline_mode=pl.Buffered(1))
    return pl.BlockSpec((None,) + tuple(shape), lambda *_: (layer,) + (0,) * nd, pipeline_mode=pl.Buffered(1))


def _rms(x, g):
    return x * lax.rsqrt(jnp.mean(x * x, axis=-1, keepdims=True) + EPS) * g


def _gelu_tanh(x):
    return 0.5 * x * (1.0 + jnp.tanh(math.sqrt(2.0 / math.pi) * (x + 0.044715 * (x * x * x))))


def _sigmoid(x):
    return 0.5 * jnp.tanh(0.5 * x) + 0.5


def _sqrt_nonneg(x):
    return jnp.where(x > 0.0, x * lax.rsqrt(x), 0.0)


def _softplus(z):
    return jnp.maximum(z, 0.0) + jnp.log1p(jnp.exp(-jnp.abs(z)))


def _in_proj_kernel(x_ref, g_ref, w_ref, o_ref, h_ref):
    @pl.when(pl.program_id(1) == 0)
    def _():
        h_ref[...] = _rms(x_ref[...], g_ref[...]).astype(BF16)

    o_ref[...] = jnp.dot(h_ref[...], w_ref[...], preferred_element_type=F32).astype(o_ref.dtype)


def _in_proj(x2, g, w, layer, tm=1024, tn=2176):
    T, D = x2.shape
    N = w.shape[2]
    return pl.pallas_call(
        _in_proj_kernel,
        grid=(T // tm, N // tn),
        in_specs=[pl.BlockSpec((tm, D), lambda i, j: (i, 0)),
                  pl.BlockSpec((1, D), lambda i, j: (0, 0)),
                  pl.BlockSpec((None, D, tn), lambda i, j: (layer, 0, j))],
        out_specs=pl.BlockSpec((tm, tn), lambda i, j: (i, j)),
        out_shape=jax.ShapeDtypeStruct((T, N), BF16),
        scratch_shapes=[pltpu.VMEM((tm, D), BF16)],
        compiler_params=_cparams(("parallel", "arbitrary")),
        name="in_proj",
    )(x2, g, w)


def _lru_kernel(xa_ref, ga_ref, cw_ref, cb_ref, wg_ref, bg_ref, lam_ref, y_ref, tail_ref, h_ref, *, R):
    @pl.when(pl.program_id(1) == 0)
    def _():
        tail_ref[...] = jnp.zeros_like(tail_ref)
        h_ref[...] = jnp.zeros_like(h_ref)

    sp = _softplus(-lam_ref[...])
    groups = R // 8
    row8 = lax.broadcasted_iota(jnp.int32, (groups, 8, LRU_BLOCK), 1)
    for n in range(LRU_BLOCKS):
        sl = slice(n * LRU_BLOCK, (n + 1) * LRU_BLOCK)
        x3 = xa_ref[0, :, sl].astype(F32).reshape(groups, 8, LRU_BLOCK)
        tail = tail_ref[:, sl].reshape(1, 8, LRU_BLOCK)
        xc3 = cb_ref[:, sl] + cw_ref[CONV_WIDTH - 1:CONV_WIDTH, sl] * x3
        for s in range(1, CONV_WIDTH):
            rolled = pltpu.roll(x3, s, axis=1)
            prev = jnp.concatenate([pltpu.roll(tail, s, axis=1), rolled[:groups - 1]], axis=0)
            w = cw_ref[CONV_WIDTH - 1 - s:CONV_WIDTH - s, sl]
            xc3 = xc3 + w * jnp.where(row8 >= s, rolled, prev)
        tail_ref[:, sl] = x3[groups - 1]
        xcn = xc3.reshape(R, LRU_BLOCK)
        gz =jnp.dot(xcn.astype(BF16), wg_ref[n], preferred_element_type=F32)
        r = _sigmoid(gz[:, :LRU_BLOCK] + bg_ref[0:1, sl])
        i = _sigmoid(gz[:, LRU_BLOCK:] + bg_ref[1:2, sl])
        a = jnp.exp((-LRU_C) * r * sp[:, sl])
        u = _sqrt_nonneg(1.0 - a * a) * (i * xcn)
        a = a.reshape(R // 8, 8, LRU_BLOCK)
        u = u.reshape(R // 8, 8, LRU_BLOCK)
        for d in (1, 2, 4):
            keep = row8 >= d
            a_s = pltpu.roll(a, d, axis=1)
            u_s = pltpu.roll(u, d, axis=1)
            u = jnp.where(keep, a * u_s + u, u)
            a = jnp.where(keep, a * a_s, a)
        hprev = h_ref[:, sl]
        hs = []
        for t in range(R // 8):
            ht = a[t] * hprev + u[t]
            hprev = ht[7:8]
            hs.append(ht)
        h_ref[:, sl] = hprev
        h = jnp.concatenate(hs, axis=0)
        y_ref[0, :, sl] = (h * _gelu_tanh(ga_ref[0, :, sl].astype(F32))).astype(y_ref.dtype)


def _lru(proj3, conv_w, conv_b, wg, bg, lam, R=256):
    B, S, _ = proj3.shape
    D = D_MODEL
    return pl.pallas_call(
        functools.partial(_lru_kernel, R=R),
        grid=(B, S // R),
        in_specs=[pl.BlockSpec((1, R, D), lambda b, s: (b, s, COL_LRU_X // D)),
                  pl.BlockSpec((1, R, D), lambda b, s: (b, s, COL_LRU_G // D)),
                  _const_spec((CONV_WIDTH, D)),
                  _const_spec((1, D)),
                  _const_spec((LRU_BLOCKS, LRU_BLOCK, 2 * LRU_BLOCK)),
                  _const_spec((2, D)),
                  _const_spec((1, D))],
        out_specs=pl.BlockSpec((1, R, D), lambda b, s: (b, s, 0)),
        out_shape=jax.ShapeDtypeStruct((B, S, D), BF16),
        scratch_shapes=[pltpu.VMEM((8, D), F32), pltpu.VMEM((1, D), F32)],
        compiler_params=_cparams(("parallel", "arbitrary")),
        name="rglru",
    )(proj3, proj3, conv_w, conv_b, wg, bg, lam)


def _gla_kernel(q_ref, k_ref, v_ref, og_ref, sm_ref, wa_ref, ba_ref, gn_ref, y_ref, st_ref, *, Tb):
    nc = Tb // GLA_CHUNK

    @pl.when(pl.program_id(1) == 0)
    def _():
        st_ref[...] = jnp.zeros_like(st_ref)

    la_pre = jnp.dot(sm_ref[0], wa_ref[...], preferred_element_type=F32)
    rowc = lax.broadcasted_iota(jnp.int32, (Tb, GLA_DK), 0) & (GLA_CHUNK - 1)
    ri = lax.broadcasted_iota(jnp.int32, (Tb, Tb), 0)
    ci = lax.broadcasted_iota(jnp.int32, (Tb, Tb), 1)
    same_chunk_causal = ((ri // GLA_CHUNK) == (ci // GLA_CHUNK)) & (ci <= ri)
    lane_chunk = lax.broadcasted_iota(jnp.int32, (GLA_DK, Tb), 1) // GLA_CHUNK

    heads = range(GLA_HEADS)
    kcols = [slice(h * GLA_DK, (h + 1) * GLA_DK) for h in heads]
    vcols = [slice(h * GLA_DV, (h + 1) * GLA_DV) for h in heads]

    def chunk_log_decay(h):
        z = la_pre[:, kcols[h]] + ba_ref[:, kcols[h]]
        b = (jnp.minimum(z, 0.0) - jnp.log(1.0 + jnp.exp(-jnp.abs(z)))) * (1.0 / GLA_TAU)
        d = 1
        while d < GLA_CHUNK:
            b = b + jnp.where(rowc >= d, pltpu.roll(b, d, axis=0), 0.0)
            d *= 2
        return b

    b = [chunk_log_decay(h) for h in heads]
    b_last = [jnp.concatenate(
        [jnp.broadcast_to(b[h][c * GLA_CHUNK + GLA_CHUNK - 1:(c + 1) * GLA_CHUNK, :], (GLA_CHUNK, GLA_DK))
         for c in range(nc)], axis=0) for h in heads]
    q = [q_ref[0, :, kcols[h]].astype(F32) * (GLA_DK ** -0.5) for h in heads]
    k = [k_ref[0, :, kcols[h]].astype(F32) for h in heads]
    v = [v_ref[0, :, vcols[h]] for h in heads]
    qt = [(q[h] * jnp.exp(b[h])).astype(BF16) for h in heads]
    kt = [(k[h] * jnp.exp(-b[h])).astype(BF16) for h in heads]
    ke_t = [(k[h] * jnp.exp(b_last[h] - b[h])).T for h in heads]
    b_t = [b[h].T for h in heads]

    att = [jnp.where(same_chunk_causal, lax.dot_general(qt[h], kt[h], _NT, preferred_element_type=F32), 0.0)
           for h in heads]
    o = [jnp.dot(att[h].astype(BF16), v[h], preferred_element_type=F32) for h in heads]
    kv_all = [jnp.dot(jnp.concatenate([jnp.where(lane_chunk == c, ke_t[h], 0.0) for c in range(nc)],
                                      axis=0).astype(BF16), v[h], preferred_element_type=F32)
              for h in heads]

    state = [st_ref[h] for h in heads]
    inter = [[] for _ in heads]
    for c in range(nc):
        rs = slice(c * GLA_CHUNK, (c + 1) * GLA_CHUNK)
        last = c * GLA_CHUNK + GLA_CHUNK - 1
        for h in heads:
            inter[h].append(jnp.dot(qt[h][rs], state[h].astype(BF16), preferred_element_type=F32))
            state[h] = state[h] * jnp.exp(b_t[h][:, last:last + 1]) + kv_all[h][c * GLA_DK:(c + 1) * GLA_DK]
    for h in heads:
        st_ref[h] = state[h]
        oh = o[h] + jnp.concatenate(inter[h], axis=0)
        on = oh * lax.rsqrt(jnp.mean(oh * oh, axis=-1, keepdims=True) + EPS) * gn_ref[...]
        og = og_ref[0, :, vcols[h]].astype(F32)
        y_ref[0, :, vcols[h]] = (on * (og * jax.nn.sigmoid(og))).astype(y_ref.dtype)


def _gla(proj3, wa_pad, ba, gn, Tb=256):
    B, S, _ = proj3.shape
    HK = GLA_HEADS * GLA_DK
    D = D_MODEL
    return pl.pallas_call(
        functools.partial(_gla_kernel, Tb=Tb),
        grid=(B, S // Tb),
        in_specs=[pl.BlockSpec((1, Tb, HK), lambda b, s: (b, s, COL_GLA_Q // HK)),
                  pl.BlockSpec((1, Tb, HK), lambda b, s: (b, s, COL_GLA_K // HK)),
                  pl.BlockSpec((1, Tb, D), lambda b, s: (b, s, COL_GLA_V // D)),
                  pl.BlockSpec((1, Tb, D), lambda b, s: (b, s, COL_GLA_OG // D)),
                  pl.BlockSpec((1, Tb, LANE), lambda b, s: (b, s, COL_SMALL // LANE)),
                  _const_spec((LANE, HK)),
                  _const_spec((1, HK)),
                  _const_spec((1, GLA_DV))],
        out_specs=pl.BlockSpec((1, Tb, D), lambda b, s: (b, s, 0)),
        out_shape=jax.ShapeDtypeStruct((B, S, D), BF16),
        scratch_shapes=[pltpu.VMEM((GLA_HEADS, GLA_DK, GLA_DV), F32)],
        compiler_params=_cparams(("parallel", "arbitrary")),
        name="gla",
    )(proj3, proj3, proj3, proj3, proj3, wa_pad, ba, gn)


def _cmp_kernel(kv_ref, pos_ref, w1_ref, w2_ref, o_ref, x_ref):
    x_ref[...] = kv_ref[0].astype(F32)
    nrow = x_ref.shape[0] // CMP_STRIDE
    first = None
    second = None
    for i in range(CMP_STRIDE):
        xi = x_ref[pl.ds(i, nrow, stride=CMP_STRIDE), :]
        f = jnp.dot((xi + pos_ref[0, i]).astype(BF16), w1_ref[0, i], preferred_element_type=F32)
        s = jnp.dot((xi + pos_ref[1, i]).astype(BF16), w1_ref[1, i], preferred_element_type=F32)
        first = f if first is None else first + f
        second = s if second is None else second + s
    pre = first + pltpu.roll(second, nrow - 1, axis=0)
    hid = _gelu_tanh(pre).astype(BF16)
    o_ref[0, 0] = jnp.dot(hid, w2_ref[...], preferred_element_type=F32).astype(o_ref.dtype)


def _compress(proj3, pos, w1, w2):
    B, S, _ = proj3.shape
    nrow = S // CMP_STRIDE
    kvb = COL_NSA_KV // LANE
    return pl.pallas_call(
        _cmp_kernel,
        grid=(B, NSA_G),
        in_specs=[pl.BlockSpec((1, S, LANE), lambda b, g: (b, 0, kvb + g)),
                  _const_spec(pos.shape), _const_spec(w1.shape), _const_spec(w2.shape)],
        out_specs=pl.BlockSpec((1, 1, nrow, LANE), lambda b, g: (b, g, 0, 0)),
        out_shape=jax.ShapeDtypeStruct((B, NSA_G, nrow, LANE), BF16),
        scratch_shapes=[pltpu.VMEM((S, LANE), F32)],
        compiler_params=_cparams(("parallel", "parallel")),
        name="nsa_compress",
    )(proj3, pos, w1, w2)


def _compress_params(cmp_pos, cmp_w1, cmp_w2):
    hd = NSA_HD
    w1 = cmp_w1.reshape(2, 2, CMP_STRIDE, hd, CMP_HIDDEN)
    z = jnp.zeros_like(w1[0])
    w1 = jnp.concatenate([jnp.concatenate([w1[0], z], axis=-1),
                          jnp.concatenate([z, w1[1]], axis=-1)], axis=-2)
    pos = cmp_pos.reshape(2, 2, CMP_STRIDE, 1, hd)
    pos = jnp.concatenate([pos[0], pos[1]], axis=-1)
    z2 = jnp.zeros_like(cmp_w2[0])
    w2 = jnp.concatenate([jnp.concatenate([cmp_w2[0], z2], axis=-1),
                          jnp.concatenate([z2, cmp_w2[1]], axis=-1)], axis=0)
    return pos, w1.astype(BF16), w2.astype(BF16)


V_ROWS = NSA_HD + 16
MASK_BIG = 1e30
LOG2_E = 1.4426950408889634


def _nsa_kernel(q_ref, kvc_ref, kvs_ref, kvw_ref, gate_ref, bc_ref, bs_ref, bw_ref, xtra_ref, ovl_ref,
                gsel_ref, y_ref, a_ref, b_ref, vs_ref, vw_ref, vc_ref, lg_ref, p_ref):
    NB = q_ref.shape[0]
    rows_nb = range(NB)
    TQ = Q_TILE
    KT = K_TILE
    RT = TQ // KT
    HW = NSA_R * TQ
    qi = pl.program_id(2)
    S = kvs_ref.shape[1]
    n_cmp_rows = kvc_ref.shape[2]

    def value_rows(tile):
        t = tile.astype(F32).T
        return jnp.concatenate([t[NSA_HD:], jnp.ones((V_ROWS - NSA_HD, t.shape[1]), F32)], axis=0).astype(BF16)

    @pl.when(qi == 0)
    def _():
        for nb in rows_nb:
            for j in range(S // LANE):
                rows = slice(j * LANE, (j + 1) * LANE)
                ks = kvs_ref[nb, rows, :]
                a_ref[nb, rows, 0:LANE] = ks
                a_ref[nb, rows, LANE:2 * LANE] = xtra_ref[rows, :]
                vs_ref[nb, :, rows] = value_rows(ks)
                vw_ref[nb, :, rows] = value_rows(kvw_ref[nb, rows, :])
            vc_ref[nb] = value_rows(kvc_ref[nb, 0])
            b_ref[nb, LANE + ovl_ref.shape[0]:, :] = jnp.zeros((LANE - ovl_ref.shape[0], HW), BF16)

    def tile_rows(j, n=1):
        return pl.ds(pl.multiple_of(j * KT, KT), n * KT)

    def normalised(acc):
        return acc[:NSA_HD] * (1.0 / acc[NSA_HD:NSA_HD + 1])

    def q_transposed(nb):
        qf = q_ref[nb].astype(F32) * (NSA_HD ** -0.5)
        zero_rows = jnp.zeros((LANE - NSA_HD, TQ), F32)
        heads = []
        for pr in range(NSA_R // 2):
            t = qf[:, pr * LANE:(pr + 1) * LANE].T
            heads.append(jnp.concatenate([t[:NSA_HD], zero_rows], axis=0))
            heads.append(jnp.concatenate([t[NSA_HD:], zero_rows], axis=0))
        return jnp.concatenate(heads, axis=1)

    qt32 = [q_transposed(nb) for nb in rows_nb]
    qt = [x.astype(BF16) for x in qt32]
    qt32_l2 = [x * LOG2_E for x in qt32]
    qt_l2 = [x.astype(BF16) for x in qt32_l2]

    first = (qi == 0).astype(jnp.int32)
    rows_w = tile_rows(jnp.maximum(RT * qi - 1, 0), RT + 1)
    bias_w = jnp.concatenate([bw_ref[0, t + first] for t in range(RT + 1)], axis=0)
    lg_w = [jnp.dot(kvw_ref[nb, rows_w, :], qt_l2[nb], preferred_element_type=F32).astype(BF16) + bias_w
            for nb in rows_nb]
    p_w = [jnp.exp2(lg - jnp.max(lg, axis=0, keepdims=True)) for lg in lg_w]
    o_win = [normalised(jnp.dot(vw_ref[nb, :, rows_w], p_w[nb], preferred_element_type=F32)) for nb in rows_nb]

    cq = TQ // CMP_STRIDE
    start_c = pl.multiple_of(bc_ref.shape[1] - n_cmp_rows - cq * qi, 8)
    bias_c = bc_ref[0, pl.ds(start_c, n_cmp_rows), :]
    valid_c = bias_c > 0.5 * NEG_INF
    lg_c = [jnp.dot(kvc_ref[nb, 0], qt[nb], preferred_element_type=F32) + bias_c for nb in rows_nb]
    e_c = [jnp.exp(lg - jnp.max(lg, axis=0, keepdims=True)) for lg in lg_c]
    p_c = [jnp.where(valid_c, e * (1.0 / jnp.sum(e, axis=0, keepdims=True)), 0.0) for e in e_c]
    o_cmp = [jnp.dot(vc_ref[nb], p_c[nb].astype(BF16), preferred_element_type=F32)[:NSA_HD] for nb in rows_nb]

    ovl = ovl_ref[...]
    ns = ovl.shape[0]
    blk = lax.broadcasted_iota(jnp.int32, (ns, TQ), 0)
    cur = (qi * TQ + lax.broadcasted_iota(jnp.int32, (ns, TQ), 1)) // SLC_BLOCK
    forced = (blk == 0) | (blk == cur) | (blk == cur - 1)
    allowed = blk <= cur

    def importance(p):
        psum = p[:, 0:TQ]
        for r in range(1, NSA_R):
            psum = psum + p[:, r * TQ:(r + 1) * TQ]
        p_hi = psum.astype(BF16)
        p_lo = (psum - p_hi.astype(F32)).astype(BF16)
        imp = jnp.dot(ovl, p_hi, preferred_element_type=F32) + jnp.dot(ovl, p_lo, preferred_element_type=F32)
        return jnp.where(forced, FORCED, jnp.where(allowed, imp, -FORCED))

    score = [importance(p) for p in p_c]
    groups = ns // 8
    blk8 = lax.broadcasted_iota(jnp.int32, (8, TQ), 0)
    for nb in rows_nb:
        sc = [score[nb][8 * v:8 * v + 8] for v in range(groups)]
        rank = [jnp.zeros((8, TQ), F32) for _ in range(groups)]
        for j in range(ns):
            sj = score[nb][j:j + 1, :]
            for v in range(groups):
                if j < 8 * v:
                    beats = sj >= sc[v]
                elif j >= 8 * v + 8:
                    beats = sj > sc[v]
                else:
                    tie = jnp.where(blk8 > j - 8 * v, 1.0, 0.0)
                    rank[v] = rank[v] + jnp.where(sj > sc[v], 1.0, jnp.where(sj == sc[v], tie, 0.0))
                    continue
                rank[v] = rank[v] + jnp.where(beats, 1.0, 0.0)
        neg_sel = jnp.where(jnp.concatenate(rank, axis=0) < float(SLC_TOP_N), 0.0, -MASK_BIG).astype(BF16)
        b_ref[nb, 0:LANE] = qt_l2[nb]
        b_ref[nb, LANE:LANE + ns] = jnp.concatenate([neg_sel] * NSA_R, axis=1)

    def logits(nb, j):
        return jnp.dot(a_ref[nb, tile_rows(j), :], b_ref[nb], preferred_element_type=F32).astype(BF16)

    for nb in rows_nb:
        lg_ref[nb] = logits(nb, 0)
        p_ref[nb] = jnp.zeros(p_ref.shape[1:], BF16)

    last = RT * qi + RT - 1

    def trip(j, carry):
        lg_next = [logits(nb, jnp.minimum(j + 1, last)) for nb in rows_nb]
        prev_rows = tile_rows(jnp.maximum(j - 1, 0))
        bias = bs_ref[0, jnp.where(j >= RT * qi - 1, j - RT * qi + 1, RT + 1)]
        out = []
        for nb in rows_nb:
            m, acc, alpha = carry[nb]
            acc = acc * alpha + jnp.dot(vs_ref[nb, :, prev_rows], p_ref[nb], preferred_element_type=F32)
            lg = lg_ref[nb] + bias
            m_new = jnp.maximum(m, jnp.max(lg, axis=0, keepdims=True).astype(F32))
            p_ref[nb] = jnp.exp2(lg - m_new.astype(BF16))
            out.append((m_new, acc, jnp.exp2(m - m_new)))
        for nb in rows_nb:
            lg_ref[nb] = lg_next[nb]
        return tuple(out)

    init = (jnp.full((1, HW), NEG_INF, F32), jnp.zeros((V_ROWS, HW), F32), jnp.ones((1, HW), F32))
    carry = lax.fori_loop(0, last + 1, trip, tuple(init for _ in rows_nb))
    o_slc = [normalised(carry[nb][1] * carry[nb][2]
                        + jnp.dot(vs_ref[nb, :, tile_rows(last)], p_ref[nb], preferred_element_type=F32))
             for nb in rows_nb]

    for nb in rows_nb:
        gs = jax.nn.sigmoid(lax.dot_general(gsel_ref[0], gate_ref[nb], _NT, preferred_element_type=F32))

        def gate_row(br):
            return jnp.concatenate([gs[br * NSA_R + r:br * NSA_R + r + 1] for r in range(NSA_R)], axis=1)

        o = gate_row(0) * o_cmp[nb] + gate_row(1) * o_slc[nb] + gate_row(2) * o_win[nb]
        o = jnp.concatenate([o[:, r * TQ:(r + 1) * TQ] for r in range(NSA_R)], axis=0)
        y_ref[nb] = o.T.astype(y_ref.dtype)


def _nsa(proj3, kvc, bias_c, bias_s, bias_w, xtra, ovl_t, gsel):
    B, S, _ = proj3.shape
    TQ = Q_TILE
    G = NSA_G
    RW = NSA_R * NSA_HD
    HW = NSA_R * TQ
    kvb = COL_NSA_KV // LANE
    ncr = kvc.shape[2]
    KT = K_TILE
    NB = NSA_BATCH_ROWS
    assert S % TQ == 0 and TQ % KT == 0 and S >= TQ + KT and WINDOW == KT and ncr % LANE == 0 and B % NB == 0
    return pl.pallas_call(
        _nsa_kernel,
        grid=(G, B // NB, S // TQ),
        in_specs=[pl.BlockSpec((NB, TQ, RW), lambda g, b, i: (b, i, COL_NSA_Q // RW + g)),
                  pl.BlockSpec((NB, 1, ncr, LANE), lambda g, b, i: (b, g, 0, 0)),
                  pl.BlockSpec((NB, S, LANE), lambda g, b, i: (b, 0, kvb + G + g)),
                  pl.BlockSpec((NB, S, LANE), lambda g, b, i: (b, 0, kvb + 2 * G + g)),
                  pl.BlockSpec((NB, TQ, LANE), lambda g, b, i: (b, i, COL_SMALL // LANE)),
                  pl.BlockSpec((1,) + bias_c.shape[1:], lambda g, b, i: (g, 0, 0)),
                  pl.BlockSpec((1,) + bias_s.shape[1:], lambda g, b, i: (g, 0, 0, 0)),
                  pl.BlockSpec((1,) + bias_w.shape[1:], lambda g, b, i: (g, 0, 0, 0)),
                  pl.BlockSpec(xtra.shape, lambda g, b, i: (0, 0)),
                  pl.BlockSpec(ovl_t.shape, lambda g, b, i: (0, 0)),
                  pl.BlockSpec((1,) + gsel.shape[1:], lambda g, b, i: (g, 0, 0))],
        out_specs=pl.BlockSpec((NB, TQ, RW), lambda g, b, i: (b, i, g)),
        out_shape=jax.ShapeDtypeStruct((B, S, D_MODEL), BF16),
        scratch_shapes=[pltpu.VMEM((NB, S, 2 * LANE), BF16), pltpu.VMEM((NB, 2 * LANE, HW), BF16),
                        pltpu.VMEM((NB, V_ROWS, S), BF16), pltpu.VMEM((NB, V_ROWS, S), BF16),
                        pltpu.VMEM((NB, V_ROWS, ncr), BF16), pltpu.VMEM((NB, KT, HW), BF16),
                        pltpu.VMEM((NB, KT, HW), BF16)],
        compiler_params=_cparams(("parallel", "parallel", "arbitrary")),
        name="nsa_attention",
    )(proj3, kvc, proj3, proj3, proj3, bias_c, bias_s, bias_w, xtra, ovl_t, gsel)


def _rel_bucket(dist):
    n = jnp.maximum(dist, 0)
    nf = jnp.maximum(n, REL_MAX_EXACT).astype(F32)
    large = REL_MAX_EXACT + (jnp.log(nf / REL_MAX_EXACT) / math.log(REL_MAX_DIST / REL_MAX_EXACT)
                             * (REL_BUCKETS - REL_MAX_EXACT)).astype(jnp.int32)
    large = jnp.minimum(large, REL_BUCKETS - 1)
    return jnp.where(n < REL_MAX_EXACT, n, large)


def _nsa_tables(rel_table, S):
    TQ = Q_TILE
    tbl = rel_table.astype(F32).reshape(REL_BUCKETS, NSA_G, NSA_R)
    tbl = tbl - tbl[REL_BUCKETS - 1]

    def table(dist, ok):
        onehot = jax.nn.one_hot(_rel_bucket(dist), REL_BUCKETS, dtype=F32)
        b = jnp.einsum("...kqn,ngr->g...krq", onehot, tbl, precision=lax.Precision.HIGHEST)
        b = jnp.where(ok[..., :, None, :], b, NEG_INF)
        return b.reshape(b.shape[:-2] + (NSA_R * dist.shape[-1],))

    KT = K_TILE
    kk = jnp.arange(KT)[:, None]
    qq = jnp.arange(TQ)[None, :]
    dist = qq - kk - KT * (jnp.arange(TQ // KT + 1)[:, None, None] - 1)
    bias_s = table(dist, dist >= 0) * LOG2_E
    bias_s = jnp.concatenate([bias_s, jnp.zeros_like(bias_s[:, :1])], axis=1).astype(BF16)
    bias_w = table(dist, (dist >= 0) & (dist < WINDOW)) * LOG2_E
    bias_w = jnp.concatenate([bias_w, jnp.full_like(bias_w[:, :1], NEG_INF)], axis=1).astype(BF16)
    n_rows = S // CMP_STRIDE
    off = n_rows - TQ // CMP_STRIDE
    c_rel = jnp.arange(off + n_rows)[:, None] - off
    dist_c = qq - CMP_STRIDE * c_rel - (CMP_BLOCK - 1)
    bias_c = table(dist_c, dist_c >= 0)

    n_slc = S // SLC_BLOCK
    n_cmp = n_rows - CMP_BLOCK // CMP_STRIDE + 1
    xtra = (jnp.arange(LANE)[None, :] == (jnp.arange(S)[:, None] // SLC_BLOCK)).astype(BF16)
    cmp_start = jnp.arange(n_rows) * CMP_STRIDE
    slc_start = jnp.arange(n_slc) * SLC_BLOCK
    overlap = jnp.clip(jnp.minimum(cmp_start[:, None] + CMP_BLOCK, slc_start[None, :] + SLC_BLOCK)
                       - jnp.maximum(cmp_start[:, None], slc_start[None, :]), 0).astype(F32) / CMP_BLOCK
    overlap = jnp.where(jnp.arange(n_rows)[:, None] < n_cmp, overlap, 0.0)
    g = jnp.arange(NSA_G)[:, None, None]
    row = jnp.arange(16)[None, :, None]
    col = jnp.arange(LANE)[None, None, :]
    src = SMALL_GATE + (row // NSA_R) * NSA_HEADS + g * NSA_R + row % NSA_R
    gsel = ((col == src) & (row < 3 * NSA_R)).astype(BF16)
    return bias_c, bias_s, bias_w, xtra, overlap.T.astype(BF16), gsel


def _merge_kernel(ya_ref, yb_ref, yc_ref, g0_ref, g1_ref, g2_ref, x_ref, wb_ref, wo_ref, gn_ref, o_ref):
    m = None
    for br, (y_ref, g_ref) in enumerate(((ya_ref, g0_ref), (yb_ref, g1_ref), (yc_ref, g2_ref))):
        t = jax.nn.sigmoid(g_ref[...].astype(F32)) * jnp.dot(y_ref[...], wb_ref[br], preferred_element_type=F32)
        m = t if m is None else m + t
    z = jnp.dot(m.astype(BF16), wo_ref[...], preferred_element_type=F32)
    o_ref[...] = x_ref[...] + _rms(z, gn_ref[...])


def _merge(ya, yb, yc, proj, x2, wb, wo, gn, layer, tm=512):
    T, D = x2.shape
    row = lambda i: (i, 0)
    mcol = COL_MERGE // D
    return pl.pallas_call(
        _merge_kernel,
        grid=(T // tm,),
        in_specs=[pl.BlockSpec((tm, D), row), pl.BlockSpec((tm, D), row), pl.BlockSpec((tm, D), row),
                  pl.BlockSpec((tm, D), lambda i: (i, mcol)),
                  pl.BlockSpec((tm, D), lambda i: (i, mcol + 1)),
                  pl.BlockSpec((tm, D), lambda i: (i, mcol + 2)),
                  pl.BlockSpec((tm, D), row),
                  _const_spec(wb.shape[1:], layer), _const_spec(wo.shape[1:], layer), _const_spec((1, D))],
        out_specs=pl.BlockSpec((tm, D), row),
        out_shape=jax.ShapeDtypeStruct((T, D), F32),
        compiler_params=_cparams(("parallel",)),
        name="merge_out",
    )(ya, yb, yc, proj, proj, proj, x2, wb, wo, gn)


def _ffn_kernel(x_ref, gpre_ref, win_ref, wout_ref, gpost_ref, o_ref, *, chunk):
    x = x_ref[...]
    h = _rms(x, gpre_ref[...]).astype(BF16)
    acc = None
    for c in range(D_FF // chunk):
        gt = jnp.dot(h, win_ref[:, c * chunk:(c + 1) * chunk], preferred_element_type=F32)
        up = jnp.dot(h, win_ref[:, D_FF + c * chunk:D_FF + (c + 1) * chunk], preferred_element_type=F32)
        a = (gt * jax.nn.sigmoid(gt) * up).astype(BF16)
        t = jnp.dot(a, wout_ref[c * chunk:(c + 1) * chunk, :], preferred_element_type=F32)
        acc = t if acc is None else acc + t
    o_ref[...] = x + _rms(acc, gpost_ref[...])


def _ffn(x2, gpre, win, wout, gpost, layer, tm=512, chunk=256):
    T, D = x2.shape
    row = lambda i: (i, 0)
    return pl.pallas_call(
        functools.partial(_ffn_kernel, chunk=chunk),
        grid=(T // tm,),
        in_specs=[pl.BlockSpec((tm, D), row), _const_spec((1, D)), _const_spec(win.shape[1:], layer),
                  _const_spec(wout.shape[1:], layer), _const_spec((1, D))],
        out_specs=pl.BlockSpec((tm, D), row),
        out_shape=jax.ShapeDtypeStruct((T, D), F32),
        compiler_params=_cparams(("parallel",)),
        name="ffn",
    )(x2, gpre, win, wout, gpost)


def _in_column_sources():
    (lru_x, lru_g, nsa_q, nsa_kv, nsa_gate, gla_q, gla_k, gla_v, gla_og, gla_lr, merge) = np.split(
        np.arange(sum(IN_SIZES)), np.cumsum(IN_SIZES)[:-1])
    kv = nsa_kv.reshape(3, 2, NSA_G, NSA_HD).transpose(0, 2, 1, 3).reshape(-1)
    pad = np.full(LANE - nsa_gate.size - gla_lr.size, -1)
    src = np.concatenate([lru_x, lru_g, nsa_q, gla_v, gla_og, merge, kv, gla_q, gla_k, nsa_gate, gla_lr, pad])
    assert src.size == PROJ_WIDTH
    return src


def _permute_plan():
    src = _in_column_sources()
    width_in = sum(IN_SIZES)
    mats, ids, plan = [], {}, []
    for c in range(PROJ_WIDTH // LANE):
        cols = src[c * LANE:(c + 1) * LANE]
        terms = []
        for b in sorted(set(int(s) // LANE for s in cols if s >= 0)):
            m = np.zeros((LANE, LANE), np.float32)
            for j, s in enumerate(cols):
                if s >= 0 and s // LANE == b:
                    m[s - b * LANE, j] = 1.0
            key = m.tobytes()
            if key not in ids:
                ids[key] = len(mats)
                mats.append(m)
            terms.append((b, ids[key], min(LANE, width_in - b * LANE)))
        plan.append(tuple(terms))
    identity = ids.get(np.eye(LANE, dtype=np.float32).tobytes(), -1)
    return tuple(plan), np.stack(mats), identity


def _permute_kernel(w_ref, m_ref, o_ref, *, plan, identity):
    for c, terms in enumerate(plan):
        dst = slice(c * LANE, (c + 1) * LANE)
        if len(terms) == 1 and terms[0][1] == identity:
            b = terms[0][0]
            o_ref[:, dst] = w_ref[:, b * LANE:(b + 1) * LANE].astype(o_ref.dtype)
            continue
        acc = jnp.zeros((o_ref.shape[0], LANE), F32)
        for b, mid, width in terms:
            x = w_ref[:, b * LANE:b * LANE + width].astype(BF16)
            acc = acc + jnp.dot(x, m_ref[mid, :width, :], preferred_element_type=F32)
        o_ref[:, dst] = acc.astype(o_ref.dtype)


def _permute_in_weights(w_in, tr=256):
    L, D, W = w_in.shape
    plan, mats, identity = _permute_plan()
    mats = jnp.asarray(mats, BF16)
    return pl.pallas_call(
        functools.partial(_permute_kernel, plan=plan, identity=identity),
        grid=(L, D // tr),
        in_specs=[pl.BlockSpec((None, tr, W), lambda l, i: (l, i, 0)), _const_spec(mats.shape)],
        out_specs=pl.BlockSpec((None, tr, PROJ_WIDTH), lambda l, i: (l, i, 0)),
        out_shape=jax.ShapeDtypeStruct((L, D, PROJ_WIDTH), BF16),
        compiler_params=_cparams(("parallel", "parallel")),
        name="permute_w_in",
    )(w_in, mats)


def _layer(x2, B, S, tables, layer, norm_g, w_in, conv_w, conv_b, lru_w_gates, lru_b_gates, lru_lambda,
           cmp_pos, cmp_w1, cmp_w2, gla_wa2, gla_ba, gla_norm, w_branch, w_out, w_ffn_in, w_ffn_out):
    D = D_MODEL
    proj = _in_proj(x2, norm_g[0][None, :], w_in, layer)
    proj3 = proj.reshape(B, S, PROJ_WIDTH)

    wg = jnp.concatenate([lru_w_gates[0], lru_w_gates[1]], axis=-1).astype(BF16)
    y_a = _lru(proj3, conv_w, conv_b[None, :], wg, lru_b_gates, lru_lambda[None, :])

    kvc = _compress(proj3, *_compress_params(cmp_pos, cmp_w1, cmp_w2))
    y_b = _nsa(proj3, kvc, *tables)

    wa_pad = jnp.zeros((LANE, GLA_HEADS * GLA_DK), F32).at[SMALL_LR:SMALL_LR + GLA_RANK].set(gla_wa2).astype(BF16)
    y_c = _gla(proj3, wa_pad, gla_ba[None, :], gla_norm[None, :])

    x2 = _merge(y_a.reshape(B * S, D), y_b.reshape(B * S, D), y_c.reshape(B * S, D), proj, x2,
                w_branch, w_out, norm_g[1][None, :], layer)
    x2 = _ffn(x2, norm_g[2][None, :], w_ffn_in, w_ffn_out, norm_g[3][None, :], layer)
    return x2


def kernel(x, rel_table, norm_g, w_in, conv_w, conv_b, lru_w_gates, lru_b_gates, lru_lambda, cmp_pos, cmp_w1,
           cmp_w2, gla_wa2, gla_ba, gla_norm, w_branch, w_out, w_ffn_in, w_ffn_out):
    B, S, D = x.shape
    tables = _nsa_tables(rel_table, S)
    x2 = x.reshape(B * S, D)
    w_in = _permute_in_weights(w_in)
    w_branch, w_out, w_ffn_in, w_ffn_out = (w.astype(BF16) for w in (w_branch, w_out, w_ffn_in, w_ffn_out))
    for l in range(norm_g.shape[0]):
        x2 = _layer(x2, B, S, tables, l, norm_g[l], w_in, conv_w[l], conv_b[l], lru_w_gates[l], lru_b_gates[l],
                    lru_lambda[l], cmp_pos[l], cmp_w1[l], cmp_w2[l], gla_wa2[l], gla_ba[l], gla_norm[l],
                    w_branch, w_out, w_ffn_in, w_ffn_out)
    return x2.reshape(B, S, D)
```

```python
import functools
import math

import jax
import jax.numpy as jnp
import numpy as np
from jax import lax
from jax.experimental import pallas as pl
from jax.experimental.pallas import tpu as pltpu

F32 = jnp.float32
BF16 = jnp.bfloat16

D_MODEL = 1024
N_BRANCH = 3
EPS = 1e-6
NEG_INF = -1e30
FORCED = 1e4

LRU_BLOCKS = 8
LRU_BLOCK = D_MODEL // LRU_BLOCKS
CONV_WIDTH = 4
LRU_C = 8.0

NSA_HEADS = 16
NSA_G = 4
NSA_R = NSA_HEADS // NSA_G
NSA_HD = D_MODEL // NSA_HEADS
CMP_BLOCK = 32
CMP_STRIDE = 16
CMP_HIDDEN = 256
SLC_BLOCK = 64
SLC_TOP_N = 8
WINDOW = 256

GLA_HEADS = 4
GLA_DK = (D_MODEL // 2) // GLA_HEADS
GLA_DV = D_MODEL // GLA_HEADS
GLA_RANK = 16
GLA_TAU = 16.0
GLA_CHUNK = 32

REL_BUCKETS = 32
REL_MAX_EXACT = 16
REL_MAX_DIST = 128

D_FF = -(-8 * D_MODEL // (3 * 256)) * 256

IN_SIZES = (D_MODEL, D_MODEL, D_MODEL, 6 * NSA_G * NSA_HD, 3 * NSA_HEADS, D_MODEL // 2, D_MODEL // 2,
            D_MODEL, D_MODEL, GLA_RANK, N_BRANCH * D_MODEL)

LANE = 128
COL_LRU_X = 0
COL_LRU_G = 1024
COL_NSA_Q = 2048
COL_GLA_V = 3072
COL_GLA_OG = 4096
COL_MERGE = 5120
COL_NSA_KV = 8192
COL_GLA_Q = 9728
COL_GLA_K = 10240
COL_SMALL = 10752
PROJ_WIDTH = 10880
SMALL_GATE = 0
SMALL_LR = 3 * NSA_HEADS

Q_TILE = 256
K_TILE = 256
NSA_BATCH_ROWS = 4
VMEM_LIMIT = 56 * 1024 * 1024
_NT = (((1,), (1,)), ((), ()))


def _cparams(sem):
    return pltpu.CompilerParams(dimension_semantics=sem, vmem_limit_bytes=VMEM_LIMIT)


def _const_spec(shape, layer=None):
    nd = len(shape)
    if layer is None:
        return pl.BlockSpec(shape, lambda *_: (0,) * nd, pipeline_mode=pl.Buffered(1))
    return pl.BlockSpec((None,) + tuple(shape), lambda *_: (layer,) + (0,) * nd, pipeline_mode=pl.Buffered(1))


def _rms(x, g):
    return x * lax.rsqrt(jnp.mean(x * x, axis=-1, keepdims=True) + EPS) * g


def _gelu_tanh(x):
    return 0.5 * x * (1.0 + jnp.tanh(math.sqrt(2.0 / math.pi) * (x + 0.044715 * (x * x * x))))


def _sigmoid(x):
    return 0.5 * jnp.tanh(0.5 * x) + 0.5


def _sqrt_nonneg(x):
    return jnp.where(x > 0.0, x * lax.rsqrt(x), 0.0)


def _softplus(z):
    return jnp.maximum(z, 0.0) + jnp.log1p(jnp.exp(-jnp.abs(z)))


def _in_proj_kernel(x_ref, g_ref, w_ref, o_ref, h_ref):
    @pl.when(pl.program_id(1) == 0)
    def _():
        h_ref[...] = _rms(x_ref[...], g_ref[...]).astype(BF16)

    o_ref[...] = jnp.dot(h_ref[...], w_ref[...], preferred_element_type=F32).astype(o_ref.dtype)


def _in_proj(x2, g, w, layer, tm=1024, tn=2176):
    T, D = x2.shape
    N = w.shape[2]
    return pl.pallas_call(
        _in_proj_kernel,
        grid=(T // tm, N // tn),
        in_specs=[pl.BlockSpec((tm, D), lambda i, j: (i, 0)),
                  pl.BlockSpec((1, D), lambda i, j: (0, 0)),
                  pl.BlockSpec((None, D, tn), lambda i, j: (layer, 0, j))],
        out_specs=pl.BlockSpec((tm, tn), lambda i, j: (i, j)),
        out_shape=jax.ShapeDtypeStruct((T, N), BF16),
        scratch_shapes=[pltpu.VMEM((tm, D), BF16)],
        compiler_params=_cparams(("parallel", "arbitrary")),
        name="in_proj",
    )(x2, g, w)


def _lru_kernel(xa_ref, ga_ref, cw_ref, cb_ref, wg_ref, bg_ref, lam_ref, y_ref, tail_ref, h_ref, *, R):
    @pl.when(pl.program_id(1) == 0)
    def _():
        tail_ref[...] = jnp.zeros_like(tail_ref)
        h_ref[...] = jnp.zeros_like(h_ref)

    sp = _softplus(-lam_ref[...])
    groups = R // 8
    row8 = lax.broadcasted_iota(jnp.int32, (groups, 8, LRU_BLOCK), 1)
    for n in range(LRU_BLOCKS):
        sl = slice(n * LRU_BLOCK, (n + 1) * LRU_BLOCK)
        x3 = xa_ref[0, :, sl].astype(F32).reshape(groups, 8, LRU_BLOCK)
        tail = tail_ref[:, sl].reshape(1, 8, LRU_BLOCK)
        xc3 = cb_ref[:, sl] + cw_ref[CONV_WIDTH - 1:CONV_WIDTH, sl] * x3
        for s in range(1, CONV_WIDTH):
            rolled = pltpu.roll(x3, s, axis=1)
            prev = jnp.concatenate([pltpu.roll(tail, s, axis=1), rolled[:groups - 1]], axis=0)
            w = cw_ref[CONV_WIDTH - 1 - s:CONV_WIDTH - s, sl]
            xc3 = xc3 + w * jnp.where(row8 >= s, rolled, prev)
        tail_ref[:, sl] = x3[groups - 1]
        xcn = xc3.reshape(R, LRU_BLOCK)
        gz =jnp.dot(xcn.astype(BF16), wg_ref[n], preferred_element_type=F32)
        r = _sigmoid(gz[:, :LRU_BLOCK] + bg_ref[0:1, sl])
        i = _sigmoid(gz[:, LRU_BLOCK:] + bg_ref[1:2, sl])
        a = jnp.exp((-LRU_C) * r * sp[:, sl])
        u = _sqrt_nonneg(1.0 - a * a) * (i * xcn)
        a = a.reshape(R // 8, 8, LRU_BLOCK)
        u = u.reshape(R // 8, 8, LRU_BLOCK)
        for d in (1, 2, 4):
            keep = row8 >= d
            a_s = pltpu.roll(a, d, axis=1)
            u_s = pltpu.roll(u, d, axis=1)
            u = jnp.where(keep, a * u_s + u, u)
            a = jnp.where(keep, a * a_s, a)
        hprev = h_ref[:, sl]
        hs = []
        for t in range(R // 8):
            ht = a[t] * hprev + u[t]
            hprev = ht[7:8]
            hs.append(ht)
        h_ref[:, sl] = hprev
        h = jnp.concatenate(hs, axis=0)
        y_ref[0, :, sl] = (h * _gelu_tanh(ga_ref[0, :, sl].astype(F32))).astype(y_ref.dtype)


def _lru(proj3, conv_w, conv_b, wg, bg, lam, R=256):
    B, S, _ = proj3.shape
    D = D_MODEL
    return pl.pallas_call(
        functools.partial(_lru_kernel, R=R),
        grid=(B, S // R),
        in_specs=[pl.BlockSpec((1, R, D), lambda b, s: (b, s, COL_LRU_X // D)),
                  pl.BlockSpec((1, R, D), lambda b, s: (b, s, COL_LRU_G // D)),
                  _const_spec((CONV_WIDTH, D)),
                  _const_spec((1, D)),
                  _const_spec((LRU_BLOCKS, LRU_BLOCK, 2 * LRU_BLOCK)),
                  _const_spec((2, D)),
                  _const_spec((1, D))],
        out_specs=pl.BlockSpec((1, R, D), lambda b, s: (b, s, 0)),
        out_shape=jax.ShapeDtypeStruct((B, S, D), BF16),
        scratch_shapes=[pltpu.VMEM((8, D), F32), pltpu.VMEM((1, D), F32)],
        compiler_params=_cparams(("parallel", "arbitrary")),
        name="rglru",
    )(proj3, proj3, conv_w, conv_b, wg, bg, lam)


def _gla_kernel(q_ref, k_ref, v_ref, og_ref, sm_ref, wa_ref, ba_ref, gn_ref, y_ref, st_ref, *, Tb):
    nc = Tb // GLA_CHUNK

    @pl.when(pl.program_id(1) == 0)
    def _():
        st_ref[...] = jnp.zeros_like(st_ref)

    la_pre = jnp.dot(sm_ref[0], wa_ref[...], preferred_element_type=F32)
    rowc = lax.broadcasted_iota(jnp.int32, (Tb, GLA_DK), 0) & (GLA_CHUNK - 1)
    ri = lax.broadcasted_iota(jnp.int32, (Tb, Tb), 0)
    ci = lax.broadcasted_iota(jnp.int32, (Tb, Tb), 1)
    same_chunk_causal = ((ri // GLA_CHUNK) == (ci // GLA_CHUNK)) & (ci <= ri)
    lane_chunk = lax.broadcasted_iota(jnp.int32, (GLA_DK, Tb), 1) // GLA_CHUNK

    heads = range(GLA_HEADS)
    kcols = [slice(h * GLA_DK, (h + 1) * GLA_DK) for h in heads]
    vcols = [slice(h * GLA_DV, (h + 1) * GLA_DV) for h in heads]

    def chunk_log_decay(h):
        z = la_pre[:, kcols[h]] + ba_ref[:, kcols[h]]
        b = (jnp.minimum(z, 0.0) - jnp.log(1.0 + jnp.exp(-jnp.abs(z)))) * (1.0 / GLA_TAU)
        d = 1
        while d < GLA_CHUNK:
            b = b + jnp.where(rowc >= d, pltpu.roll(b, d, axis=0), 0.0)
            d *= 2
        return b

    b = [chunk_log_decay(h) for h in heads]
    b_last = [jnp.concatenate(
        [jnp.broadcast_to(b[h][c * GLA_CHUNK + GLA_CHUNK - 1:(c + 1) * GLA_CHUNK, :], (GLA_CHUNK, GLA_DK))
         for c in range(nc)], axis=0) for h in heads]
    q = [q_ref[0, :, kcols[h]].astype(F32) * (GLA_DK ** -0.5) for h in heads]
    k = [k_ref[0, :, kcols[h]].astype(F32) for h in heads]
    v = [v_ref[0, :, vcols[h]] for h in heads]
    qt = [(q[h] * jnp.exp(b[h])).astype(BF16) for h in heads]
    kt = [(k[h] * jnp.exp(-b[h])).astype(BF16) for h in heads]
    ke_t = [(k[h] * jnp.exp(b_last[h] - b[h])).T for h in heads]
    b_t = [b[h].T for h in heads]

    att = [jnp.where(same_chunk_causal, lax.dot_general(qt[h], kt[h], _NT, preferred_element_type=F32), 0.0)
           for h in heads]
    o = [jnp.dot(att[h].astype(BF16), v[h], preferred_element_type=F32) for h in heads]
    kv_all = [jnp.dot(jnp.concatenate([jnp.where(lane_chunk == c, ke_t[h], 0.0) for c in range(nc)],
                                      axis=0).astype(BF16), v[h], preferred_element_type=F32)
              for h in heads]

    state = [st_ref[h] for h in heads]
    inter = [[] for _ in heads]
    for c in range(nc):
        rs = slice(c * GLA_CHUNK, (c + 1) * GLA_CHUNK)
        last = c * GLA_CHUNK + GLA_CHUNK - 1
        for h in heads:
            inter[h].append(jnp.dot(qt[h][rs], state[h].astype(BF16), preferred_element_type=F32))
            state[h] = state[h] * jnp.exp(b_t[h][:, last:last + 1]) + kv_all[h][c * GLA_DK:(c + 1) * GLA_DK]
    for h in heads:
        st_ref[h] = state[h]
        oh = o[h] + jnp.concatenate(inter[h], axis=0)
        on = oh * lax.rsqrt(jnp.mean(oh * oh, axis=-1, keepdims=True) + EPS) * gn_ref[...]
        og = og_ref[0, :, vcols[h]].astype(F32)
        y_ref[0, :, vcols[h]] = (on * (og * jax.nn.sigmoid(og))).astype(y_ref.dtype)


def _gla(proj3, wa_pad, ba, gn, Tb=256):
    B, S, _ = proj3.shape
    HK = GLA_HEADS * GLA_DK
    D = D_MODEL
    return pl.pallas_call(
        functools.partial(_gla_kernel, Tb=Tb),
        grid=(B, S // Tb),
        in_specs=[pl.BlockSpec((1, Tb, HK), lambda b, s: (b, s, COL_GLA_Q // HK)),
                  pl.BlockSpec((1, Tb, HK), lambda b, s: (b, s, COL_GLA_K // HK)),
                  pl.BlockSpec((1, Tb, D), lambda b, s: (b, s, COL_GLA_V // D)),
                  pl.BlockSpec((1, Tb, D), lambda b, s: (b, s, COL_GLA_OG // D)),
                  pl.BlockSpec((1, Tb, LANE), lambda b, s: (b, s, COL_SMALL // LANE)),
                  _const_spec((LANE, HK)),
                  _const_spec((1, HK)),
                  _const_spec((1, GLA_DV))],
        out_specs=pl.BlockSpec((1, Tb, D), lambda b, s: (b, s, 0)),
        out_shape=jax.ShapeDtypeStruct((B, S, D), BF16),
        scratch_shapes=[pltpu.VMEM((GLA_HEADS, GLA_DK, GLA_DV), F32)],
        compiler_params=_cparams(("parallel", "arbitrary")),
        name="gla",
    )(proj3, proj3, proj3, proj3, proj3, wa_pad, ba, gn)


def _cmp_kernel(kv_ref, pos_ref, w1_ref, w2_ref, o_ref, x_ref):
    x_ref[...] = kv_ref[0].astype(F32)
    nrow = x_ref.shape[0] // CMP_STRIDE
    first = None
    second = None
    for i in range(CMP_STRIDE):
        xi = x_ref[pl.ds(i, nrow, stride=CMP_STRIDE), :]
        f = jnp.dot((xi + pos_ref[0, i]).astype(BF16), w1_ref[0, i], preferred_element_type=F32)
        s = jnp.dot((xi + pos_ref[1, i]).astype(BF16), w1_ref[1, i], preferred_element_type=F32)
        first = f if first is None else first + f
        second = s if second is None else second + s
    pre = first + pltpu.roll(second, nrow - 1, axis=0)
    hid = _gelu_tanh(pre).astype(BF16)
    o_ref[0, 0] = jnp.dot(hid, w2_ref[...], preferred_element_type=F32).astype(o_ref.dtype)


def _compress(proj3, pos, w1, w2):
    B, S, _ = proj3.shape
    nrow = S // CMP_STRIDE
    kvb = COL_NSA_KV // LANE
    return pl.pallas_call(
        _cmp_kernel,
        grid=(B, NSA_G),
        in_specs=[pl.BlockSpec((1, S, LANE), lambda b, g: (b, 0, kvb + g)),
                  _const_spec(pos.shape), _const_spec(w1.shape), _const_spec(w2.shape)],
        out_specs=pl.BlockSpec((1, 1, nrow, LANE), lambda b, g: (b, g, 0, 0)),
        out_shape=jax.ShapeDtypeStruct((B, NSA_G, nrow, LANE), BF16),
        scratch_shapes=[pltpu.VMEM((S, LANE), F32)],
        compiler_params=_cparams(("parallel", "parallel")),
        name="nsa_compress",
    )(proj3, pos, w1, w2)


def _compress_params(cmp_pos, cmp_w1, cmp_w2):
    hd = NSA_HD
    w1 = cmp_w1.reshape(2, 2, CMP_STRIDE, hd, CMP_HIDDEN)
    z = jnp.zeros_like(w1[0])
    w1 = jnp.concatenate([jnp.concatenate([w1[0], z], axis=-1),
                          jnp.concatenate([z, w1[1]], axis=-1)], axis=-2)
    pos = cmp_pos.reshape(2, 2, CMP_STRIDE, 1, hd)
    pos = jnp.concatenate([pos[0], pos[1]], axis=-1)
    z2 = jnp.zeros_like(cmp_w2[0])
    w2 = jnp.concatenate([jnp.concatenate([cmp_w2[0], z2], axis=-1),
                          jnp.concatenate([z2, cmp_w2[1]], axis=-1)], axis=0)
    return pos, w1.astype(BF16), w2.astype(BF16)


V_ROWS = NSA_HD + 16
MASK_BIG = 1e30
LOG2_E = 1.4426950408889634


def _nsa_kernel(q_ref, kvc_ref, kvs_ref, kvw_ref, gate_ref, bc_ref, bs_ref, bw_ref, xtra_ref, ovl_ref,
                gsel_ref, y_ref, a_ref, b_ref, vs_ref, vw_ref, vc_ref, lg_ref, p_ref):
    NB = q_ref.shape[0]
    rows_nb = range(NB)
    TQ = Q_TILE
    KT = K_TILE
    RT = TQ // KT
    HW = NSA_R * TQ
    qi = pl.program_id(2)
    S = kvs_ref.shape[1]
    n_cmp_rows = kvc_ref.shape[2]

    def value_rows(tile):
        t = tile.astype(F32).T
        return jnp.concatenate([t[NSA_HD:], jnp.ones((V_ROWS - NSA_HD, t.shape[1]), F32)], axis=0).astype(BF16)

    @pl.when(qi == 0)
    def _():
        for nb in rows_nb:
            for j in range(S // LANE):
                rows = slice(j * LANE, (j + 1) * LANE)
                ks = kvs_ref[nb, rows, :]
                a_ref[nb, rows, 0:LANE] = ks
                a_ref[nb, rows, LANE:2 * LANE] = xtra_ref[rows, :]
                vs_ref[nb, :, rows] = value_rows(ks)
                vw_ref[nb, :, rows] = value_rows(kvw_ref[nb, rows, :])
            vc_ref[nb] = value_rows(kvc_ref[nb, 0])
            b_ref[nb, LANE + ovl_ref.shape[0]:, :] = jnp.zeros((LANE - ovl_ref.shape[0], HW), BF16)

    def tile_rows(j, n=1):
        return pl.ds(pl.multiple_of(j * KT, KT), n * KT)

    def normalised(acc):
        return acc[:NSA_HD] * (1.0 / acc[NSA_HD:NSA_HD + 1])

    def q_transposed(nb):
        qf = q_ref[nb].astype(F32) * (NSA_HD ** -0.5)
        zero_rows = jnp.zeros((LANE - NSA_HD, TQ), F32)
        heads = []
        for pr in range(NSA_R // 2):
            t = qf[:, pr * LANE:(pr + 1) * LANE].T
            heads.append(jnp.concatenate([t[:NSA_HD], zero_rows], axis=0))
            heads.append(jnp.concatenate([t[NSA_HD:], zero_rows], axis=0))
        return jnp.concatenate(heads, axis=1)

    qt32 = [q_transposed(nb) for nb in rows_nb]
    qt = [x.astype(BF16) for x in qt32]
    qt32_l2 = [x * LOG2_E for x in qt32]
    qt_l2 = [x.astype(BF16) for x in qt32_l2]

    first = (qi == 0).astype(jnp.int32)
    rows_w = tile_rows(jnp.maximum(RT * qi - 1, 0), RT + 1)
    bias_w = jnp.concatenate([bw_ref[0, t + first] for t in range(RT + 1)], axis=0)
    lg_w = [jnp.dot(kvw_ref[nb, rows_w, :], qt_l2[nb], preferred_element_type=F32).astype(BF16) + bias_w
            for nb in rows_nb]
    p_w = [jnp.exp2(lg - jnp.max(lg, axis=0, keepdims=True)) for lg in lg_w]
    o_win = [normalised(jnp.dot(vw_ref[nb, :, rows_w], p_w[nb], preferred_element_type=F32)) for nb in rows_nb]

    cq = TQ // CMP_STRIDE
    start_c = pl.multiple_of(bc_ref.shape[1] - n_cmp_rows - cq * qi, 8)
    bias_c = bc_ref[0, pl.ds(start_c, n_cmp_rows), :]
    valid_c = bias_c > 0.5 * NEG_INF
    lg_c = [jnp.dot(kvc_ref[nb, 0], qt[nb], preferred_element_type=F32) + bias_c for nb in rows_nb]
    e_c = [jnp.exp(lg - jnp.max(lg, axis=0, keepdims=True)) for lg in lg_c]
    p_c = [jnp.where(valid_c, e * (1.0 / jnp.sum(e, axis=0, keepdims=True)), 0.0) for e in e_c]
    o_cmp = [jnp.dot(vc_ref[nb], p_c[nb].astype(BF16), preferred_element_type=F32)[:NSA_HD] for nb in rows_nb]

    ovl = ovl_ref[...]
    ns = ovl.shape[0]
    blk = lax.broadcasted_iota(jnp.int32, (ns, TQ), 0)
    cur = (qi * TQ + lax.broadcasted_iota(jnp.int32, (ns, TQ), 1)) // SLC_BLOCK
    forced = (blk == 0) | (blk == cur) | (blk == cur - 1)
    allowed = blk <= cur

    def importance(p):
        psum = p[:, 0:TQ]
        for r in range(1, NSA_R):
            psum = psum + p[:, r * TQ:(r + 1) * TQ]
        p_hi = psum.astype(BF16)
        p_lo = (psum - p_hi.astype(F32)).astype(BF16)
        imp = jnp.dot(ovl, p_hi, preferred_element_type=F32) + jnp.dot(ovl, p_lo, preferred_element_type=F32)
        return jnp.where(forced, FORCED, jnp.where(allowed, imp, -FORCED))

    score = [importance(p) for p in p_c]
    groups = ns // 8
    blk8 = lax.broadcasted_iota(jnp.int32, (8, TQ), 0)
    for nb in rows_nb:
        sc = [score[nb][8 * v:8 * v + 8] for v in range(groups)]
        rank = [jnp.zeros((8, TQ), F32) for _ in range(groups)]
        for j in range(ns):
            sj = score[nb][j:j + 1, :]
            for v in range(groups):
                if j < 8 * v:
                    beats = sj >= sc[v]
                elif j >= 8 * v + 8:
                    beats = sj > sc[v]
                else:
                    tie = jnp.where(blk8 > j - 8 * v, 1.0, 0.0)
                    rank[v] = rank[v] + jnp.where(sj > sc[v], 1.0, jnp.where(sj == sc[v], tie, 0.0))
                    continue
                rank[v] = rank[v] + jnp.where(beats, 1.0, 0.0)
        neg_sel = jnp.where(jnp.concatenate(rank, axis=0) < float(SLC_TOP_N), 0.0, -MASK_BIG).astype(BF16)
        b_ref[nb, 0:LANE] = qt_l2[nb]
        b_ref[nb, LANE:LANE + ns] = jnp.concatenate([neg_sel] * NSA_R, axis=1)

    def logits(nb, j):
        return jnp.dot(a_ref[nb, tile_rows(j), :], b_ref[nb], preferred_element_type=F32).astype(BF16)

    for nb in rows_nb:
        lg_ref[nb] = logits(nb, 0)
        p_ref[nb] = jnp.zeros(p_ref.shape[1:], BF16)

    last = RT * qi + RT - 1

    def trip(j, carry):
        lg_next = [logits(nb, jnp.minimum(j + 1, last)) for nb in rows_nb]
        prev_rows = tile_rows(jnp.maximum(j - 1, 0))
        bias = bs_ref[0, jnp.where(j >= RT * qi - 1, j - RT * qi + 1, RT + 1)]
        out = []
        for nb in rows_nb:
            m, acc, alpha = carry[nb]
            acc = acc * alpha + jnp.dot(vs_ref[nb, :, prev_rows], p_ref[nb], preferred_element_type=F32)
            lg = lg_ref[nb] + bias
            m_new = jnp.maximum(m, jnp.max(lg, axis=0, keepdims=True).astype(F32))
            p_ref[nb] = jnp.exp2(lg - m_new.astype(BF16))
            out.append((m_new, acc, jnp.exp2(m - m_new)))
        for nb in rows_nb:
            lg_ref[nb] = lg_next[nb]
        return tuple(out)

    init = (jnp.full((1, HW), NEG_INF, F32), jnp.zeros((V_ROWS, HW), F32), jnp.ones((1, HW), F32))
    carry = lax.fori_loop(0, last + 1, trip, tuple(init for _ in rows_nb))
    o_slc = [normalised(carry[nb][1] * carry[nb][2]
                        + jnp.dot(vs_ref[nb, :, tile_rows(last)], p_ref[nb], preferred_element_type=F32))
             for nb in rows_nb]

    for nb in rows_nb:
        gs = jax.nn.sigmoid(lax.dot_general(gsel_ref[0], gate_ref[nb], _NT, preferred_element_type=F32))

        def gate_row(br):
            return jnp.concatenate([gs[br * NSA_R + r:br * NSA_R + r + 1] for r in range(NSA_R)], axis=1)

        o = gate_row(0) * o_cmp[nb] + gate_row(1) * o_slc[nb] + gate_row(2) * o_win[nb]
        o = jnp.concatenate([o[:, r * TQ:(r + 1) * TQ] for r in range(NSA_R)], axis=0)
        y_ref[nb] = o.T.astype(y_ref.dtype)


def _nsa(proj3, kvc, bias_c, bias_s, bias_w, xtra, ovl_t, gsel):
    B, S, _ = proj3.shape
    TQ = Q_TILE
    G = NSA_G
    RW = NSA_R * NSA_HD
    HW = NSA_R * TQ
    kvb = COL_NSA_KV // LANE
    ncr = kvc.shape[2]
    KT = K_TILE
    NB = NSA_BATCH_ROWS
    assert S % TQ == 0 and TQ % KT == 0 and S >= TQ + KT and WINDOW == KT and ncr % LANE == 0 and B % NB == 0
    return pl.pallas_call(
        _nsa_kernel,
        grid=(G, B // NB, S // TQ),
        in_specs=[pl.BlockSpec((NB, TQ, RW), lambda g, b, i: (b, i, COL_NSA_Q // RW + g)),
                  pl.BlockSpec((NB, 1, ncr, LANE), lambda g, b, i: (b, g, 0, 0)),
                  pl.BlockSpec((NB, S, LANE), lambda g, b, i: (b, 0, kvb + G + g)),
                  pl.BlockSpec((NB, S, LANE), lambda g, b, i: (b, 0, kvb + 2 * G + g)),
                  pl.BlockSpec((NB, TQ, LANE), lambda g, b, i: (b, i, COL_SMALL // LANE)),
                  pl.BlockSpec((1,) + bias_c.shape[1:], lambda g, b, i: (g, 0, 0)),
                  pl.BlockSpec((1,) + bias_s.shape[1:], lambda g, b, i: (g, 0, 0, 0)),
                  pl.BlockSpec((1,) + bias_w.shape[1:], lambda g, b, i: (g, 0, 0, 0)),
                  pl.BlockSpec(xtra.shape, lambda g, b, i: (0, 0)),
                  pl.BlockSpec(ovl_t.shape, lambda g, b, i: (0, 0)),
                  pl.BlockSpec((1,) + gsel.shape[1:], lambda g, b, i: (g, 0, 0))],
        out_specs=pl.BlockSpec((NB, TQ, RW), lambda g, b, i: (b, i, g)),
        out_shape=jax.ShapeDtypeStruct((B, S, D_MODEL), BF16),
        scratch_shapes=[pltpu.VMEM((NB, S, 2 * LANE), BF16), pltpu.VMEM((NB, 2 * LANE, HW), BF16),
                        pltpu.VMEM((NB, V_ROWS, S), BF16), pltpu.VMEM((NB, V_ROWS, S), BF16),
                        pltpu.VMEM((NB, V_ROWS, ncr), BF16), pltpu.VMEM((NB, KT, HW), BF16),
                        pltpu.VMEM((NB, KT, HW), BF16)],
        compiler_params=_cparams(("parallel", "parallel", "arbitrary")),
        name="nsa_attention",
    )(proj3, kvc, proj3, proj3, proj3, bias_c, bias_s, bias_w, xtra, ovl_t, gsel)


def _rel_bucket(dist):
    n = jnp.maximum(dist, 0)
    nf = jnp.maximum(n, REL_MAX_EXACT).astype(F32)
    large = REL_MAX_EXACT + (jnp.log(nf / REL_MAX_EXACT) / math.log(REL_MAX_DIST / REL_MAX_EXACT)
                             * (REL_BUCKETS - REL_MAX_EXACT)).astype(jnp.int32)
    large = jnp.minimum(large, REL_BUCKETS - 1)
    return jnp.where(n < REL_MAX_EXACT, n, large)


def _nsa_tables(rel_table, S):
    TQ = Q_TILE
    tbl = rel_table.astype(F32).reshape(REL_BUCKETS, NSA_G, NSA_R)
    tbl = tbl - tbl[REL_BUCKETS - 1]

    def table(dist, ok):
        onehot = jax.nn.one_hot(_rel_bucket(dist), REL_BUCKETS, dtype=F32)
        b = jnp.einsum("...kqn,ngr->g...krq", onehot, tbl, precision=lax.Precision.HIGHEST)
        b = jnp.where(ok[..., :, None, :], b, NEG_INF)
        return b.reshape(b.shape[:-2] + (NSA_R * dist.shape[-1],))

    KT = K_TILE
    kk = jnp.arange(KT)[:, None]
    qq = jnp.arange(TQ)[None, :]
    dist = qq - kk - KT * (jnp.arange(TQ // KT + 1)[:, None, None] - 1)
    bias_s = table(dist, dist >= 0) * LOG2_E
    bias_s = jnp.concatenate([bias_s, jnp.zeros_like(bias_s[:, :1])], axis=1).astype(BF16)
    bias_w = table(dist, (dist >= 0) & (dist < WINDOW)) * LOG2_E
    bias_w = jnp.concatenate([bias_w, jnp.full_like(bias_w[:, :1], NEG_INF)], axis=1).astype(BF16)
    n_rows = S // CMP_STRIDE
    off = n_rows - TQ // CMP_STRIDE
    c_rel = jnp.arange(off + n_rows)[:, None] - off
    dist_c = qq - CMP_STRIDE * c_rel - (CMP_BLOCK - 1)
    bias_c = table(dist_c, dist_c >= 0)

    n_slc = S // SLC_BLOCK
    n_cmp = n_rows - CMP_BLOCK // CMP_STRIDE + 1
    xtra = (jnp.arange(LANE)[None, :] == (jnp.arange(S)[:, None] // SLC_BLOCK)).astype(BF16)
    cmp_start = jnp.arange(n_rows) * CMP_STRIDE
    slc_start = jnp.arange(n_slc) * SLC_BLOCK
    overlap = jnp.clip(jnp.minimum(cmp_start[:, None] + CMP_BLOCK, slc_start[None, :] + SLC_BLOCK)
                       - jnp.maximum(cmp_start[:, None], slc_start[None, :]), 0).astype(F32) / CMP_BLOCK
    overlap = jnp.where(jnp.arange(n_rows)[:, None] < n_cmp, overlap, 0.0)
    g = jnp.arange(NSA_G)[:, None, None]
    row = jnp.arange(16)[None, :, None]
    col = jnp.arange(LANE)[None, None, :]
    src = SMALL_GATE + (row // NSA_R) * NSA_HEADS + g * NSA_R + row % NSA_R
    gsel = ((col == src) & (row < 3 * NSA_R)).astype(BF16)
    return bias_c, bias_s, bias_w, xtra, overlap.T.astype(BF16), gsel


def _merge_kernel(ya_ref, yb_ref, yc_ref, g0_ref, g1_ref, g2_ref, x_ref, wb_ref, wo_ref, gn_ref, o_ref):
    m = None
    for br, (y_ref, g_ref) in enumerate(((ya_ref, g0_ref), (yb_ref, g1_ref), (yc_ref, g2_ref))):
        t = jax.nn.sigmoid(g_ref[...].astype(F32)) * jnp.dot(y_ref[...], wb_ref[br], preferred_element_type=F32)
        m = t if m is None else m + t
    z = jnp.dot(m.astype(BF16), wo_ref[...], preferred_element_type=F32)
    o_ref[...] = x_ref[...] + _rms(z, gn_ref[...])


def _merge(ya, yb, yc, proj, x2, wb, wo, gn, layer, tm=512):
    T, D = x2.shape
    row = lambda i: (i, 0)
    mcol = COL_MERGE // D
    return pl.pallas_call(
        _merge_kernel,
        grid=(T // tm,),
        in_specs=[pl.BlockSpec((tm, D), row), pl.BlockSpec((tm, D), row), pl.BlockSpec((tm, D), row),
                  pl.BlockSpec((tm, D), lambda i: (i, mcol)),
                  pl.BlockSpec((tm, D), lambda i: (i, mcol + 1)),
                  pl.BlockSpec((tm, D), lambda i: (i, mcol + 2)),
                  pl.BlockSpec((tm, D), row),
                  _const_spec(wb.shape[1:], layer), _const_spec(wo.shape[1:], layer), _const_spec((1, D))],
        out_specs=pl.BlockSpec((tm, D), row),
        out_shape=jax.ShapeDtypeStruct((T, D), F32),
        compiler_params=_cparams(("parallel",)),
        name="merge_out",
    )(ya, yb, yc, proj, proj, proj, x2, wb, wo, gn)


def _ffn_kernel(x_ref, gpre_ref, win_ref, wout_ref, gpost_ref, o_ref, *, chunk):
    x = x_ref[...]
    h = _rms(x, gpre_ref[...]).astype(BF16)
    acc = None
    for c in range(D_FF // chunk):
        gt = jnp.dot(h, win_ref[:, c * chunk:(c + 1) * chunk], preferred_element_type=F32)
        up = jnp.dot(h, win_ref[:, D_FF + c * chunk:D_FF + (c + 1) * chunk], preferred_element_type=F32)
        a = (gt * jax.nn.sigmoid(gt) * up).astype(BF16)
        t = jnp.dot(a, wout_ref[c * chunk:(c + 1) * chunk, :], preferred_element_type=F32)
        acc = t if acc is None else acc + t
    o_ref[...] = x + _rms(acc, gpost_ref[...])


def _ffn(x2, gpre, win, wout, gpost, layer, tm=512, chunk=256):
    T, D = x2.shape
    row = lambda i: (i, 0)
    return pl.pallas_call(
        functools.partial(_ffn_kernel, chunk=chunk),
        grid=(T // tm,),
        in_specs=[pl.BlockSpec((tm, D), row), _const_spec((1, D)), _const_spec(win.shape[1:], layer),
                  _const_spec(wout.shape[1:], layer), _const_spec((1, D))],
        out_specs=pl.BlockSpec((tm, D), row),
        out_shape=jax.ShapeDtypeStruct((T, D), F32),
        compiler_params=_cparams(("parallel",)),
        name="ffn",
    )(x2, gpre, win, wout, gpost)


def _in_column_sources():
    (lru_x, lru_g, nsa_q, nsa_kv, nsa_gate, gla_q, gla_k, gla_v, gla_og, gla_lr, merge) = np.split(
        np.arange(sum(IN_SIZES)), np.cumsum(IN_SIZES)[:-1])
    kv = nsa_kv.reshape(3, 2, NSA_G, NSA_HD).transpose(0, 2, 1, 3).reshape(-1)
    pad = np.full(LANE - nsa_gate.size - gla_lr.size, -1)
    src = np.concatenate([lru_x, lru_g, nsa_q, gla_v, gla_og, merge, kv, gla_q, gla_k, nsa_gate, gla_lr, pad])
    assert src.size == PROJ_WIDTH
    return src


def _permute_plan():
    src = _in_column_sources()
    plan = []
    for c in range(PROJ_WIDTH // LANE):
        runs = []
        for s in src[c * LANE:(c + 1) * LANE]:
            s = int(s)
            if runs and ((s < 0 and runs[-1][0] < 0) or (s >= 0 and runs[-1][0] >= 0 and runs[-1][0] + runs[-1][1] == s)):
                runs[-1][1] += 1
            else:
                runs.append([s, 1])
        assert all(n % 8 == 0 and (s < 0 or s % 8 == 0) for s, n in runs)
        plan.append(tuple((s, n) for s, n in runs))
    return tuple(plan)


def _permute_kernel(wt_ref, o_ref, *, plan):
    tc = wt_ref.shape[1]
    for c, runs in enumerate(plan):
        parts = [wt_ref[s:s + n, :] if s >= 0 else jnp.zeros((n, tc), F32) for s, n in runs]
        block = parts[0] if len(parts) == 1 else jnp.concatenate(parts, axis=0)
        o_ref[:, c * LANE:(c + 1) * LANE] = block.T.astype(o_ref.dtype)


def _permute_in_weights(w_in, tc=128):
    L, D, W = w_in.shape
    return pl.pallas_call(
        functools.partial(_permute_kernel, plan=_permute_plan()),
        grid=(L, D // tc),
        in_specs=[pl.BlockSpec((None, W, tc), lambda l, i: (l, 0, i))],
        out_specs=pl.BlockSpec((None, tc, PROJ_WIDTH), lambda l, i: (l, i, 0)),
        out_shape=jax.ShapeDtypeStruct((L, D, PROJ_WIDTH), BF16),
        compiler_params=_cparams(("parallel", "parallel")),
        name="permute_w_in",
    )(jnp.swapaxes(w_in, 1, 2))


def _layer(x2, B, S, tables, layer, norm_g, w_in, conv_w, conv_b, lru_w_gates, lru_b_gates, lru_lambda,
           cmp_pos, cmp_w1, cmp_w2, gla_wa2, gla_ba, gla_norm, w_branch, w_out, w_ffn_in, w_ffn_out):
    D = D_MODEL
    proj = _in_proj(x2, norm_g[0][None, :], w_in, layer)
    proj3 = proj.reshape(B, S, PROJ_WIDTH)

    wg = jnp.concatenate([lru_w_gates[0], lru_w_gates[1]], axis=-1).astype(BF16)
    y_a = _lru(proj3, conv_w, conv_b[None, :], wg, lru_b_gates, lru_lambda[None, :])

    kvc = _compress(proj3, *_compress_params(cmp_pos, cmp_w1, cmp_w2))
    y_b = _nsa(proj3, kvc, *tables)

    wa_pad = jnp.zeros((LANE, GLA_HEADS * GLA_DK), F32).at[SMALL_LR:SMALL_LR + GLA_RANK].set(gla_wa2).astype(BF16)
    y_c = _gla(proj3, wa_pad, gla_ba[None, :], gla_norm[None, :])

    x2 = _merge(y_a.reshape(B * S, D), y_b.reshape(B * S, D), y_c.reshape(B * S, D), proj, x2,
                w_branch, w_out, norm_g[1][None, :], layer)
    x2 = _ffn(x2, norm_g[2][None, :], w_ffn_in, w_ffn_out, norm_g[3][None, :], layer)
    return x2


def kernel(x, rel_table, norm_g, w_in, conv_w, conv_b, lru_w_gates, lru_b_gates, lru_lambda, cmp_pos, cmp_w1,
           cmp_w2, gla_wa2, gla_ba, gla_norm, w_branch, w_out, w_ffn_in, w_ffn_out):
    B, S, D = x.shape
    tables = _nsa_tables(rel_table, S)
    x2 = x.reshape(B * S, D)
    w_in = _permute_in_weights(w_in)
    w_branch, w_out, w_ffn_in, w_ffn_out = (w.astype(BF16) for w in (w_branch, w_out, w_ffn_in, w_ffn_out))
    for l in range(norm_g.shape[0]):
        x2 = _layer(x2, B, S, tables, l, norm_g[l], w_in, conv_w[l], conv_b[l], lru_w_gates[l], lru_b_gates[l],
                    lru_lambda[l], cmp_pos[l], cmp_w1[l], cmp_w2[l], gla_wa2[l], gla_ba[l], gla_norm[l],
                    w_branch, w_out, w_ffn_in, w_ffn_out)
    return x2.reshape(B, S, D)
```

```python
import functools
import math

import jax
import jax.numpy as jnp
import numpy as np
from jax import lax
from jax.experimental import pallas as pl
from jax.experimental.pallas import tpu as pltpu

F32 = jnp.float32
BF16 = jnp.bfloat16

D_MODEL = 1024
N_BRANCH = 3
EPS = 1e-6
NEG_INF = -1e30
FORCED = 1e4

LRU_BLOCKS = 8
LRU_BLOCK = D_MODEL // LRU_BLOCKS
CONV_WIDTH = 4
LRU_C = 8.0

NSA_HEADS = 16
NSA_G = 4
NSA_R = NSA_HEADS // NSA_G
NSA_HD = D_MODEL // NSA_HEADS
CMP_BLOCK = 32
CMP_STRIDE = 16
CMP_HIDDEN = 256
SLC_BLOCK = 64
SLC_TOP_N = 8
WINDOW = 256

GLA_HEADS = 4
GLA_DK = (D_MODEL // 2) // GLA_HEADS
GLA_DV = D_MODEL // GLA_HEADS
GLA_RANK = 16
GLA_TAU = 16.0
GLA_CHUNK = 32

REL_BUCKETS = 32
REL_MAX_EXACT = 16
REL_MAX_DIST = 128

D_FF = -(-8 * D_MODEL // (3 * 256)) * 256

IN_SIZES = (D_MODEL, D_MODEL, D_MODEL, 6 * NSA_G * NSA_HD, 3 * NSA_HEADS, D_MODEL // 2, D_MODEL // 2,
            D_MODEL, D_MODEL, GLA_RANK, N_BRANCH * D_MODEL)

LANE = 128
COL_LRU_X = 0
COL_LRU_G = 1024
COL_NSA_Q = 2048
COL_GLA_V = 3072
COL_GLA_OG = 4096
COL_MERGE = 5120
COL_NSA_KV = 8192
COL_GLA_Q = 9728
COL_GLA_K = 10240
COL_SMALL = 10752
PROJ_WIDTH = 10880
SMALL_GATE = 0
SMALL_LR = 3 * NSA_HEADS

Q_TILE = 256
K_TILE = 256
NSA_BATCH_ROWS = 4
WIN_SPAN = WINDOW // LANE + 1
VMEM_LIMIT = 56 * 1024 * 1024
_NT = (((1,), (1,)), ((), ()))


def _cparams(sem):
    return pltpu.CompilerParams(dimension_semantics=sem, vmem_limit_bytes=VMEM_LIMIT)


def _const_spec(shape, layer=None):
    nd = len(shape)
    if layer is None:
        return pl.BlockSpec(shape, lambda *_: (0,) * nd, pipeline_mode=pl.Buffered(1))
    return pl.BlockSpec((None,) + tuple(shape), lambda *_: (layer,) + (0,) * nd, pipeline_mode=pl.Buffered(1))


def _rms(x, g):
    return x * lax.rsqrt(jnp.mean(x * x, axis=-1, keepdims=True) + EPS) * g


def _gelu_tanh(x):
    return 0.5 * x * (1.0 + jnp.tanh(math.sqrt(2.0 / math.pi) * (x + 0.044715 * (x * x * x))))


def _sigmoid(x):
    return 0.5 * jnp.tanh(0.5 * x) + 0.5


def _sqrt_nonneg(x):
    return jnp.where(x > 0.0, x * lax.rsqrt(x), 0.0)


def _softplus(z):
    return jnp.maximum(z, 0.0) + jnp.log1p(jnp.exp(-jnp.abs(z)))


def _in_proj_kernel(x_ref, g_ref, w_ref, o_ref, h_ref):
    @pl.when(pl.program_id(1) == 0)
    def _():
        h_ref[...] = _rms(x_ref[...], g_ref[...]).astype(BF16)

    o_ref[...] = jnp.dot(h_ref[...], w_ref[...], preferred_element_type=F32).astype(o_ref.dtype)


def _in_proj(x2, g, w, layer, tm=1024, tn=2176):
    T, D = x2.shape
    N = w.shape[2]
    return pl.pallas_call(
        _in_proj_kernel,
        grid=(T // tm, N // tn),
        in_specs=[pl.BlockSpec((tm, D), lambda i, j: (i, 0)),
                  pl.BlockSpec((1, D), lambda i, j: (0, 0)),
                  pl.BlockSpec((None, D, tn), lambda i, j: (layer, 0, j))],
        out_specs=pl.BlockSpec((tm, tn), lambda i, j: (i, j)),
        out_shape=jax.ShapeDtypeStruct((T, N), BF16),
        scratch_shapes=[pltpu.VMEM((tm, D), BF16)],
        compiler_params=_cparams(("parallel", "arbitrary")),
        name="in_proj",
    )(x2, g, w)


def _lru_kernel(xa_ref, ga_ref, cw_ref, cb_ref, wg_ref, bg_ref, lam_ref, y_ref, tail_ref, h_ref, *, R):
    @pl.when(pl.program_id(1) == 0)
    def _():
        tail_ref[...] = jnp.zeros_like(tail_ref)
        h_ref[...] = jnp.zeros_like(h_ref)

    sp = _softplus(-lam_ref[...])
    groups = R // 8
    row8 = lax.broadcasted_iota(jnp.int32, (groups, 8, LRU_BLOCK), 1)
    for n in range(LRU_BLOCKS):
        sl = slice(n * LRU_BLOCK, (n + 1) * LRU_BLOCK)
        x3 = xa_ref[0, :, sl].astype(F32).reshape(groups, 8, LRU_BLOCK)
        tail = tail_ref[:, sl].reshape(1, 8, LRU_BLOCK)
        xc3 = cb_ref[:, sl] + cw_ref[CONV_WIDTH - 1:CONV_WIDTH, sl] * x3
        for s in range(1, CONV_WIDTH):
            rolled = pltpu.roll(x3, s, axis=1)
            prev = jnp.concatenate([pltpu.roll(tail, s, axis=1), rolled[:groups - 1]], axis=0)
            w = cw_ref[CONV_WIDTH - 1 - s:CONV_WIDTH - s, sl]
            xc3 = xc3 + w * jnp.where(row8 >= s, rolled, prev)
        tail_ref[:, sl] = x3[groups - 1]
        xcn = xc3.reshape(R, LRU_BLOCK)
        gz =jnp.dot(xcn.astype(BF16), wg_ref[n], preferred_element_type=F32)
        r = _sigmoid(gz[:, :LRU_BLOCK] + bg_ref[0:1, sl])
        i = _sigmoid(gz[:, LRU_BLOCK:] + bg_ref[1:2, sl])
        a = jnp.exp((-LRU_C) * r * sp[:, sl])
        u = _sqrt_nonneg(1.0 - a * a) * (i * xcn)
        a = a.reshape(R // 8, 8, LRU_BLOCK)
        u = u.reshape(R // 8, 8, LRU_BLOCK)
        for d in (1, 2, 4):
            keep = row8 >= d
            a_s = pltpu.roll(a, d, axis=1)
            u_s = pltpu.roll(u, d, axis=1)
            u = jnp.where(keep, a * u_s + u, u)
            a = jnp.where(keep, a * a_s, a)
        hprev = h_ref[:, sl]
        hs = []
        for t in range(R // 8):
            ht = a[t] * hprev + u[t]
            hprev = ht[7:8]
            hs.append(ht)
        h_ref[:, sl] = hprev
        h = jnp.concatenate(hs, axis=0)
        y_ref[0, :, sl] = (h * _gelu_tanh(ga_ref[0, :, sl].astype(F32))).astype(y_ref.dtype)


def _lru(proj3, conv_w, conv_b, wg, bg, lam, R=256):
    B, S, _ = proj3.shape
    D = D_MODEL
    return pl.pallas_call(
        functools.partial(_lru_kernel, R=R),
        grid=(B, S // R),
        in_specs=[pl.BlockSpec((1, R, D), lambda b, s: (b, s, COL_LRU_X // D)),
                  pl.BlockSpec((1, R, D), lambda b, s: (b, s, COL_LRU_G // D)),
                  _const_spec((CONV_WIDTH, D)),
                  _const_spec((1, D)),
                  _const_spec((LRU_BLOCKS, LRU_BLOCK, 2 * LRU_BLOCK)),
                  _const_spec((2, D)),
                  _const_spec((1, D))],
        out_specs=pl.BlockSpec((1, R, D), lambda b, s: (b, s, 0)),
        out_shape=jax.ShapeDtypeStruct((B, S, D), BF16),
        scratch_shapes=[pltpu.VMEM((8, D), F32), pltpu.VMEM((1, D), F32)],
        compiler_params=_cparams(("parallel", "arbitrary")),
        name="rglru",
    )(proj3, proj3, conv_w, conv_b, wg, bg, lam)


def _gla_kernel(q_ref, k_ref, v_ref, og_ref, sm_ref, wa_ref, ba_ref, gn_ref, y_ref, st_ref, *, Tb):
    nc = Tb // GLA_CHUNK

    @pl.when(pl.program_id(1) == 0)
    def _():
        st_ref[...] = jnp.zeros_like(st_ref)

    la_pre = jnp.dot(sm_ref[0], wa_ref[...], preferred_element_type=F32)
    rowc = lax.broadcasted_iota(jnp.int32, (Tb, GLA_DK), 0) & (GLA_CHUNK - 1)
    ri = lax.broadcasted_iota(jnp.int32, (Tb, Tb), 0)
    ci = lax.broadcasted_iota(jnp.int32, (Tb, Tb), 1)
    same_chunk_causal = ((ri // GLA_CHUNK) == (ci // GLA_CHUNK)) & (ci <= ri)
    lane_chunk = lax.broadcasted_iota(jnp.int32, (GLA_DK, Tb), 1) // GLA_CHUNK

    heads = range(GLA_HEADS)
    kcols = [slice(h * GLA_DK, (h + 1) * GLA_DK) for h in heads]
    vcols = [slice(h * GLA_DV, (h + 1) * GLA_DV) for h in heads]

    def chunk_log_decay(h):
        z = la_pre[:, kcols[h]] + ba_ref[:, kcols[h]]
        b = (jnp.minimum(z, 0.0) - jnp.log(1.0 + jnp.exp(-jnp.abs(z)))) * (1.0 / GLA_TAU)
        d = 1
        while d < GLA_CHUNK:
            b = b + jnp.where(rowc >= d, pltpu.roll(b, d, axis=0), 0.0)
            d *= 2
        return b

    b = [chunk_log_decay(h) for h in heads]
    b_last = [jnp.concatenate(
        [jnp.broadcast_to(b[h][c * GLA_CHUNK + GLA_CHUNK - 1:(c + 1) * GLA_CHUNK, :], (GLA_CHUNK, GLA_DK))
         for c in range(nc)], axis=0) for h in heads]
    q = [q_ref[0, :, kcols[h]].astype(F32) * (GLA_DK ** -0.5) for h in heads]
    k = [k_ref[0, :, kcols[h]].astype(F32) for h in heads]
    v = [v_ref[0, :, vcols[h]] for h in heads]
    qt = [(q[h] * jnp.exp(b[h])).astype(BF16) for h in heads]
    kt = [(k[h] * jnp.exp(-b[h])).astype(BF16) for h in heads]
    ke_t = [(k[h] * jnp.exp(b_last[h] - b[h])).T for h in heads]
    b_t = [b[h].T for h in heads]

    att = [jnp.where(same_chunk_causal, lax.dot_general(qt[h], kt[h], _NT, preferred_element_type=F32), 0.0)
           for h in heads]
    o = [jnp.dot(att[h].astype(BF16), v[h], preferred_element_type=F32) for h in heads]
    kv_all = [jnp.dot(jnp.concatenate([jnp.where(lane_chunk == c, ke_t[h], 0.0) for c in range(nc)],
                                      axis=0).astype(BF16), v[h], preferred_element_type=F32)
              for h in heads]

    state = [st_ref[h] for h in heads]
    inter = [[] for _ in heads]
    for c in range(nc):
        rs = slice(c * GLA_CHUNK, (c + 1) * GLA_CHUNK)
        last = c * GLA_CHUNK + GLA_CHUNK - 1
        for h in heads:
            inter[h].append(jnp.dot(qt[h][rs], state[h].astype(BF16), preferred_element_type=F32))
            state[h] = state[h] * jnp.exp(b_t[h][:, last:last + 1]) + kv_all[h][c * GLA_DK:(c + 1) * GLA_DK]
    for h in heads:
        st_ref[h] = state[h]
        oh = o[h] + jnp.concatenate(inter[h], axis=0)
        on = oh * lax.rsqrt(jnp.mean(oh * oh, axis=-1, keepdims=True) + EPS) * gn_ref[...]
        og = og_ref[0, :, vcols[h]].astype(F32)
        y_ref[0, :, vcols[h]] = (on * (og * jax.nn.sigmoid(og))).astype(y_ref.dtype)


def _gla(proj3, wa_pad, ba, gn, Tb=256):
    B, S, _ = proj3.shape
    HK = GLA_HEADS * GLA_DK
    D = D_MODEL
    return pl.pallas_call(
        functools.partial(_gla_kernel, Tb=Tb),
        grid=(B, S // Tb),
        in_specs=[pl.BlockSpec((1, Tb, HK), lambda b, s: (b, s, COL_GLA_Q // HK)),
                  pl.BlockSpec((1, Tb, HK), lambda b, s: (b, s, COL_GLA_K // HK)),
                  pl.BlockSpec((1, Tb, D), lambda b, s: (b, s, COL_GLA_V // D)),
                  pl.BlockSpec((1, Tb, D), lambda b, s: (b, s, COL_GLA_OG // D)),
                  pl.BlockSpec((1, Tb, LANE), lambda b, s: (b, s, COL_SMALL // LANE)),
                  _const_spec((LANE, HK)),
                  _const_spec((1, HK)),
                  _const_spec((1, GLA_DV))],
        out_specs=pl.BlockSpec((1, Tb, D), lambda b, s: (b, s, 0)),
        out_shape=jax.ShapeDtypeStruct((B, S, D), BF16),
        scratch_shapes=[pltpu.VMEM((GLA_HEADS, GLA_DK, GLA_DV), F32)],
        compiler_params=_cparams(("parallel", "arbitrary")),
        name="gla",
    )(proj3, proj3, proj3, proj3, proj3, wa_pad, ba, gn)


def _cmp_kernel(kv_ref, pos_ref, w1_ref, w2_ref, o_ref, x_ref):
    x_ref[...] = kv_ref[0].astype(F32)
    nrow = x_ref.shape[0] // CMP_STRIDE
    first = None
    second = None
    for i in range(CMP_STRIDE):
        xi = x_ref[pl.ds(i, nrow, stride=CMP_STRIDE), :]
        f = jnp.dot((xi + pos_ref[0, i]).astype(BF16), w1_ref[0, i], preferred_element_type=F32)
        s = jnp.dot((xi + pos_ref[1, i]).astype(BF16), w1_ref[1, i], preferred_element_type=F32)
        first = f if first is None else first + f
        second = s if second is None else second + s
    pre = first + pltpu.roll(second, nrow - 1, axis=0)
    hid = _gelu_tanh(pre).astype(BF16)
    o_ref[0, 0] = jnp.dot(hid, w2_ref[...], preferred_element_type=F32).astype(o_ref.dtype)


def _compress(proj3, pos, w1, w2):
    B, S, _ = proj3.shape
    nrow = S // CMP_STRIDE
    kvb = COL_NSA_KV // LANE
    return pl.pallas_call(
        _cmp_kernel,
        grid=(B, NSA_G),
        in_specs=[pl.BlockSpec((1, S, LANE), lambda b, g: (b, 0, kvb + g)),
                  _const_spec(pos.shape), _const_spec(w1.shape), _const_spec(w2.shape)],
        out_specs=pl.BlockSpec((1, 1, nrow, LANE), lambda b, g: (b, g, 0, 0)),
        out_shape=jax.ShapeDtypeStruct((B, NSA_G, nrow, LANE), BF16),
        scratch_shapes=[pltpu.VMEM((S, LANE), F32)],
        compiler_params=_cparams(("parallel", "parallel")),
        name="nsa_compress",
    )(proj3, pos, w1, w2)


def _compress_params(cmp_pos, cmp_w1, cmp_w2):
    hd = NSA_HD
    w1 = cmp_w1.reshape(2, 2, CMP_STRIDE, hd, CMP_HIDDEN)
    z = jnp.zeros_like(w1[0])
    w1 = jnp.concatenate([jnp.concatenate([w1[0], z], axis=-1),
                          jnp.concatenate([z, w1[1]], axis=-1)], axis=-2)
    pos = cmp_pos.reshape(2, 2, CMP_STRIDE, 1, hd)
    pos = jnp.concatenate([pos[0], pos[1]], axis=-1)
    z2 = jnp.zeros_like(cmp_w2[0])
    w2 = jnp.concatenate([jnp.concatenate([cmp_w2[0], z2], axis=-1),
                          jnp.concatenate([z2, cmp_w2[1]], axis=-1)], axis=0)
    return pos, w1.astype(BF16), w2.astype(BF16)


V_ROWS = NSA_HD + 16
MASK_BIG = 1e30
LOG2_E = 1.4426950408889634


def _nsa_kernel(q_ref, kvc_ref, kvs_ref, kvw_ref, gate_ref, bc_ref, bs_ref, bw_ref, xtra_ref, ovl_ref,
                gsel_ref, y_ref, a_ref, b_ref, vs_ref, vw_ref, vc_ref, lg_ref, p_ref):
    NB = q_ref.shape[0]
    rows_nb = range(NB)
    TQ = Q_TILE
    KT = K_TILE
    RT = TQ // KT
    HW = NSA_R * TQ
    qi = pl.program_id(2)
    S = kvs_ref.shape[1]
    n_cmp_rows = kvc_ref.shape[2]

    def value_rows(tile):
        t = tile.astype(F32).T
        return jnp.concatenate([t[NSA_HD:], jnp.ones((V_ROWS - NSA_HD, t.shape[1]), F32)], axis=0).astype(BF16)

    @pl.when(qi == 0)
    def _():
        for nb in rows_nb:
            for j in range(S // LANE):
                rows = slice(j * LANE, (j + 1) * LANE)
                ks = kvs_ref[nb, rows, :]
                a_ref[nb, rows, 0:LANE] = ks
                a_ref[nb, rows, LANE:2 * LANE] = xtra_ref[rows, :]
                vs_ref[nb, :, rows] = value_rows(ks)
                vw_ref[nb, :, rows] = value_rows(kvw_ref[nb, rows, :])
            vc_ref[nb] = value_rows(kvc_ref[nb, 0])
            b_ref[nb, LANE + ovl_ref.shape[0]:, :] = jnp.zeros((LANE - ovl_ref.shape[0], HW), BF16)

    def tile_rows(j, n=1):
        return pl.ds(pl.multiple_of(j * KT, KT), n * KT)

    def normalised(acc):
        return acc[:NSA_HD] * (1.0 / acc[NSA_HD:NSA_HD + 1])

    def q_transposed(nb):
        qf = q_ref[nb].astype(F32) * (NSA_HD ** -0.5)
        zero_rows = jnp.zeros((LANE - NSA_HD, TQ), F32)
        heads = []
        for pr in range(NSA_R // 2):
            t = qf[:, pr * LANE:(pr + 1) * LANE].T
            heads.append(jnp.concatenate([t[:NSA_HD], zero_rows], axis=0))
            heads.append(jnp.concatenate([t[NSA_HD:], zero_rows], axis=0))
        return jnp.concatenate(heads, axis=1)

    qt32 = [q_transposed(nb) for nb in rows_nb]
    qt = [x.astype(BF16) for x in qt32]
    qt32_l2 = [x * LOG2_E for x in qt32]
    qt_l2 = [x.astype(BF16) for x in qt32_l2]

    parts = TQ // LANE
    o_win_parts = [[None] * parts for _ in rows_nb]
    for h in range(parts):
        own = parts * qi + h
        skipped = jnp.maximum(WIN_SPAN - 1 - own, 0)
        rows_w = pl.ds(pl.multiple_of(jnp.maximum(own - (WIN_SPAN - 1), 0) * LANE, LANE), WIN_SPAN * LANE)
        bias_w = jnp.concatenate([bw_ref[0, jnp.minimum(t + skipped, WIN_SPAN)] for t in range(WIN_SPAN)], axis=0)
        q_part = [jnp.concatenate([qt_l2[nb][:, r * TQ + h * LANE:r * TQ + (h + 1) * LANE] for r in range(NSA_R)],
                                  axis=1) for nb in rows_nb]
        lg_w = [jnp.dot(kvw_ref[nb, rows_w, :], q_part[nb], preferred_element_type=F32).astype(BF16) + bias_w
                for nb in rows_nb]
        p_w = [jnp.exp2(lg - jnp.max(lg, axis=0, keepdims=True)) for lg in lg_w]
        for nb in rows_nb:
            o_win_parts[nb][h] = normalised(jnp.dot(vw_ref[nb, :, rows_w], p_w[nb], preferred_element_type=F32))
    o_win = [jnp.concatenate([o_win_parts[nb][h][:, r * LANE:(r + 1) * LANE]
                              for r in range(NSA_R) for h in range(parts)], axis=1) for nb in rows_nb]

    cq = TQ // CMP_STRIDE
    start_c = pl.multiple_of(bc_ref.shape[1] - n_cmp_rows - cq * qi, 8)
    bias_c = bc_ref[0, pl.ds(start_c, n_cmp_rows), :]
    valid_c = bias_c > 0.5 * NEG_INF
    lg_c = [jnp.dot(kvc_ref[nb, 0], qt[nb], preferred_element_type=F32) + bias_c for nb in rows_nb]
    e_c = [jnp.exp(lg - jnp.max(lg, axis=0, keepdims=True)) for lg in lg_c]
    p_c = [jnp.where(valid_c, e * (1.0 / jnp.sum(e, axis=0, keepdims=True)), 0.0) for e in e_c]
    o_cmp = [jnp.dot(vc_ref[nb], p_c[nb].astype(BF16), preferred_element_type=F32)[:NSA_HD] for nb in rows_nb]

    ovl = ovl_ref[...]
    ns = ovl.shape[0]
    blk = lax.broadcasted_iota(jnp.int32, (ns, TQ), 0)
    cur = (qi * TQ + lax.broadcasted_iota(jnp.int32, (ns, TQ), 1)) // SLC_BLOCK
    forced = (blk == 0) | (blk == cur) | (blk == cur - 1)
    allowed = blk <= cur

    def importance(p):
        psum = p[:, 0:TQ]
        for r in range(1, NSA_R):
            psum = psum + p[:, r * TQ:(r + 1) * TQ]
        p_hi = psum.astype(BF16)
        p_lo = (psum - p_hi.astype(F32)).astype(BF16)
        imp = jnp.dot(ovl, p_hi, preferred_element_type=F32) + jnp.dot(ovl, p_lo, preferred_element_type=F32)
        return jnp.where(forced, FORCED, jnp.where(allowed, imp, -FORCED))

    score = [importance(p) for p in p_c]
    groups = ns // 8
    blk8 = lax.broadcasted_iota(jnp.int32, (8, TQ), 0)
    for nb in rows_nb:
        sc = [score[nb][8 * v:8 * v + 8] for v in range(groups)]
        rank = [jnp.zeros((8, TQ), F32) for _ in range(groups)]
        for j in range(ns):
            sj = score[nb][j:j + 1, :]
            for v in range(groups):
                if j < 8 * v:
                    beats = sj >= sc[v]
                elif j >= 8 * v + 8:
                    beats = sj > sc[v]
                else:
                    tie = jnp.where(blk8 > j - 8 * v, 1.0, 0.0)
                    rank[v] = rank[v] + jnp.where(sj > sc[v], 1.0, jnp.where(sj == sc[v], tie, 0.0))
                    continue
                rank[v] = rank[v] + jnp.where(beats, 1.0, 0.0)
        neg_sel = jnp.where(jnp.concatenate(rank, axis=0) < float(SLC_TOP_N), 0.0, -MASK_BIG).astype(BF16)
        b_ref[nb, 0:LANE] = qt_l2[nb]
        b_ref[nb, LANE:LANE + ns] = jnp.concatenate([neg_sel] * NSA_R, axis=1)

    def logits(nb, j):
        return jnp.dot(a_ref[nb, tile_rows(j), :], b_ref[nb], preferred_element_type=F32).astype(BF16)

    for nb in rows_nb:
        lg_ref[nb] = logits(nb, 0)
        p_ref[nb] = jnp.zeros(p_ref.shape[1:], BF16)

    last = RT * qi + RT - 1

    def trip(j, carry):
        lg_next = [logits(nb, jnp.minimum(j + 1, last)) for nb in rows_nb]
        prev_rows = tile_rows(jnp.maximum(j - 1, 0))
        bias = bs_ref[0, jnp.where(j >= RT * qi - 1, j - RT * qi + 1, RT + 1)]
        out = []
        for nb in rows_nb:
            m, acc, alpha = carry[nb]
            acc = acc * alpha + jnp.dot(vs_ref[nb, :, prev_rows], p_ref[nb], preferred_element_type=F32)
            lg = lg_ref[nb] + bias
            m_new = jnp.maximum(m, jnp.max(lg, axis=0, keepdims=True).astype(F32))
            p_ref[nb] = jnp.exp2(lg - m_new.astype(BF16))
            out.append((m_new, acc, jnp.exp2(m - m_new)))
        for nb in rows_nb:
            lg_ref[nb] = lg_next[nb]
        return tuple(out)

    init = (jnp.full((1, HW), NEG_INF, F32), jnp.zeros((V_ROWS, HW), F32), jnp.ones((1, HW), F32))
    carry = lax.fori_loop(0, last + 1, trip, tuple(init for _ in rows_nb))
    o_slc = [normalised(carry[nb][1] * carry[nb][2]
                        + jnp.dot(vs_ref[nb, :, tile_rows(last)], p_ref[nb], preferred_element_type=F32))
             for nb in rows_nb]

    for nb in rows_nb:
        gs = jax.nn.sigmoid(lax.dot_general(gsel_ref[0], gate_ref[nb], _NT, preferred_element_type=F32))

        def gate_row(br):
            return jnp.concatenate([gs[br * NSA_R + r:br * NSA_R + r + 1] for r in range(NSA_R)], axis=1)

        o = gate_row(0) * o_cmp[nb] + gate_row(1) * o_slc[nb] + gate_row(2) * o_win[nb]
        o = jnp.concatenate([o[:, r * TQ:(r + 1) * TQ] for r in range(NSA_R)], axis=0)
        y_ref[nb] = o.T.astype(y_ref.dtype)


def _nsa(proj3, kvc, bias_c, bias_s, bias_w, xtra, ovl_t, gsel):
    B, S, _ = proj3.shape
    TQ = Q_TILE
    G = NSA_G
    RW = NSA_R * NSA_HD
    HW = NSA_R * TQ
    kvb = COL_NSA_KV // LANE
    ncr = kvc.shape[2]
    KT = K_TILE
    NB = NSA_BATCH_ROWS
    assert S % TQ == 0 and TQ % KT == 0 and S >= TQ + KT and WINDOW == KT and ncr % LANE == 0 and B % NB == 0
    return pl.pallas_call(
        _nsa_kernel,
        grid=(G, B // NB, S // TQ),
        in_specs=[pl.BlockSpec((NB, TQ, RW), lambda g, b, i: (b, i, COL_NSA_Q // RW + g)),
                  pl.BlockSpec((NB, 1, ncr, LANE), lambda g, b, i: (b, g, 0, 0)),
                  pl.BlockSpec((NB, S, LANE), lambda g, b, i: (b, 0, kvb + G + g)),
                  pl.BlockSpec((NB, S, LANE), lambda g, b, i: (b, 0, kvb + 2 * G + g)),
                  pl.BlockSpec((NB, TQ, LANE), lambda g, b, i: (b, i, COL_SMALL // LANE)),
                  pl.BlockSpec((1,) + bias_c.shape[1:], lambda g, b, i: (g, 0, 0)),
                  pl.BlockSpec((1,) + bias_s.shape[1:], lambda g, b, i: (g, 0, 0, 0)),
                  pl.BlockSpec((1,) + bias_w.shape[1:], lambda g, b, i: (g, 0, 0, 0)),
                  pl.BlockSpec(xtra.shape, lambda g, b, i: (0, 0)),
                  pl.BlockSpec(ovl_t.shape, lambda g, b, i: (0, 0)),
                  pl.BlockSpec((1,) + gsel.shape[1:], lambda g, b, i: (g, 0, 0))],
        out_specs=pl.BlockSpec((NB, TQ, RW), lambda g, b, i: (b, i, g)),
        out_shape=jax.ShapeDtypeStruct((B, S, D_MODEL), BF16),
        scratch_shapes=[pltpu.VMEM((NB, S, 2 * LANE), BF16), pltpu.VMEM((NB, 2 * LANE, HW), BF16),
                        pltpu.VMEM((NB, V_ROWS, S), BF16), pltpu.VMEM((NB, V_ROWS, S), BF16),
                        pltpu.VMEM((NB, V_ROWS, ncr), BF16), pltpu.VMEM((NB, KT, HW), BF16),
                        pltpu.VMEM((NB, KT, HW), BF16)],
        compiler_params=_cparams(("parallel", "parallel", "arbitrary")),
        name="nsa_attention",
    )(proj3, kvc, proj3, proj3, proj3, bias_c, bias_s, bias_w, xtra, ovl_t, gsel)


def _rel_bucket(dist):
    n = jnp.maximum(dist, 0)
    nf = jnp.maximum(n, REL_MAX_EXACT).astype(F32)
    large = REL_MAX_EXACT + (jnp.log(nf / REL_MAX_EXACT) / math.log(REL_MAX_DIST / REL_MAX_EXACT)
                             * (REL_BUCKETS - REL_MAX_EXACT)).astype(jnp.int32)
    large = jnp.minimum(large, REL_BUCKETS - 1)
    return jnp.where(n < REL_MAX_EXACT, n, large)


def _nsa_tables(rel_table, S):
    TQ = Q_TILE
    tbl = rel_table.astype(F32).reshape(REL_BUCKETS, NSA_G, NSA_R)
    tbl = tbl - tbl[REL_BUCKETS - 1]

    def table(dist, ok):
        onehot = jax.nn.one_hot(_rel_bucket(dist), REL_BUCKETS, dtype=F32)
        b = jnp.einsum("...kqn,ngr->g...krq", onehot, tbl, precision=lax.Precision.HIGHEST)
        b = jnp.where(ok[..., :, None, :], b, NEG_INF)
        return b.reshape(b.shape[:-2] + (NSA_R * dist.shape[-1],))

    KT = K_TILE
    kk = jnp.arange(KT)[:, None]
    qq = jnp.arange(TQ)[None, :]
    dist = qq - kk - KT * (jnp.arange(TQ // KT + 1)[:, None, None] - 1)
    bias_s = table(dist, dist >= 0) * LOG2_E
    bias_s = jnp.concatenate([bias_s, jnp.zeros_like(bias_s[:, :1])], axis=1).astype(BF16)
    k128 = jnp.arange(LANE)[:, None]
    q128 = jnp.arange(LANE)[None, :]
    dist_w = q128 - k128 + LANE * (WIN_SPAN - 1 - jnp.arange(WIN_SPAN)[:, None, None])
    bias_w = table(dist_w, (dist_w >= 0) & (dist_w < WINDOW)) * LOG2_E
    bias_w = jnp.concatenate([bias_w, jnp.full_like(bias_w[:, :1], NEG_INF)], axis=1).astype(BF16)
    n_rows = S // CMP_STRIDE
    off = n_rows - TQ // CMP_STRIDE
    c_rel = jnp.arange(off + n_rows)[:, None] - off
    dist_c = qq - CMP_STRIDE * c_rel - (CMP_BLOCK - 1)
    bias_c = table(dist_c, dist_c >= 0)

    n_slc = S // SLC_BLOCK
    n_cmp = n_rows - CMP_BLOCK // CMP_STRIDE + 1
    xtra = (jnp.arange(LANE)[None, :] == (jnp.arange(S)[:, None] // SLC_BLOCK)).astype(BF16)
    cmp_start = jnp.arange(n_rows) * CMP_STRIDE
    slc_start = jnp.arange(n_slc) * SLC_BLOCK
    overlap = jnp.clip(jnp.minimum(cmp_start[:, None] + CMP_BLOCK, slc_start[None, :] + SLC_BLOCK)
                       - jnp.maximum(cmp_start[:, None], slc_start[None, :]), 0).astype(F32) / CMP_BLOCK
    overlap = jnp.where(jnp.arange(n_rows)[:, None] < n_cmp, overlap, 0.0)
    g = jnp.arange(NSA_G)[:, None, None]
    row = jnp.arange(16)[None, :, None]
    col = jnp.arange(LANE)[None, None, :]
    src = SMALL_GATE + (row // NSA_R) * NSA_HEADS + g * NSA_R + row % NSA_R
    gsel = ((col == src) & (row < 3 * NSA_R)).astype(BF16)
    return bias_c, bias_s, bias_w, xtra, overlap.T.astype(BF16), gsel


def _merge_kernel(ya_ref, yb_ref, yc_ref, g0_ref, g1_ref, g2_ref, x_ref, wb_ref, wo_ref, gn_ref, o_ref):
    m = None
    for br, (y_ref, g_ref) in enumerate(((ya_ref, g0_ref), (yb_ref, g1_ref), (yc_ref, g2_ref))):
        t = jax.nn.sigmoid(g_ref[...].astype(F32)) * jnp.dot(y_ref[...], wb_ref[br], preferred_element_type=F32)
        m = t if m is None else m + t
    z = jnp.dot(m.astype(BF16), wo_ref[...], preferred_element_type=F32)
    o_ref[...] = x_ref[...] + _rms(z, gn_ref[...])


def _merge(ya, yb, yc, proj, x2, wb, wo, gn, layer, tm=512):
    T, D = x2.shape
    row = lambda i: (i, 0)
    mcol = COL_MERGE // D
    return pl.pallas_call(
        _merge_kernel,
        grid=(T // tm,),
        in_specs=[pl.BlockSpec((tm, D), row), pl.BlockSpec((tm, D), row), pl.BlockSpec((tm, D), row),
                  pl.BlockSpec((tm, D), lambda i: (i, mcol)),
                  pl.BlockSpec((tm, D), lambda i: (i, mcol + 1)),
                  pl.BlockSpec((tm, D), lambda i: (i, mcol + 2)),
                  pl.BlockSpec((tm, D), row),
                  _const_spec(wb.shape[1:], layer), _const_spec(wo.shape[1:], layer), _const_spec((1, D))],
        out_specs=pl.BlockSpec((tm, D), row),
        out_shape=jax.ShapeDtypeStruct((T, D), F32),
        compiler_params=_cparams(("parallel",)),
        name="merge_out",
    )(ya, yb, yc, proj, proj, proj, x2, wb, wo, gn)


def _ffn_kernel(x_ref, gpre_ref, win_ref, wout_ref, gpost_ref, o_ref, *, chunk):
    x = x_ref[...]
    h = _rms(x, gpre_ref[...]).astype(BF16)
    acc = None
    for c in range(D_FF // chunk):
        gt = jnp.dot(h, win_ref[:, c * chunk:(c + 1) * chunk], preferred_element_type=F32)
        up = jnp.dot(h, win_ref[:, D_FF + c * chunk:D_FF + (c + 1) * chunk], preferred_element_type=F32)
        a = (gt * jax.nn.sigmoid(gt) * up).astype(BF16)
        t = jnp.dot(a, wout_ref[c * chunk:(c + 1) * chunk, :], preferred_element_type=F32)
        acc = t if acc is None else acc + t
    o_ref[...] = x + _rms(acc, gpost_ref[...])


def _ffn(x2, gpre, win, wout, gpost, layer, tm=512, chunk=256):
    T, D = x2.shape
    row = lambda i: (i, 0)
    return pl.pallas_call(
        functools.partial(_ffn_kernel, chunk=chunk),
        grid=(T // tm,),
        in_specs=[pl.BlockSpec((tm, D), row), _const_spec((1, D)), _const_spec(win.shape[1:], layer),
                  _const_spec(wout.shape[1:], layer), _const_spec((1, D))],
        out_specs=pl.BlockSpec((tm, D), row),
        out_shape=jax.ShapeDtypeStruct((T, D), F32),
        compiler_params=_cparams(("parallel",)),
        name="ffn",
    )(x2, gpre, win, wout, gpost)


def _in_column_sources():
    (lru_x, lru_g, nsa_q, nsa_kv, nsa_gate, gla_q, gla_k, gla_v, gla_og, gla_lr, merge) = np.split(
        np.arange(sum(IN_SIZES)), np.cumsum(IN_SIZES)[:-1])
    kv = nsa_kv.reshape(3, 2, NSA_G, NSA_HD).transpose(0, 2, 1, 3).reshape(-1)
    pad = np.full(LANE - nsa_gate.size - gla_lr.size, -1)
    src = np.concatenate([lru_x, lru_g, nsa_q, gla_v, gla_og, merge, kv, gla_q, gla_k, nsa_gate, gla_lr, pad])
    assert src.size == PROJ_WIDTH
    return src


def _permute_plan():
    src = _in_column_sources()
    plan = []
    for c in range(PROJ_WIDTH // LANE):
        runs = []
        for s in src[c * LANE:(c + 1) * LANE]:
            s = int(s)
            if runs and ((s < 0 and runs[-1][0] < 0) or (s >= 0 and runs[-1][0] >= 0 and runs[-1][0] + runs[-1][1] == s)):
                runs[-1][1] += 1
            else:
                runs.append([s, 1])
        assert all(n % 8 == 0 and (s < 0 or s % 8 == 0) for s, n in runs)
        plan.append(tuple((s, n) for s, n in runs))
    return tuple(plan)


def _permute_kernel(wt_ref, o_ref, *, plan):
    tc = wt_ref.shape[1]
    for c, runs in enumerate(plan):
        parts = [wt_ref[s:s + n, :] if s >= 0 else jnp.zeros((n, tc), F32) for s, n in runs]
        block = parts[0] if len(parts) == 1 else jnp.concatenate(parts, axis=0)
        o_ref[:, c * LANE:(c + 1) * LANE] = block.T.astype(o_ref.dtype)


def _permute_in_weights(w_in, tc=128):
    L, D, W = w_in.shape
    return pl.pallas_call(
        functools.partial(_permute_kernel, plan=_permute_plan()),
        grid=(L, D // tc),
        in_specs=[pl.BlockSpec((None, W, tc), lambda l, i: (l, 0, i))],
        out_specs=pl.BlockSpec((None, tc, PROJ_WIDTH), lambda l, i: (l, i, 0)),
        out_shape=jax.ShapeDtypeStruct((L, D, PROJ_WIDTH), BF16),
        compiler_params=_cparams(("parallel", "parallel")),
        name="permute_w_in",
    )(jnp.swapaxes(w_in, 1, 2))


def _layer(x2, B, S, tables, layer, norm_g, w_in, conv_w, conv_b, lru_w_gates, lru_b_gates, lru_lambda,
           cmp_pos, cmp_w1, cmp_w2, gla_wa2, gla_ba, gla_norm, w_branch, w_out, w_ffn_in, w_ffn_out):
    D = D_MODEL
    proj = _in_proj(x2, norm_g[0][None, :], w_in, layer)
    proj3 = proj.reshape(B, S, PROJ_WIDTH)

    wg = jnp.concatenate([lru_w_gates[0], lru_w_gates[1]], axis=-1).astype(BF16)
    y_a = _lru(proj3, conv_w, conv_b[None, :], wg, lru_b_gates, lru_lambda[None, :])

    kvc = _compress(proj3, *_compress_params(cmp_pos, cmp_w1, cmp_w2))
    y_b = _nsa(proj3, kvc, *tables)

    wa_pad = jnp.zeros((LANE, GLA_HEADS * GLA_DK), F32).at[SMALL_LR:SMALL_LR + GLA_RANK].set(gla_wa2).astype(BF16)
    y_c = _gla(proj3, wa_pad, gla_ba[None, :], gla_norm[None, :])

    x2 = _merge(y_a.reshape(B * S, D), y_b.reshape(B * S, D), y_c.reshape(B * S, D), proj, x2,
                w_branch, w_out, norm_g[1][None, :], layer)
    x2 = _ffn(x2, norm_g[2][None, :], w_ffn_in, w_ffn_out, norm_g[3][None, :], layer)
    return x2


def kernel(x, rel_table, norm_g, w_in, conv_w, conv_b, lru_w_gates, lru_b_gates, lru_lambda, cmp_pos, cmp_w1,
           cmp_w2, gla_wa2, gla_ba, gla_norm, w_branch, w_out, w_ffn_in, w_ffn_out):
    B, S, D = x.shape
    tables = _nsa_tables(rel_table, S)
    x2 = x.reshape(B * S, D)
    w_in = _permute_in_weights(w_in)
    w_branch, w_out, w_ffn_in, w_ffn_out = (w.astype(BF16) for w in (w_branch, w_out, w_ffn_in, w_ffn_out))
    for l in range(norm_g.shape[0]):
        x2 = _layer(x2, B, S, tables, l, norm_g[l], w_in, conv_w[l], conv_b[l], lru_w_gates[l], lru_b_gates[l],
                    lru_lambda[l], cmp_pos[l], cmp_w1[l], cmp_w2[l], gla_wa2[l], gla_ba[l], gla_norm[l],
                    w_branch, w_out, w_ffn_in, w_ffn_out)
    return x2.reshape(B, S, D)
```

```python
import functools
import math

import jax
import jax.numpy as jnp
import numpy as np
from jax import lax
from jax.experimental import pallas as pl
from jax.experimental.pallas import tpu as pltpu

F32 = jnp.float32
BF16 = jnp.bfloat16

D_MODEL = 1024
N_BRANCH = 3
EPS = 1e-6
NEG_INF = -1e30
FORCED = 1e4

LRU_BLOCKS = 8
LRU_BLOCK = D_MODEL // LRU_BLOCKS
CONV_WIDTH = 4
LRU_C = 8.0

NSA_HEADS = 16
NSA_G = 4
NSA_R = NSA_HEADS // NSA_G
NSA_HD = D_MODEL // NSA_HEADS
CMP_BLOCK = 32
CMP_STRIDE = 16
CMP_HIDDEN = 256
SLC_BLOCK = 64
SLC_TOP_N = 8
WINDOW = 256

GLA_HEADS = 4
GLA_DK = (D_MODEL // 2) // GLA_HEADS
GLA_DV = D_MODEL // GLA_HEADS
GLA_RANK = 16
GLA_TAU = 16.0
GLA_CHUNK = 32

REL_BUCKETS = 32
REL_MAX_EXACT = 16
REL_MAX_DIST = 128

D_FF = -(-8 * D_MODEL // (3 * 256)) * 256

IN_SIZES = (D_MODEL, D_MODEL, D_MODEL, 6 * NSA_G * NSA_HD, 3 * NSA_HEADS, D_MODEL // 2, D_MODEL // 2,
            D_MODEL, D_MODEL, GLA_RANK, N_BRANCH * D_MODEL)

LANE = 128
COL_LRU_X = 0
COL_LRU_G = 1024
COL_NSA_Q = 2048
COL_GLA_V = 3072
COL_GLA_OG = 4096
COL_MERGE = 5120
COL_NSA_KV = 8192
COL_GLA_Q = 9728
COL_GLA_K = 10240
COL_SMALL = 10752
PROJ_WIDTH = 10880
SMALL_GATE = 0
SMALL_LR = 3 * NSA_HEADS

Q_TILE = 256
K_TILE = 256
NSA_BATCH_ROWS = 4
WIN_SPAN = WINDOW // LANE + 1
VMEM_LIMIT = 56 * 1024 * 1024
_NT = (((1,), (1,)), ((), ()))


def _cparams(sem):
    return pltpu.CompilerParams(dimension_semantics=sem, vmem_limit_bytes=VMEM_LIMIT)


def _const_spec(shape, layer=None):
    nd = len(shape)
    if layer is None:
        return pl.BlockSpec(shape, lambda *_: (0,) * nd, pipeline_mode=pl.Buffered(1))
    return pl.BlockSpec((None,) + tuple(shape), lambda *_: (layer,) + (0,) * nd, pipeline_mode=pl.Buffered(1))


def _rms(x, g):
    return x * lax.rsqrt(jnp.mean(x * x, axis=-1, keepdims=True) + EPS) * g


def _gelu_tanh(x):
    return 0.5 * x * (1.0 + jnp.tanh(math.sqrt(2.0 / math.pi) * (x + 0.044715 * (x * x * x))))


def _sigmoid(x):
    return 0.5 * jnp.tanh(0.5 * x) + 0.5


def _sqrt_nonneg(x):
    return jnp.where(x > 0.0, x * lax.rsqrt(x), 0.0)


def _softplus(z):
    return jnp.maximum(z, 0.0) + jnp.log1p(jnp.exp(-jnp.abs(z)))


def _in_proj_kernel(x_ref, g_ref, w_ref, o_ref, h_ref):
    @pl.when(pl.program_id(1) == 0)
    def _():
        h_ref[...] = _rms(x_ref[...], g_ref[...]).astype(BF16)

    o_ref[...] = jnp.dot(h_ref[...], w_ref[...], preferred_element_type=F32).astype(o_ref.dtype)


def _in_proj(x2, g, w, layer, tm=1024, tn=2176):
    T, D = x2.shape
    N = w.shape[2]
    return pl.pallas_call(
        _in_proj_kernel,
        grid=(T // tm, N // tn),
        in_specs=[pl.BlockSpec((tm, D), lambda i, j: (i, 0)),
                  pl.BlockSpec((1, D), lambda i, j: (0, 0)),
                  pl.BlockSpec((None, D, tn), lambda i, j: (layer, 0, j))],
        out_specs=pl.BlockSpec((tm, tn), lambda i, j: (i, j)),
        out_shape=jax.ShapeDtypeStruct((T, N), BF16),
        scratch_shapes=[pltpu.VMEM((tm, D), BF16)],
        compiler_params=_cparams(("parallel", "arbitrary")),
        name="in_proj",
    )(x2, g, w)


def _lru_kernel(xa_ref, ga_ref, cw_ref, cb_ref, wg_ref, bg_ref, lam_ref, y_ref, tail_ref, h_ref, *, R):
    @pl.when(pl.program_id(1) == 0)
    def _():
        tail_ref[...] = jnp.zeros_like(tail_ref)
        h_ref[...] = jnp.zeros_like(h_ref)

    sp = _softplus(-lam_ref[...])
    groups = R // 8
    row8 = lax.broadcasted_iota(jnp.int32, (groups, 8, LRU_BLOCK), 1)
    for n in range(LRU_BLOCKS):
        sl = slice(n * LRU_BLOCK, (n + 1) * LRU_BLOCK)
        x3 = xa_ref[0, :, sl].astype(F32).reshape(groups, 8, LRU_BLOCK)
        tail = tail_ref[:, sl].reshape(1, 8, LRU_BLOCK)
        xc3 = cb_ref[:, sl] + cw_ref[CONV_WIDTH - 1:CONV_WIDTH, sl] * x3
        for s in range(1, CONV_WIDTH):
            rolled = pltpu.roll(x3, s, axis=1)
            prev = jnp.concatenate([pltpu.roll(tail, s, axis=1), rolled[:groups - 1]], axis=0)
            w = cw_ref[CONV_WIDTH - 1 - s:CONV_WIDTH - s, sl]
            xc3 = xc3 + w * jnp.where(row8 >= s, rolled, prev)
        tail_ref[:, sl] = x3[groups - 1]
        xcn = xc3.reshape(R, LRU_BLOCK)
        gz =jnp.dot(xcn.astype(BF16), wg_ref[n], preferred_element_type=F32)
        r = _sigmoid(gz[:, :LRU_BLOCK] + bg_ref[0:1, sl])
        i = _sigmoid(gz[:, LRU_BLOCK:] + bg_ref[1:2, sl])
        a = jnp.exp((-LRU_C) * r * sp[:, sl])
        u = _sqrt_nonneg(1.0 - a * a) * (i * xcn)
        a = a.reshape(R // 8, 8, LRU_BLOCK)
        u = u.reshape(R // 8, 8, LRU_BLOCK)
        for d in (1, 2, 4):
            keep = row8 >= d
            a_s = pltpu.roll(a, d, axis=1)
            u_s = pltpu.roll(u, d, axis=1)
            u = jnp.where(keep, a * u_s + u, u)
            a = jnp.where(keep, a * a_s, a)
        hprev = h_ref[:, sl]
        hs = []
        for t in range(R // 8):
            ht = a[t] * hprev + u[t]
            hprev = ht[7:8]
            hs.append(ht)
        h_ref[:, sl] = hprev
        h = jnp.concatenate(hs, axis=0)
        y_ref[0, :, sl] = (h * _gelu_tanh(ga_ref[0, :, sl].astype(F32))).astype(y_ref.dtype)


def _lru(proj3, conv_w, conv_b, wg, bg, lam, R=256):
    B, S, _ = proj3.shape
    D = D_MODEL
    return pl.pallas_call(
        functools.partial(_lru_kernel, R=R),
        grid=(B, S // R),
        in_specs=[pl.BlockSpec((1, R, D), lambda b, s: (b, s, COL_LRU_X // D)),
                  pl.BlockSpec((1, R, D), lambda b, s: (b, s, COL_LRU_G // D)),
                  _const_spec((CONV_WIDTH, D)),
                  _const_spec((1, D)),
                  _const_spec((LRU_BLOCKS, LRU_BLOCK, 2 * LRU_BLOCK)),
                  _const_spec((2, D)),
                  _const_spec((1, D))],
        out_specs=pl.BlockSpec((1, R, D), lambda b, s: (b, s, 0)),
        out_shape=jax.ShapeDtypeStruct((B, S, D), BF16),
        scratch_shapes=[pltpu.VMEM((8, D), F32), pltpu.VMEM((1, D), F32)],
        compiler_params=_cparams(("parallel", "arbitrary")),
        name="rglru",
    )(proj3, proj3, conv_w, conv_b, wg, bg, lam)


def _gla_kernel(q_ref, k_ref, v_ref, og_ref, sm_ref, wa_ref, ba_ref, gn_ref, y_ref, st_ref, *, Tb):
    nc = Tb // GLA_CHUNK

    @pl.when(pl.program_id(1) == 0)
    def _():
        st_ref[...] = jnp.zeros_like(st_ref)

    la_pre = jnp.dot(sm_ref[0], wa_ref[...], preferred_element_type=F32)
    rowc = lax.broadcasted_iota(jnp.int32, (Tb, GLA_DK), 0) & (GLA_CHUNK - 1)
    ri = lax.broadcasted_iota(jnp.int32, (Tb, Tb), 0)
    ci = lax.broadcasted_iota(jnp.int32, (Tb, Tb), 1)
    same_chunk_causal = ((ri // GLA_CHUNK) == (ci // GLA_CHUNK)) & (ci <= ri)
    lane_chunk = lax.broadcasted_iota(jnp.int32, (GLA_DK, Tb), 1) // GLA_CHUNK

    heads = range(GLA_HEADS)
    kcols = [slice(h * GLA_DK, (h + 1) * GLA_DK) for h in heads]
    vcols = [slice(h * GLA_DV, (h + 1) * GLA_DV) for h in heads]

    def chunk_log_decay(h):
        z = la_pre[:, kcols[h]] + ba_ref[:, kcols[h]]
        b = (jnp.minimum(z, 0.0) - jnp.log(1.0 + jnp.exp(-jnp.abs(z)))) * (1.0 / GLA_TAU)
        d = 1
        while d < GLA_CHUNK:
            b = b + jnp.where(rowc >= d, pltpu.roll(b, d, axis=0), 0.0)
            d *= 2
        return b

    b = [chunk_log_decay(h) for h in heads]
    b_last = [jnp.concatenate(
        [jnp.broadcast_to(b[h][c * GLA_CHUNK + GLA_CHUNK - 1:(c + 1) * GLA_CHUNK, :], (GLA_CHUNK, GLA_DK))
         for c in range(nc)], axis=0) for h in heads]
    q = [q_ref[0, :, kcols[h]].astype(F32) * (GLA_DK ** -0.5) for h in heads]
    k = [k_ref[0, :, kcols[h]].astype(F32) for h in heads]
    v = [v_ref[0, :, vcols[h]] for h in heads]
    qt = [(q[h] * jnp.exp(b[h])).astype(BF16) for h in heads]
    kt = [(k[h] * jnp.exp(-b[h])).astype(BF16) for h in heads]
    ke_t = [(k[h] * jnp.exp(b_last[h] - b[h])).T for h in heads]
    b_t = [b[h].T for h in heads]

    att = [jnp.where(same_chunk_causal, lax.dot_general(qt[h], kt[h], _NT, preferred_element_type=F32), 0.0)
           for h in heads]
    o = [jnp.dot(att[h].astype(BF16), v[h], preferred_element_type=F32) for h in heads]
    kv_all = [jnp.dot(jnp.concatenate([jnp.where(lane_chunk == c, ke_t[h], 0.0) for c in range(nc)],
                                      axis=0).astype(BF16), v[h], preferred_element_type=F32)
              for h in heads]

    state = [st_ref[h] for h in heads]
    inter = [[] for _ in heads]
    for c in range(nc):
        rs = slice(c * GLA_CHUNK, (c + 1) * GLA_CHUNK)
        last = c * GLA_CHUNK + GLA_CHUNK - 1
        for h in heads:
            inter[h].append(jnp.dot(qt[h][rs], state[h].astype(BF16), preferred_element_type=F32))
            state[h] = state[h] * jnp.exp(b_t[h][:, last:last + 1]) + kv_all[h][c * GLA_DK:(c + 1) * GLA_DK]
    for h in heads:
        st_ref[h] = state[h]
        oh = o[h] + jnp.concatenate(inter[h], axis=0)
        on = oh * lax.rsqrt(jnp.mean(oh * oh, axis=-1, keepdims=True) + EPS) * gn_ref[...]
        og = og_ref[0, :, vcols[h]].astype(F32)
        y_ref[0, :, vcols[h]] = (on * (og * jax.nn.sigmoid(og))).astype(y_ref.dtype)


def _gla(proj3, wa_pad, ba, gn, Tb=256):
    B, S, _ = proj3.shape
    HK = GLA_HEADS * GLA_DK
    D = D_MODEL
    return pl.pallas_call(
        functools.partial(_gla_kernel, Tb=Tb),
        grid=(B, S // Tb),
        in_specs=[pl.BlockSpec((1, Tb, HK), lambda b, s: (b, s, COL_GLA_Q // HK)),
                  pl.BlockSpec((1, Tb, HK), lambda b, s: (b, s, COL_GLA_K // HK)),
                  pl.BlockSpec((1, Tb, D), lambda b, s: (b, s, COL_GLA_V // D)),
                  pl.BlockSpec((1, Tb, D), lambda b, s: (b, s, COL_GLA_OG // D)),
                  pl.BlockSpec((1, Tb, LANE), lambda b, s: (b, s, COL_SMALL // LANE)),
                  _const_spec((LANE, HK)),
                  _const_spec((1, HK)),
                  _const_spec((1, GLA_DV))],
        out_specs=pl.BlockSpec((1, Tb, D), lambda b, s: (b, s, 0)),
        out_shape=jax.ShapeDtypeStruct((B, S, D), BF16),
        scratch_shapes=[pltpu.VMEM((GLA_HEADS, GLA_DK, GLA_DV), F32)],
        compiler_params=_cparams(("parallel", "arbitrary")),
        name="gla",
    )(proj3, proj3, proj3, proj3, proj3, wa_pad, ba, gn)


def _cmp_kernel(kv_ref, pos_ref, w1_ref, w2_ref, o_ref, x_ref):
    x_ref[...] = kv_ref[0].astype(F32)
    nrow = x_ref.shape[0] // CMP_STRIDE
    first = None
    second = None
    for i in range(CMP_STRIDE):
        xi = x_ref[pl.ds(i, nrow, stride=CMP_STRIDE), :]
        f = jnp.dot((xi + pos_ref[0, i]).astype(BF16), w1_ref[0, i], preferred_element_type=F32)
        s = jnp.dot((xi + pos_ref[1, i]).astype(BF16), w1_ref[1, i], preferred_element_type=F32)
        first = f if first is None else first + f
        second = s if second is None else second + s
    pre = first + pltpu.roll(second, nrow - 1, axis=0)
    hid = _gelu_tanh(pre).astype(BF16)
    o_ref[0, 0] = jnp.dot(hid, w2_ref[...], preferred_element_type=F32).astype(o_ref.dtype)


def _compress(proj3, pos, w1, w2):
    B, S, _ = proj3.shape
    nrow = S // CMP_STRIDE
    kvb = COL_NSA_KV // LANE
    return pl.pallas_call(
        _cmp_kernel,
        grid=(B, NSA_G),
        in_specs=[pl.BlockSpec((1, S, LANE), lambda b, g: (b, 0, kvb + g)),
                  _const_spec(pos.shape), _const_spec(w1.shape), _const_spec(w2.shape)],
        out_specs=pl.BlockSpec((1, 1, nrow, LANE), lambda b, g: (b, g, 0, 0)),
        out_shape=jax.ShapeDtypeStruct((B, NSA_G, nrow, LANE), BF16),
        scratch_shapes=[pltpu.VMEM((S, LANE), F32)],
        compiler_params=_cparams(("parallel", "parallel")),
        name="nsa_compress",
    )(proj3, pos, w1, w2)


def _compress_params(cmp_pos, cmp_w1, cmp_w2):
    hd = NSA_HD
    w1 = cmp_w1.reshape(2, 2, CMP_STRIDE, hd, CMP_HIDDEN)
    z = jnp.zeros_like(w1[0])
    w1 = jnp.concatenate([jnp.concatenate([w1[0], z], axis=-1),
                          jnp.concatenate([z, w1[1]], axis=-1)], axis=-2)
    pos = cmp_pos.reshape(2, 2, CMP_STRIDE, 1, hd)
    pos = jnp.concatenate([pos[0], pos[1]], axis=-1)
    z2 = jnp.zeros_like(cmp_w2[0])
    w2 = jnp.concatenate([jnp.concatenate([cmp_w2[0], z2], axis=-1),
                          jnp.concatenate([z2, cmp_w2[1]], axis=-1)], axis=0)
    return pos, w1.astype(BF16), w2.astype(BF16)


V_ROWS = NSA_HD + 16
MASK_BIG = 1e30
LOG2_E = 1.4426950408889634


def _nsa_kernel(q_ref, kvc_ref, kvs_ref, kvw_ref, gate_ref, bc_ref, bs_ref, bw_ref, xtra_ref, ovl_ref,
                gsel_ref, y_ref, a_ref, b_ref, vs_ref, vw_ref, vc_ref, lg_ref, p_ref):
    NB = q_ref.shape[0]
    rows_nb = range(NB)
    TQ = Q_TILE
    KT = K_TILE
    RT = TQ // KT
    HW = NSA_R * TQ
    qi = pl.program_id(2)
    S = kvs_ref.shape[1]
    n_cmp_rows = kvc_ref.shape[2]

    def value_rows(tile):
        t = tile.astype(F32).T
        return jnp.concatenate([t[NSA_HD:], jnp.ones((V_ROWS - NSA_HD, t.shape[1]), F32)], axis=0).astype(BF16)

    @pl.when(qi == 0)
    def _():
        for nb in rows_nb:
            for j in range(S // LANE):
                rows = slice(j * LANE, (j + 1) * LANE)
                ks = kvs_ref[nb, rows, :]
                a_ref[nb, rows, 0:LANE] = ks
                a_ref[nb, rows, LANE:2 * LANE] = xtra_ref[rows, :]
                vs_ref[nb, :, rows] = value_rows(ks)
                vw_ref[nb, :, rows] = value_rows(kvw_ref[nb, rows, :])
            vc_ref[nb] = value_rows(kvc_ref[nb, 0])
            b_ref[nb, LANE + ovl_ref.shape[0]:, :] = jnp.zeros((LANE - ovl_ref.shape[0], HW), BF16)

    def tile_rows(j, n=1):
        return pl.ds(pl.multiple_of(j * KT, KT), n * KT)

    def normalised(acc):
        return acc[:NSA_HD] * (1.0 / acc[NSA_HD:NSA_HD + 1])

    def q_transposed(nb):
        qf = q_ref[nb].astype(F32) * (NSA_HD ** -0.5)
        zero_rows = jnp.zeros((LANE - NSA_HD, TQ), F32)
        heads = []
        for pr in range(NSA_R // 2):
            t = qf[:, pr * LANE:(pr + 1) * LANE].T
            heads.append(jnp.concatenate([t[:NSA_HD], zero_rows], axis=0))
            heads.append(jnp.concatenate([t[NSA_HD:], zero_rows], axis=0))
        return jnp.concatenate(heads, axis=1)

    qt32 = [q_transposed(nb) for nb in rows_nb]
    qt = [x.astype(BF16) for x in qt32]
    qt32_l2 = [x * LOG2_E for x in qt32]
    qt_l2 = [x.astype(BF16) for x in qt32_l2]

    parts = TQ // LANE
    o_win_parts = [[None] * parts for _ in rows_nb]
    for h in range(parts):
        own = parts * qi + h
        skipped = jnp.maximum(WIN_SPAN - 1 - own, 0)
        rows_w = pl.ds(pl.multiple_of(jnp.maximum(own - (WIN_SPAN - 1), 0) * LANE, LANE), WIN_SPAN * LANE)
        bias_w = jnp.concatenate([bw_ref[0, jnp.minimum(t + skipped, WIN_SPAN)] for t in range(WIN_SPAN)], axis=0)
        q_part = [jnp.concatenate([qt_l2[nb][:, r * TQ + h * LANE:r * TQ + (h + 1) * LANE] for r in range(NSA_R)],
                                  axis=1) for nb in rows_nb]
        lg_w = [jnp.dot(kvw_ref[nb, rows_w, :], q_part[nb], preferred_element_type=F32).astype(BF16) + bias_w
                for nb in rows_nb]
        p_w = [jnp.exp2(lg - jnp.max(lg, axis=0, keepdims=True)) for lg in lg_w]
        for nb in rows_nb:
            o_win_parts[nb][h] = normalised(jnp.dot(vw_ref[nb, :, rows_w], p_w[nb], preferred_element_type=F32))
    o_win = [jnp.concatenate([o_win_parts[nb][h][:, r * LANE:(r + 1) * LANE]
                              for r in range(NSA_R) for h in range(parts)], axis=1) for nb in rows_nb]

    cq = TQ // CMP_STRIDE
    start_c = pl.multiple_of(bc_ref.shape[1] - n_cmp_rows - cq * qi, 8)
    bias_c = bc_ref[0, pl.ds(start_c, n_cmp_rows), :]
    valid_c = bias_c > 0.5 * NEG_INF
    lg_c = [jnp.dot(kvc_ref[nb, 0], qt[nb], preferred_element_type=F32) + bias_c for nb in rows_nb]
    e_c = [jnp.exp(lg - jnp.max(lg, axis=0, keepdims=True)) for lg in lg_c]
    p_c = [jnp.where(valid_c, e * (1.0 / jnp.sum(e, axis=0, keepdims=True)), 0.0) for e in e_c]
    o_cmp = [jnp.dot(vc_ref[nb], p_c[nb].astype(BF16), preferred_element_type=F32)[:NSA_HD] for nb in rows_nb]

    ovl = ovl_ref[...]
    ns = ovl.shape[0]
    blk = lax.broadcasted_iota(jnp.int32, (ns, TQ), 0)
    cur = (qi * TQ + lax.broadcasted_iota(jnp.int32, (ns, TQ), 1)) // SLC_BLOCK
    forced = (blk == 0) | (blk == cur) | (blk == cur - 1)
    allowed = blk <= cur

    def importance(p):
        psum = p[:, 0:TQ]
        for r in range(1, NSA_R):
            psum = psum + p[:, r * TQ:(r + 1) * TQ]
        p_hi = psum.astype(BF16)
        p_lo = (psum - p_hi.astype(F32)).astype(BF16)
        imp = jnp.dot(ovl, p_hi, preferred_element_type=F32) + jnp.dot(ovl, p_lo, preferred_element_type=F32)
        return jnp.where(forced, FORCED, jnp.where(allowed, imp, -FORCED))

    score = [importance(p) for p in p_c]
    groups = ns // 8
    blk8 = lax.broadcasted_iota(jnp.int32, (8, TQ), 0)
    for nb in rows_nb:
        sc = [score[nb][8 * v:8 * v + 8] for v in range(groups)]
        rank = [jnp.zeros((8, TQ), F32) for _ in range(groups)]
        for j in range(ns):
            sj = score[nb][j:j + 1, :]
            for v in range(groups):
                if j < 8 * v:
                    beats = sj >= sc[v]
                elif j >= 8 * v + 8:
                    beats = sj > sc[v]
                else:
                    tie = jnp.where(blk8 > j - 8 * v, 1.0, 0.0)
                    rank[v] = rank[v] + jnp.where(sj > sc[v], 1.0, jnp.where(sj == sc[v], tie, 0.0))
                    continue
                rank[v] = rank[v] + jnp.where(beats, 1.0, 0.0)
        neg_sel = jnp.where(jnp.concatenate(rank, axis=0) < float(SLC_TOP_N), 0.0, -MASK_BIG).astype(BF16)
        b_ref[nb, 0:LANE] = qt_l2[nb]
        b_ref[nb, LANE:LANE + ns] = jnp.concatenate([neg_sel] * NSA_R, axis=1)

    def logits(nb, j):
        return jnp.dot(a_ref[nb, tile_rows(j), :], b_ref[nb], preferred_element_type=F32).astype(BF16)

    for nb in rows_nb:
        lg_ref[nb] = logits(nb, 0)
        p_ref[nb] = jnp.zeros(p_ref.shape[1:], BF16)

    last = RT * qi + RT - 1

    def trip(j, carry):
        prev_rows = tile_rows(jnp.maximum(j - 1, 0))
        bias = bs_ref[0, jnp.where(j >= RT * qi - 1, j - RT * qi + 1, RT + 1)]
        out = []
        for nb in rows_nb:
            m, acc, alpha = carry[nb]
            acc = acc * alpha + jnp.dot(vs_ref[nb, :, prev_rows], p_ref[nb], preferred_element_type=F32)
            lg = lg_ref[nb] + bias
            m_new = jnp.maximum(m, jnp.max(lg, axis=0, keepdims=True).astype(F32))
            p_ref[nb] = jnp.exp2(lg - m_new.astype(BF16))
            out.append((m_new, acc, jnp.exp2(m - m_new)))
        for nb in rows_nb:
            lg_ref[nb] = logits(nb, jnp.minimum(j + 1, last))
        return tuple(out)

    init = (jnp.full((1, HW), NEG_INF, F32), jnp.zeros((V_ROWS, HW), F32), jnp.ones((1, HW), F32))
    carry = lax.fori_loop(0, last + 1, trip, tuple(init for _ in rows_nb))
    o_slc = [normalised(carry[nb][1] * carry[nb][2]
                        + jnp.dot(vs_ref[nb, :, tile_rows(last)], p_ref[nb], preferred_element_type=F32))
             for nb in rows_nb]

    for nb in rows_nb:
        gs = jax.nn.sigmoid(lax.dot_general(gsel_ref[0], gate_ref[nb], _NT, preferred_element_type=F32))

        def gate_row(br):
            return jnp.concatenate([gs[br * NSA_R + r:br * NSA_R + r + 1] for r in range(NSA_R)], axis=1)

        o = gate_row(0) * o_cmp[nb] + gate_row(1) * o_slc[nb] + gate_row(2) * o_win[nb]
        o = jnp.concatenate([o[:, r * TQ:(r + 1) * TQ] for r in range(NSA_R)], axis=0)
        y_ref[nb] = o.T.astype(y_ref.dtype)


def _nsa(proj3, kvc, bias_c, bias_s, bias_w, xtra, ovl_t, gsel):
    B, S, _ = proj3.shape
    TQ = Q_TILE
    G = NSA_G
    RW = NSA_R * NSA_HD
    HW = NSA_R * TQ
    kvb = COL_NSA_KV // LANE
    ncr = kvc.shape[2]
    KT = K_TILE
    NB = NSA_BATCH_ROWS
    assert S % TQ == 0 and TQ % KT == 0 and S >= TQ + KT and WINDOW == KT and ncr % LANE == 0 and B % NB == 0
    return pl.pallas_call(
        _nsa_kernel,
        grid=(G, B // NB, S // TQ),
        in_specs=[pl.BlockSpec((NB, TQ, RW), lambda g, b, i: (b, i, COL_NSA_Q // RW + g)),
                  pl.BlockSpec((NB, 1, ncr, LANE), lambda g, b, i: (b, g, 0, 0)),
                  pl.BlockSpec((NB, S, LANE), lambda g, b, i: (b, 0, kvb + G + g)),
                  pl.BlockSpec((NB, S, LANE), lambda g, b, i: (b, 0, kvb + 2 * G + g)),
                  pl.BlockSpec((NB, TQ, LANE), lambda g, b, i: (b, i, COL_SMALL // LANE)),
                  pl.BlockSpec((1,) + bias_c.shape[1:], lambda g, b, i: (g, 0, 0)),
                  pl.BlockSpec((1,) + bias_s.shape[1:], lambda g, b, i: (g, 0, 0, 0)),
                  pl.BlockSpec((1,) + bias_w.shape[1:], lambda g, b, i: (g, 0, 0, 0)),
                  pl.BlockSpec(xtra.shape, lambda g, b, i: (0, 0)),
                  pl.BlockSpec(ovl_t.shape, lambda g, b, i: (0, 0)),
                  pl.BlockSpec((1,) + gsel.shape[1:], lambda g, b, i: (g, 0, 0))],
        out_specs=pl.BlockSpec((NB, TQ, RW), lambda g, b, i: (b, i, g)),
        out_shape=jax.ShapeDtypeStruct((B, S, D_MODEL), BF16),
        scratch_shapes=[pltpu.VMEM((NB, S, 2 * LANE), BF16), pltpu.VMEM((NB, 2 * LANE, HW), BF16),
                        pltpu.VMEM((NB, V_ROWS, S), BF16), pltpu.VMEM((NB, V_ROWS, S), BF16),
                        pltpu.VMEM((NB, V_ROWS, ncr), BF16), pltpu.VMEM((NB, KT, HW), BF16),
                        pltpu.VMEM((NB, KT, HW), BF16)],
        compiler_params=_cparams(("parallel", "parallel", "arbitrary")),
        name="nsa_attention",
    )(proj3, kvc, proj3, proj3, proj3, bias_c, bias_s, bias_w, xtra, ovl_t, gsel)


def _rel_bucket(dist):
    n = jnp.maximum(dist, 0)
    nf = jnp.maximum(n, REL_MAX_EXACT).astype(F32)
    large = REL_MAX_EXACT + (jnp.log(nf / REL_MAX_EXACT) / math.log(REL_MAX_DIST / REL_MAX_EXACT)
                             * (REL_BUCKETS - REL_MAX_EXACT)).astype(jnp.int32)
    large = jnp.minimum(large, REL_BUCKETS - 1)
    return jnp.where(n < REL_MAX_EXACT, n, large)


def _nsa_tables(rel_table, S):
    TQ = Q_TILE
    tbl = rel_table.astype(F32).reshape(REL_BUCKETS, NSA_G, NSA_R)
    tbl = tbl - tbl[REL_BUCKETS - 1]

    def table(dist, ok):
        onehot = jax.nn.one_hot(_rel_bucket(dist), REL_BUCKETS, dtype=F32)
        b = jnp.einsum("...kqn,ngr->g...krq", onehot, tbl, precision=lax.Precision.HIGHEST)
        b = jnp.where(ok[..., :, None, :], b, NEG_INF)
        return b.reshape(b.shape[:-2] + (NSA_R * dist.shape[-1],))

    KT = K_TILE
    kk = jnp.arange(KT)[:, None]
    qq = jnp.arange(TQ)[None, :]
    dist = qq - kk - KT * (jnp.arange(TQ // KT + 1)[:, None, None] - 1)
    bias_s = table(dist, dist >= 0) * LOG2_E
    bias_s = jnp.concatenate([bias_s, jnp.zeros_like(bias_s[:, :1])], axis=1).astype(BF16)
    k128 = jnp.arange(LANE)[:, None]
    q128 = jnp.arange(LANE)[None, :]
    dist_w = q128 - k128 + LANE * (WIN_SPAN - 1 - jnp.arange(WIN_SPAN)[:, None, None])
    bias_w = table(dist_w, (dist_w >= 0) & (dist_w < WINDOW)) * LOG2_E
    bias_w = jnp.concatenate([bias_w, jnp.full_like(bias_w[:, :1], NEG_INF)], axis=1).astype(BF16)
    n_rows = S // CMP_STRIDE
    off = n_rows - TQ // CMP_STRIDE
    c_rel = jnp.arange(off + n_rows)[:, None] - off
    dist_c = qq - CMP_STRIDE * c_rel - (CMP_BLOCK - 1)
    bias_c = table(dist_c, dist_c >= 0)

    n_slc = S // SLC_BLOCK
    n_cmp = n_rows - CMP_BLOCK // CMP_STRIDE + 1
    xtra = (jnp.arange(LANE)[None, :] == (jnp.arange(S)[:, None] // SLC_BLOCK)).astype(BF16)
    cmp_start = jnp.arange(n_rows) * CMP_STRIDE
    slc_start = jnp.arange(n_slc) * SLC_BLOCK
    overlap = jnp.clip(jnp.minimum(cmp_start[:, None] + CMP_BLOCK, slc_start[None, :] + SLC_BLOCK)
                       - jnp.maximum(cmp_start[:, None], slc_start[None, :]), 0).astype(F32) / CMP_BLOCK
    overlap = jnp.where(jnp.arange(n_rows)[:, None] < n_cmp, overlap, 0.0)
    g = jnp.arange(NSA_G)[:, None, None]
    row = jnp.arange(16)[None, :, None]
    col = jnp.arange(LANE)[None, None, :]
    src = SMALL_GATE + (row // NSA_R) * NSA_HEADS + g * NSA_R + row % NSA_R
    gsel = ((col == src) & (row < 3 * NSA_R)).astype(BF16)
    return bias_c, bias_s, bias_w, xtra, overlap.T.astype(BF16), gsel


def _merge_kernel(ya_ref, yb_ref, yc_ref, g0_ref, g1_ref, g2_ref, x_ref, wb_ref, wo_ref, gn_ref, o_ref):
    m = None
    for br, (y_ref, g_ref) in enumerate(((ya_ref, g0_ref), (yb_ref, g1_ref), (yc_ref, g2_ref))):
        t = jax.nn.sigmoid(g_ref[...].astype(F32)) * jnp.dot(y_ref[...], wb_ref[br], preferred_element_type=F32)
        m = t if m is None else m + t
    z = jnp.dot(m.astype(BF16), wo_ref[...], preferred_element_type=F32)
    o_ref[...] = x_ref[...] + _rms(z, gn_ref[...])


def _merge(ya, yb, yc, proj, x2, wb, wo, gn, layer, tm=512):
    T, D = x2.shape
    row = lambda i: (i, 0)
    mcol = COL_MERGE // D
    return pl.pallas_call(
        _merge_kernel,
        grid=(T // tm,),
        in_specs=[pl.BlockSpec((tm, D), row), pl.BlockSpec((tm, D), row), pl.BlockSpec((tm, D), row),
                  pl.BlockSpec((tm, D), lambda i: (i, mcol)),
                  pl.BlockSpec((tm, D), lambda i: (i, mcol + 1)),
                  pl.BlockSpec((tm, D), lambda i: (i, mcol + 2)),
                  pl.BlockSpec((tm, D), row),
                  _const_spec(wb.shape[1:], layer), _const_spec(wo.shape[1:], layer), _const_spec((1, D))],
        out_specs=pl.BlockSpec((tm, D), row),
        out_shape=jax.ShapeDtypeStruct((T, D), F32),
        compiler_params=_cparams(("parallel",)),
        name="merge_out",
    )(ya, yb, yc, proj, proj, proj, x2, wb, wo, gn)


def _ffn_kernel(x_ref, gpre_ref, win_ref, wout_ref, gpost_ref, o_ref, *, chunk):
    x = x_ref[...]
    h = _rms(x, gpre_ref[...]).astype(BF16)
    acc = None
    for c in range(D_FF // chunk):
        gt = jnp.dot(h, win_ref[:, c * chunk:(c + 1) * chunk], preferred_element_type=F32)
        up = jnp.dot(h, win_ref[:, D_FF + c * chunk:D_FF + (c + 1) * chunk], preferred_element_type=F32)
        a = (gt * jax.nn.sigmoid(gt) * up).astype(BF16)
        t = jnp.dot(a, wout_ref[c * chunk:(c + 1) * chunk, :], preferred_element_type=F32)
        acc = t if acc is None else acc + t
    o_ref[...] = x + _rms(acc, gpost_ref[...])


def _ffn(x2, gpre, win, wout, gpost, layer, tm=512, chunk=256):
    T, D = x2.shape
    row = lambda i: (i, 0)
    return pl.pallas_call(
        functools.partial(_ffn_kernel, chunk=chunk),
        grid=(T // tm,),
        in_specs=[pl.BlockSpec((tm, D), row), _const_spec((1, D)), _const_spec(win.shape[1:], layer),
                  _const_spec(wout.shape[1:], layer), _const_spec((1, D))],
        out_specs=pl.BlockSpec((tm, D), row),
        out_shape=jax.ShapeDtypeStruct((T, D), F32),
        compiler_params=_cparams(("parallel",)),
        name="ffn",
    )(x2, gpre, win, wout, gpost)


def _in_column_sources():
    (lru_x, lru_g, nsa_q, nsa_kv, nsa_gate, gla_q, gla_k, gla_v, gla_og, gla_lr, merge) = np.split(
        np.arange(sum(IN_SIZES)), np.cumsum(IN_SIZES)[:-1])
    kv = nsa_kv.reshape(3, 2, NSA_G, NSA_HD).transpose(0, 2, 1, 3).reshape(-1)
    pad = np.full(LANE - nsa_gate.size - gla_lr.size, -1)
    src = np.concatenate([lru_x, lru_g, nsa_q, gla_v, gla_og, merge, kv, gla_q, gla_k, nsa_gate, gla_lr, pad])
    assert src.size == PROJ_WIDTH
    return src


def _permute_plan():
    src = _in_column_sources()
    plan = []
    for c in range(PROJ_WIDTH // LANE):
        runs = []
        for s in src[c * LANE:(c + 1) * LANE]:
            s = int(s)
            if runs and ((s < 0 and runs[-1][0] < 0) or (s >= 0 and runs[-1][0] >= 0 and runs[-1][0] + runs[-1][1] == s)):
                runs[-1][1] += 1
            else:
                runs.append([s, 1])
        assert all(n % 8 == 0 and (s < 0 or s % 8 == 0) for s, n in runs)
        plan.append(tuple((s, n) for s, n in runs))
    return tuple(plan)


def _permute_kernel(wt_ref, o_ref, *, plan):
    tc = wt_ref.shape[1]
    for c, runs in enumerate(plan):
        parts = [wt_ref[s:s + n, :] if s >= 0 else jnp.zeros((n, tc), F32) for s, n in runs]
        block = parts[0] if len(parts) == 1 else jnp.concatenate(parts, axis=0)
        o_ref[:, c * LANE:(c + 1) * LANE] = block.T.astype(o_ref.dtype)


def _permute_in_weights(w_in, tc=128):
    L, D, W = w_in.shape
    return pl.pallas_call(
        functools.partial(_permute_kernel, plan=_permute_plan()),
        grid=(L, D // tc),
        in_specs=[pl.BlockSpec((None, W, tc), lambda l, i: (l, 0, i))],
        out_specs=pl.BlockSpec((None, tc, PROJ_WIDTH), lambda l, i: (l, i, 0)),
        out_shape=jax.ShapeDtypeStruct((L, D, PROJ_WIDTH), BF16),
        compiler_params=_cparams(("parallel", "parallel")),
        name="permute_w_in",
    )(jnp.swapaxes(w_in, 1, 2))


def _layer(x2, B, S, tables, layer, norm_g, w_in, conv_w, conv_b, lru_w_gates, lru_b_gates, lru_lambda,
           cmp_pos, cmp_w1, cmp_w2, gla_wa2, gla_ba, gla_norm, w_branch, w_out, w_ffn_in, w_ffn_out):
    D = D_MODEL
    proj = _in_proj(x2, norm_g[0][None, :], w_in, layer)
    proj3 = proj.reshape(B, S, PROJ_WIDTH)

    wg = jnp.concatenate([lru_w_gates[0], lru_w_gates[1]], axis=-1).astype(BF16)
    y_a = _lru(proj3, conv_w, conv_b[None, :], wg, lru_b_gates, lru_lambda[None, :])

    kvc = _compress(proj3, *_compress_params(cmp_pos, cmp_w1, cmp_w2))
    y_b = _nsa(proj3, kvc, *tables)

    wa_pad = jnp.zeros((LANE, GLA_HEADS * GLA_DK), F32).at[SMALL_LR:SMALL_LR + GLA_RANK].set(gla_wa2).astype(BF16)
    y_c = _gla(proj3, wa_pad, gla_ba[None, :], gla_norm[None, :])

    x2 = _merge(y_a.reshape(B * S, D), y_b.reshape(B * S, D), y_c.reshape(B * S, D), proj, x2,
                w_branch, w_out, norm_g[1][None, :], layer)
    x2 = _ffn(x2, norm_g[2][None, :], w_ffn_in, w_ffn_out, norm_g[3][None, :], layer)
    return x2


def kernel(x, rel_table, norm_g, w_in, conv_w, conv_b, lru_w_gates, lru_b_gates, lru_lambda, cmp_pos, cmp_w1,
           cmp_w2, gla_wa2, gla_ba, gla_norm, w_branch, w_out, w_ffn_in, w_ffn_out):
    B, S, D = x.shape
    tables = _nsa_tables(rel_table, S)
    x2 = x.reshape(B * S, D)
    w_in = _permute_in_weights(w_in)
    w_branch, w_out, w_ffn_in, w_ffn_out = (w.astype(BF16) for w in (w_branch, w_out, w_ffn_in, w_ffn_out))
    for l in range(norm_g.shape[0]):
        x2 = _layer(x2, B, S, tables, l, norm_g[l], w_in, conv_w[l], conv_b[l], lru_w_gates[l], lru_b_gates[l],
                    lru_lambda[l], cmp_pos[l], cmp_w1[l], cmp_w2[l], gla_wa2[l], gla_ba[l], gla_norm[l],
                    w_branch, w_out, w_ffn_in, w_ffn_out)
    return x2.reshape(B, S, D)
```

```python
import functools
import math

import jax
import jax.numpy as jnp
import numpy as np
from jax import lax
from jax.experimental import pallas as pl
from jax.experimental.pallas import tpu as pltpu

F32 = jnp.float32
BF16 = jnp.bfloat16

D_MODEL = 1024
N_BRANCH = 3
EPS = 1e-6
NEG_INF = -1e30
FORCED = 1e4

LRU_BLOCKS = 8
LRU_BLOCK = D_MODEL // LRU_BLOCKS
CONV_WIDTH = 4
LRU_C = 8.0

NSA_HEADS = 16
NSA_G = 4
NSA_R = NSA_HEADS // NSA_G
NSA_HD = D_MODEL // NSA_HEADS
CMP_BLOCK = 32
CMP_STRIDE = 16
CMP_HIDDEN = 256
SLC_BLOCK = 64
SLC_TOP_N = 8
WINDOW = 256

GLA_HEADS = 4
GLA_DK = (D_MODEL // 2) // GLA_HEADS
GLA_DV = D_MODEL // GLA_HEADS
GLA_RANK = 16
GLA_TAU = 16.0
GLA_CHUNK = 32

REL_BUCKETS = 32
REL_MAX_EXACT = 16
REL_MAX_DIST = 128

D_FF = -(-8 * D_MODEL // (3 * 256)) * 256

IN_SIZES = (D_MODEL, D_MODEL, D_MODEL, 6 * NSA_G * NSA_HD, 3 * NSA_HEADS, D_MODEL // 2, D_MODEL // 2,
            D_MODEL, D_MODEL, GLA_RANK, N_BRANCH * D_MODEL)

LANE = 128
COL_LRU_X = 0
COL_LRU_G = 1024
COL_NSA_Q = 2048
COL_GLA_V = 3072
COL_GLA_OG = 4096
COL_MERGE = 5120
COL_NSA_KV = 8192
COL_GLA_Q = 9728
COL_GLA_K = 10240
COL_SMALL = 10752
PROJ_WIDTH = 10880
SMALL_GATE = 0
SMALL_LR = 3 * NSA_HEADS

Q_TILE = 256
K_TILE = 256
NSA_BATCH_ROWS = 4
WIN_SPAN = WINDOW // LANE + 1
VMEM_LIMIT = 56 * 1024 * 1024
_NT = (((1,), (1,)), ((), ()))


def _cparams(sem):
    return pltpu.CompilerParams(dimension_semantics=sem, vmem_limit_bytes=VMEM_LIMIT)


def _const_spec(shape, layer=None):
    nd = len(shape)
    if layer is None:
        return pl.BlockSpec(shape, lambda *_: (0,) * nd, pipeline_mode=pl.Buffered(1))
    return pl.BlockSpec((None,) + tuple(shape), lambda *_: (layer,) + (0,) * nd, pipeline_mode=pl.Buffered(1))


def _rms(x, g):
    return x * lax.rsqrt(jnp.mean(x * x, axis=-1, keepdims=True) + EPS) * g


def _gelu_tanh(x):
    return 0.5 * x * (1.0 + jnp.tanh(math.sqrt(2.0 / math.pi) * (x + 0.044715 * (x * x * x))))


def _sigmoid(x):
    return 0.5 * jnp.tanh(0.5 * x) + 0.5


def _sqrt_nonneg(x):
    return jnp.where(x > 0.0, x * lax.rsqrt(x), 0.0)


def _softplus(z):
    return jnp.maximum(z, 0.0) + jnp.log1p(jnp.exp(-jnp.abs(z)))


def _in_proj_kernel(x_ref, g_ref, w_ref, o_ref, h_ref):
    @pl.when(pl.program_id(1) == 0)
    def _():
        h_ref[...] = _rms(x_ref[...], g_ref[...]).astype(BF16)

    o_ref[...] = jnp.dot(h_ref[...], w_ref[...], preferred_element_type=F32).astype(o_ref.dtype)


def _in_proj(x2, g, w, layer, tm=1024, tn=2176):
    T, D = x2.shape
    N = w.shape[2]
    return pl.pallas_call(
        _in_proj_kernel,
        grid=(T // tm, N // tn),
        in_specs=[pl.BlockSpec((tm, D), lambda i, j: (i, 0)),
                  pl.BlockSpec((1, D), lambda i, j: (0, 0)),
                  pl.BlockSpec((None, D, tn), lambda i, j: (layer, 0, j))],
        out_specs=pl.BlockSpec((tm, tn), lambda i, j: (i, j)),
        out_shape=jax.ShapeDtypeStruct((T, N), BF16),
        scratch_shapes=[pltpu.VMEM((tm, D), BF16)],
        compiler_params=_cparams(("parallel", "arbitrary")),
        name="in_proj",
    )(x2, g, w)


def _lru_kernel(xa_ref, ga_ref, cw_ref, cb_ref, wg_ref, bg_ref, lam_ref, y_ref, tail_ref, h_ref, *, R):
    @pl.when(pl.program_id(1) == 0)
    def _():
        tail_ref[...] = jnp.zeros_like(tail_ref)
        h_ref[...] = jnp.zeros_like(h_ref)

    sp = _softplus(-lam_ref[...])
    groups = R // 8
    row8 = lax.broadcasted_iota(jnp.int32, (groups, 8, LRU_BLOCK), 1)
    for n in range(LRU_BLOCKS):
        sl = slice(n * LRU_BLOCK, (n + 1) * LRU_BLOCK)
        x3 = xa_ref[0, :, sl].astype(F32).reshape(groups, 8, LRU_BLOCK)
        tail = tail_ref[:, sl].reshape(1, 8, LRU_BLOCK)
        xc3 = cb_ref[:, sl] + cw_ref[CONV_WIDTH - 1:CONV_WIDTH, sl] * x3
        for s in range(1, CONV_WIDTH):
            rolled = pltpu.roll(x3, s, axis=1)
            prev = jnp.concatenate([pltpu.roll(tail, s, axis=1), rolled[:groups - 1]], axis=0)
            w = cw_ref[CONV_WIDTH - 1 - s:CONV_WIDTH - s, sl]
            xc3 = xc3 + w * jnp.where(row8 >= s, rolled, prev)
        tail_ref[:, sl] = x3[groups - 1]
        xcn = xc3.reshape(R, LRU_BLOCK)
        gz =jnp.dot(xcn.astype(BF16), wg_ref[n], preferred_element_type=F32)
        r = _sigmoid(gz[:, :LRU_BLOCK] + bg_ref[0:1, sl])
        i = _sigmoid(gz[:, LRU_BLOCK:] + bg_ref[1:2, sl])
        a = jnp.exp((-LRU_C) * r * sp[:, sl])
        u = _sqrt_nonneg(1.0 - a * a) * (i * xcn)
        a = a.reshape(R // 8, 8, LRU_BLOCK)
        u = u.reshape(R // 8, 8, LRU_BLOCK)
        for d in (1, 2, 4):
            keep = row8 >= d
            a_s = pltpu.roll(a, d, axis=1)
            u_s = pltpu.roll(u, d, axis=1)
            u = jnp.where(keep, a * u_s + u, u)
            a = jnp.where(keep, a * a_s, a)
        hprev = h_ref[:, sl]
        hs = []
        for t in range(R // 8):
            ht = a[t] * hprev + u[t]
            hprev = ht[7:8]
            hs.append(ht)
        h_ref[:, sl] = hprev
        h = jnp.concatenate(hs, axis=0)
        y_ref[0, :, sl] = (h * _gelu_tanh(ga_ref[0, :, sl].astype(F32))).astype(y_ref.dtype)


def _lru(proj3, conv_w, conv_b, wg, bg, lam, R=256):
    B, S, _ = proj3.shape
    D = D_MODEL
    return pl.pallas_call(
        functools.partial(_lru_kernel, R=R),
        grid=(B, S // R),
        in_specs=[pl.BlockSpec((1, R, D), lambda b, s: (b, s, COL_LRU_X // D)),
                  pl.BlockSpec((1, R, D), lambda b, s: (b, s, COL_LRU_G // D)),
                  _const_spec((CONV_WIDTH, D)),
                  _const_spec((1, D)),
                  _const_spec((LRU_BLOCKS, LRU_BLOCK, 2 * LRU_BLOCK)),
                  _const_spec((2, D)),
                  _const_spec((1, D))],
        out_specs=pl.BlockSpec((1, R, D), lambda b, s: (b, s, 0)),
        out_shape=jax.ShapeDtypeStruct((B, S, D), BF16),
        scratch_shapes=[pltpu.VMEM((8, D), F32), pltpu.VMEM((1, D), F32)],
        compiler_params=_cparams(("parallel", "arbitrary")),
        name="rglru",
    )(proj3, proj3, conv_w, conv_b, wg, bg, lam)


def _lru_rows(load_x, load_g, store_y, cw_ref, cb_ref, wg_ref, bg_ref, lam_ref, tail_ref, st_ref, slot, R,
              before_block=None):
    sp = _softplus(-lam_ref[...])
    groups = R // 8
    row8 = lax.broadcasted_iota(jnp.int32, (groups, 8, LRU_BLOCK), 1)
    for n in range(LRU_BLOCKS):
        if before_block is not None:
            before_block(n)
        sl = slice(n * LRU_BLOCK, (n + 1) * LRU_BLOCK)
        x3 = load_x(sl).astype(F32).reshape(groups, 8, LRU_BLOCK)
        tail = tail_ref[slot, :, sl].reshape(1, 8, LRU_BLOCK)
        xc3 = cb_ref[:, sl] + cw_ref[CONV_WIDTH - 1:CONV_WIDTH, sl] * x3
        for s in range(1, CONV_WIDTH):
            rolled = pltpu.roll(x3, s, axis=1)
            prev = jnp.concatenate([pltpu.roll(tail, s, axis=1), rolled[:groups - 1]], axis=0)
            w = cw_ref[CONV_WIDTH - 1 - s:CONV_WIDTH - s, sl]
            xc3 = xc3 + w * jnp.where(row8 >= s, rolled, prev)
        tail_ref[slot, :, sl] = x3[groups - 1]
        xcn = xc3.reshape(R, LRU_BLOCK)
        gz = jnp.dot(xcn.astype(BF16), wg_ref[n], preferred_element_type=F32)
        r = _sigmoid(gz[:, :LRU_BLOCK] + bg_ref[0:1, sl])
        i = _sigmoid(gz[:, LRU_BLOCK:] + bg_ref[1:2, sl])
        a = jnp.exp((-LRU_C) * r * sp[:, sl])
        u = _sqrt_nonneg(1.0 - a * a) * (i * xcn)
        a = a.reshape(groups, 8, LRU_BLOCK)
        u = u.reshape(groups, 8, LRU_BLOCK)
        for d in (1, 2, 4):
            keep = row8 >= d
            a_s = pltpu.roll(a, d, axis=1)
            u_s = pltpu.roll(u, d, axis=1)
            u = jnp.where(keep, a * u_s + u, u)
            a = jnp.where(keep, a * a_s, a)
        hprev = st_ref[slot, :, sl]
        hs = []
        for t in range(groups):
            ht = a[t] * hprev + u[t]
            hprev = ht[7:8]
            hs.append(ht)
        st_ref[slot, :, sl] = hprev
        store_y(sl, jnp.concatenate(hs, axis=0) * _gelu_tanh(load_g(sl).astype(F32)))


def _in_proj_lru_kernel(x_ref, g_ref, w_ref, cw_ref, cb_ref, wg_ref, bg_ref, lam_ref, o_ref, y_ref,
                        h_ref, xg_ref, tail_ref, st_ref, *, R, seq_rows):
    i = pl.program_id(0)
    j = pl.program_id(1)
    tm, D = x_ref.shape

    @pl.when(j == 0)
    def _():
        h_ref[...] = _rms(x_ref[...], g_ref[...]).astype(BF16)

    @pl.when((j == 0) & (i == 0))
    def _():
        xg_ref[...] = jnp.zeros_like(xg_ref)

    @pl.when((j == 0) & ((i * tm) % seq_rows == 0))
    def _():
        tail_ref[...] = jnp.zeros_like(tail_ref)
        st_ref[...] = jnp.zeros_like(st_ref)

    tn = o_ref.shape[1]
    chunk = -(-tn // (LRU_BLOCKS + 1) // LANE) * LANE

    def project(c):
        cols = slice(c * chunk, min((c + 1) * chunk, tn))
        o_ref[:, cols] = jnp.dot(h_ref[...], w_ref[:, cols], preferred_element_type=F32).astype(o_ref.dtype)

    rows = pl.ds(pl.multiple_of(jnp.maximum(j - 1, 0) * R, R), R)

    def store_y(sl, val):
        y_ref[:, sl] = val.astype(y_ref.dtype)

    _lru_rows(lambda sl: xg_ref[rows, sl], lambda sl: xg_ref[rows, slice(D + sl.start, D + sl.stop)], store_y,
              cw_ref, cb_ref, wg_ref, bg_ref, lam_ref, tail_ref, st_ref, jnp.where(j == 0, 1, 0), R,
              before_block=project)
    for c in range(LRU_BLOCKS, -(-tn // chunk)):
        project(c)

    @pl.when(j == 0)
    def _():
        xg_ref[...] = o_ref[:, COL_LRU_X:COL_LRU_X + 2 * D]


def _in_proj_lru(x2, g, w, layer, conv_w, conv_b, wg, bg, lam, seq_rows, tm=1024, tn=2176, R=256):
    T, D = x2.shape
    N = w.shape[2]
    assert N // tn == tm // R + 1 and COL_LRU_X == 0 and COL_LRU_G == D and tn >= 2 * D and seq_rows % tm == 0
    return pl.pallas_call(
        functools.partial(_in_proj_lru_kernel, R=R, seq_rows=seq_rows),
        grid=(T // tm, N // tn),
        in_specs=[pl.BlockSpec((tm, D), lambda i, j: (i, 0)),
                  pl.BlockSpec((1, D), lambda i, j: (0, 0)),
                  pl.BlockSpec((None, D, tn), lambda i, j: (layer, 0, j)),
                  _const_spec((CONV_WIDTH, D)),
                  _const_spec((1, D)),
                  _const_spec((LRU_BLOCKS, LRU_BLOCK, 2 * LRU_BLOCK)),
                  _const_spec((2, D)),
                  _const_spec((1, D))],
        out_specs=[pl.BlockSpec((tm, tn), lambda i, j: (i, j)),
                   pl.BlockSpec((R, D), lambda i, j: (i * (tm // R) + jnp.maximum(j - 1, 0), 0))],
        out_shape=[jax.ShapeDtypeStruct((T, N), BF16), jax.ShapeDtypeStruct((T, D), BF16)],
        scratch_shapes=[pltpu.VMEM((tm, D), BF16), pltpu.VMEM((tm, 2 * D), BF16),
                        pltpu.VMEM((2, 8, D), F32), pltpu.VMEM((2, 1, D), F32)],
        compiler_params=_cparams(("arbitrary", "arbitrary")),
        name="in_proj_lru",
    )(x2, g, w, conv_w, conv_b, wg, bg, lam)


def _gla_kernel(q_ref, k_ref, v_ref, og_ref, sm_ref, wa_ref, ba_ref, gn_ref, y_ref, st_ref, *, Tb):
    nc = Tb // GLA_CHUNK

    @pl.when(pl.program_id(1) == 0)
    def _():
        st_ref[...] = jnp.zeros_like(st_ref)

    la_pre = jnp.dot(sm_ref[0], wa_ref[...], preferred_element_type=F32)
    rowc = lax.broadcasted_iota(jnp.int32, (Tb, GLA_DK), 0) & (GLA_CHUNK - 1)
    ri = lax.broadcasted_iota(jnp.int32, (Tb, Tb), 0)
    ci = lax.broadcasted_iota(jnp.int32, (Tb, Tb), 1)
    same_chunk_causal = ((ri // GLA_CHUNK) == (ci // GLA_CHUNK)) & (ci <= ri)
    lane_chunk = lax.broadcasted_iota(jnp.int32, (GLA_DK, Tb), 1) // GLA_CHUNK

    heads = range(GLA_HEADS)
    kcols = [slice(h * GLA_DK, (h + 1) * GLA_DK) for h in heads]
    vcols = [slice(h * GLA_DV, (h + 1) * GLA_DV) for h in heads]

    def chunk_log_decay(h):
        z = la_pre[:, kcols[h]] + ba_ref[:, kcols[h]]
        b = (jnp.minimum(z, 0.0) - jnp.log(1.0 + jnp.exp(-jnp.abs(z)))) * (1.0 / GLA_TAU)
        d = 1
        while d < GLA_CHUNK:
            b = b + jnp.where(rowc >= d, pltpu.roll(b, d, axis=0), 0.0)
            d *= 2
        return b

    b = [chunk_log_decay(h) for h in heads]
    b_last = [jnp.concatenate(
        [jnp.broadcast_to(b[h][c * GLA_CHUNK + GLA_CHUNK - 1:(c + 1) * GLA_CHUNK, :], (GLA_CHUNK, GLA_DK))
         for c in range(nc)], axis=0) for h in heads]
    q = [q_ref[0, :, kcols[h]].astype(F32) * (GLA_DK ** -0.5) for h in heads]
    k = [k_ref[0, :, kcols[h]].astype(F32) for h in heads]
    v = [v_ref[0, :, vcols[h]] for h in heads]
    qt = [(q[h] * jnp.exp(b[h])).astype(BF16) for h in heads]
    kt = [(k[h] * jnp.exp(-b[h])).astype(BF16) for h in heads]
    ke_t = [(k[h] * jnp.exp(b_last[h] - b[h])).T for h in heads]
    b_t = [b[h].T for h in heads]

    att = [jnp.where(same_chunk_causal, lax.dot_general(qt[h], kt[h], _NT, preferred_element_type=F32), 0.0)
           for h in heads]
    o = [jnp.dot(att[h].astype(BF16), v[h], preferred_element_type=F32) for h in heads]
    kv_all = [jnp.dot(jnp.concatenate([jnp.where(lane_chunk == c, ke_t[h], 0.0) for c in range(nc)],
                                      axis=0).astype(BF16), v[h], preferred_element_type=F32)
              for h in heads]

    state = [st_ref[h] for h in heads]
    inter = [[] for _ in heads]
    for c in range(nc):
        rs = slice(c * GLA_CHUNK, (c + 1) * GLA_CHUNK)
        last = c * GLA_CHUNK + GLA_CHUNK - 1
        for h in heads:
            inter[h].append(jnp.dot(qt[h][rs], state[h].astype(BF16), preferred_element_type=F32))
            state[h] = state[h] * jnp.exp(b_t[h][:, last:last + 1]) + kv_all[h][c * GLA_DK:(c + 1) * GLA_DK]
    for h in heads:
        st_ref[h] = state[h]
        oh = o[h] + jnp.concatenate(inter[h], axis=0)
        on = oh * lax.rsqrt(jnp.mean(oh * oh, axis=-1, keepdims=True) + EPS) * gn_ref[...]
        og = og_ref[0, :, vcols[h]].astype(F32)
        y_ref[0, :, vcols[h]] = (on * (og * jax.nn.sigmoid(og))).astype(y_ref.dtype)


def _gla(proj3, wa_pad, ba, gn, Tb=256):
    B, S, _ = proj3.shape
    HK = GLA_HEADS * GLA_DK
    D = D_MODEL
    return pl.pallas_call(
        functools.partial(_gla_kernel, Tb=Tb),
        grid=(B, S // Tb),
        in_specs=[pl.BlockSpec((1, Tb, HK), lambda b, s: (b, s, COL_GLA_Q // HK)),
                  pl.BlockSpec((1, Tb, HK), lambda b, s: (b, s, COL_GLA_K // HK)),
                  pl.BlockSpec((1, Tb, D), lambda b, s: (b, s, COL_GLA_V // D)),
                  pl.BlockSpec((1, Tb, D), lambda b, s: (b, s, COL_GLA_OG // D)),
                  pl.BlockSpec((1, Tb, LANE), lambda b, s: (b, s, COL_SMALL // LANE)),
                  _const_spec((LANE, HK)),
                  _const_spec((1, HK)),
                  _const_spec((1, GLA_DV))],
        out_specs=pl.BlockSpec((1, Tb, D), lambda b, s: (b, s, 0)),
        out_shape=jax.ShapeDtypeStruct((B, S, D), BF16),
        scratch_shapes=[pltpu.VMEM((GLA_HEADS, GLA_DK, GLA_DV), F32)],
        compiler_params=_cparams(("parallel", "arbitrary")),
        name="gla",
    )(proj3, proj3, proj3, proj3, proj3, wa_pad, ba, gn)


def _cmp_kernel(kv_ref, pos_ref, w1_ref, w2_ref, o_ref, x_ref):
    x_ref[...] = kv_ref[0].astype(F32)
    nrow = x_ref.shape[0] // CMP_STRIDE
    first = None
    second = None
    for i in range(CMP_STRIDE):
        xi = x_ref[pl.ds(i, nrow, stride=CMP_STRIDE), :]
        f = jnp.dot((xi + pos_ref[0, i]).astype(BF16), w1_ref[0, i], preferred_element_type=F32)
        s = jnp.dot((xi + pos_ref[1, i]).astype(BF16), w1_ref[1, i], preferred_element_type=F32)
        first = f if first is None else first + f
        second = s if second is None else second + s
    pre = first + pltpu.roll(second, nrow - 1, axis=0)
    hid = _gelu_tanh(pre).astype(BF16)
    o_ref[0, 0] = jnp.dot(hid, w2_ref[...], preferred_element_type=F32).astype(o_ref.dtype)


def _compress(proj3, pos, w1, w2):
    B, S, _ = proj3.shape
    nrow = S // CMP_STRIDE
    kvb = COL_NSA_KV // LANE
    return pl.pallas_call(
        _cmp_kernel,
        grid=(B, NSA_G),
        in_specs=[pl.BlockSpec((1, S, LANE), lambda b, g: (b, 0, kvb + g)),
                  _const_spec(pos.shape), _const_spec(w1.shape), _const_spec(w2.shape)],
        out_specs=pl.BlockSpec((1, 1, nrow, LANE), lambda b, g: (b, g, 0, 0)),
        out_shape=jax.ShapeDtypeStruct((B, NSA_G, nrow, LANE), BF16),
        scratch_shapes=[pltpu.VMEM((S, LANE), F32)],
        compiler_params=_cparams(("parallel", "parallel")),
        name="nsa_compress",
    )(proj3, pos, w1, w2)


def _compress_params(cmp_pos, cmp_w1, cmp_w2):
    hd = NSA_HD
    w1 = cmp_w1.reshape(2, 2, CMP_STRIDE, hd, CMP_HIDDEN)
    z = jnp.zeros_like(w1[0])
    w1 = jnp.concatenate([jnp.concatenate([w1[0], z], axis=-1),
                          jnp.concatenate([z, w1[1]], axis=-1)], axis=-2)
    pos = cmp_pos.reshape(2, 2, CMP_STRIDE, 1, hd)
    pos = jnp.concatenate([pos[0], pos[1]], axis=-1)
    z2 = jnp.zeros_like(cmp_w2[0])
    w2 = jnp.concatenate([jnp.concatenate([cmp_w2[0], z2], axis=-1),
                          jnp.concatenate([z2, cmp_w2[1]], axis=-1)], axis=0)
    return pos, w1.astype(BF16), w2.astype(BF16)


V_ROWS = NSA_HD + 16
MASK_BIG = 1e30
LOG2_E = 1.4426950408889634


def _nsa_kernel(q_ref, kvc_ref, kvs_ref, kvw_ref, gate_ref, bc_ref, bs_ref, bw_ref, xtra_ref, ovl_ref,
                gsel_ref, y_ref, a_ref, b_ref, vs_ref, vw_ref, vc_ref, lg_ref, p_ref):
    NB = q_ref.shape[0]
    rows_nb = range(NB)
    TQ = Q_TILE
    KT = K_TILE
    RT = TQ // KT
    HW = NSA_R * TQ
    qi = pl.program_id(2)
    S = kvs_ref.shape[1]
    n_cmp_rows = kvc_ref.shape[2]

    def value_rows(tile):
        t = tile.astype(F32).T
        return jnp.concatenate([t[NSA_HD:], jnp.ones((V_ROWS - NSA_HD, t.shape[1]), F32)], axis=0).astype(BF16)

    @pl.when(qi == 0)
    def _():
        for nb in rows_nb:
            for j in range(S // LANE):
                rows = slice(j * LANE, (j + 1) * LANE)
                ks = kvs_ref[nb, rows, :]
                a_ref[nb, rows, 0:LANE] = ks
                a_ref[nb, rows, LANE:2 * LANE] = xtra_ref[rows, :]
                vs_ref[nb, :, rows] = value_rows(ks)
                vw_ref[nb, :, rows] = value_rows(kvw_ref[nb, rows, :])
            vc_ref[nb] = value_rows(kvc_ref[nb, 0])
            b_ref[nb, LANE + ovl_ref.shape[0]:, :] = jnp.zeros((LANE - ovl_ref.shape[0], HW), BF16)

    def tile_rows(j, n=1):
        return pl.ds(pl.multiple_of(j * KT, KT), n * KT)

    def normalised(acc):
        return acc[:NSA_HD] * (1.0 / acc[NSA_HD:NSA_HD + 1])

    def q_transposed(nb):
        qf = q_ref[nb].astype(F32) * (NSA_HD ** -0.5)
        zero_rows = jnp.zeros((LANE - NSA_HD, TQ), F32)
        heads = []
        for pr in range(NSA_R // 2):
            t = qf[:, pr * LANE:(pr + 1) * LANE].T
            heads.append(jnp.concatenate([t[:NSA_HD], zero_rows], axis=0))
            heads.append(jnp.concatenate([t[NSA_HD:], zero_rows], axis=0))
        return jnp.concatenate(heads, axis=1)

    qt32 = [q_transposed(nb) for nb in rows_nb]
    qt = [x.astype(BF16) for x in qt32]
    qt32_l2 = [x * LOG2_E for x in qt32]
    qt_l2 = [x.astype(BF16) for x in qt32_l2]

    parts = TQ // LANE
    o_win_parts = [[None] * parts for _ in rows_nb]
    for h in range(parts):
        own = parts * qi + h
        skipped = jnp.maximum(WIN_SPAN - 1 - own, 0)
        rows_w = pl.ds(pl.multiple_of(jnp.maximum(own - (WIN_SPAN - 1), 0) * LANE, LANE), WIN_SPAN * LANE)
        bias_w = jnp.concatenate([bw_ref[0, jnp.minimum(t + skipped, WIN_SPAN)] for t in range(WIN_SPAN)], axis=0)
        q_part = [jnp.concatenate([qt_l2[nb][:, r * TQ + h * LANE:r * TQ + (h + 1) * LANE] for r in range(NSA_R)],
                                  axis=1) for nb in rows_nb]
        lg_w = [jnp.dot(kvw_ref[nb, rows_w, :], q_part[nb], preferred_element_type=F32).astype(BF16) + bias_w
                for nb in rows_nb]
        p_w = [jnp.exp2(lg - jnp.max(lg, axis=0, keepdims=True)) for lg in lg_w]
        for nb in rows_nb:
            o_win_parts[nb][h] = normalised(jnp.dot(vw_ref[nb, :, rows_w], p_w[nb], preferred_element_type=F32))
    o_win = [jnp.concatenate([o_win_parts[nb][h][:, r * LANE:(r + 1) * LANE]
                              for r in range(NSA_R) for h in range(parts)], axis=1) for nb in rows_nb]

    cq = TQ // CMP_STRIDE
    start_c = pl.multiple_of(bc_ref.shape[1] - n_cmp_rows - cq * qi, 8)
    bias_c = bc_ref[0, pl.ds(start_c, n_cmp_rows), :]
    valid_c = bias_c > 0.5 * NEG_INF
    lg_c = [jnp.dot(kvc_ref[nb, 0], qt[nb], preferred_element_type=F32) + bias_c for nb in rows_nb]
    e_c = [jnp.exp(lg - jnp.max(lg, axis=0, keepdims=True)) for lg in lg_c]
    p_c = [jnp.where(valid_c, e * (1.0 / jnp.sum(e, axis=0, keepdims=True)), 0.0) for e in e_c]
    o_cmp = [jnp.dot(vc_ref[nb], p_c[nb].astype(BF16), preferred_element_type=F32)[:NSA_HD] for nb in rows_nb]

    ovl = ovl_ref[...]
    ns = ovl.shape[0]
    blk = lax.broadcasted_iota(jnp.int32, (ns, TQ), 0)
    cur = (qi * TQ + lax.broadcasted_iota(jnp.int32, (ns, TQ), 1)) // SLC_BLOCK
    forced = (blk == 0) | (blk == cur) | (blk == cur - 1)
    allowed = blk <= cur

    def importance(p):
        psum = p[:, 0:TQ]
        for r in range(1, NSA_R):
            psum = psum + p[:, r * TQ:(r + 1) * TQ]
        p_hi = psum.astype(BF16)
        p_lo = (psum - p_hi.astype(F32)).astype(BF16)
        imp = jnp.dot(ovl, p_hi, preferred_element_type=F32) + jnp.dot(ovl, p_lo, preferred_element_type=F32)
        return jnp.where(forced, FORCED, jnp.where(allowed, imp, -FORCED))

    score = [importance(p) for p in p_c]
    groups = ns // 8
    blk8 = lax.broadcasted_iota(jnp.int32, (8, TQ), 0)
    for nb in rows_nb:
        sc = [score[nb][8 * v:8 * v + 8] for v in range(groups)]
        rank = [jnp.zeros((8, TQ), F32) for _ in range(groups)]
        for j in range(ns):
            sj = score[nb][j:j + 1, :]
            for v in range(groups):
                if j < 8 * v:
                    beats = sj >= sc[v]
                elif j >= 8 * v + 8:
                    beats = sj > sc[v]
                else:
                    tie = jnp.where(blk8 > j - 8 * v, 1.0, 0.0)
                    rank[v] = rank[v] + jnp.where(sj > sc[v], 1.0, jnp.where(sj == sc[v], tie, 0.0))
                    continue
                rank[v] = rank[v] + jnp.where(beats, 1.0, 0.0)
        neg_sel = jnp.where(jnp.concatenate(rank, axis=0) < float(SLC_TOP_N), 0.0, -MASK_BIG).astype(BF16)
        b_ref[nb, 0:LANE] = qt_l2[nb]
        b_ref[nb, LANE:LANE + ns] = jnp.concatenate([neg_sel] * NSA_R, axis=1)

    def logits(nb, j):
        return jnp.dot(a_ref[nb, tile_rows(j), :], b_ref[nb], preferred_element_type=F32).astype(BF16)

    for nb in rows_nb:
        lg_ref[nb] = logits(nb, 0)
        p_ref[nb] = jnp.zeros(p_ref.shape[1:], BF16)

    last = RT * qi + RT - 1

    def trip(j, carry):
        lg_next = [logits(nb, jnp.minimum(j + 1, last)) for nb in rows_nb]
        prev_rows = tile_rows(jnp.maximum(j - 1, 0))
        bias = bs_ref[0, jnp.where(j >= RT * qi - 1, j - RT * qi + 1, RT + 1)]
        out = []
        for nb in rows_nb:
            m, acc, alpha = carry[nb]
            acc = acc * alpha + jnp.dot(vs_ref[nb, :, prev_rows], p_ref[nb], preferred_element_type=F32)
            lg = lg_ref[nb] + bias
            m_new = jnp.maximum(m, jnp.max(lg, axis=0, keepdims=True).astype(F32))
            p_ref[nb] = jnp.exp2(lg - m_new.astype(BF16))
            out.append((m_new, acc, jnp.exp2(m - m_new)))
        for nb in rows_nb:
            lg_ref[nb] = lg_next[nb]
        return tuple(out)

    init = (jnp.full((1, HW), NEG_INF, F32), jnp.zeros((V_ROWS, HW), F32), jnp.ones((1, HW), F32))
    carry = lax.fori_loop(0, last + 1, trip, tuple(init for _ in rows_nb))
    o_slc = [normalised(carry[nb][1] * carry[nb][2]
                        + jnp.dot(vs_ref[nb, :, tile_rows(last)], p_ref[nb], preferred_element_type=F32))
             for nb in rows_nb]

    for nb in rows_nb:
        gs = jax.nn.sigmoid(lax.dot_general(gsel_ref[0], gate_ref[nb], _NT, preferred_element_type=F32))

        def gate_row(br):
            return jnp.concatenate([gs[br * NSA_R + r:br * NSA_R + r + 1] for r in range(NSA_R)], axis=1)

        o = gate_row(0) * o_cmp[nb] + gate_row(1) * o_slc[nb] + gate_row(2) * o_win[nb]
        o = jnp.concatenate([o[:, r * TQ:(r + 1) * TQ] for r in range(NSA_R)], axis=0)
        y_ref[nb] = o.T.astype(y_ref.dtype)


def _nsa(proj3, kvc, bias_c, bias_s, bias_w, xtra, ovl_t, gsel):
    B, S, _ = proj3.shape
    TQ = Q_TILE
    G = NSA_G
    RW = NSA_R * NSA_HD
    HW = NSA_R * TQ
    kvb = COL_NSA_KV // LANE
    ncr = kvc.shape[2]
    KT = K_TILE
    NB = NSA_BATCH_ROWS
    assert S % TQ == 0 and TQ % KT == 0 and S >= TQ + KT and WINDOW == KT and ncr % LANE == 0 and B % NB == 0
    return pl.pallas_call(
        _nsa_kernel,
        grid=(G, B // NB, S // TQ),
        in_specs=[pl.BlockSpec((NB, TQ, RW), lambda g, b, i: (b, i, COL_NSA_Q // RW + g)),
                  pl.BlockSpec((NB, 1, ncr, LANE), lambda g, b, i: (b, g, 0, 0)),
                  pl.BlockSpec((NB, S, LANE), lambda g, b, i: (b, 0, kvb + G + g)),
                  pl.BlockSpec((NB, S, LANE), lambda g, b, i: (b, 0, kvb + 2 * G + g)),
                  pl.BlockSpec((NB, TQ, LANE), lambda g, b, i: (b, i, COL_SMALL // LANE)),
                  pl.BlockSpec((1,) + bias_c.shape[1:], lambda g, b, i: (g, 0, 0)),
                  pl.BlockSpec((1,) + bias_s.shape[1:], lambda g, b, i: (g, 0, 0, 0)),
                  pl.BlockSpec((1,) + bias_w.shape[1:], lambda g, b, i: (g, 0, 0, 0)),
                  pl.BlockSpec(xtra.shape, lambda g, b, i: (0, 0)),
                  pl.BlockSpec(ovl_t.shape, lambda g, b, i: (0, 0)),
                  pl.BlockSpec((1,) + gsel.shape[1:], lambda g, b, i: (g, 0, 0))],
        out_specs=pl.BlockSpec((NB, TQ, RW), lambda g, b, i: (b, i, g)),
        out_shape=jax.ShapeDtypeStruct((B, S, D_MODEL), BF16),
        scratch_shapes=[pltpu.VMEM((NB, S, 2 * LANE), BF16), pltpu.VMEM((NB, 2 * LANE, HW), BF16),
                        pltpu.VMEM((NB, V_ROWS, S), BF16), pltpu.VMEM((NB, V_ROWS, S), BF16),
                        pltpu.VMEM((NB, V_ROWS, ncr), BF16), pltpu.VMEM((NB, KT, HW), BF16),
                        pltpu.VMEM((NB, KT, HW), BF16)],
        compiler_params=_cparams(("parallel", "parallel", "arbitrary")),
        name="nsa_attention",
    )(proj3, kvc, proj3, proj3, proj3, bias_c, bias_s, bias_w, xtra, ovl_t, gsel)


def _rel_bucket(dist):
    n = jnp.maximum(dist, 0)
    nf = jnp.maximum(n, REL_MAX_EXACT).astype(F32)
    large = REL_MAX_EXACT + (jnp.log(nf / REL_MAX_EXACT) / math.log(REL_MAX_DIST / REL_MAX_EXACT)
                             * (REL_BUCKETS - REL_MAX_EXACT)).astype(jnp.int32)
    large = jnp.minimum(large, REL_BUCKETS - 1)
    return jnp.where(n < REL_MAX_EXACT, n, large)


def _nsa_tables(rel_table, S):
    TQ = Q_TILE
    tbl = rel_table.astype(F32).reshape(REL_BUCKETS, NSA_G, NSA_R)
    tbl = tbl - tbl[REL_BUCKETS - 1]

    def table(dist, ok):
        onehot = jax.nn.one_hot(_rel_bucket(dist), REL_BUCKETS, dtype=F32)
        b = jnp.einsum("...kqn,ngr->g...krq", onehot, tbl, precision=lax.Precision.HIGHEST)
        b = jnp.where(ok[..., :, None, :], b, NEG_INF)
        return b.reshape(b.shape[:-2] + (NSA_R * dist.shape[-1],))

    KT = K_TILE
    kk = jnp.arange(KT)[:, None]
    qq = jnp.arange(TQ)[None, :]
    dist = qq - kk - KT * (jnp.arange(TQ // KT + 1)[:, None, None] - 1)
    bias_s = table(dist, dist >= 0) * LOG2_E
    bias_s = jnp.concatenate([bias_s, jnp.zeros_like(bias_s[:, :1])], axis=1).astype(BF16)
    k128 = jnp.arange(LANE)[:, None]
    q128 = jnp.arange(LANE)[None, :]
    dist_w = q128 - k128 + LANE * (WIN_SPAN - 1 - jnp.arange(WIN_SPAN)[:, None, None])
    bias_w = table(dist_w, (dist_w >= 0) & (dist_w < WINDOW)) * LOG2_E
    bias_w = jnp.concatenate([bias_w, jnp.full_like(bias_w[:, :1], NEG_INF)], axis=1).astype(BF16)
    n_rows = S // CMP_STRIDE
    off = n_rows - TQ // CMP_STRIDE
    c_rel = jnp.arange(off + n_rows)[:, None] - off
    dist_c = qq - CMP_STRIDE * c_rel - (CMP_BLOCK - 1)
    bias_c = table(dist_c, dist_c >= 0)

    n_slc = S // SLC_BLOCK
    n_cmp = n_rows - CMP_BLOCK // CMP_STRIDE + 1
    xtra = (jnp.arange(LANE)[None, :] == (jnp.arange(S)[:, None] // SLC_BLOCK)).astype(BF16)
    cmp_start = jnp.arange(n_rows) * CMP_STRIDE
    slc_start = jnp.arange(n_slc) * SLC_BLOCK
    overlap = jnp.clip(jnp.minimum(cmp_start[:, None] + CMP_BLOCK, slc_start[None, :] + SLC_BLOCK)
                       - jnp.maximum(cmp_start[:, None], slc_start[None, :]), 0).astype(F32) / CMP_BLOCK
    overlap = jnp.where(jnp.arange(n_rows)[:, None] < n_cmp, overlap, 0.0)
    g = jnp.arange(NSA_G)[:, None, None]
    row = jnp.arange(16)[None, :, None]
    col = jnp.arange(LANE)[None, None, :]
    src = SMALL_GATE + (row // NSA_R) * NSA_HEADS + g * NSA_R + row % NSA_R
    gsel = ((col == src) & (row < 3 * NSA_R)).astype(BF16)
    return bias_c, bias_s, bias_w, xtra, overlap.T.astype(BF16), gsel


def _merge_kernel(ya_ref, yb_ref, yc_ref, g0_ref, g1_ref, g2_ref, x_ref, wb_ref, wo_ref, gn_ref, o_ref):
    m = None
    for br, (y_ref, g_ref) in enumerate(((ya_ref, g0_ref), (yb_ref, g1_ref), (yc_ref, g2_ref))):
        t = jax.nn.sigmoid(g_ref[...].astype(F32)) * jnp.dot(y_ref[...], wb_ref[br], preferred_element_type=F32)
        m = t if m is None else m + t
    z = jnp.dot(m.astype(BF16), wo_ref[...], preferred_element_type=F32)
    o_ref[...] = x_ref[...] + _rms(z, gn_ref[...])


def _merge(ya, yb, yc, proj, x2, wb, wo, gn, layer, tm=512):
    T, D = x2.shape
    row = lambda i: (i, 0)
    mcol = COL_MERGE // D
    return pl.pallas_call(
        _merge_kernel,
        grid=(T // tm,),
        in_specs=[pl.BlockSpec((tm, D), row), pl.BlockSpec((tm, D), row), pl.BlockSpec((tm, D), row),
                  pl.BlockSpec((tm, D), lambda i: (i, mcol)),
                  pl.BlockSpec((tm, D), lambda i: (i, mcol + 1)),
                  pl.BlockSpec((tm, D), lambda i: (i, mcol + 2)),
                  pl.BlockSpec((tm, D), row),
                  _const_spec(wb.shape[1:], layer), _const_spec(wo.shape[1:], layer), _const_spec((1, D))],
        out_specs=pl.BlockSpec((tm, D), row),
        out_shape=jax.ShapeDtypeStruct((T, D), F32),
        compiler_params=_cparams(("parallel",)),
        name="merge_out",
    )(ya, yb, yc, proj, proj, proj, x2, wb, wo, gn)


def _ffn_kernel(x_ref, gpre_ref, win_ref, wout_ref, gpost_ref, o_ref, *, chunk):
    x = x_ref[...]
    h = _rms(x, gpre_ref[...]).astype(BF16)
    acc = None
    for c in range(D_FF // chunk):
        gt = jnp.dot(h, win_ref[:, c * chunk:(c + 1) * chunk], preferred_element_type=F32)
        up = jnp.dot(h, win_ref[:, D_FF + c * chunk:D_FF + (c + 1) * chunk], preferred_element_type=F32)
        a = (gt * jax.nn.sigmoid(gt) * up).astype(BF16)
        t = jnp.dot(a, wout_ref[c * chunk:(c + 1) * chunk, :], preferred_element_type=F32)
        acc = t if acc is None else acc + t
    o_ref[...] = x + _rms(acc, gpost_ref[...])


def _ffn(x2, gpre, win, wout, gpost, layer, tm=512, chunk=256):
    T, D = x2.shape
    row = lambda i: (i, 0)
    return pl.pallas_call(
        functools.partial(_ffn_kernel, chunk=chunk),
        grid=(T // tm,),
        in_specs=[pl.BlockSpec((tm, D), row), _const_spec((1, D)), _const_spec(win.shape[1:], layer),
                  _const_spec(wout.shape[1:], layer), _const_spec((1, D))],
        out_specs=pl.BlockSpec((tm, D), row),
        out_shape=jax.ShapeDtypeStruct((T, D), F32),
        compiler_params=_cparams(("parallel",)),
        name="ffn",
    )(x2, gpre, win, wout, gpost)


def _in_column_sources():
    (lru_x, lru_g, nsa_q, nsa_kv, nsa_gate, gla_q, gla_k, gla_v, gla_og, gla_lr, merge) = np.split(
        np.arange(sum(IN_SIZES)), np.cumsum(IN_SIZES)[:-1])
    kv = nsa_kv.reshape(3, 2, NSA_G, NSA_HD).transpose(0, 2, 1, 3).reshape(-1)
    pad = np.full(LANE - nsa_gate.size - gla_lr.size, -1)
    src = np.concatenate([lru_x, lru_g, nsa_q, gla_v, gla_og, merge, kv, gla_q, gla_k, nsa_gate, gla_lr, pad])
    assert src.size == PROJ_WIDTH
    return src


def _permute_plan():
    src = _in_column_sources()
    plan = []
    for c in range(PROJ_WIDTH // LANE):
        runs = []
        for s in src[c * LANE:(c + 1) * LANE]:
            s = int(s)
            if runs and ((s < 0 and runs[-1][0] < 0) or (s >= 0 and runs[-1][0] >= 0 and runs[-1][0] + runs[-1][1] == s)):
                runs[-1][1] += 1
            else:
                runs.append([s, 1])
        assert all(n % 8 == 0 and (s < 0 or s % 8 == 0) for s, n in runs)
        plan.append(tuple((s, n) for s, n in runs))
    return tuple(plan)


def _permute_kernel(wt_ref, o_ref, *, plan):
    tc = wt_ref.shape[1]
    for c, runs in enumerate(plan):
        parts = [wt_ref[s:s + n, :] if s >= 0 else jnp.zeros((n, tc), F32) for s, n in runs]
        block = parts[0] if len(parts) == 1 else jnp.concatenate(parts, axis=0)
        o_ref[:, c * LANE:(c + 1) * LANE] = block.T.astype(o_ref.dtype)


def _permute_in_weights(w_in, tc=128):
    L, D, W = w_in.shape
    return pl.pallas_call(
        functools.partial(_permute_kernel, plan=_permute_plan()),
        grid=(L, D // tc),
        in_specs=[pl.BlockSpec((None, W, tc), lambda l, i: (l, 0, i))],
        out_specs=pl.BlockSpec((None, tc, PROJ_WIDTH), lambda l, i: (l, i, 0)),
        out_shape=jax.ShapeDtypeStruct((L, D, PROJ_WIDTH), BF16),
        compiler_params=_cparams(("parallel", "parallel")),
        name="permute_w_in",
    )(jnp.swapaxes(w_in, 1, 2))


def _layer(x2, B, S, tables, layer, norm_g, w_in, conv_w, conv_b, lru_w_gates, lru_b_gates, lru_lambda,
           cmp_pos, cmp_w1, cmp_w2, gla_wa2, gla_ba, gla_norm, w_branch, w_out, w_ffn_in, w_ffn_out):
    D = D_MODEL
    wg = jnp.concatenate([lru_w_gates[0], lru_w_gates[1]], axis=-1).astype(BF16)
    proj, y_a = _in_proj_lru(x2, norm_g[0][None, :], w_in, layer, conv_w, conv_b[None, :], wg, lru_b_gates,
                             lru_lambda[None, :], seq_rows=S)
    proj3 = proj.reshape(B, S, PROJ_WIDTH)

    kvc = _compress(proj3, *_compress_params(cmp_pos, cmp_w1, cmp_w2))
    y_b = _nsa(proj3, kvc, *tables)

    wa_pad = jnp.zeros((LANE, GLA_HEADS * GLA_DK), F32).at[SMALL_LR:SMALL_LR + GLA_RANK].set(gla_wa2).astype(BF16)
    y_c = _gla(proj3, wa_pad, gla_ba[None, :], gla_norm[None, :])

    x2 = _merge(y_a, y_b.reshape(B * S, D), y_c.reshape(B * S, D), proj, x2,
                w_branch, w_out, norm_g[1][None, :], layer)
    x2 = _ffn(x2, norm_g[2][None, :], w_ffn_in, w_ffn_out, norm_g[3][None, :], layer)
    return x2


def kernel(x, rel_table, norm_g, w_in, conv_w, conv_b, lru_w_gates, lru_b_gates, lru_lambda, cmp_pos, cmp_w1,
           cmp_w2, gla_wa2, gla_ba, gla_norm, w_branch, w_out, w_ffn_in, w_ffn_out):
    B, S, D = x.shape
    tables = _nsa_tables(rel_table, S)
    x2 = x.reshape(B * S, D)
    w_in = _permute_in_weights(w_in)
    w_branch, w_out, w_ffn_in, w_ffn_out = (w.astype(BF16) for w in (w_branch, w_out, w_ffn_in, w_ffn_out))
    for l in range(norm_g.shape[0]):
        x2 = _layer(x2, B, S, tables, l, norm_g[l], w_in, conv_w[l], conv_b[l], lru_w_gates[l], lru_b_gates[l],
                    lru_lambda[l], cmp_pos[l], cmp_w1[l], cmp_w2[l], gla_wa2[l], gla_ba[l], gla_norm[l],
                    w_branch, w_out, w_ffn_in, w_ffn_out)
    return x2.reshape(B, S, D)
```

```python
import functools
import math

import jax
import jax.numpy as jnp
import numpy as np
from jax import lax
from jax.experimental import pallas as pl
from jax.experimental.pallas import tpu as pltpu

F32 = jnp.float32
BF16 = jnp.bfloat16

D_MODEL = 1024
N_BRANCH = 3
EPS = 1e-6
NEG_INF = -1e30
FORCED = 1e4

LRU_BLOCKS = 8
LRU_BLOCK = D_MODEL // LRU_BLOCKS
CONV_WIDTH = 4
LRU_C = 8.0

NSA_HEADS = 16
NSA_G = 4
NSA_R = NSA_HEADS // NSA_G
NSA_HD = D_MODEL // NSA_HEADS
CMP_BLOCK = 32
CMP_STRIDE = 16
CMP_HIDDEN = 256
SLC_BLOCK = 64
SLC_TOP_N = 8
WINDOW = 256

GLA_HEADS = 4
GLA_DK = (D_MODEL // 2) // GLA_HEADS
GLA_DV = D_MODEL // GLA_HEADS
GLA_RANK = 16
GLA_TAU = 16.0
GLA_CHUNK = 32

REL_BUCKETS = 32
REL_MAX_EXACT = 16
REL_MAX_DIST = 128

D_FF = -(-8 * D_MODEL // (3 * 256)) * 256

IN_SIZES = (D_MODEL, D_MODEL, D_MODEL, 6 * NSA_G * NSA_HD, 3 * NSA_HEADS, D_MODEL // 2, D_MODEL // 2,
            D_MODEL, D_MODEL, GLA_RANK, N_BRANCH * D_MODEL)

LANE = 128
COL_LRU_X = 0
COL_LRU_G = 1024
COL_NSA_Q = 2048
COL_GLA_V = 3072
COL_GLA_OG = 4096
COL_MERGE = 5120
COL_NSA_KV = 8192
COL_GLA_Q = 9728
COL_GLA_K = 10240
COL_SMALL = 10752
PROJ_WIDTH = 10880
SMALL_GATE = 0
SMALL_LR = 3 * NSA_HEADS

Q_TILE = 256
K_TILE = 256
NSA_BATCH_ROWS = 4
WIN_SPAN = WINDOW // LANE + 1
VMEM_LIMIT = 56 * 1024 * 1024
_NT = (((1,), (1,)), ((), ()))


def _cparams(sem):
    return pltpu.CompilerParams(dimension_semantics=sem, vmem_limit_bytes=VMEM_LIMIT)


def _const_spec(shape, layer=None):
    nd = len(shape)
    if layer is None:
        return pl.BlockSpec(shape, lambda *_: (0,) * nd, pipeline_mode=pl.Buffered(1))
    return pl.BlockSpec((None,) + tuple(shape), lambda *_: (layer,) + (0,) * nd, pipeline_mode=pl.Buffered(1))


def _rms(x, g):
    return x * lax.rsqrt(jnp.mean(x * x, axis=-1, keepdims=True) + EPS) * g


def _gelu_tanh(x):
    return 0.5 * x * (1.0 + jnp.tanh(math.sqrt(2.0 / math.pi) * (x + 0.044715 * (x * x * x))))


def _sigmoid(x):
    return 0.5 * jnp.tanh(0.5 * x) + 0.5


def _sqrt_nonneg(x):
    return jnp.where(x > 0.0, x * lax.rsqrt(x), 0.0)


def _softplus(z):
    return jnp.maximum(z, 0.0) + jnp.log1p(jnp.exp(-jnp.abs(z)))


def _in_proj_kernel(x_ref, g_ref, w_ref, o_ref, h_ref):
    @pl.when(pl.program_id(1) == 0)
    def _():
        h_ref[...] = _rms(x_ref[...], g_ref[...]).astype(BF16)

    o_ref[...] = jnp.dot(h_ref[...], w_ref[...], preferred_element_type=F32).astype(o_ref.dtype)


def _in_proj(x2, g, w, layer, tm=1024, tn=2176):
    T, D = x2.shape
    N = w.shape[2]
    return pl.pallas_call(
        _in_proj_kernel,
        grid=(T // tm, N // tn),
        in_specs=[pl.BlockSpec((tm, D), lambda i, j: (i, 0)),
                  pl.BlockSpec((1, D), lambda i, j: (0, 0)),
                  pl.BlockSpec((None, D, tn), lambda i, j: (layer, 0, j))],
        out_specs=pl.BlockSpec((tm, tn), lambda i, j: (i, j)),
        out_shape=jax.ShapeDtypeStruct((T, N), BF16),
        scratch_shapes=[pltpu.VMEM((tm, D), BF16)],
        compiler_params=_cparams(("parallel", "arbitrary")),
        name="in_proj",
    )(x2, g, w)


def _lru_kernel(xa_ref, ga_ref, cw_ref, cb_ref, wg_ref, bg_ref, lam_ref, y_ref, tail_ref, h_ref, *, R):
    @pl.when(pl.program_id(1) == 0)
    def _():
        tail_ref[...] = jnp.zeros_like(tail_ref)
        h_ref[...] = jnp.zeros_like(h_ref)

    sp = _softplus(-lam_ref[...])
    groups = R // 8
    row8 = lax.broadcasted_iota(jnp.int32, (groups, 8, LRU_BLOCK), 1)
    for n in range(LRU_BLOCKS):
        sl = slice(n * LRU_BLOCK, (n + 1) * LRU_BLOCK)
        x3 = xa_ref[0, :, sl].astype(F32).reshape(groups, 8, LRU_BLOCK)
        tail = tail_ref[:, sl].reshape(1, 8, LRU_BLOCK)
        xc3 = cb_ref[:, sl] + cw_ref[CONV_WIDTH - 1:CONV_WIDTH, sl] * x3
        for s in range(1, CONV_WIDTH):
            rolled = pltpu.roll(x3, s, axis=1)
            prev = jnp.concatenate([pltpu.roll(tail, s, axis=1), rolled[:groups - 1]], axis=0)
            w = cw_ref[CONV_WIDTH - 1 - s:CONV_WIDTH - s, sl]
            xc3 = xc3 + w * jnp.where(row8 >= s, rolled, prev)
        tail_ref[:, sl] = x3[groups - 1]
        xcn = xc3.reshape(R, LRU_BLOCK)
        gz =jnp.dot(xcn.astype(BF16), wg_ref[n], preferred_element_type=F32)
        r = _sigmoid(gz[:, :LRU_BLOCK] + bg_ref[0:1, sl])
        i = _sigmoid(gz[:, LRU_BLOCK:] + bg_ref[1:2, sl])
        a = jnp.exp((-LRU_C) * r * sp[:, sl])
        u = _sqrt_nonneg(1.0 - a * a) * (i * xcn)
        a = a.reshape(R // 8, 8, LRU_BLOCK)
        u = u.reshape(R // 8, 8, LRU_BLOCK)
        for d in (1, 2, 4):
            keep = row8 >= d
            a_s = pltpu.roll(a, d, axis=1)
            u_s = pltpu.roll(u, d, axis=1)
            u = jnp.where(keep, a * u_s + u, u)
            a = jnp.where(keep, a * a_s, a)
        hprev = h_ref[:, sl]
        hs = []
        for t in range(R // 8):
            ht = a[t] * hprev + u[t]
            hprev = ht[7:8]
            hs.append(ht)
        h_ref[:, sl] = hprev
        h = jnp.concatenate(hs, axis=0)
        y_ref[0, :, sl] = (h * _gelu_tanh(ga_ref[0, :, sl].astype(F32))).astype(y_ref.dtype)


def _lru(proj3, conv_w, conv_b, wg, bg, lam, R=256):
    B, S, _ = proj3.shape
    D = D_MODEL
    return pl.pallas_call(
        functools.partial(_lru_kernel, R=R),
        grid=(B, S // R),
        in_specs=[pl.BlockSpec((1, R, D), lambda b, s: (b, s, COL_LRU_X // D)),
                  pl.BlockSpec((1, R, D), lambda b, s: (b, s, COL_LRU_G // D)),
                  _const_spec((CONV_WIDTH, D)),
                  _const_spec((1, D)),
                  _const_spec((LRU_BLOCKS, LRU_BLOCK, 2 * LRU_BLOCK)),
                  _const_spec((2, D)),
                  _const_spec((1, D))],
        out_specs=pl.BlockSpec((1, R, D), lambda b, s: (b, s, 0)),
        out_shape=jax.ShapeDtypeStruct((B, S, D), BF16),
        scratch_shapes=[pltpu.VMEM((8, D), F32), pltpu.VMEM((1, D), F32)],
        compiler_params=_cparams(("parallel", "arbitrary")),
        name="rglru",
    )(proj3, proj3, conv_w, conv_b, wg, bg, lam)


def _gla_kernel(q_ref, k_ref, v_ref, og_ref, sm_ref, wa_ref, ba_ref, gn_ref, y_ref, st_ref, *, Tb):
    nc = Tb // GLA_CHUNK

    @pl.when(pl.program_id(1) == 0)
    def _():
        st_ref[...] = jnp.zeros_like(st_ref)

    la_pre = jnp.dot(sm_ref[0], wa_ref[...], preferred_element_type=F32)
    rowc = lax.broadcasted_iota(jnp.int32, (Tb, GLA_DK), 0) & (GLA_CHUNK - 1)
    ri = lax.broadcasted_iota(jnp.int32, (Tb, Tb), 0)
    ci = lax.broadcasted_iota(jnp.int32, (Tb, Tb), 1)
    same_chunk_causal = ((ri // GLA_CHUNK) == (ci // GLA_CHUNK)) & (ci <= ri)
    lane_chunk = lax.broadcasted_iota(jnp.int32, (GLA_DK, Tb), 1) // GLA_CHUNK

    heads = range(GLA_HEADS)
    kcols = [slice(h * GLA_DK, (h + 1) * GLA_DK) for h in heads]
    vcols = [slice(h * GLA_DV, (h + 1) * GLA_DV) for h in heads]

    def chunk_log_decay(h):
        z = la_pre[:, kcols[h]] + ba_ref[:, kcols[h]]
        b = (jnp.minimum(z, 0.0) - jnp.log(1.0 + jnp.exp(-jnp.abs(z)))) * (1.0 / GLA_TAU)
        d = 1
        while d < GLA_CHUNK:
            b = b + jnp.where(rowc >= d, pltpu.roll(b, d, axis=0), 0.0)
            d *= 2
        return b

    b = [chunk_log_decay(h) for h in heads]
    b_last = [jnp.concatenate(
        [jnp.broadcast_to(b[h][c * GLA_CHUNK + GLA_CHUNK - 1:(c + 1) * GLA_CHUNK, :], (GLA_CHUNK, GLA_DK))
         for c in range(nc)], axis=0) for h in heads]
    q = [q_ref[0, :, kcols[h]].astype(F32) * (GLA_DK ** -0.5) for h in heads]
    k = [k_ref[0, :, kcols[h]].astype(F32) for h in heads]
    v = [v_ref[0, :, vcols[h]] for h in heads]
    qt = [(q[h] * jnp.exp(b[h])).astype(BF16) for h in heads]
    kt = [(k[h] * jnp.exp(-b[h])).astype(BF16) for h in heads]
    ke_t = [(k[h] * jnp.exp(b_last[h] - b[h])).T for h in heads]
    b_t = [b[h].T for h in heads]

    att = [jnp.where(same_chunk_causal, lax.dot_general(qt[h], kt[h], _NT, preferred_element_type=F32), 0.0)
           for h in heads]
    o = [jnp.dot(att[h].astype(BF16), v[h], preferred_element_type=F32) for h in heads]
    kv_all = [jnp.dot(jnp.concatenate([jnp.where(lane_chunk == c, ke_t[h], 0.0) for c in range(nc)],
                                      axis=0).astype(BF16), v[h], preferred_element_type=F32)
              for h in heads]

    state = [st_ref[h] for h in heads]
    inter = [[] for _ in heads]
    for c in range(nc):
        rs = slice(c * GLA_CHUNK, (c + 1) * GLA_CHUNK)
        last = c * GLA_CHUNK + GLA_CHUNK - 1
        for h in heads:
            inter[h].append(jnp.dot(qt[h][rs], state[h].astype(BF16), preferred_element_type=F32))
            state[h] = state[h] * jnp.exp(b_t[h][:, last:last + 1]) + kv_all[h][c * GLA_DK:(c + 1) * GLA_DK]
    for h in heads:
        st_ref[h] = state[h]
        oh = o[h] + jnp.concatenate(inter[h], axis=0)
        on = oh * lax.rsqrt(jnp.mean(oh * oh, axis=-1, keepdims=True) + EPS) * gn_ref[...]
        og = og_ref[0, :, vcols[h]].astype(F32)
        y_ref[0, :, vcols[h]] = (on * (og * jax.nn.sigmoid(og))).astype(y_ref.dtype)


def _gla(proj3, wa_pad, ba, gn, Tb=256):
    B, S, _ = proj3.shape
    HK = GLA_HEADS * GLA_DK
    D = D_MODEL
    return pl.pallas_call(
        functools.partial(_gla_kernel, Tb=Tb),
        grid=(B, S // Tb),
        in_specs=[pl.BlockSpec((1, Tb, HK), lambda b, s: (b, s, COL_GLA_Q // HK)),
                  pl.BlockSpec((1, Tb, HK), lambda b, s: (b, s, COL_GLA_K // HK)),
                  pl.BlockSpec((1, Tb, D), lambda b, s: (b, s, COL_GLA_V // D)),
                  pl.BlockSpec((1, Tb, D), lambda b, s: (b, s, COL_GLA_OG // D)),
                  pl.BlockSpec((1, Tb, LANE), lambda b, s: (b, s, COL_SMALL // LANE)),
                  _const_spec((LANE, HK)),
                  _const_spec((1, HK)),
                  _const_spec((1, GLA_DV))],
        out_specs=pl.BlockSpec((1, Tb, D), lambda b, s: (b, s, 0)),
        out_shape=jax.ShapeDtypeStruct((B, S, D), BF16),
        scratch_shapes=[pltpu.VMEM((GLA_HEADS, GLA_DK, GLA_DV), F32)],
        compiler_params=_cparams(("parallel", "arbitrary")),
        name="gla",
    )(proj3, proj3, proj3, proj3, proj3, wa_pad, ba, gn)


def _cmp_kernel(kv_ref, pos_ref, w1_ref, w2_ref, o_ref, x_ref):
    x_ref[...] = kv_ref[0].astype(F32)
    nrow = x_ref.shape[0] // CMP_STRIDE
    first = None
    second = None
    for i in range(CMP_STRIDE):
        xi = x_ref[pl.ds(i, nrow, stride=CMP_STRIDE), :]
        f = jnp.dot((xi + pos_ref[0, i]).astype(BF16), w1_ref[0, i], preferred_element_type=F32)
        s = jnp.dot((xi + pos_ref[1, i]).astype(BF16), w1_ref[1, i], preferred_element_type=F32)
        first = f if first is None else first + f
        second = s if second is None else second + s
    pre = first + pltpu.roll(second, nrow - 1, axis=0)
    hid = _gelu_tanh(pre).astype(BF16)
    o_ref[0, 0] = jnp.dot(hid, w2_ref[...], preferred_element_type=F32).astype(o_ref.dtype)


def _compress(proj3, pos, w1, w2):
    B, S, _ = proj3.shape
    nrow = S // CMP_STRIDE
    kvb = COL_NSA_KV // LANE
    return pl.pallas_call(
        _cmp_kernel,
        grid=(B, NSA_G),
        in_specs=[pl.BlockSpec((1, S, LANE), lambda b, g: (b, 0, kvb + g)),
                  _const_spec(pos.shape), _const_spec(w1.shape), _const_spec(w2.shape)],
        out_specs=pl.BlockSpec((1, 1, nrow, LANE), lambda b, g: (b, g, 0, 0)),
        out_shape=jax.ShapeDtypeStruct((B, NSA_G, nrow, LANE), BF16),
        scratch_shapes=[pltpu.VMEM((S, LANE), F32)],
        compiler_params=_cparams(("parallel", "parallel")),
        name="nsa_compress",
    )(proj3, pos, w1, w2)


def _compress_params(cmp_pos, cmp_w1, cmp_w2):
    hd = NSA_HD
    w1 = cmp_w1.reshape(2, 2, CMP_STRIDE, hd, CMP_HIDDEN)
    z = jnp.zeros_like(w1[0])
    w1 = jnp.concatenate([jnp.concatenate([w1[0], z], axis=-1),
                          jnp.concatenate([z, w1[1]], axis=-1)], axis=-2)
    pos = cmp_pos.reshape(2, 2, CMP_STRIDE, 1, hd)
    pos = jnp.concatenate([pos[0], pos[1]], axis=-1)
    z2 = jnp.zeros_like(cmp_w2[0])
    w2 = jnp.concatenate([jnp.concatenate([cmp_w2[0], z2], axis=-1),
                          jnp.concatenate([z2, cmp_w2[1]], axis=-1)], axis=0)
    return pos, w1.astype(BF16), w2.astype(BF16)


V_ROWS = NSA_HD + 16
MASK_BIG = 1e30
LOG2_E = 1.4426950408889634


def _nsa_kernel(q_ref, kvc_ref, kvs_ref, kvw_ref, gate_ref, bc_ref, bs_ref, bw_ref, xtra_ref, ovl_ref,
                gsel_ref, y_ref, a_ref, b_ref, vs_ref, vw_ref, vc_ref, lg_ref, p_ref):
    NB = q_ref.shape[0]
    rows_nb = range(NB)
    TQ = Q_TILE
    KT = K_TILE
    RT = TQ // KT
    HW = NSA_R * TQ
    qi = pl.program_id(2)
    S = kvs_ref.shape[1]
    n_cmp_rows = kvc_ref.shape[2]

    def value_rows(tile):
        t = tile.astype(F32).T
        return jnp.concatenate([t[NSA_HD:], jnp.ones((V_ROWS - NSA_HD, t.shape[1]), F32)], axis=0).astype(BF16)

    n_blk = ovl_ref.shape[0]
    k_lanes = lax.broadcasted_iota(jnp.int32, (LANE, LANE), 1) < NSA_HD

    @pl.when(qi == 0)
    def _():
        for nb in rows_nb:
            for j in range(S // LANE):
                rows = slice(j * LANE, (j + 1) * LANE)
                ks = kvs_ref[nb, rows, :]
                a_ref[nb, rows, :] = jnp.where(k_lanes, ks, xtra_ref[rows, :])
                vs_ref[nb, :, rows] = value_rows(ks)
                vw_ref[nb, :, rows] = value_rows(kvw_ref[nb, rows, :])
            vc_ref[nb] = value_rows(kvc_ref[nb, 0])
            b_ref[nb, NSA_HD + n_blk:, :] = jnp.zeros((LANE - NSA_HD - n_blk, HW), BF16)

    def tile_rows(j, n=1):
        return pl.ds(pl.multiple_of(j * KT, KT), n * KT)

    def normalised(acc):
        return acc[:NSA_HD] * (1.0 / acc[NSA_HD:NSA_HD + 1])

    def q_transposed(nb):
        qf = q_ref[nb].astype(F32) * (NSA_HD ** -0.5)
        zero_rows = jnp.zeros((LANE - NSA_HD, TQ), F32)
        heads = []
        for pr in range(NSA_R // 2):
            t = qf[:, pr * LANE:(pr + 1) * LANE].T
            heads.append(jnp.concatenate([t[:NSA_HD], zero_rows], axis=0))
            heads.append(jnp.concatenate([t[NSA_HD:], zero_rows], axis=0))
        return jnp.concatenate(heads, axis=1)

    qt32 = [q_transposed(nb) for nb in rows_nb]
    qt = [x.astype(BF16) for x in qt32]
    qt32_l2 = [x * LOG2_E for x in qt32]
    qt_l2 = [x.astype(BF16) for x in qt32_l2]

    parts = TQ // LANE
    o_win_parts = [[None] * parts for _ in rows_nb]
    for h in range(parts):
        own = parts * qi + h
        skipped = jnp.maximum(WIN_SPAN - 1 - own, 0)
        rows_w = pl.ds(pl.multiple_of(jnp.maximum(own - (WIN_SPAN - 1), 0) * LANE, LANE), WIN_SPAN * LANE)
        bias_w = jnp.concatenate([bw_ref[0, jnp.minimum(t + skipped, WIN_SPAN)] for t in range(WIN_SPAN)], axis=0)
        q_part = [jnp.concatenate([qt_l2[nb][:, r * TQ + h * LANE:r * TQ + (h + 1) * LANE] for r in range(NSA_R)],
                                  axis=1) for nb in rows_nb]
        lg_w = [jnp.dot(kvw_ref[nb, rows_w, :], q_part[nb], preferred_element_type=F32).astype(BF16) + bias_w
                for nb in rows_nb]
        p_w = [jnp.exp2(lg - jnp.max(lg, axis=0, keepdims=True)) for lg in lg_w]
        for nb in rows_nb:
            o_win_parts[nb][h] = normalised(jnp.dot(vw_ref[nb, :, rows_w], p_w[nb], preferred_element_type=F32))
    o_win = [jnp.concatenate([o_win_parts[nb][h][:, r * LANE:(r + 1) * LANE]
                              for r in range(NSA_R) for h in range(parts)], axis=1) for nb in rows_nb]

    cq = TQ // CMP_STRIDE
    start_c = pl.multiple_of(bc_ref.shape[1] - n_cmp_rows - cq * qi, 8)
    bias_c = bc_ref[0, pl.ds(start_c, n_cmp_rows), :]
    valid_c = bias_c > 0.5 * NEG_INF
    lg_c = [jnp.dot(kvc_ref[nb, 0], qt[nb], preferred_element_type=F32) + bias_c for nb in rows_nb]
    e_c = [jnp.exp(lg - jnp.max(lg, axis=0, keepdims=True)) for lg in lg_c]
    p_c = [jnp.where(valid_c, e * (1.0 / jnp.sum(e, axis=0, keepdims=True)), 0.0) for e in e_c]
    o_cmp = [jnp.dot(vc_ref[nb], p_c[nb].astype(BF16), preferred_element_type=F32)[:NSA_HD] for nb in rows_nb]

    ovl = ovl_ref[...]
    ns = ovl.shape[0]
    blk = lax.broadcasted_iota(jnp.int32, (ns, TQ), 0)
    cur = (qi * TQ + lax.broadcasted_iota(jnp.int32, (ns, TQ), 1)) // SLC_BLOCK
    forced = (blk == 0) | (blk == cur) | (blk == cur - 1)
    allowed = blk <= cur

    def importance(p):
        psum = p[:, 0:TQ]
        for r in range(1, NSA_R):
            psum = psum + p[:, r * TQ:(r + 1) * TQ]
        p_hi = psum.astype(BF16)
        p_lo = (psum - p_hi.astype(F32)).astype(BF16)
        imp = jnp.dot(ovl, p_hi, preferred_element_type=F32) + jnp.dot(ovl, p_lo, preferred_element_type=F32)
        return jnp.where(forced, FORCED, jnp.where(allowed, imp, -FORCED))

    score = [importance(p) for p in p_c]
    groups = ns // 8
    blk8 = lax.broadcasted_iota(jnp.int32, (8, TQ), 0)
    for nb in rows_nb:
        sc = [score[nb][8 * v:8 * v + 8] for v in range(groups)]
        rank = [jnp.zeros((8, TQ), F32) for _ in range(groups)]
        for j in range(ns):
            sj = score[nb][j:j + 1, :]
            for v in range(groups):
                if j < 8 * v:
                    beats = sj >= sc[v]
                elif j >= 8 * v + 8:
                    beats = sj > sc[v]
                else:
                    tie = jnp.where(blk8 > j - 8 * v, 1.0, 0.0)
                    rank[v] = rank[v] + jnp.where(sj > sc[v], 1.0, jnp.where(sj == sc[v], tie, 0.0))
                    continue
                rank[v] = rank[v] + jnp.where(beats, 1.0, 0.0)
        neg_sel = jnp.where(jnp.concatenate(rank, axis=0) < float(SLC_TOP_N), 0.0, -MASK_BIG).astype(BF16)
        b_ref[nb, 0:NSA_HD] = qt_l2[nb][:NSA_HD]
        b_ref[nb, NSA_HD:NSA_HD + ns] = jnp.concatenate([neg_sel] * NSA_R, axis=1)

    def logits(nb, j):
        return jnp.dot(a_ref[nb, tile_rows(j), :], b_ref[nb], preferred_element_type=F32).astype(BF16)

    for nb in rows_nb:
        lg_ref[nb] = logits(nb, 0)
        p_ref[nb] = jnp.zeros(p_ref.shape[1:], BF16)

    last = RT * qi + RT - 1

    def trip(j, carry):
        lg_next = [logits(nb, jnp.minimum(j + 1, last)) for nb in rows_nb]
        prev_rows = tile_rows(jnp.maximum(j - 1, 0))
        bias = bs_ref[0, jnp.where(j >= RT * qi - 1, j - RT * qi + 1, RT + 1)]
        out = []
        for nb in rows_nb:
            m, acc, alpha = carry[nb]
            acc = acc * alpha + jnp.dot(vs_ref[nb, :, prev_rows], p_ref[nb], preferred_element_type=F32)
            lg = lg_ref[nb] + bias
            m_new = jnp.maximum(m, jnp.max(lg, axis=0, keepdims=True).astype(F32))
            p_ref[nb] = jnp.exp2(lg - m_new.astype(BF16))
            out.append((m_new, acc, jnp.exp2(m - m_new)))
        for nb in rows_nb:
            lg_ref[nb] = lg_next[nb]
        return tuple(out)

    init = (jnp.full((1, HW), NEG_INF, F32), jnp.zeros((V_ROWS, HW), F32), jnp.ones((1, HW), F32))
    carry = lax.fori_loop(0, last + 1, trip, tuple(init for _ in rows_nb))
    o_slc = [normalised(carry[nb][1] * carry[nb][2]
                        + jnp.dot(vs_ref[nb, :, tile_rows(last)], p_ref[nb], preferred_element_type=F32))
             for nb in rows_nb]

    for nb in rows_nb:
        gs = jax.nn.sigmoid(lax.dot_general(gsel_ref[0], gate_ref[nb], _NT, preferred_element_type=F32))

        def gate_row(br):
            return jnp.concatenate([gs[br * NSA_R + r:br * NSA_R + r + 1] for r in range(NSA_R)], axis=1)

        o = gate_row(0) * o_cmp[nb] + gate_row(1) * o_slc[nb] + gate_row(2) * o_win[nb]
        o = jnp.concatenate([o[:, r * TQ:(r + 1) * TQ] for r in range(NSA_R)], axis=0)
        y_ref[nb] = o.T.astype(y_ref.dtype)


def _nsa(proj3, kvc, bias_c, bias_s, bias_w, xtra, ovl_t, gsel):
    B, S, _ = proj3.shape
    TQ = Q_TILE
    G = NSA_G
    RW = NSA_R * NSA_HD
    HW = NSA_R * TQ
    kvb = COL_NSA_KV // LANE
    ncr = kvc.shape[2]
    KT = K_TILE
    NB = NSA_BATCH_ROWS
    assert S % TQ == 0 and TQ % KT == 0 and S >= TQ + KT and WINDOW == KT and ncr % LANE == 0 and B % NB == 0
    return pl.pallas_call(
        _nsa_kernel,
        grid=(G, B // NB, S // TQ),
        in_specs=[pl.BlockSpec((NB, TQ, RW), lambda g, b, i: (b, i, COL_NSA_Q // RW + g)),
                  pl.BlockSpec((NB, 1, ncr, LANE), lambda g, b, i: (b, g, 0, 0)),
                  pl.BlockSpec((NB, S, LANE), lambda g, b, i: (b, 0, kvb + G + g)),
                  pl.BlockSpec((NB, S, LANE), lambda g, b, i: (b, 0, kvb + 2 * G + g)),
                  pl.BlockSpec((NB, TQ, LANE), lambda g, b, i: (b, i, COL_SMALL // LANE)),
                  pl.BlockSpec((1,) + bias_c.shape[1:], lambda g, b, i: (g, 0, 0)),
                  pl.BlockSpec((1,) + bias_s.shape[1:], lambda g, b, i: (g, 0, 0, 0)),
                  pl.BlockSpec((1,) + bias_w.shape[1:], lambda g, b, i: (g, 0, 0, 0)),
                  pl.BlockSpec(xtra.shape, lambda g, b, i: (0, 0)),
                  pl.BlockSpec(ovl_t.shape, lambda g, b, i: (0, 0)),
                  pl.BlockSpec((1,) + gsel.shape[1:], lambda g, b, i: (g, 0, 0))],
        out_specs=pl.BlockSpec((NB, TQ, RW), lambda g, b, i: (b, i, g)),
        out_shape=jax.ShapeDtypeStruct((B, S, D_MODEL), BF16),
        scratch_shapes=[pltpu.VMEM((NB, S, LANE), BF16), pltpu.VMEM((NB, LANE, HW), BF16),
                        pltpu.VMEM((NB, V_ROWS, S), BF16), pltpu.VMEM((NB, V_ROWS, S), BF16),
                        pltpu.VMEM((NB, V_ROWS, ncr), BF16), pltpu.VMEM((NB, KT, HW), BF16),
                        pltpu.VMEM((NB, KT, HW), BF16)],
        compiler_params=_cparams(("parallel", "parallel", "arbitrary")),
        name="nsa_attention",
    )(proj3, kvc, proj3, proj3, proj3, bias_c, bias_s, bias_w, xtra, ovl_t, gsel)


def _rel_bucket(dist):
    n = jnp.maximum(dist, 0)
    nf = jnp.maximum(n, REL_MAX_EXACT).astype(F32)
    large = REL_MAX_EXACT + (jnp.log(nf / REL_MAX_EXACT) / math.log(REL_MAX_DIST / REL_MAX_EXACT)
                             * (REL_BUCKETS - REL_MAX_EXACT)).astype(jnp.int32)
    large = jnp.minimum(large, REL_BUCKETS - 1)
    return jnp.where(n < REL_MAX_EXACT, n, large)


def _nsa_tables(rel_table, S):
    TQ = Q_TILE
    tbl = rel_table.astype(F32).reshape(REL_BUCKETS, NSA_G, NSA_R)
    tbl = tbl - tbl[REL_BUCKETS - 1]

    def table(dist, ok):
        onehot = jax.nn.one_hot(_rel_bucket(dist), REL_BUCKETS, dtype=F32)
        b = jnp.einsum("...kqn,ngr->g...krq", onehot, tbl, precision=lax.Precision.HIGHEST)
        b = jnp.where(ok[..., :, None, :], b, NEG_INF)
        return b.reshape(b.shape[:-2] + (NSA_R * dist.shape[-1],))

    KT = K_TILE
    kk = jnp.arange(KT)[:, None]
    qq = jnp.arange(TQ)[None, :]
    dist = qq - kk - KT * (jnp.arange(TQ // KT + 1)[:, None, None] - 1)
    bias_s = table(dist, dist >= 0) * LOG2_E
    bias_s = jnp.concatenate([bias_s, jnp.zeros_like(bias_s[:, :1])], axis=1).astype(BF16)
    k128 = jnp.arange(LANE)[:, None]
    q128 = jnp.arange(LANE)[None, :]
    dist_w = q128 - k128 + LANE * (WIN_SPAN - 1 - jnp.arange(WIN_SPAN)[:, None, None])
    bias_w = table(dist_w, (dist_w >= 0) & (dist_w < WINDOW)) * LOG2_E
    bias_w = jnp.concatenate([bias_w, jnp.full_like(bias_w[:, :1], NEG_INF)], axis=1).astype(BF16)
    n_rows = S // CMP_STRIDE
    off = n_rows - TQ // CMP_STRIDE
    c_rel = jnp.arange(off + n_rows)[:, None] - off
    dist_c = qq - CMP_STRIDE * c_rel - (CMP_BLOCK - 1)
    bias_c = table(dist_c, dist_c >= 0)

    n_slc = S // SLC_BLOCK
    n_cmp = n_rows - CMP_BLOCK // CMP_STRIDE + 1
    xtra = (jnp.arange(LANE)[None, :] - NSA_HD == (jnp.arange(S)[:, None] // SLC_BLOCK)).astype(BF16)
    cmp_start = jnp.arange(n_rows) * CMP_STRIDE
    slc_start = jnp.arange(n_slc) * SLC_BLOCK
    overlap = jnp.clip(jnp.minimum(cmp_start[:, None] + CMP_BLOCK, slc_start[None, :] + SLC_BLOCK)
                       - jnp.maximum(cmp_start[:, None], slc_start[None, :]), 0).astype(F32) / CMP_BLOCK
    overlap = jnp.where(jnp.arange(n_rows)[:, None] < n_cmp, overlap, 0.0)
    g = jnp.arange(NSA_G)[:, None, None]
    row = jnp.arange(16)[None, :, None]
    col = jnp.arange(LANE)[None, None, :]
    src = SMALL_GATE + (row // NSA_R) * NSA_HEADS + g * NSA_R + row % NSA_R
    gsel = ((col == src) & (row < 3 * NSA_R)).astype(BF16)
    return bias_c, bias_s, bias_w, xtra, overlap.T.astype(BF16), gsel


def _merge_kernel(ya_ref, yb_ref, yc_ref, g0_ref, g1_ref, g2_ref, x_ref, wb_ref, wo_ref, gn_ref, o_ref):
    m = None
    for br, (y_ref, g_ref) in enumerate(((ya_ref, g0_ref), (yb_ref, g1_ref), (yc_ref, g2_ref))):
        t = jax.nn.sigmoid(g_ref[...].astype(F32)) * jnp.dot(y_ref[...], wb_ref[br], preferred_element_type=F32)
        m = t if m is None else m + t
    z = jnp.dot(m.astype(BF16), wo_ref[...], preferred_element_type=F32)
    o_ref[...] = x_ref[...] + _rms(z, gn_ref[...])


def _merge(ya, yb, yc, proj, x2, wb, wo, gn, layer, tm=512):
    T, D = x2.shape
    row = lambda i: (i, 0)
    mcol = COL_MERGE // D
    return pl.pallas_call(
        _merge_kernel,
        grid=(T // tm,),
        in_specs=[pl.BlockSpec((tm, D), row), pl.BlockSpec((tm, D), row), pl.BlockSpec((tm, D), row),
                  pl.BlockSpec((tm, D), lambda i: (i, mcol)),
                  pl.BlockSpec((tm, D), lambda i: (i, mcol + 1)),
                  pl.BlockSpec((tm, D), lambda i: (i, mcol + 2)),
                  pl.BlockSpec((tm, D), row),
                  _const_spec(wb.shape[1:], layer), _const_spec(wo.shape[1:], layer), _const_spec((1, D))],
        out_specs=pl.BlockSpec((tm, D), row),
        out_shape=jax.ShapeDtypeStruct((T, D), F32),
        compiler_params=_cparams(("parallel",)),
        name="merge_out",
    )(ya, yb, yc, proj, proj, proj, x2, wb, wo, gn)


def _ffn_kernel(x_ref, gpre_ref, win_ref, wout_ref, gpost_ref, o_ref, *, chunk):
    x = x_ref[...]
    h = _rms(x, gpre_ref[...]).astype(BF16)
    acc = None
    for c in range(D_FF // chunk):
        gt = jnp.dot(h, win_ref[:, c * chunk:(c + 1) * chunk], preferred_element_type=F32)
        up = jnp.dot(h, win_ref[:, D_FF + c * chunk:D_FF + (c + 1) * chunk], preferred_element_type=F32)
        a = (gt * jax.nn.sigmoid(gt) * up).astype(BF16)
        t = jnp.dot(a, wout_ref[c * chunk:(c + 1) * chunk, :], preferred_element_type=F32)
        acc = t if acc is None else acc + t
    o_ref[...] = x + _rms(acc, gpost_ref[...])


def _ffn(x2, gpre, win, wout, gpost, layer, tm=512, chunk=256):
    T, D = x2.shape
    row = lambda i: (i, 0)
    return pl.pallas_call(
        functools.partial(_ffn_kernel, chunk=chunk),
        grid=(T // tm,),
        in_specs=[pl.BlockSpec((tm, D), row), _const_spec((1, D)), _const_spec(win.shape[1:], layer),
                  _const_spec(wout.shape[1:], layer), _const_spec((1, D))],
        out_specs=pl.BlockSpec((tm, D), row),
        out_shape=jax.ShapeDtypeStruct((T, D), F32),
        compiler_params=_cparams(("parallel",)),
        name="ffn",
    )(x2, gpre, win, wout, gpost)


def _in_column_sources():
    (lru_x, lru_g, nsa_q, nsa_kv, nsa_gate, gla_q, gla_k, gla_v, gla_og, gla_lr, merge) = np.split(
        np.arange(sum(IN_SIZES)), np.cumsum(IN_SIZES)[:-1])
    kv = nsa_kv.reshape(3, 2, NSA_G, NSA_HD).transpose(0, 2, 1, 3).reshape(-1)
    pad = np.full(LANE - nsa_gate.size - gla_lr.size, -1)
    src = np.concatenate([lru_x, lru_g, nsa_q, gla_v, gla_og, merge, kv, gla_q, gla_k, nsa_gate, gla_lr, pad])
    assert src.size == PROJ_WIDTH
    return src


def _permute_plan():
    src = _in_column_sources()
    plan = []
    for c in range(PROJ_WIDTH // LANE):
        runs = []
        for s in src[c * LANE:(c + 1) * LANE]:
            s = int(s)
            if runs and ((s < 0 and runs[-1][0] < 0) or (s >= 0 and runs[-1][0] >= 0 and runs[-1][0] + runs[-1][1] == s)):
                runs[-1][1] += 1
            else:
                runs.append([s, 1])
        assert all(n % 8 == 0 and (s < 0 or s % 8 == 0) for s, n in runs)
        plan.append(tuple((s, n) for s, n in runs))
    return tuple(plan)


def _permute_kernel(wt_ref, o_ref, *, plan):
    tc = wt_ref.shape[1]
    for c, runs in enumerate(plan):
        parts = [wt_ref[s:s + n, :] if s >= 0 else jnp.zeros((n, tc), F32) for s, n in runs]
        block = parts[0] if len(parts) == 1 else jnp.concatenate(parts, axis=0)
        o_ref[:, c * LANE:(c + 1) * LANE] = block.T.astype(o_ref.dtype)


def _permute_in_weights(w_in, tc=128):
    L, D, W = w_in.shape
    return pl.pallas_call(
        functools.partial(_permute_kernel, plan=_permute_plan()),
        grid=(L, D // tc),
        in_specs=[pl.BlockSpec((None, W, tc), lambda l, i: (l, 0, i))],
        out_specs=pl.BlockSpec((None, tc, PROJ_WIDTH), lambda l, i: (l, i, 0)),
        out_shape=jax.ShapeDtypeStruct((L, D, PROJ_WIDTH), BF16),
        compiler_params=_cparams(("parallel", "parallel")),
        name="permute_w_in",
    )(jnp.swapaxes(w_in, 1, 2))


def _layer(x2, B, S, tables, layer, norm_g, w_in, conv_w, conv_b, lru_w_gates, lru_b_gates, lru_lambda,
           cmp_pos, cmp_w1, cmp_w2, gla_wa2, gla_ba, gla_norm, w_branch, w_out, w_ffn_in, w_ffn_out):
    D = D_MODEL
    proj = _in_proj(x2, norm_g[0][None, :], w_in, layer)
    proj3 = proj.reshape(B, S, PROJ_WIDTH)

    wg = jnp.concatenate([lru_w_gates[0], lru_w_gates[1]], axis=-1).astype(BF16)
    y_a = _lru(proj3, conv_w, conv_b[None, :], wg, lru_b_gates, lru_lambda[None, :])

    kvc = _compress(proj3, *_compress_params(cmp_pos, cmp_w1, cmp_w2))
    y_b = _nsa(proj3, kvc, *tables)

    wa_pad = jnp.zeros((LANE, GLA_HEADS * GLA_DK), F32).at[SMALL_LR:SMALL_LR + GLA_RANK].set(gla_wa2).astype(BF16)
    y_c = _gla(proj3, wa_pad, gla_ba[None, :], gla_norm[None, :])

    x2 = _merge(y_a.reshape(B * S, D), y_b.reshape(B * S, D), y_c.reshape(B * S, D), proj, x2,
                w_branch, w_out, norm_g[1][None, :], layer)
    x2 = _ffn(x2, norm_g[2][None, :], w_ffn_in, w_ffn_out, norm_g[3][None, :], layer)
    return x2


def kernel(x, rel_table, norm_g, w_in, conv_w, conv_b, lru_w_gates, lru_b_gates, lru_lambda, cmp_pos, cmp_w1,
           cmp_w2, gla_wa2, gla_ba, gla_norm, w_branch, w_out, w_ffn_in, w_ffn_out):
    B, S, D = x.shape
    tables = _nsa_tables(rel_table, S)
    x2 = x.reshape(B * S, D)
    w_in = _permute_in_weights(w_in)
    w_branch, w_out, w_ffn_in, w_ffn_out = (w.astype(BF16) for w in (w_branch, w_out, w_ffn_in, w_ffn_out))
    for l in range(norm_g.shape[0]):
        x2 = _layer(x2, B, S, tables, l, norm_g[l], w_in, conv_w[l], conv_b[l], lru_w_gates[l], lru_b_gates[l],
                    lru_lambda[l], cmp_pos[l], cmp_w1[l], cmp_w2[l], gla_wa2[l], gla_ba[l], gla_norm[l],
                    w_branch, w_out, w_ffn_in, w_ffn_out)
    return x2.reshape(B, S, D)
```

```python
import functools
import math

import jax
import jax.numpy as jnp
import numpy as np
from jax import lax
from jax.experimental import pallas as pl
from jax.experimental.pallas import tpu as pltpu

F32 = jnp.float32
BF16 = jnp.bfloat16

D_MODEL = 1024
N_BRANCH = 3
EPS = 1e-6
NEG_INF = -1e30
FORCED = 1e4

LRU_BLOCKS = 8
LRU_BLOCK = D_MODEL // LRU_BLOCKS
CONV_WIDTH = 4
LRU_C = 8.0

NSA_HEADS = 16
NSA_G = 4
NSA_R = NSA_HEADS // NSA_G
NSA_HD = D_MODEL // NSA_HEADS
CMP_BLOCK = 32
CMP_STRIDE = 16
CMP_HIDDEN = 256
SLC_BLOCK = 64
SLC_TOP_N = 8
WINDOW = 256

GLA_HEADS = 4
GLA_DK = (D_MODEL // 2) // GLA_HEADS
GLA_DV = D_MODEL // GLA_HEADS
GLA_RANK = 16
GLA_TAU = 16.0
GLA_CHUNK = 32

REL_BUCKETS = 32
REL_MAX_EXACT = 16
REL_MAX_DIST = 128

D_FF = -(-8 * D_MODEL // (3 * 256)) * 256

IN_SIZES = (D_MODEL, D_MODEL, D_MODEL, 6 * NSA_G * NSA_HD, 3 * NSA_HEADS, D_MODEL // 2, D_MODEL // 2,
            D_MODEL, D_MODEL, GLA_RANK, N_BRANCH * D_MODEL)

LANE = 128
COL_LRU_X = 0
COL_LRU_G = 1024
COL_NSA_Q = 2048
COL_GLA_V = 3072
COL_GLA_OG = 4096
COL_MERGE = 5120
COL_NSA_KV = 8192
COL_GLA_Q = 9728
COL_GLA_K = 10240
COL_SMALL = 10752
PROJ_WIDTH = 10880
SMALL_GATE = 0
SMALL_LR = 3 * NSA_HEADS

Q_TILE = 256
K_TILE = 256
NSA_BATCH_ROWS = 4
WIN_SPAN = WINDOW // LANE + 1
VMEM_LIMIT = 56 * 1024 * 1024
_NT = (((1,), (1,)), ((), ()))


def _cparams(sem):
    return pltpu.CompilerParams(dimension_semantics=sem, vmem_limit_bytes=VMEM_LIMIT)


def _const_spec(shape, layer=None):
    nd = len(shape)
    if layer is None:
        return pl.BlockSpec(shape, lambda *_: (0,) * nd, pipeline_mode=pl.Buffered(1))
    return pl.BlockSpec((None,) + tuple(shape), lambda *_: (layer,) + (0,) * nd, pipeline_mode=pl.Buffered(1))


def _rms(x, g):
    return x * lax.rsqrt(jnp.mean(x * x, axis=-1, keepdims=True) + EPS) * g


def _gelu_tanh(x):
    return 0.5 * x * (1.0 + jnp.tanh(math.sqrt(2.0 / math.pi) * (x + 0.044715 * (x * x * x))))


def _sigmoid(x):
    return 0.5 * jnp.tanh(0.5 * x) + 0.5


def _sqrt_nonneg(x):
    return jnp.where(x > 0.0, x * lax.rsqrt(x), 0.0)


def _softplus(z):
    return jnp.maximum(z, 0.0) + jnp.log1p(jnp.exp(-jnp.abs(z)))


def _in_proj_kernel(x_ref, g_ref, w_ref, o_ref, h_ref):
    @pl.when(pl.program_id(1) == 0)
    def _():
        h_ref[...] = _rms(x_ref[...], g_ref[...]).astype(BF16)

    o_ref[...] = jnp.dot(h_ref[...], w_ref[...], preferred_element_type=F32).astype(o_ref.dtype)


def _in_proj(x2, g, w, layer, tm=1024, tn=2176):
    T, D = x2.shape
    N = w.shape[2]
    return pl.pallas_call(
        _in_proj_kernel,
        grid=(T // tm, N // tn),
        in_specs=[pl.BlockSpec((tm, D), lambda i, j: (i, 0)),
                  pl.BlockSpec((1, D), lambda i, j: (0, 0)),
                  pl.BlockSpec((None, D, tn), lambda i, j: (layer, 0, j))],
        out_specs=pl.BlockSpec((tm, tn), lambda i, j: (i, j)),
        out_shape=jax.ShapeDtypeStruct((T, N), BF16),
        scratch_shapes=[pltpu.VMEM((tm, D), BF16)],
        compiler_params=_cparams(("parallel", "arbitrary")),
        name="in_proj",
    )(x2, g, w)


def _lru_kernel(xa_ref, ga_ref, cw_ref, cb_ref, wg_ref, bg_ref, lam_ref, y_ref, tail_ref, h_ref, *, R):
    @pl.when(pl.program_id(1) == 0)
    def _():
        tail_ref[...] = jnp.zeros_like(tail_ref)
        h_ref[...] = jnp.zeros_like(h_ref)

    sp = _softplus(-lam_ref[...])
    groups = R // 8
    row8 = lax.broadcasted_iota(jnp.int32, (groups, 8, LRU_BLOCK), 1)
    for n in range(LRU_BLOCKS):
        sl = slice(n * LRU_BLOCK, (n + 1) * LRU_BLOCK)
        x3 = xa_ref[0, :, sl].astype(F32).reshape(groups, 8, LRU_BLOCK)
        tail = tail_ref[:, sl].reshape(1, 8, LRU_BLOCK)
        xc3 = cb_ref[:, sl] + cw_ref[CONV_WIDTH - 1:CONV_WIDTH, sl] * x3
        for s in range(1, CONV_WIDTH):
            rolled = pltpu.roll(x3, s, axis=1)
            prev = jnp.concatenate([pltpu.roll(tail, s, axis=1), rolled[:groups - 1]], axis=0)
            w = cw_ref[CONV_WIDTH - 1 - s:CONV_WIDTH - s, sl]
            xc3 = xc3 + w * jnp.where(row8 >= s, rolled, prev)
        tail_ref[:, sl] = x3[groups - 1]
        xcn = xc3.reshape(R, LRU_BLOCK)
        gz =jnp.dot(xcn.astype(BF16), wg_ref[n], preferred_element_type=F32)
        r = _sigmoid(gz[:, :LRU_BLOCK] + bg_ref[0:1, sl])
        i = _sigmoid(gz[:, LRU_BLOCK:] + bg_ref[1:2, sl])
        a = jnp.exp((-LRU_C) * r * sp[:, sl])
        u = _sqrt_nonneg(1.0 - a * a) * (i * xcn)
        a = a.reshape(R // 8, 8, LRU_BLOCK)
        u = u.reshape(R // 8, 8, LRU_BLOCK)
        for d in (1, 2, 4):
            keep = row8 >= d
            a_s = pltpu.roll(a, d, axis=1)
            u_s = pltpu.roll(u, d, axis=1)
            u = jnp.where(keep, a * u_s + u, u)
            a = jnp.where(keep, a * a_s, a)
        hprev = h_ref[:, sl]
        hs = []
        for t in range(R // 8):
            ht = a[t] * hprev + u[t]
            hprev = ht[7:8]
            hs.append(ht)
        h_ref[:, sl] = hprev
        h = jnp.concatenate(hs, axis=0)
        y_ref[0, :, sl] = (h * _gelu_tanh(ga_ref[0, :, sl].astype(F32))).astype(y_ref.dtype)


def _lru(proj3, conv_w, conv_b, wg, bg, lam, R=256):
    B, S, _ = proj3.shape
    D = D_MODEL
    return pl.pallas_call(
        functools.partial(_lru_kernel, R=R),
        grid=(B, S // R),
        in_specs=[pl.BlockSpec((1, R, D), lambda b, s: (b, s, COL_LRU_X // D)),
                  pl.BlockSpec((1, R, D), lambda b, s: (b, s, COL_LRU_G // D)),
                  _const_spec((CONV_WIDTH, D)),
                  _const_spec((1, D)),
                  _const_spec((LRU_BLOCKS, LRU_BLOCK, 2 * LRU_BLOCK)),
                  _const_spec((2, D)),
                  _const_spec((1, D))],
        out_specs=pl.BlockSpec((1, R, D), lambda b, s: (b, s, 0)),
        out_shape=jax.ShapeDtypeStruct((B, S, D), BF16),
        scratch_shapes=[pltpu.VMEM((8, D), F32), pltpu.VMEM((1, D), F32)],
        compiler_params=_cparams(("parallel", "arbitrary")),
        name="rglru",
    )(proj3, proj3, conv_w, conv_b, wg, bg, lam)


def _gla_kernel(q_ref, k_ref, v_ref, og_ref, sm_ref, wa_ref, ba_ref, gn_ref, y_ref, st_ref, *, Tb):
    nc = Tb // GLA_CHUNK

    @pl.when(pl.program_id(1) == 0)
    def _():
        st_ref[...] = jnp.zeros_like(st_ref)

    la_pre = jnp.dot(sm_ref[0], wa_ref[...], preferred_element_type=F32)
    rowc = lax.broadcasted_iota(jnp.int32, (Tb, GLA_DK), 0) & (GLA_CHUNK - 1)
    ri = lax.broadcasted_iota(jnp.int32, (Tb, Tb), 0)
    ci = lax.broadcasted_iota(jnp.int32, (Tb, Tb), 1)
    same_chunk_causal = ((ri // GLA_CHUNK) == (ci // GLA_CHUNK)) & (ci <= ri)
    lane_chunk = lax.broadcasted_iota(jnp.int32, (GLA_DK, Tb), 1) // GLA_CHUNK

    heads = range(GLA_HEADS)
    kcols = [slice(h * GLA_DK, (h + 1) * GLA_DK) for h in heads]
    vcols = [slice(h * GLA_DV, (h + 1) * GLA_DV) for h in heads]

    def chunk_log_decay(h):
        z = la_pre[:, kcols[h]] + ba_ref[:, kcols[h]]
        b = (jnp.minimum(z, 0.0) - jnp.log(1.0 + jnp.exp(-jnp.abs(z)))) * (1.0 / GLA_TAU)
        d = 1
        while d < GLA_CHUNK:
            b = b + jnp.where(rowc >= d, pltpu.roll(b, d, axis=0), 0.0)
            d *= 2
        return b

    b = [chunk_log_decay(h) for h in heads]
    b_last = [jnp.concatenate(
        [jnp.broadcast_to(b[h][c * GLA_CHUNK + GLA_CHUNK - 1:(c + 1) * GLA_CHUNK, :], (GLA_CHUNK, GLA_DK))
         for c in range(nc)], axis=0) for h in heads]
    q = [q_ref[0, :, kcols[h]].astype(F32) * (GLA_DK ** -0.5) for h in heads]
    k = [k_ref[0, :, kcols[h]].astype(F32) for h in heads]
    v = [v_ref[0, :, vcols[h]] for h in heads]
    qt = [(q[h] * jnp.exp(b[h])).astype(BF16) for h in heads]
    kt = [(k[h] * jnp.exp(-b[h])).astype(BF16) for h in heads]
    ke_t = [(k[h] * jnp.exp(b_last[h] - b[h])).T for h in heads]
    b_t = [b[h].T for h in heads]

    att = [jnp.where(same_chunk_causal, lax.dot_general(qt[h], kt[h], _NT, preferred_element_type=F32), 0.0)
           for h in heads]
    o = [jnp.dot(att[h].astype(BF16), v[h], preferred_element_type=F32) for h in heads]
    kv_all = [jnp.dot(jnp.concatenate([jnp.where(lane_chunk == c, ke_t[h], 0.0) for c in range(nc)],
                                      axis=0).astype(BF16), v[h], preferred_element_type=F32)
              for h in heads]

    state = [st_ref[h] for h in heads]
    inter = [[] for _ in heads]
    for c in range(nc):
        rs = slice(c * GLA_CHUNK, (c + 1) * GLA_CHUNK)
        last = c * GLA_CHUNK + GLA_CHUNK - 1
        for h in heads:
            inter[h].append(jnp.dot(qt[h][rs], state[h].astype(BF16), preferred_element_type=F32))
            state[h] = state[h] * jnp.exp(b_t[h][:, last:last + 1]) + kv_all[h][c * GLA_DK:(c + 1) * GLA_DK]
    for h in heads:
        st_ref[h] = state[h]
        oh = o[h] + jnp.concatenate(inter[h], axis=0)
        on = oh * lax.rsqrt(jnp.mean(oh * oh, axis=-1, keepdims=True) + EPS) * gn_ref[...]
        og = og_ref[0, :, vcols[h]].astype(F32)
        y_ref[0, :, vcols[h]] = (on * (og * jax.nn.sigmoid(og))).astype(y_ref.dtype)


def _gla(proj3, wa_pad, ba, gn, Tb=256):
    B, S, _ = proj3.shape
    HK = GLA_HEADS * GLA_DK
    D = D_MODEL
    return pl.pallas_call(
        functools.partial(_gla_kernel, Tb=Tb),
        grid=(B, S // Tb),
        in_specs=[pl.BlockSpec((1, Tb, HK), lambda b, s: (b, s, COL_GLA_Q // HK)),
                  pl.BlockSpec((1, Tb, HK), lambda b, s: (b, s, COL_GLA_K // HK)),
                  pl.BlockSpec((1, Tb, D), lambda b, s: (b, s, COL_GLA_V // D)),
                  pl.BlockSpec((1, Tb, D), lambda b, s: (b, s, COL_GLA_OG // D)),
                  pl.BlockSpec((1, Tb, LANE), lambda b, s: (b, s, COL_SMALL // LANE)),
                  _const_spec((LANE, HK)),
                  _const_spec((1, HK)),
                  _const_spec((1, GLA_DV))],
        out_specs=pl.BlockSpec((1, Tb, D), lambda b, s: (b, s, 0)),
        out_shape=jax.ShapeDtypeStruct((B, S, D), BF16),
        scratch_shapes=[pltpu.VMEM((GLA_HEADS, GLA_DK, GLA_DV), F32)],
        compiler_params=_cparams(("parallel", "arbitrary")),
        name="gla",
    )(proj3, proj3, proj3, proj3, proj3, wa_pad, ba, gn)


def _cmp_kernel(kv_ref, pos_ref, w1_ref, w2_ref, o_ref, x_ref):
    x_ref[...] = kv_ref[0].astype(F32)
    nrow = x_ref.shape[0] // CMP_STRIDE
    first = None
    second = None
    for i in range(CMP_STRIDE):
        xi = x_ref[pl.ds(i, nrow, stride=CMP_STRIDE), :]
        f = jnp.dot((xi + pos_ref[0, i]).astype(BF16), w1_ref[0, i], preferred_element_type=F32)
        s = jnp.dot((xi + pos_ref[1, i]).astype(BF16), w1_ref[1, i], preferred_element_type=F32)
        first = f if first is None else first + f
        second = s if second is None else second + s
    pre = first + pltpu.roll(second, nrow - 1, axis=0)
    hid = _gelu_tanh(pre).astype(BF16)
    o_ref[0, 0] = jnp.dot(hid, w2_ref[...], preferred_element_type=F32).astype(o_ref.dtype)


def _compress(proj3, pos, w1, w2):
    B, S, _ = proj3.shape
    nrow = S // CMP_STRIDE
    kvb = COL_NSA_KV // LANE
    return pl.pallas_call(
        _cmp_kernel,
        grid=(B, NSA_G),
        in_specs=[pl.BlockSpec((1, S, LANE), lambda b, g: (b, 0, kvb + g)),
                  _const_spec(pos.shape), _const_spec(w1.shape), _const_spec(w2.shape)],
        out_specs=pl.BlockSpec((1, 1, nrow, LANE), lambda b, g: (b, g, 0, 0)),
        out_shape=jax.ShapeDtypeStruct((B, NSA_G, nrow, LANE), BF16),
        scratch_shapes=[pltpu.VMEM((S, LANE), F32)],
        compiler_params=_cparams(("parallel", "parallel")),
        name="nsa_compress",
    )(proj3, pos, w1, w2)


def _compress_params(cmp_pos, cmp_w1, cmp_w2):
    hd = NSA_HD
    w1 = cmp_w1.reshape(2, 2, CMP_STRIDE, hd, CMP_HIDDEN)
    z = jnp.zeros_like(w1[0])
    w1 = jnp.concatenate([jnp.concatenate([w1[0], z], axis=-1),
                          jnp.concatenate([z, w1[1]], axis=-1)], axis=-2)
    pos = cmp_pos.reshape(2, 2, CMP_STRIDE, 1, hd)
    pos = jnp.concatenate([pos[0], pos[1]], axis=-1)
    z2 = jnp.zeros_like(cmp_w2[0])
    w2 = jnp.concatenate([jnp.concatenate([cmp_w2[0], z2], axis=-1),
                          jnp.concatenate([z2, cmp_w2[1]], axis=-1)], axis=0)
    return pos, w1.astype(BF16), w2.astype(BF16)


V_ROWS = NSA_HD + 16
MASK_BIG = 1e30
LOG2_E = 1.4426950408889634


def _nsa_kernel(q_ref, kvc_ref, kvs_ref, kvw_ref, gate_ref, bc_ref, bs_ref, bw_ref, xtra_ref, ovl_ref,
                gsel_ref, y_ref, a_ref, b_ref, vs_ref, vw_ref, vc_ref, lg_ref, p_ref):
    NB = q_ref.shape[0]
    rows_nb = range(NB)
    TQ = Q_TILE
    KT = K_TILE
    RT = TQ // KT
    HW = NSA_R * TQ
    qi = pl.program_id(2)
    S = kvs_ref.shape[1]
    n_cmp_rows = kvc_ref.shape[2]

    def value_rows(tile):
        t = tile.astype(F32).T
        return jnp.concatenate([t[NSA_HD:], jnp.ones((V_ROWS - NSA_HD, t.shape[1]), F32)], axis=0).astype(BF16)

    n_blk = ovl_ref.shape[0]
    k_lanes = lax.broadcasted_iota(jnp.int32, (LANE, LANE), 1) < NSA_HD

    @pl.when(qi == 0)
    def _():
        for nb in rows_nb:
            for j in range(S // LANE):
                rows = slice(j * LANE, (j + 1) * LANE)
                ks = kvs_ref[nb, rows, :]
                a_ref[nb, rows, :] = jnp.where(k_lanes, ks, xtra_ref[rows, :])
                vs_ref[nb, :, rows] = value_rows(ks)
                vw_ref[nb, :, rows] = value_rows(kvw_ref[nb, rows, :])
            vc_ref[nb] = value_rows(kvc_ref[nb, 0])
            b_ref[nb, NSA_HD + n_blk:, :] = jnp.zeros((LANE - NSA_HD - n_blk, HW), BF16)

    def tile_rows(j, n=1):
        return pl.ds(pl.multiple_of(j * KT, KT), n * KT)

    def normalised(acc):
        return acc[:NSA_HD] * (1.0 / acc[NSA_HD:NSA_HD + 1])

    def q_transposed(nb):
        qf = q_ref[nb].astype(F32) * (NSA_HD ** -0.5)
        zero_rows = jnp.zeros((LANE - NSA_HD, TQ), F32)
        heads = []
        for pr in range(NSA_R // 2):
            t = qf[:, pr * LANE:(pr + 1) * LANE].T
            heads.append(jnp.concatenate([t[:NSA_HD], zero_rows], axis=0))
            heads.append(jnp.concatenate([t[NSA_HD:], zero_rows], axis=0))
        return jnp.concatenate(heads, axis=1)

    qt32 = [q_transposed(nb) for nb in rows_nb]
    qt = [x.astype(BF16) for x in qt32]
    qt32_l2 = [x * LOG2_E for x in qt32]
    qt_l2 = [x.astype(BF16) for x in qt32_l2]

    parts = TQ // LANE
    o_win_parts = [[None] * parts for _ in rows_nb]
    for h in range(parts):
        own = parts * qi + h
        skipped = jnp.maximum(WIN_SPAN - 1 - own, 0)
        rows_w = pl.ds(pl.multiple_of(jnp.maximum(own - (WIN_SPAN - 1), 0) * LANE, LANE), WIN_SPAN * LANE)
        bias_w = jnp.concatenate([bw_ref[0, jnp.minimum(t + skipped, WIN_SPAN)] for t in range(WIN_SPAN)], axis=0)
        q_part = [jnp.concatenate([qt_l2[nb][:, r * TQ + h * LANE:r * TQ + (h + 1) * LANE] for r in range(NSA_R)],
                                  axis=1) for nb in rows_nb]
        lg_w = [jnp.dot(kvw_ref[nb, rows_w, :], q_part[nb], preferred_element_type=F32).astype(BF16) + bias_w
                for nb in rows_nb]
        p_w = [jnp.exp2(lg - jnp.max(lg, axis=0, keepdims=True)) for lg in lg_w]
        for nb in rows_nb:
            o_win_parts[nb][h] = normalised(jnp.dot(vw_ref[nb, :, rows_w], p_w[nb], preferred_element_type=F32))
    o_win = [jnp.concatenate([o_win_parts[nb][h][:, r * LANE:(r + 1) * LANE]
                              for r in range(NSA_R) for h in range(parts)], axis=1) for nb in rows_nb]

    cq = TQ // CMP_STRIDE
    start_c = pl.multiple_of(bc_ref.shape[1] - n_cmp_rows - cq * qi, 8)
    bias_c = bc_ref[0, pl.ds(start_c, n_cmp_rows), :]
    valid_c = bias_c > 0.5 * NEG_INF
    lg_c = [jnp.dot(kvc_ref[nb, 0], qt[nb], preferred_element_type=F32) + bias_c for nb in rows_nb]
    e_c = [jnp.exp(lg - jnp.max(lg, axis=0, keepdims=True)) for lg in lg_c]
    p_c = [jnp.where(valid_c, e * (1.0 / jnp.sum(e, axis=0, keepdims=True)), 0.0) for e in e_c]
    o_cmp = [jnp.dot(vc_ref[nb], p_c[nb].astype(BF16), preferred_element_type=F32)[:NSA_HD] for nb in rows_nb]

    ovl = ovl_ref[...]
    ns = ovl.shape[0]
    blk = lax.broadcasted_iota(jnp.int32, (ns, TQ), 0)
    cur = (qi * TQ + lax.broadcasted_iota(jnp.int32, (ns, TQ), 1)) // SLC_BLOCK
    forced = (blk == 0) | (blk == cur) | (blk == cur - 1)
    allowed = blk <= cur

    def importance(p):
        psum = p[:, 0:TQ]
        for r in range(1, NSA_R):
            psum = psum + p[:, r * TQ:(r + 1) * TQ]
        p_hi = psum.astype(BF16)
        p_lo = (psum - p_hi.astype(F32)).astype(BF16)
        imp = jnp.dot(ovl, p_hi, preferred_element_type=F32) + jnp.dot(ovl, p_lo, preferred_element_type=F32)
        return jnp.where(forced, FORCED, jnp.where(allowed, imp, -FORCED))

    score = [importance(p) for p in p_c]
    groups = ns // 8
    blk8 = lax.broadcasted_iota(jnp.int32, (8, TQ), 0)
    for nb in rows_nb:
        sc = [score[nb][8 * v:8 * v + 8] for v in range(groups)]
        rank = [jnp.zeros((8, TQ), F32) for _ in range(groups)]
        for j in range(ns):
            sj = score[nb][j:j + 1, :]
            for v in range(groups):
                if j < 8 * v:
                    beats = sj >= sc[v]
                elif j >= 8 * v + 8:
                    beats = sj > sc[v]
                else:
                    tie = jnp.where(blk8 > j - 8 * v, 1.0, 0.0)
                    rank[v] = rank[v] + jnp.where(sj > sc[v], 1.0, jnp.where(sj == sc[v], tie, 0.0))
                    continue
                rank[v] = rank[v] + jnp.where(beats, 1.0, 0.0)
        neg_sel = jnp.where(jnp.concatenate(rank, axis=0) < float(SLC_TOP_N), 0.0, -MASK_BIG).astype(BF16)
        b_ref[nb, 0:NSA_HD] = qt_l2[nb][:NSA_HD]
        b_ref[nb, NSA_HD:NSA_HD + ns] = jnp.concatenate([neg_sel] * NSA_R, axis=1)

    def logits(nb, j):
        return jnp.dot(a_ref[nb, tile_rows(j), :], b_ref[nb], preferred_element_type=F32).astype(BF16)

    for nb in rows_nb:
        lg_ref[nb] = logits(nb, 0)
        p_ref[nb] = jnp.zeros(p_ref.shape[1:], BF16)

    last = RT * qi + RT - 1

    def trip(j, carry):
        lg_next = [logits(nb, jnp.minimum(j + 1, last)) for nb in rows_nb]
        prev_rows = tile_rows(jnp.maximum(j - 1, 0))
        bias = bs_ref[0, jnp.where(j >= RT * qi - 1, j - RT * qi + 1, RT + 1)]
        out = []
        for nb in rows_nb:
            m, acc, alpha = carry[nb]
            acc = acc * alpha + jnp.dot(vs_ref[nb, :, prev_rows], p_ref[nb], preferred_element_type=F32)
            lg = lg_ref[nb] + bias
            m_new = jnp.maximum(m, jnp.max(lg, axis=0, keepdims=True).astype(F32))
            p_ref[nb] = jnp.exp2(lg - m_new.astype(BF16))
            out.append((m_new, acc, jnp.exp2(m - m_new)))
        for nb in rows_nb:
            lg_ref[nb] = lg_next[nb]
        return tuple(out)

    init = (jnp.full((1, HW), NEG_INF, F32), jnp.zeros((V_ROWS, HW), F32), jnp.ones((1, HW), F32))
    carry = lax.fori_loop(0, last + 1, trip, tuple(init for _ in rows_nb))
    o_slc = [normalised(carry[nb][1] * carry[nb][2]
                        + jnp.dot(vs_ref[nb, :, tile_rows(last)], p_ref[nb], preferred_element_type=F32))
             for nb in rows_nb]

    for nb in rows_nb:
        gs = jax.nn.sigmoid(lax.dot_general(gsel_ref[0], gate_ref[nb], _NT, preferred_element_type=F32))

        def gate_row(br):
            return jnp.concatenate([gs[br * NSA_R + r:br * NSA_R + r + 1] for r in range(NSA_R)], axis=1)

        o = gate_row(0) * o_cmp[nb] + gate_row(1) * o_slc[nb] + gate_row(2) * o_win[nb]
        o = jnp.concatenate([o[:, r * TQ:(r + 1) * TQ] for r in range(NSA_R)], axis=0)
        y_ref[nb] = o.T.astype(y_ref.dtype)


def _nsa(proj3, kvc, bias_c, bias_s, bias_w, xtra, ovl_t, gsel):
    B, S, _ = proj3.shape
    TQ = Q_TILE
    G = NSA_G
    RW = NSA_R * NSA_HD
    HW = NSA_R * TQ
    kvb = COL_NSA_KV // LANE
    ncr = kvc.shape[2]
    KT = K_TILE
    NB = NSA_BATCH_ROWS
    assert S % TQ == 0 and TQ % KT == 0 and S >= TQ + KT and WINDOW == KT and ncr % LANE == 0 and B % NB == 0
    return pl.pallas_call(
        _nsa_kernel,
        grid=(G, B // NB, S // TQ),
        in_specs=[pl.BlockSpec((NB, TQ, RW), lambda g, b, i: (b, i, COL_NSA_Q // RW + g)),
                  pl.BlockSpec((NB, 1, ncr, LANE), lambda g, b, i: (b, g, 0, 0)),
                  pl.BlockSpec((NB, S, LANE), lambda g, b, i: (b, 0, kvb + G + g)),
                  pl.BlockSpec((NB, S, LANE), lambda g, b, i: (b, 0, kvb + 2 * G + g)),
                  pl.BlockSpec((NB, TQ, LANE), lambda g, b, i: (b, i, COL_SMALL // LANE)),
                  pl.BlockSpec((1,) + bias_c.shape[1:], lambda g, b, i: (g, 0, 0)),
                  pl.BlockSpec((1,) + bias_s.shape[1:], lambda g, b, i: (g, 0, 0, 0)),
                  pl.BlockSpec((1,) + bias_w.shape[1:], lambda g, b, i: (g, 0, 0, 0)),
                  pl.BlockSpec(xtra.shape, lambda g, b, i: (0, 0)),
                  pl.BlockSpec(ovl_t.shape, lambda g, b, i: (0, 0)),
                  pl.BlockSpec((1,) + gsel.shape[1:], lambda g, b, i: (g, 0, 0))],
        out_specs=pl.BlockSpec((NB, TQ, RW), lambda g, b, i: (b, i, g)),
        out_shape=jax.ShapeDtypeStruct((B, S, D_MODEL), BF16),
        scratch_shapes=[pltpu.VMEM((NB, S, LANE), BF16), pltpu.VMEM((NB, LANE, HW), BF16),
                        pltpu.VMEM((NB, V_ROWS, S), BF16), pltpu.VMEM((NB, V_ROWS, S), BF16),
                        pltpu.VMEM((NB, V_ROWS, ncr), BF16), pltpu.VMEM((NB, KT, HW), BF16),
                        pltpu.VMEM((NB, KT, HW), BF16)],
        compiler_params=_cparams(("parallel", "parallel", "arbitrary")),
        name="nsa_attention",
    )(proj3, kvc, proj3, proj3, proj3, bias_c, bias_s, bias_w, xtra, ovl_t, gsel)


def _rel_bucket(dist):
    n = jnp.maximum(dist, 0)
    nf = jnp.maximum(n, REL_MAX_EXACT).astype(F32)
    large = REL_MAX_EXACT + (jnp.log(nf / REL_MAX_EXACT) / math.log(REL_MAX_DIST / REL_MAX_EXACT)
                             * (REL_BUCKETS - REL_MAX_EXACT)).astype(jnp.int32)
    large = jnp.minimum(large, REL_BUCKETS - 1)
    return jnp.where(n < REL_MAX_EXACT, n, large)


def _nsa_tables(rel_table, S):
    TQ = Q_TILE
    tbl = rel_table.astype(F32).reshape(REL_BUCKETS, NSA_G, NSA_R)
    tbl = tbl - tbl[REL_BUCKETS - 1]

    def table(dist, ok):
        onehot = jax.nn.one_hot(_rel_bucket(dist), REL_BUCKETS, dtype=F32)
        b = jnp.einsum("...kqn,ngr->g...krq", onehot, tbl, precision=lax.Precision.HIGHEST)
        b = jnp.where(ok[..., :, None, :], b, NEG_INF)
        return b.reshape(b.shape[:-2] + (NSA_R * dist.shape[-1],))

    KT = K_TILE
    kk = jnp.arange(KT)[:, None]
    qq = jnp.arange(TQ)[None, :]
    dist = qq - kk - KT * (jnp.arange(TQ // KT + 1)[:, None, None] - 1)
    bias_s = table(dist, dist >= 0) * LOG2_E
    bias_s = jnp.concatenate([bias_s, jnp.zeros_like(bias_s[:, :1])], axis=1).astype(BF16)
    k128 = jnp.arange(LANE)[:, None]
    q128 = jnp.arange(LANE)[None, :]
    dist_w = q128 - k128 + LANE * (WIN_SPAN - 1 - jnp.arange(WIN_SPAN)[:, None, None])
    bias_w = table(dist_w, (dist_w >= 0) & (dist_w < WINDOW)) * LOG2_E
    bias_w = jnp.concatenate([bias_w, jnp.full_like(bias_w[:, :1], NEG_INF)], axis=1).astype(BF16)
    n_rows = S // CMP_STRIDE
    off = n_rows - TQ // CMP_STRIDE
    c_rel = jnp.arange(off + n_rows)[:, None] - off
    dist_c = qq - CMP_STRIDE * c_rel - (CMP_BLOCK - 1)
    bias_c = table(dist_c, dist_c >= 0)

    n_slc = S // SLC_BLOCK
    n_cmp = n_rows - CMP_BLOCK // CMP_STRIDE + 1
    xtra = (jnp.arange(LANE)[None, :] - NSA_HD == (jnp.arange(S)[:, None] // SLC_BLOCK)).astype(BF16)
    cmp_start = jnp.arange(n_rows) * CMP_STRIDE
    slc_start = jnp.arange(n_slc) * SLC_BLOCK
    overlap = jnp.clip(jnp.minimum(cmp_start[:, None] + CMP_BLOCK, slc_start[None, :] + SLC_BLOCK)
                       - jnp.maximum(cmp_start[:, None], slc_start[None, :]), 0).astype(F32) / CMP_BLOCK
    overlap = jnp.where(jnp.arange(n_rows)[:, None] < n_cmp, overlap, 0.0)
    g = jnp.arange(NSA_G)[:, None, None]
    row = jnp.arange(16)[None, :, None]
    col = jnp.arange(LANE)[None, None, :]
    src = SMALL_GATE + (row // NSA_R) * NSA_HEADS + g * NSA_R + row % NSA_R
    gsel = ((col == src) & (row < 3 * NSA_R)).astype(BF16)
    return bias_c, bias_s, bias_w, xtra, overlap.T.astype(BF16), gsel


def _merge_ffn_kernel(ya_ref, yb_ref, yc_ref, g0_ref, g1_ref, g2_ref, x_ref, wb_ref, wo_ref, gn_ref,
                      gpre_ref, win_ref, wout_ref, gpost_ref, o_ref, *, chunk):
    m = None
    for br, (y_ref, g_ref) in enumerate(((ya_ref, g0_ref), (yb_ref, g1_ref), (yc_ref, g2_ref))):
        t = jax.nn.sigmoid(g_ref[...].astype(F32)) * jnp.dot(y_ref[...], wb_ref[br], preferred_element_type=F32)
        m = t if m is None else m + t
    z = jnp.dot(m.astype(BF16), wo_ref[...], preferred_element_type=F32)
    x = x_ref[...] + _rms(z, gn_ref[...])
    h = _rms(x, gpre_ref[...]).astype(BF16)
    acc = None
    for c in range(D_FF // chunk):
        gt = jnp.dot(h, win_ref[:, c * chunk:(c + 1) * chunk], preferred_element_type=F32)
        up = jnp.dot(h, win_ref[:, D_FF + c * chunk:D_FF + (c + 1) * chunk], preferred_element_type=F32)
        a = (gt * jax.nn.sigmoid(gt) * up).astype(BF16)
        t = jnp.dot(a, wout_ref[c * chunk:(c + 1) * chunk, :], preferred_element_type=F32)
        acc = t if acc is None else acc + t
    o_ref[...] = x + _rms(acc, gpost_ref[...])


def _merge_ffn(ya, yb, yc, proj, x2, wb, wo, gn, gpre, win, wout, gpost, layer, tm=512, chunk=256):
    T, D = x2.shape
    row = lambda i: (i, 0)
    mcol = COL_MERGE // D
    return pl.pallas_call(
        functools.partial(_merge_ffn_kernel, chunk=chunk),
        grid=(T // tm,),
        in_specs=[pl.BlockSpec((tm, D), row), pl.BlockSpec((tm, D), row), pl.BlockSpec((tm, D), row),
                  pl.BlockSpec((tm, D), lambda i: (i, mcol)),
                  pl.BlockSpec((tm, D), lambda i: (i, mcol + 1)),
                  pl.BlockSpec((tm, D), lambda i: (i, mcol + 2)),
                  pl.BlockSpec((tm, D), row),
                  _const_spec(wb.shape[1:], layer), _const_spec(wo.shape[1:], layer), _const_spec((1, D)),
                  _const_spec((1, D)), _const_spec(win.shape[1:], layer), _const_spec(wout.shape[1:], layer),
                  _const_spec((1, D))],
        out_specs=pl.BlockSpec((tm, D), row),
        out_shape=jax.ShapeDtypeStruct((T, D), F32),
        compiler_params=_cparams(("parallel",)),
        name="merge_ffn",
    )(ya, yb, yc, proj, proj, proj, x2, wb, wo, gn, gpre, win, wout, gpost)


def _in_column_sources():
    (lru_x, lru_g, nsa_q, nsa_kv, nsa_gate, gla_q, gla_k, gla_v, gla_og, gla_lr, merge) = np.split(
        np.arange(sum(IN_SIZES)), np.cumsum(IN_SIZES)[:-1])
    kv = nsa_kv.reshape(3, 2, NSA_G, NSA_HD).transpose(0, 2, 1, 3).reshape(-1)
    pad = np.full(LANE - nsa_gate.size - gla_lr.size, -1)
    src = np.concatenate([lru_x, lru_g, nsa_q, gla_v, gla_og, merge, kv, gla_q, gla_k, nsa_gate, gla_lr, pad])
    assert src.size == PROJ_WIDTH
    return src


def _permute_plan():
    src = _in_column_sources()
    plan = []
    for c in range(PROJ_WIDTH // LANE):
        runs = []
        for s in src[c * LANE:(c + 1) * LANE]:
            s = int(s)
            if runs and ((s < 0 and runs[-1][0] < 0) or (s >= 0 and runs[-1][0] >= 0 and runs[-1][0] + runs[-1][1] == s)):
                runs[-1][1] += 1
            else:
                runs.append([s, 1])
        assert all(n % 8 == 0 and (s < 0 or s % 8 == 0) for s, n in runs)
        plan.append(tuple((s, n) for s, n in runs))
    return tuple(plan)


def _permute_kernel(wt_ref, o_ref, *, plan):
    tc = wt_ref.shape[1]
    for c, runs in enumerate(plan):
        parts = [wt_ref[s:s + n, :] if s >= 0 else jnp.zeros((n, tc), F32) for s, n in runs]
        block = parts[0] if len(parts) == 1 else jnp.concatenate(parts, axis=0)
        o_ref[:, c * LANE:(c + 1) * LANE] = block.T.astype(o_ref.dtype)


def _permute_in_weights(w_in, tc=128):
    L, D, W = w_in.shape
    return pl.pallas_call(
        functools.partial(_permute_kernel, plan=_permute_plan()),
        grid=(L, D // tc),
        in_specs=[pl.BlockSpec((None, W, tc), lambda l, i: (l, 0, i))],
        out_specs=pl.BlockSpec((None, tc, PROJ_WIDTH), lambda l, i: (l, i, 0)),
        out_shape=jax.ShapeDtypeStruct((L, D, PROJ_WIDTH), BF16),
        compiler_params=_cparams(("parallel", "parallel")),
        name="permute_w_in",
    )(jnp.swapaxes(w_in, 1, 2))


def _layer(x2, B, S, tables, layer, norm_g, w_in, conv_w, conv_b, lru_w_gates, lru_b_gates, lru_lambda,
           cmp_pos, cmp_w1, cmp_w2, gla_wa2, gla_ba, gla_norm, w_branch, w_out, w_ffn_in, w_ffn_out):
    D = D_MODEL
    proj = _in_proj(x2, norm_g[0][None, :], w_in, layer)
    proj3 = proj.reshape(B, S, PROJ_WIDTH)

    wg = jnp.concatenate([lru_w_gates[0], lru_w_gates[1]], axis=-1).astype(BF16)
    y_a = _lru(proj3, conv_w, conv_b[None, :], wg, lru_b_gates, lru_lambda[None, :])

    kvc = _compress(proj3, *_compress_params(cmp_pos, cmp_w1, cmp_w2))
    y_b = _nsa(proj3, kvc, *tables)

    wa_pad = jnp.zeros((LANE, GLA_HEADS * GLA_DK), F32).at[SMALL_LR:SMALL_LR + GLA_RANK].set(gla_wa2).astype(BF16)
    y_c = _gla(proj3, wa_pad, gla_ba[None, :], gla_norm[None, :])

    return _merge_ffn(y_a.reshape(B * S, D), y_b.reshape(B * S, D), y_c.reshape(B * S, D), proj, x2,
                      w_branch, w_out, norm_g[1][None, :], norm_g[2][None, :], w_ffn_in, w_ffn_out,
                      norm_g[3][None, :], layer)


def kernel(x, rel_table, norm_g, w_in, conv_w, conv_b, lru_w_gates, lru_b_gates, lru_lambda, cmp_pos, cmp_w1,
           cmp_w2, gla_wa2, gla_ba, gla_norm, w_branch, w_out, w_ffn_in, w_ffn_out):
    B, S, D = x.shape
    tables = _nsa_tables(rel_table, S)
    x2 = x.reshape(B * S, D)
    w_in = _permute_in_weights(w_in)
    w_branch, w_out, w_ffn_in, w_ffn_out = (w.astype(BF16) for w in (w_branch, w_out, w_ffn_in, w_ffn_out))
    for l in range(norm_g.shape[0]):
        x2 = _layer(x2, B, S, tables, l, norm_g[l], w_in, conv_w[l], conv_b[l], lru_w_gates[l], lru_b_gates[l],
                    lru_lambda[l], cmp_pos[l], cmp_w1[l], cmp_w2[l], gla_wa2[l], gla_ba[l], gla_norm[l],
                    w_branch, w_out, w_ffn_in, w_ffn_out)
    return x2.reshape(B, S, D)
```

```python
import functools
import math

import jax
import jax.numpy as jnp
import numpy as np
from jax import lax
from jax.experimental import pallas as pl
from jax.experimental.pallas import tpu as pltpu

F32 = jnp.float32
BF16 = jnp.bfloat16

D_MODEL = 1024
N_BRANCH = 3
EPS = 1e-6
NEG_INF = -1e30
FORCED = 1e4

LRU_BLOCKS = 8
LRU_BLOCK = D_MODEL // LRU_BLOCKS
CONV_WIDTH = 4
LRU_C = 8.0

NSA_HEADS = 16
NSA_G = 4
NSA_R = NSA_HEADS // NSA_G
NSA_HD = D_MODEL // NSA_HEADS
CMP_BLOCK = 32
CMP_STRIDE = 16
CMP_HIDDEN = 256
SLC_BLOCK = 64
SLC_TOP_N = 8
WINDOW = 256

GLA_HEADS = 4
GLA_DK = (D_MODEL // 2) // GLA_HEADS
GLA_DV = D_MODEL // GLA_HEADS
GLA_RANK = 16
GLA_TAU = 16.0
GLA_CHUNK = 32

REL_BUCKETS = 32
REL_MAX_EXACT = 16
REL_MAX_DIST = 128

D_FF = -(-8 * D_MODEL // (3 * 256)) * 256

IN_SIZES = (D_MODEL, D_MODEL, D_MODEL, 6 * NSA_G * NSA_HD, 3 * NSA_HEADS, D_MODEL // 2, D_MODEL // 2,
            D_MODEL, D_MODEL, GLA_RANK, N_BRANCH * D_MODEL)

LANE = 128
COL_LRU_X = 0
COL_LRU_G = 1024
COL_NSA_Q = 2048
COL_GLA_V = 3072
COL_GLA_OG = 4096
COL_MERGE = 5120
COL_NSA_KV = 8192
COL_GLA_Q = 9728
COL_GLA_K = 10240
COL_SMALL = 10752
PROJ_WIDTH = 10880
SMALL_GATE = 0
SMALL_LR = 3 * NSA_HEADS

Q_TILE = 256
K_TILE = 256
NSA_BATCH_ROWS = 4
WIN_SPAN = WINDOW // LANE + 1
VMEM_LIMIT = 56 * 1024 * 1024
_NT = (((1,), (1,)), ((), ()))


def _cparams(sem):
    return pltpu.CompilerParams(dimension_semantics=sem, vmem_limit_bytes=VMEM_LIMIT)


def _const_spec(shape, layer=None):
    nd = len(shape)
    if layer is None:
        return pl.BlockSpec(shape, lambda *_: (0,) * nd, pipeline_mode=pl.Buffered(1))
    return pl.BlockSpec((None,) + tuple(shape), lambda *_: (layer,) + (0,) * nd, pipeline_mode=pl.Buffered(1))


def _rms(x, g):
    return x * lax.rsqrt(jnp.mean(x * x, axis=-1, keepdims=True) + EPS) * g


def _gelu_tanh(x):
    return 0.5 * x * (1.0 + jnp.tanh(math.sqrt(2.0 / math.pi) * (x + 0.044715 * (x * x * x))))


def _sigmoid(x):
    return 0.5 * jnp.tanh(0.5 * x) + 0.5


def _sqrt_nonneg(x):
    return jnp.where(x > 0.0, x * lax.rsqrt(x), 0.0)


def _softplus(z):
    return jnp.maximum(z, 0.0) + jnp.log1p(jnp.exp(-jnp.abs(z)))


def _in_proj_kernel(x_ref, g_ref, w_ref, o_ref, h_ref):
    @pl.when(pl.program_id(1) == 0)
    def _():
        h_ref[...] = _rms(x_ref[...], g_ref[...]).astype(BF16)

    o_ref[...] = jnp.dot(h_ref[...], w_ref[...], preferred_element_type=F32).astype(o_ref.dtype)


def _in_proj(x2, g, w, layer, tm=1024, tn=2176):
    T, D = x2.shape
    N = w.shape[2]
    return pl.pallas_call(
        _in_proj_kernel,
        grid=(T // tm, N // tn),
        in_specs=[pl.BlockSpec((tm, D), lambda i, j: (i, 0)),
                  pl.BlockSpec((1, D), lambda i, j: (0, 0)),
                  pl.BlockSpec((None, D, tn), lambda i, j: (layer, 0, j))],
        out_specs=pl.BlockSpec((tm, tn), lambda i, j: (i, j)),
        out_shape=jax.ShapeDtypeStruct((T, N), BF16),
        scratch_shapes=[pltpu.VMEM((tm, D), BF16)],
        compiler_params=_cparams(("parallel", "arbitrary")),
        name="in_proj",
    )(x2, g, w)


def _lru_kernel(xa_ref, ga_ref, cw_ref, cb_ref, wg_ref, bg_ref, lam_ref, y_ref, tail_ref, h_ref, *, R):
    @pl.when(pl.program_id(1) == 0)
    def _():
        tail_ref[...] = jnp.zeros_like(tail_ref)
        h_ref[...] = jnp.zeros_like(h_ref)

    sp = _softplus(-lam_ref[...])
    groups = R // 8
    row8 = lax.broadcasted_iota(jnp.int32, (groups, 8, LRU_BLOCK), 1)
    for n in range(LRU_BLOCKS):
        sl = slice(n * LRU_BLOCK, (n + 1) * LRU_BLOCK)
        x3 = xa_ref[0, :, sl].astype(F32).reshape(groups, 8, LRU_BLOCK)
        tail = tail_ref[:, sl].reshape(1, 8, LRU_BLOCK)
        xc3 = cb_ref[:, sl] + cw_ref[CONV_WIDTH - 1:CONV_WIDTH, sl] * x3
        for s in range(1, CONV_WIDTH):
            rolled = pltpu.roll(x3, s, axis=1)
            prev = jnp.concatenate([pltpu.roll(tail, s, axis=1), rolled[:groups - 1]], axis=0)
            w = cw_ref[CONV_WIDTH - 1 - s:CONV_WIDTH - s, sl]
            xc3 = xc3 + w * jnp.where(row8 >= s, rolled, prev)
        tail_ref[:, sl] = x3[groups - 1]
        xcn = xc3.reshape(R, LRU_BLOCK)
        gz =jnp.dot(xcn.astype(BF16), wg_ref[n], preferred_element_type=F32)
        r = _sigmoid(gz[:, :LRU_BLOCK] + bg_ref[0:1, sl])
        i = _sigmoid(gz[:, LRU_BLOCK:] + bg_ref[1:2, sl])
        a = jnp.exp((-LRU_C) * r * sp[:, sl])
        u = _sqrt_nonneg(1.0 - a * a) * (i * xcn)
        a = a.reshape(R // 8, 8, LRU_BLOCK)
        u = u.reshape(R // 8, 8, LRU_BLOCK)
        for d in (1, 2, 4):
            keep = row8 >= d
            a_s = pltpu.roll(a, d, axis=1)
            u_s = pltpu.roll(u, d, axis=1)
            u = jnp.where(keep, a * u_s + u, u)
            a = jnp.where(keep, a * a_s, a)
        hprev = h_ref[:, sl]
        hs = []
        for t in range(R // 8):
            ht = a[t] * hprev + u[t]
            hprev = ht[7:8]
            hs.append(ht)
        h_ref[:, sl] = hprev
        h = jnp.concatenate(hs, axis=0)
        y_ref[0, :, sl] = (h * _gelu_tanh(ga_ref[0, :, sl].astype(F32))).astype(y_ref.dtype)


def _lru(proj3, conv_w, conv_b, wg, bg, lam, R=256):
    B, S, _ = proj3.shape
    D = D_MODEL
    return pl.pallas_call(
        functools.partial(_lru_kernel, R=R),
        grid=(B, S // R),
        in_specs=[pl.BlockSpec((1, R, D), lambda b, s: (b, s, COL_LRU_X // D)),
                  pl.BlockSpec((1, R, D), lambda b, s: (b, s, COL_LRU_G // D)),
                  _const_spec((CONV_WIDTH, D)),
                  _const_spec((1, D)),
                  _const_spec((LRU_BLOCKS, LRU_BLOCK, 2 * LRU_BLOCK)),
                  _const_spec((2, D)),
                  _const_spec((1, D))],
        out_specs=pl.BlockSpec((1, R, D), lambda b, s: (b, s, 0)),
        out_shape=jax.ShapeDtypeStruct((B, S, D), BF16),
        scratch_shapes=[pltpu.VMEM((8, D), F32), pltpu.VMEM((1, D), F32)],
        compiler_params=_cparams(("parallel", "arbitrary")),
        name="rglru",
    )(proj3, proj3, conv_w, conv_b, wg, bg, lam)


def _gla_kernel(q_ref, k_ref, v_ref, og_ref, sm_ref, wa_ref, ba_ref, gn_ref, y_ref, st_ref, *, Tb):
    nc = Tb // GLA_CHUNK

    @pl.when(pl.program_id(1) == 0)
    def _():
        st_ref[...] = jnp.zeros_like(st_ref)

    la_pre = jnp.dot(sm_ref[0], wa_ref[...], preferred_element_type=F32)
    rowc = lax.broadcasted_iota(jnp.int32, (Tb, GLA_DK), 0) & (GLA_CHUNK - 1)
    ri = lax.broadcasted_iota(jnp.int32, (Tb, Tb), 0)
    ci = lax.broadcasted_iota(jnp.int32, (Tb, Tb), 1)
    same_chunk_causal = ((ri // GLA_CHUNK) == (ci // GLA_CHUNK)) & (ci <= ri)
    lane_chunk = lax.broadcasted_iota(jnp.int32, (GLA_DK, Tb), 1) // GLA_CHUNK

    heads = range(GLA_HEADS)
    kcols = [slice(h * GLA_DK, (h + 1) * GLA_DK) for h in heads]
    vcols = [slice(h * GLA_DV, (h + 1) * GLA_DV) for h in heads]

    def chunk_log_decay(h):
        z = la_pre[:, kcols[h]] + ba_ref[:, kcols[h]]
        b = (jnp.minimum(z, 0.0) - jnp.log(1.0 + jnp.exp(-jnp.abs(z)))) * (1.0 / GLA_TAU)
        d = 1
        while d < GLA_CHUNK:
            b = b + jnp.where(rowc >= d, pltpu.roll(b, d, axis=0), 0.0)
            d *= 2
        return b

    b = [chunk_log_decay(h) for h in heads]
    b_last = [jnp.concatenate(
        [jnp.broadcast_to(b[h][c * GLA_CHUNK + GLA_CHUNK - 1:(c + 1) * GLA_CHUNK, :], (GLA_CHUNK, GLA_DK))
         for c in range(nc)], axis=0) for h in heads]
    q = [q_ref[0, :, kcols[h]].astype(F32) * (GLA_DK ** -0.5) for h in heads]
    k = [k_ref[0, :, kcols[h]].astype(F32) for h in heads]
    v = [v_ref[0, :, vcols[h]] for h in heads]
    qt = [(q[h] * jnp.exp(b[h])).astype(BF16) for h in heads]
    kt = [(k[h] * jnp.exp(-b[h])).astype(BF16) for h in heads]
    ke_t = [(k[h] * jnp.exp(b_last[h] - b[h])).T for h in heads]
    b_t = [b[h].T for h in heads]

    att = [jnp.where(same_chunk_causal, lax.dot_general(qt[h], kt[h], _NT, preferred_element_type=F32), 0.0)
           for h in heads]
    o = [jnp.dot(att[h].astype(BF16), v[h], preferred_element_type=F32) for h in heads]
    kv_all = [jnp.dot(jnp.concatenate([jnp.where(lane_chunk == c, ke_t[h], 0.0) for c in range(nc)],
                                      axis=0).astype(BF16), v[h], preferred_element_type=F32)
              for h in heads]

    state = [st_ref[h] for h in heads]
    inter = [[] for _ in heads]
    for c in range(nc):
        rs = slice(c * GLA_CHUNK, (c + 1) * GLA_CHUNK)
        last = c * GLA_CHUNK + GLA_CHUNK - 1
        for h in heads:
            inter[h].append(jnp.dot(qt[h][rs], state[h].astype(BF16), preferred_element_type=F32))
            state[h] = state[h] * jnp.exp(b_t[h][:, last:last + 1]) + kv_all[h][c * GLA_DK:(c + 1) * GLA_DK]
    for h in heads:
        st_ref[h] = state[h]
        oh = o[h] + jnp.concatenate(inter[h], axis=0)
        on = oh * lax.rsqrt(jnp.mean(oh * oh, axis=-1, keepdims=True) + EPS) * gn_ref[...]
        og = og_ref[0, :, vcols[h]].astype(F32)
        y_ref[0, :, vcols[h]] = (on * (og * jax.nn.sigmoid(og))).astype(y_ref.dtype)


def _gla(proj3, wa_pad, ba, gn, Tb=256):
    B, S, _ = proj3.shape
    HK = GLA_HEADS * GLA_DK
    D = D_MODEL
    return pl.pallas_call(
        functools.partial(_gla_kernel, Tb=Tb),
        grid=(B, S // Tb),
        in_specs=[pl.BlockSpec((1, Tb, HK), lambda b, s: (b, s, COL_GLA_Q // HK)),
                  pl.BlockSpec((1, Tb, HK), lambda b, s: (b, s, COL_GLA_K // HK)),
                  pl.BlockSpec((1, Tb, D), lambda b, s: (b, s, COL_GLA_V // D)),
                  pl.BlockSpec((1, Tb, D), lambda b, s: (b, s, COL_GLA_OG // D)),
                  pl.BlockSpec((1, Tb, LANE), lambda b, s: (b, s, COL_SMALL // LANE)),
                  _const_spec((LANE, HK)),
                  _const_spec((1, HK)),
                  _const_spec((1, GLA_DV))],
        out_specs=pl.BlockSpec((1, Tb, D), lambda b, s: (b, s, 0)),
        out_shape=jax.ShapeDtypeStruct((B, S, D), BF16),
        scratch_shapes=[pltpu.VMEM((GLA_HEADS, GLA_DK, GLA_DV), F32)],
        compiler_params=_cparams(("parallel", "arbitrary")),
        name="gla",
    )(proj3, proj3, proj3, proj3, proj3, wa_pad, ba, gn)


def _lru_gla_kernel(xa_ref, ga_ref, cw_ref, cb_ref, wg_ref, bg_ref, lam_ref, q_ref, k_ref, v_ref, og_ref, sm_ref,
                    wa_ref, ba_ref, gn_ref, ya_ref, yc_ref, tail_ref, h_ref, st_ref, *, R):
    _lru_kernel(xa_ref, ga_ref, cw_ref, cb_ref, wg_ref, bg_ref, lam_ref, ya_ref, tail_ref, h_ref, R=R)
    _gla_kernel(q_ref, k_ref, v_ref, og_ref, sm_ref, wa_ref, ba_ref, gn_ref, yc_ref, st_ref, Tb=R)


def _lru_gla(proj3, conv_w, conv_b, wg, bg, lam, wa_pad, ba, gn, R=256):
    B, S, _ = proj3.shape
    D = D_MODEL
    HK = GLA_HEADS * GLA_DK
    blk = lambda width, col: pl.BlockSpec((1, R, width), lambda b, s: (b, s, col // width))
    out = pl.BlockSpec((1, R, D), lambda b, s: (b, s, 0))
    return pl.pallas_call(
        functools.partial(_lru_gla_kernel, R=R),
        grid=(B, S // R),
        in_specs=[blk(D, COL_LRU_X), blk(D, COL_LRU_G),
                  _const_spec((CONV_WIDTH, D)), _const_spec((1, D)),
                  _const_spec((LRU_BLOCKS, LRU_BLOCK, 2 * LRU_BLOCK)), _const_spec((2, D)), _const_spec((1, D)),
                  blk(HK, COL_GLA_Q), blk(HK, COL_GLA_K), blk(D, COL_GLA_V), blk(D, COL_GLA_OG),
                  blk(LANE, COL_SMALL),
                  _const_spec((LANE, HK)), _const_spec((1, HK)), _const_spec((1, GLA_DV))],
        out_specs=[out, out],
        out_shape=[jax.ShapeDtypeStruct((B, S, D), BF16), jax.ShapeDtypeStruct((B, S, D), BF16)],
        scratch_shapes=[pltpu.VMEM((8, D), F32), pltpu.VMEM((1, D), F32),
                        pltpu.VMEM((GLA_HEADS, GLA_DK, GLA_DV), F32)],
        compiler_params=_cparams(("parallel", "arbitrary")),
        name="rglru_gla",
    )(proj3, proj3, conv_w, conv_b, wg, bg, lam, proj3, proj3, proj3, proj3, proj3, wa_pad, ba, gn)


def _cmp_kernel(kv_ref, pos_ref, w1_ref, w2_ref, o_ref, x_ref):
    x_ref[...] = kv_ref[0].astype(F32)
    nrow = x_ref.shape[0] // CMP_STRIDE
    first = None
    second = None
    for i in range(CMP_STRIDE):
        xi = x_ref[pl.ds(i, nrow, stride=CMP_STRIDE), :]
        f = jnp.dot((xi + pos_ref[0, i]).astype(BF16), w1_ref[0, i], preferred_element_type=F32)
        s = jnp.dot((xi + pos_ref[1, i]).astype(BF16), w1_ref[1, i], preferred_element_type=F32)
        first = f if first is None else first + f
        second = s if second is None else second + s
    pre = first + pltpu.roll(second, nrow - 1, axis=0)
    hid = _gelu_tanh(pre).astype(BF16)
    o_ref[0, 0] = jnp.dot(hid, w2_ref[...], preferred_element_type=F32).astype(o_ref.dtype)


def _compress(proj3, pos, w1, w2):
    B, S, _ = proj3.shape
    nrow = S // CMP_STRIDE
    kvb = COL_NSA_KV // LANE
    return pl.pallas_call(
        _cmp_kernel,
        grid=(B, NSA_G),
        in_specs=[pl.BlockSpec((1, S, LANE), lambda b, g: (b, 0, kvb + g)),
                  _const_spec(pos.shape), _const_spec(w1.shape), _const_spec(w2.shape)],
        out_specs=pl.BlockSpec((1, 1, nrow, LANE), lambda b, g: (b, g, 0, 0)),
        out_shape=jax.ShapeDtypeStruct((B, NSA_G, nrow, LANE), BF16),
        scratch_shapes=[pltpu.VMEM((S, LANE), F32)],
        compiler_params=_cparams(("parallel", "parallel")),
        name="nsa_compress",
    )(proj3, pos, w1, w2)


def _compress_params(cmp_pos, cmp_w1, cmp_w2):
    hd = NSA_HD
    w1 = cmp_w1.reshape(2, 2, CMP_STRIDE, hd, CMP_HIDDEN)
    z = jnp.zeros_like(w1[0])
    w1 = jnp.concatenate([jnp.concatenate([w1[0], z], axis=-1),
                          jnp.concatenate([z, w1[1]], axis=-1)], axis=-2)
    pos = cmp_pos.reshape(2, 2, CMP_STRIDE, 1, hd)
    pos = jnp.concatenate([pos[0], pos[1]], axis=-1)
    z2 = jnp.zeros_like(cmp_w2[0])
    w2 = jnp.concatenate([jnp.concatenate([cmp_w2[0], z2], axis=-1),
                          jnp.concatenate([z2, cmp_w2[1]], axis=-1)], axis=0)
    return pos, w1.astype(BF16), w2.astype(BF16)


V_ROWS = NSA_HD + 16
MASK_BIG = 1e30
LOG2_E = 1.4426950408889634


def _nsa_kernel(q_ref, kvc_ref, kvs_ref, kvw_ref, gate_ref, bc_ref, bs_ref, bw_ref, xtra_ref, ovl_ref,
                gsel_ref, y_ref, a_ref, b_ref, vs_ref, vw_ref, vc_ref, lg_ref, p_ref):
    NB = q_ref.shape[0]
    rows_nb = range(NB)
    TQ = Q_TILE
    KT = K_TILE
    RT = TQ // KT
    HW = NSA_R * TQ
    qi = pl.program_id(2)
    S = kvs_ref.shape[1]
    n_cmp_rows = kvc_ref.shape[2]

    def value_rows(tile):
        t = tile.astype(F32).T
        return jnp.concatenate([t[NSA_HD:], jnp.ones((V_ROWS - NSA_HD, t.shape[1]), F32)], axis=0).astype(BF16)

    n_blk = ovl_ref.shape[0]
    k_lanes = lax.broadcasted_iota(jnp.int32, (LANE, LANE), 1) < NSA_HD

    @pl.when(qi == 0)
    def _():
        for nb in rows_nb:
            for j in range(S // LANE):
                rows = slice(j * LANE, (j + 1) * LANE)
                ks = kvs_ref[nb, rows, :]
                a_ref[nb, rows, :] = jnp.where(k_lanes, ks, xtra_ref[rows, :])
                vs_ref[nb, :, rows] = value_rows(ks)
                vw_ref[nb, :, rows] = value_rows(kvw_ref[nb, rows, :])
            vc_ref[nb] = value_rows(kvc_ref[nb, 0])
            b_ref[nb, NSA_HD + n_blk:, :] = jnp.zeros((LANE - NSA_HD - n_blk, HW), BF16)

    def tile_rows(j, n=1):
        return pl.ds(pl.multiple_of(j * KT, KT), n * KT)

    def normalised(acc):
        return acc[:NSA_HD] * (1.0 / acc[NSA_HD:NSA_HD + 1])

    def q_transposed(nb):
        qf = q_ref[nb].astype(F32) * (NSA_HD ** -0.5)
        zero_rows = jnp.zeros((LANE - NSA_HD, TQ), F32)
        heads = []
        for pr in range(NSA_R // 2):
            t = qf[:, pr * LANE:(pr + 1) * LANE].T
            heads.append(jnp.concatenate([t[:NSA_HD], zero_rows], axis=0))
            heads.append(jnp.concatenate([t[NSA_HD:], zero_rows], axis=0))
        return jnp.concatenate(heads, axis=1)

    qt32 = [q_transposed(nb) for nb in rows_nb]
    qt = [x.astype(BF16) for x in qt32]
    qt32_l2 = [x * LOG2_E for x in qt32]
    qt_l2 = [x.astype(BF16) for x in qt32_l2]

    parts = TQ // LANE
    o_win_parts = [[None] * parts for _ in rows_nb]
    for h in range(parts):
        own = parts * qi + h
        skipped = jnp.maximum(WIN_SPAN - 1 - own, 0)
        rows_w = pl.ds(pl.multiple_of(jnp.maximum(own - (WIN_SPAN - 1), 0) * LANE, LANE), WIN_SPAN * LANE)
        bias_w = jnp.concatenate([bw_ref[0, jnp.minimum(t + skipped, WIN_SPAN)] for t in range(WIN_SPAN)], axis=0)
        q_part = [jnp.concatenate([qt_l2[nb][:, r * TQ + h * LANE:r * TQ + (h + 1) * LANE] for r in range(NSA_R)],
                                  axis=1) for nb in rows_nb]
        lg_w = [jnp.dot(kvw_ref[nb, rows_w, :], q_part[nb], preferred_element_type=F32).astype(BF16) + bias_w
                for nb in rows_nb]
        p_w = [jnp.exp2(lg - jnp.max(lg, axis=0, keepdims=True)) for lg in lg_w]
        for nb in rows_nb:
            o_win_parts[nb][h] = normalised(jnp.dot(vw_ref[nb, :, rows_w], p_w[nb], preferred_element_type=F32))
    o_win = [jnp.concatenate([o_win_parts[nb][h][:, r * LANE:(r + 1) * LANE]
                              for r in range(NSA_R) for h in range(parts)], axis=1) for nb in rows_nb]

    cq = TQ // CMP_STRIDE
    start_c = pl.multiple_of(bc_ref.shape[1] - n_cmp_rows - cq * qi, 8)
    bias_c = bc_ref[0, pl.ds(start_c, n_cmp_rows), :]
    valid_c = bias_c > 0.5 * NEG_INF
    lg_c = [jnp.dot(kvc_ref[nb, 0], qt[nb], preferred_element_type=F32) + bias_c for nb in rows_nb]
    e_c = [jnp.exp(lg - jnp.max(lg, axis=0, keepdims=True)) for lg in lg_c]
    p_c = [jnp.where(valid_c, e * (1.0 / jnp.sum(e, axis=0, keepdims=True)), 0.0) for e in e_c]
    o_cmp = [jnp.dot(vc_ref[nb], p_c[nb].astype(BF16), preferred_element_type=F32)[:NSA_HD] for nb in rows_nb]

    ovl = ovl_ref[...]
    ns = ovl.shape[0]
    blk = lax.broadcasted_iota(jnp.int32, (ns, TQ), 0)
    cur = (qi * TQ + lax.broadcasted_iota(jnp.int32, (ns, TQ), 1)) // SLC_BLOCK
    forced = (blk == 0) | (blk == cur) | (blk == cur - 1)
    allowed = blk <= cur

    def importance(p):
        psum = p[:, 0:TQ]
        for r in range(1, NSA_R):
            psum = psum + p[:, r * TQ:(r + 1) * TQ]
        p_hi = psum.astype(BF16)
        p_lo = (psum - p_hi.astype(F32)).astype(BF16)
        imp = jnp.dot(ovl, p_hi, preferred_element_type=F32) + jnp.dot(ovl, p_lo, preferred_element_type=F32)
        return jnp.where(forced, FORCED, jnp.where(allowed, imp, -FORCED))

    score = [importance(p) for p in p_c]
    groups = ns // 8
    blk8 = lax.broadcasted_iota(jnp.int32, (8, TQ), 0)
    for nb in rows_nb:
        sc = [score[nb][8 * v:8 * v + 8] for v in range(groups)]
        rank = [jnp.zeros((8, TQ), F32) for _ in range(groups)]
        for j in range(ns):
            sj = score[nb][j:j + 1, :]
            for v in range(groups):
                if j < 8 * v:
                    beats = sj >= sc[v]
                elif j >= 8 * v + 8:
                    beats = sj > sc[v]
                else:
                    tie = jnp.where(blk8 > j - 8 * v, 1.0, 0.0)
                    rank[v] = rank[v] + jnp.where(sj > sc[v], 1.0, jnp.where(sj == sc[v], tie, 0.0))
                    continue
                rank[v] = rank[v] + jnp.where(beats, 1.0, 0.0)
        neg_sel = jnp.where(jnp.concatenate(rank, axis=0) < float(SLC_TOP_N), 0.0, -MASK_BIG).astype(BF16)
        b_ref[nb, 0:NSA_HD] = qt_l2[nb][:NSA_HD]
        b_ref[nb, NSA_HD:NSA_HD + ns] = jnp.concatenate([neg_sel] * NSA_R, axis=1)

    def logits(nb, j):
        return jnp.dot(a_ref[nb, tile_rows(j), :], b_ref[nb], preferred_element_type=F32).astype(BF16)

    for nb in rows_nb:
        lg_ref[nb] = logits(nb, 0)
        p_ref[nb] = jnp.zeros(p_ref.shape[1:], BF16)

    last = RT * qi + RT - 1

    def trip(j, carry):
        lg_next = [logits(nb, jnp.minimum(j + 1, last)) for nb in rows_nb]
        prev_rows = tile_rows(jnp.maximum(j - 1, 0))
        bias = bs_ref[0, jnp.where(j >= RT * qi - 1, j - RT * qi + 1, RT + 1)]
        out = []
        for nb in rows_nb:
            m, acc, alpha = carry[nb]
            acc = acc * alpha + jnp.dot(vs_ref[nb, :, prev_rows], p_ref[nb], preferred_element_type=F32)
            lg = lg_ref[nb] + bias
            m_new = jnp.maximum(m, jnp.max(lg, axis=0, keepdims=True).astype(F32))
            p_ref[nb] = jnp.exp2(lg - m_new.astype(BF16))
            out.append((m_new, acc, jnp.exp2(m - m_new)))
        for nb in rows_nb:
            lg_ref[nb] = lg_next[nb]
        return tuple(out)

    init = (jnp.full((1, HW), NEG_INF, F32), jnp.zeros((V_ROWS, HW), F32), jnp.ones((1, HW), F32))
    carry = lax.fori_loop(0, last + 1, trip, tuple(init for _ in rows_nb))
    o_slc = [normalised(carry[nb][1] * carry[nb][2]
                        + jnp.dot(vs_ref[nb, :, tile_rows(last)], p_ref[nb], preferred_element_type=F32))
             for nb in rows_nb]

    for nb in rows_nb:
        gs = jax.nn.sigmoid(lax.dot_general(gsel_ref[0], gate_ref[nb], _NT, preferred_element_type=F32))

        def gate_row(br):
            return jnp.concatenate([gs[br * NSA_R + r:br * NSA_R + r + 1] for r in range(NSA_R)], axis=1)

        o = gate_row(0) * o_cmp[nb] + gate_row(1) * o_slc[nb] + gate_row(2) * o_win[nb]
        o = jnp.concatenate([o[:, r * TQ:(r + 1) * TQ] for r in range(NSA_R)], axis=0)
        y_ref[nb] = o.T.astype(y_ref.dtype)


def _nsa(proj3, kvc, bias_c, bias_s, bias_w, xtra, ovl_t, gsel):
    B, S, _ = proj3.shape
    TQ = Q_TILE
    G = NSA_G
    RW = NSA_R * NSA_HD
    HW = NSA_R * TQ
    kvb = COL_NSA_KV // LANE
    ncr = kvc.shape[2]
    KT = K_TILE
    NB = NSA_BATCH_ROWS
    assert S % TQ == 0 and TQ % KT == 0 and S >= TQ + KT and WINDOW == KT and ncr % LANE == 0 and B % NB == 0
    return pl.pallas_call(
        _nsa_kernel,
        grid=(G, B // NB, S // TQ),
        in_specs=[pl.BlockSpec((NB, TQ, RW), lambda g, b, i: (b, i, COL_NSA_Q // RW + g)),
                  pl.BlockSpec((NB, 1, ncr, LANE), lambda g, b, i: (b, g, 0, 0)),
                  pl.BlockSpec((NB, S, LANE), lambda g, b, i: (b, 0, kvb + G + g)),
                  pl.BlockSpec((NB, S, LANE), lambda g, b, i: (b, 0, kvb + 2 * G + g)),
                  pl.BlockSpec((NB, TQ, LANE), lambda g, b, i: (b, i, COL_SMALL // LANE)),
                  pl.BlockSpec((1,) + bias_c.shape[1:], lambda g, b, i: (g, 0, 0)),
                  pl.BlockSpec((1,) + bias_s.shape[1:], lambda g, b, i: (g, 0, 0, 0)),
                  pl.BlockSpec((1,) + bias_w.shape[1:], lambda g, b, i: (g, 0, 0, 0)),
                  pl.BlockSpec(xtra.shape, lambda g, b, i: (0, 0)),
                  pl.BlockSpec(ovl_t.shape, lambda g, b, i: (0, 0)),
                  pl.BlockSpec((1,) + gsel.shape[1:], lambda g, b, i: (g, 0, 0))],
        out_specs=pl.BlockSpec((NB, TQ, RW), lambda g, b, i: (b, i, g)),
        out_shape=jax.ShapeDtypeStruct((B, S, D_MODEL), BF16),
        scratch_shapes=[pltpu.VMEM((NB, S, LANE), BF16), pltpu.VMEM((NB, LANE, HW), BF16),
                        pltpu.VMEM((NB, V_ROWS, S), BF16), pltpu.VMEM((NB, V_ROWS, S), BF16),
                        pltpu.VMEM((NB, V_ROWS, ncr), BF16), pltpu.VMEM((NB, KT, HW), BF16),
                        pltpu.VMEM((NB, KT, HW), BF16)],
        compiler_params=_cparams(("parallel", "parallel", "arbitrary")),
        name="nsa_attention",
    )(proj3, kvc, proj3, proj3, proj3, bias_c, bias_s, bias_w, xtra, ovl_t, gsel)


def _rel_bucket(dist):
    n = jnp.maximum(dist, 0)
    nf = jnp.maximum(n, REL_MAX_EXACT).astype(F32)
    large = REL_MAX_EXACT + (jnp.log(nf / REL_MAX_EXACT) / math.log(REL_MAX_DIST / REL_MAX_EXACT)
                             * (REL_BUCKETS - REL_MAX_EXACT)).astype(jnp.int32)
    large = jnp.minimum(large, REL_BUCKETS - 1)
    return jnp.where(n < REL_MAX_EXACT, n, large)


def _nsa_tables(rel_table, S):
    TQ = Q_TILE
    tbl = rel_table.astype(F32).reshape(REL_BUCKETS, NSA_G, NSA_R)
    tbl = tbl - tbl[REL_BUCKETS - 1]

    def table(dist, ok):
        onehot = jax.nn.one_hot(_rel_bucket(dist), REL_BUCKETS, dtype=F32)
        b = jnp.einsum("...kqn,ngr->g...krq", onehot, tbl, precision=lax.Precision.HIGHEST)
        b = jnp.where(ok[..., :, None, :], b, NEG_INF)
        return b.reshape(b.shape[:-2] + (NSA_R * dist.shape[-1],))

    KT = K_TILE
    kk = jnp.arange(KT)[:, None]
    qq = jnp.arange(TQ)[None, :]
    dist = qq - kk - KT * (jnp.arange(TQ // KT + 1)[:, None, None] - 1)
    bias_s = table(dist, dist >= 0) * LOG2_E
    bias_s = jnp.concatenate([bias_s, jnp.zeros_like(bias_s[:, :1])], axis=1).astype(BF16)
    k128 = jnp.arange(LANE)[:, None]
    q128 = jnp.arange(LANE)[None, :]
    dist_w = q128 - k128 + LANE * (WIN_SPAN - 1 - jnp.arange(WIN_SPAN)[:, None, None])
    bias_w = table(dist_w, (dist_w >= 0) & (dist_w < WINDOW)) * LOG2_E
    bias_w = jnp.concatenate([bias_w, jnp.full_like(bias_w[:, :1], NEG_INF)], axis=1).astype(BF16)
    n_rows = S // CMP_STRIDE
    off = n_rows - TQ // CMP_STRIDE
    c_rel = jnp.arange(off + n_rows)[:, None] - off
    dist_c = qq - CMP_STRIDE * c_rel - (CMP_BLOCK - 1)
    bias_c = table(dist_c, dist_c >= 0)

    n_slc = S // SLC_BLOCK
    n_cmp = n_rows - CMP_BLOCK // CMP_STRIDE + 1
    xtra = (jnp.arange(LANE)[None, :] - NSA_HD == (jnp.arange(S)[:, None] // SLC_BLOCK)).astype(BF16)
    cmp_start = jnp.arange(n_rows) * CMP_STRIDE
    slc_start = jnp.arange(n_slc) * SLC_BLOCK
    overlap = jnp.clip(jnp.minimum(cmp_start[:, None] + CMP_BLOCK, slc_start[None, :] + SLC_BLOCK)
                       - jnp.maximum(cmp_start[:, None], slc_start[None, :]), 0).astype(F32) / CMP_BLOCK
    overlap = jnp.where(jnp.arange(n_rows)[:, None] < n_cmp, overlap, 0.0)
    g = jnp.arange(NSA_G)[:, None, None]
    row = jnp.arange(16)[None, :, None]
    col = jnp.arange(LANE)[None, None, :]
    src = SMALL_GATE + (row // NSA_R) * NSA_HEADS + g * NSA_R + row % NSA_R
    gsel = ((col == src) & (row < 3 * NSA_R)).astype(BF16)
    return bias_c, bias_s, bias_w, xtra, overlap.T.astype(BF16), gsel


def _merge_ffn_kernel(ya_ref, yb_ref, yc_ref, g0_ref, g1_ref, g2_ref, x_ref, wb_ref, wo_ref, gn_ref,
                      gpre_ref, win_ref, wout_ref, gpost_ref, o_ref, *, chunk):
    m = None
    for br, (y_ref, g_ref) in enumerate(((ya_ref, g0_ref), (yb_ref, g1_ref), (yc_ref, g2_ref))):
        t = jax.nn.sigmoid(g_ref[...].astype(F32)) * jnp.dot(y_ref[...], wb_ref[br], preferred_element_type=F32)
        m = t if m is None else m + t
    z = jnp.dot(m.astype(BF16), wo_ref[...], preferred_element_type=F32)
    x = x_ref[...] + _rms(z, gn_ref[...])
    h = _rms(x, gpre_ref[...]).astype(BF16)
    acc = None
    for c in range(D_FF // chunk):
        gt = jnp.dot(h, win_ref[:, c * chunk:(c + 1) * chunk], preferred_element_type=F32)
        up = jnp.dot(h, win_ref[:, D_FF + c * chunk:D_FF + (c + 1) * chunk], preferred_element_type=F32)
        a = (gt * jax.nn.sigmoid(gt) * up).astype(BF16)
        t = jnp.dot(a, wout_ref[c * chunk:(c + 1) * chunk, :], preferred_element_type=F32)
        acc = t if acc is None else acc + t
    o_ref[...] = x + _rms(acc, gpost_ref[...])


def _merge_ffn(ya, yb, yc, proj, x2, wb, wo, gn, gpre, win, wout, gpost, layer, tm=512, chunk=256):
    T, D = x2.shape
    row = lambda i: (i, 0)
    mcol = COL_MERGE // D
    return pl.pallas_call(
        functools.partial(_merge_ffn_kernel, chunk=chunk),
        grid=(T // tm,),
        in_specs=[pl.BlockSpec((tm, D), row), pl.BlockSpec((tm, D), row), pl.BlockSpec((tm, D), row),
                  pl.BlockSpec((tm, D), lambda i: (i, mcol)),
                  pl.BlockSpec((tm, D), lambda i: (i, mcol + 1)),
                  pl.BlockSpec((tm, D), lambda i: (i, mcol + 2)),
                  pl.BlockSpec((tm, D), row),
                  _const_spec(wb.shape[1:], layer), _const_spec(wo.shape[1:], layer), _const_spec((1, D)),
                  _const_spec((1, D)), _const_spec(win.shape[1:], layer), _const_spec(wout.shape[1:], layer),
                  _const_spec((1, D))],
        out_specs=pl.BlockSpec((tm, D), row),
        out_shape=jax.ShapeDtypeStruct((T, D), F32),
        compiler_params=_cparams(("parallel",)),
        name="merge_ffn",
    )(ya, yb, yc, proj, proj, proj, x2, wb, wo, gn, gpre, win, wout, gpost)


def _in_column_sources():
    (lru_x, lru_g, nsa_q, nsa_kv, nsa_gate, gla_q, gla_k, gla_v, gla_og, gla_lr, merge) = np.split(
        np.arange(sum(IN_SIZES)), np.cumsum(IN_SIZES)[:-1])
    kv = nsa_kv.reshape(3, 2, NSA_G, NSA_HD).transpose(0, 2, 1, 3).reshape(-1)
    pad = np.full(LANE - nsa_gate.size - gla_lr.size, -1)
    src = np.concatenate([lru_x, lru_g, nsa_q, gla_v, gla_og, merge, kv, gla_q, gla_k, nsa_gate, gla_lr, pad])
    assert src.size == PROJ_WIDTH
    return src


def _permute_plan():
    src = _in_column_sources()
    plan = []
    for c in range(PROJ_WIDTH // LANE):
        runs = []
        for s in src[c * LANE:(c + 1) * LANE]:
            s = int(s)
            if runs and ((s < 0 and runs[-1][0] < 0) or (s >= 0 and runs[-1][0] >= 0 and runs[-1][0] + runs[-1][1] == s)):
                runs[-1][1] += 1
            else:
                runs.append([s, 1])
        assert all(n % 8 == 0 and (s < 0 or s % 8 == 0) for s, n in runs)
        plan.append(tuple((s, n) for s, n in runs))
    return tuple(plan)


def _permute_kernel(wt_ref, o_ref, *, plan):
    tc = wt_ref.shape[1]
    for c, runs in enumerate(plan):
        parts = [wt_ref[s:s + n, :] if s >= 0 else jnp.zeros((n, tc), F32) for s, n in runs]
        block = parts[0] if len(parts) == 1 else jnp.concatenate(parts, axis=0)
        o_ref[:, c * LANE:(c + 1) * LANE] = block.T.astype(o_ref.dtype)


def _permute_in_weights(w_in, tc=128):
    L, D, W = w_in.shape
    return pl.pallas_call(
        functools.partial(_permute_kernel, plan=_permute_plan()),
        grid=(L, D // tc),
        in_specs=[pl.BlockSpec((None, W, tc), lambda l, i: (l, 0, i))],
        out_specs=pl.BlockSpec((None, tc, PROJ_WIDTH), lambda l, i: (l, i, 0)),
        out_shape=jax.ShapeDtypeStruct((L, D, PROJ_WIDTH), BF16),
        compiler_params=_cparams(("parallel", "parallel")),
        name="permute_w_in",
    )(jnp.swapaxes(w_in, 1, 2))


def _layer(x2, B, S, tables, layer, norm_g, w_in, conv_w, conv_b, lru_w_gates, lru_b_gates, lru_lambda,
           cmp_pos, cmp_w1, cmp_w2, gla_wa2, gla_ba, gla_norm, w_branch, w_out, w_ffn_in, w_ffn_out):
    D = D_MODEL
    proj = _in_proj(x2, norm_g[0][None, :], w_in, layer)
    proj3 = proj.reshape(B, S, PROJ_WIDTH)

    wg = jnp.concatenate([lru_w_gates[0], lru_w_gates[1]], axis=-1).astype(BF16)
    wa_pad = jnp.zeros((LANE, GLA_HEADS * GLA_DK), F32).at[SMALL_LR:SMALL_LR + GLA_RANK].set(gla_wa2).astype(BF16)
    y_a, y_c = _lru_gla(proj3, conv_w, conv_b[None, :], wg, lru_b_gates, lru_lambda[None, :],
                        wa_pad, gla_ba[None, :], gla_norm[None, :])

    kvc = _compress(proj3, *_compress_params(cmp_pos, cmp_w1, cmp_w2))
    y_b = _nsa(proj3, kvc, *tables)

    return _merge_ffn(y_a.reshape(B * S, D), y_b.reshape(B * S, D), y_c.reshape(B * S, D), proj, x2,
                      w_branch, w_out, norm_g[1][None, :], norm_g[2][None, :], w_ffn_in, w_ffn_out,
                      norm_g[3][None, :], layer)


def kernel(x, rel_table, norm_g, w_in, conv_w, conv_b, lru_w_gates, lru_b_gates, lru_lambda, cmp_pos, cmp_w1,
           cmp_w2, gla_wa2, gla_ba, gla_norm, w_branch, w_out, w_ffn_in, w_ffn_out):
    B, S, D = x.shape
    tables = _nsa_tables(rel_table, S)
    x2 = x.reshape(B * S, D)
    w_in = _permute_in_weights(w_in)
    w_branch, w_out, w_ffn_in, w_ffn_out = (w.astype(BF16) for w in (w_branch, w_out, w_ffn_in, w_ffn_out))
    for l in range(norm_g.shape[0]):
        x2 = _layer(x2, B, S, tables, l, norm_g[l], w_in, conv_w[l], conv_b[l], lru_w_gates[l], lru_b_gates[l],
                    lru_lambda[l], cmp_pos[l], cmp_w1[l], cmp_w2[l], gla_wa2[l], gla_ba[l], gla_norm[l],
                    w_branch, w_out, w_ffn_in, w_ffn_out)
    return x2.reshape(B, S, D)
```
